```python
import math
import jax, jax.numpy as jnp
from jax import lax
import numpy as np

D_MODEL = 1024
BATCH = 8
SEQ = 2048
DEPTH = 2
DEC_BATCH = 128
DEC_SEQ = 4
PAST_LEN = 16384
PAGE_SIZE = 128

S5_GROUP = 16
S5_WIDTH = D_MODEL // 2
S5_GROUPS = S5_WIDTH // S5_GROUP
S5_STATE = 64
GLA_HEADS = 4
GLA_DK = D_MODEL // 16
GLA_DV = D_MODEL // 8
GLA_RANK = 16
GLA_TAU = 16.0
GLA_CHUNK = 64
SWA_HEADS = 8
SWA_KV_HEADS = 2
SWA_GQ = SWA_HEADS // SWA_KV_HEADS
SWA_HD = 64
WINDOW = 128
ROPE_THETA = 10000.0
N_BRANCH = 3
FFN_HIDDEN = -(-8 * D_MODEL // (3 * 256)) * 256
EPS = 1e-6

GLA_QK = GLA_HEADS * GLA_DK
GLA_V = GLA_HEADS * GLA_DV
SWA_Q = SWA_HEADS * SWA_HD
SWA_KV = SWA_KV_HEADS * SWA_HD
IN_SPLITS = (S5_WIDTH, GLA_QK, GLA_QK, GLA_V, GLA_V, GLA_RANK, SWA_Q, SWA_KV, SWA_KV, N_BRANCH * D_MODEL)
N_IN = sum(IN_SPLITS)
BRANCH_WIDTHS = (S5_WIDTH, GLA_V, SWA_Q)
MIX_WIDTH = sum(BRANCH_WIDTHS)

kernel_name = "hybrid_s5_gla_swa_gated_decoder_step"

F32 = jnp.float32


def rms_norm(x, g):
    xf = x.astype(F32)
    y = xf * lax.rsqrt(jnp.mean(xf * xf, axis=-1, keepdims=True) + EPS)
    return (y * g.astype(F32)).astype(x.dtype)


def rope(x, pos):
    half = SWA_HD // 2
    inv = ROPE_THETA ** (-jnp.arange(half, dtype=F32) * (2.0 / SWA_HD))
    ang = pos.astype(F32)[:, None] * inv[None, :]
    cos = jnp.cos(ang)[None, :, None, :]
    sin = jnp.sin(ang)[None, :, None, :]
    xf = x.astype(F32)
    x1, x2 = xf[..., :half], xf[..., half:]
    return jnp.concatenate([x1 * cos - x2 * sin, x2 * cos + x1 * sin], axis=-1).astype(x.dtype)


def s5_mixer(u, h0_re, h0_im, lam_re, lam_im, log_dt, b_re, b_im, c_re, c_im, d_skip, w_glu, b_glu):
    bsz, t, _ = u.shape
    uf = u.astype(F32).reshape(bsz, t, S5_GROUPS, S5_GROUP)
    lr, li = lam_re.astype(F32), lam_im.astype(F32)
    dt = jnp.exp(log_dt.astype(F32))[:, None]
    mag = jnp.exp(lr * dt)
    ar, ai = mag * jnp.cos(li * dt), mag * jnp.sin(li * dt)
    den = lr * lr + li * li
    zr, zi = ar - 1.0, ai
    er = (zr * lr + zi * li) / den
    ei = (zi * lr - zr * li) / den
    br, bi = b_re.astype(F32), b_im.astype(F32)
    bbr = er[:, None, :] * br - ei[:, None, :] * bi
    bbi = er[:, None, :] * bi + ei[:, None, :] * br
    xr = jnp.einsum('btgc,gcn->btgn', uf, bbr)
    xi = jnp.einsum('btgc,gcn->btgn', uf, bbi)
    arb = jnp.broadcast_to(ar, xr.shape)
    aib = jnp.broadcast_to(ai, xr.shape)

    def combine(e1, e2):
        a1r, a1i, b1r, b1i = e1
        a2r, a2i, b2r, b2i = e2
        return (a1r * a2r - a1i * a2i, a1r * a2i + a1i * a2r,
                a2r * b1r - a2i * b1i + b2r, a2r * b1i + a2i * b1r + b2i)

    pr, pi, sr, si = lax.associative_scan(combine, (arb, aib, xr, xi), axis=1)
    h0r = h0_re.astype(F32)[:, None]
    h0i = h0_im.astype(F32)[:, None]
    hr = sr + pr * h0r - pi * h0i
    hi = si + pr * h0i + pi * h0r
    y = (jnp.einsum('btgn,gnc->btgc', hr, c_re.astype(F32))
         - jnp.einsum('btgn,gnc->btgc', hi, c_im.astype(F32)))
    y = y.reshape(bsz, t, S5_WIDTH) + d_skip.astype(F32) * u.astype(F32)
    y = jax.nn.gelu(y)
    y = y * jax.nn.sigmoid(y @ w_glu.astype(F32) + b_glu.astype(F32))
    return y.astype(u.dtype), hr[:, -1], hi[:, -1]


def gla_recurrence(q, k, v, log_a, s0):
    bsz, t, h, dk = q.shape
    dv = v.shape[-1]
    L = math.gcd(t, GLA_CHUNK)
    n = t // L
    q, k, v, log_a = [z.reshape(bsz, n, L, h, z.shape[-1]) for z in (q, k, v, log_a)]
    b = jnp.cumsum(log_a, axis=2)
    b_last = b[:, :, -1:]
    qd = q * jnp.exp(b)
    kd = k * jnp.exp(-b)
    kst = k * jnp.exp(b_last - b)
    causal = jnp.tril(jnp.ones((L, L), dtype=bool))
    att = jnp.where(causal, jnp.einsum('bnlhk,bnmhk->bnhlm', qd, kd), 0.0)
    o_intra = jnp.einsum('bnhlm,bnmhv->bnlhv', att, v)
    ds = jnp.einsum('bnlhk,bnlhv->bnhkv', kst, v)
    decay = jnp.exp(b_last[:, :, 0])

    def step(s, xs):
        dec, d = xs
        return dec[..., None] * s + d, s

    s_fin, s_before = lax.scan(step, s0, (jnp.moveaxis(decay, 1, 0), jnp.moveaxis(ds, 1, 0)))
    s_before = jnp.moveaxis(s_before, 0, 1)
    o_inter = jnp.einsum('bnlhk,bnhkv->bnlhv', qd, s_before)
    return (o_intra + o_inter).reshape(bsz, t, h, dv), s_fin


def sink_attention(q, k, v, q_pos, k_pos, sinks):
    s = jnp.einsum('bnqhgd,bnkhd->bnhgqk', q.astype(F32), k.astype(F32)) * (SWA_HD ** -0.5)
    qp = q_pos[:, :, None]
    kp = k_pos[:, None, :]
    mask = (kp <= qp) & (kp > qp - WINDOW) & (kp >= 0)
    s = jnp.where(mask[None, :, None, None], s, -jnp.inf)
    sk = sinks.astype(F32)[None, None, :, :, None, None]
    m = jnp.maximum(jnp.max(s, axis=-1, keepdims=True), sk)
    p = jnp.exp(s - m)
    den = jnp.sum(p, axis=-1, keepdims=True) + jnp.exp(sk - m)
    return jnp.einsum('bnhgqk,bnkhd->bnqhgd', p / den, v.astype(F32))


def mixer_block(xn, pos, h0_re, h0_im, s0, win_k, win_v, lp):
    bsz, t, _ = xn.shape
    dt = xn.dtype
    splits = np.cumsum(np.array(IN_SPLITS))[:-1].tolist()
    proj = xn @ lp['w_in']
    u_a, q_b, k_b, v_b, r_b, a_low, q_c, k_c, v_c, gates = jnp.split(proj, splits, axis=-1)

    y_a, h_re, h_im = s5_mixer(u_a, h0_re, h0_im, lp['lam_re'], lp['lam_im'], lp['log_dt'],
                               lp['b_re'], lp['b_im'], lp['c_re'], lp['c_im'], lp['d_skip'],
                               lp['w_glu'], lp['b_glu'])

    qb = q_b.astype(F32).reshape(bsz, t, GLA_HEADS, GLA_DK) * (GLA_DK ** -0.5)
    kb = k_b.astype(F32).reshape(bsz, t, GLA_HEADS, GLA_DK)
    vb = v_b.astype(F32).reshape(bsz, t, GLA_HEADS, GLA_DV)
    log_a = jax.nn.log_sigmoid((a_low @ lp['w_gla_a2'] + lp['b_gla_a']).astype(F32)) / GLA_TAU
    o_b, s_new = gla_recurrence(qb, kb, vb, log_a.reshape(bsz, t, GLA_HEADS, GLA_DK), s0.astype(F32))
    o_b = rms_norm(o_b, lp['g_gla_norm']) * jax.nn.silu(r_b.astype(F32).reshape(bsz, t, GLA_HEADS, GLA_DV))
    y_b = o_b.reshape(bsz, t, GLA_V).astype(dt)

    qc = rope(q_c.reshape(bsz, t, SWA_HEADS, SWA_HD), pos)
    kc = rope(k_c.reshape(bsz, t, SWA_KV_HEADS, SWA_HD), pos)
    vc = v_c.reshape(bsz, t, SWA_KV_HEADS, SWA_HD)
    sinks = lp['sinks'].reshape(SWA_KV_HEADS, SWA_GQ)
    if win_k is None:
        nb = t // WINDOW
        qq = qc.reshape(bsz, nb, WINDOW, SWA_KV_HEADS, SWA_GQ, SWA_HD)
        kb_ = kc.reshape(bsz, nb, WINDOW, SWA_KV_HEADS, SWA_HD)
        vb_ = vc.reshape(bsz, nb, WINDOW, SWA_KV_HEADS, SWA_HD)
        kk = jnp.concatenate([jnp.concatenate([jnp.zeros_like(kb_[:, :1]), kb_[:, :-1]], axis=1), kb_], axis=2)
        vv = jnp.concatenate([jnp.concatenate([jnp.zeros_like(vb_[:, :1]), vb_[:, :-1]], axis=1), vb_], axis=2)
        qpos = pos.reshape(nb, WINDOW)
        kpos = jnp.concatenate([qpos - WINDOW, qpos], axis=1)
        o_c = sink_attention(qq, kk, vv, qpos, kpos, sinks)
        w = min(WINDOW, t)
        new_k, new_v = kc[:, t - w:], vc[:, t - w:]
    else:
        w = win_k.shape[1]
        kall = jnp.concatenate([win_k.astype(kc.dtype), kc], axis=1)
        vall = jnp.concatenate([win_v.astype(vc.dtype), vc], axis=1)
        qq = qc.reshape(bsz, 1, t, SWA_KV_HEADS, SWA_GQ, SWA_HD)
        kpos = (PAST_LEN - w + jnp.arange(w + t, dtype=jnp.int32))[None]
        o_c = sink_attention(qq, kall[:, None], vall[:, None], pos[None], kpos, sinks)
        new_k, new_v = kall[:, t:], vall[:, t:]
    y_c = o_c.reshape(bsz, t, SWA_Q).astype(dt)

    wb = lp['w_branch']
    e1, e2 = BRANCH_WIDTHS[0], BRANCH_WIDTHS[0] + BRANCH_WIDTHS[1]
    g = jax.nn.sigmoid(gates.astype(F32)).reshape(bsz, t, N_BRANCH, D_MODEL).astype(dt)
    merged = (g[:, :, 0] * (y_a @ wb[:e1]) + g[:, :, 1] * (y_b @ wb[e1:e2])
              + g[:, :, 2] * (y_c @ wb[e2:]))
    return merged @ lp['w_out'], h_re, h_im, s_new, new_k, new_v


def swiglu(x, wg, wu, wd):
    return (jax.nn.silu(x @ wg) * (x @ wu)) @ wd


def setup_inputs(seed: int = 0) -> dict:
    key = jax.random.key(seed)
    ks = jax.random.split(key, 40)
    nrm = lambda k, shape, s: s * jax.random.normal(k, shape, F32)
    w_cache = min(WINDOW, PAST_LEN)
    lam_im = (jnp.pi * jnp.arange(S5_STATE, dtype=F32))[None, None, :] + nrm(ks[7], (DEPTH, S5_GROUPS, S5_STATE), 0.01)
    return {
        'x_prompt': nrm(ks[0], (BATCH, SEQ, D_MODEL), 1.0),
        'x_sample': nrm(ks[1], (DEC_BATCH, DEC_SEQ, D_MODEL), 1.0),
        'state_s5_re': nrm(ks[2], (DEPTH, DEC_BATCH, S5_GROUPS, S5_STATE), 0.5),
        'state_s5_im': nrm(ks[3], (DEPTH, DEC_BATCH, S5_GROUPS, S5_STATE), 0.5),
        'state_gla': nrm(ks[4], (DEPTH, DEC_BATCH, GLA_HEADS, GLA_DK, GLA_DV), 0.5),
        'cache_swa_k': nrm(ks[5], (DEPTH, DEC_BATCH, w_cache, SWA_KV_HEADS, SWA_HD), 1.0),
        'cache_swa_v': nrm(ks[6], (DEPTH, DEC_BATCH, w_cache, SWA_KV_HEADS, SWA_HD), 1.0),
        'g_mix': 1.0 + nrm(ks[8], (DEPTH, D_MODEL), 0.02),
        'w_in': nrm(ks[9], (DEPTH, D_MODEL, N_IN), D_MODEL ** -0.5),
        's5_lambda_re': -0.5 + nrm(ks[10], (DEPTH, S5_GROUPS, S5_STATE), 0.01),
        's5_lambda_im': lam_im,
        's5_log_dt': jax.random.uniform(ks[11], (DEPTH, S5_GROUPS), F32, math.log(1e-3), math.log(1e-1)),
        's5_b_re': nrm(ks[12], (DEPTH, S5_GROUPS, S5_GROUP, S5_STATE), (2 * S5_GROUP) ** -0.5),
        's5_b_im': nrm(ks[13], (DEPTH, S5_GROUPS, S5_GROUP, S5_STATE), (2 * S5_GROUP) ** -0.5),
        's5_c_re': nrm(ks[14], (DEPTH, S5_GROUPS, S5_STATE, S5_GROUP), S5_STATE ** -0.5),
        's5_c_im': nrm(ks[15], (DEPTH, S5_GROUPS, S5_STATE, S5_GROUP), S5_STATE ** -0.5),
        's5_d': nrm(ks[16], (DEPTH, S5_WIDTH), 1.0),
        'w_glu': nrm(ks[17], (DEPTH, S5_WIDTH, S5_WIDTH), S5_WIDTH ** -0.5),
        'b_glu': nrm(ks[18], (DEPTH, S5_WIDTH), 0.01),
        'w_gla_a2': nrm(ks[19], (DEPTH, GLA_RANK, GLA_QK), GLA_RANK ** -0.5),
        'b_gla_a': nrm(ks[20], (DEPTH, GLA_QK), 0.1),
        'g_gla_norm': 1.0 + nrm(ks[21], (DEPTH, GLA_DV), 0.02),
        'swa_sinks': nrm(ks[22], (DEPTH, SWA_HEADS), 1.0),
        'w_branch': nrm(ks[23], (DEPTH, MIX_WIDTH, D_MODEL), (MIX_WIDTH // N_BRANCH) ** -0.5),
        'w_out': nrm(ks[24], (DEPTH, D_MODEL, D_MODEL), D_MODEL ** -0.5),
        'g_ffn': 1.0 + nrm(ks[25], (DEPTH, D_MODEL), 0.02),
        'w_ffn_gate': nrm(ks[26], (DEPTH, D_MODEL, FFN_HIDDEN), D_MODEL ** -0.5),
        'w_ffn_up': nrm(ks[27], (DEPTH, D_MODEL, FFN_HIDDEN), D_MODEL ** -0.5),
        'w_ffn_down': nrm(ks[28], (DEPTH, FFN_HIDDEN, D_MODEL), FFN_HIDDEN ** -0.5),
        'g_final': 1.0 + nrm(ks[29], (D_MODEL,), 0.02),
    }


def reference(x_prompt, x_sample, state_s5_re, state_s5_im, state_gla, cache_swa_k, cache_swa_v,
              g_mix, w_in, s5_lambda_re, s5_lambda_im, s5_log_dt, s5_b_re, s5_b_im, s5_c_re, s5_c_im,
              s5_d, w_glu, b_glu, w_gla_a2, b_gla_a, g_gla_norm, swa_sinks, w_branch, w_out,
              g_ffn, w_ffn_gate, w_ffn_up, w_ffn_down, g_final):
    bp = x_prompt.shape[0]
    pos_p = jnp.arange(x_prompt.shape[1], dtype=jnp.int32)
    pos_s = PAST_LEN + jnp.arange(x_sample.shape[1], dtype=jnp.int32)
    hp, hs = x_prompt, x_sample
    p_re, p_im, p_gla, p_k, p_v = [], [], [], [], []
    s_re, s_im, s_gla, s_k, s_v = [], [], [], [], []
    for l in range(DEPTH):
        lp = dict(w_in=w_in[l], lam_re=s5_lambda_re[l], lam_im=s5_lambda_im[l], log_dt=s5_log_dt[l],
                  b_re=s5_b_re[l], b_im=s5_b_im[l], c_re=s5_c_re[l], c_im=s5_c_im[l], d_skip=s5_d[l],
                  w_glu=w_glu[l], b_glu=b_glu[l], w_gla_a2=w_gla_a2[l], b_gla_a=b_gla_a[l],
                  g_gla_norm=g_gla_norm[l], sinks=swa_sinks[l], w_branch=w_branch[l], w_out=w_out[l])
        z_s5 = jnp.zeros((bp, S5_GROUPS, S5_STATE), F32)
        z_gla = jnp.zeros((bp, GLA_HEADS, GLA_DK, GLA_DV), F32)
        mp, hr, hi, sg, nk, nv = mixer_block(rms_norm(hp, g_mix[l]), pos_p, z_s5, z_s5, z_gla, None, None, lp)
        p_re.append(hr); p_im.append(hi); p_gla.append(sg); p_k.append(nk); p_v.append(nv)
        ms, hr, hi, sg, nk, nv = mixer_block(rms_norm(hs, g_mix[l]), pos_s, state_s5_re[l], state_s5_im[l],
                                             state_gla[l], cache_swa_k[l], cache_swa_v[l], lp)
        s_re.append(hr); s_im.append(hi); s_gla.append(sg); s_k.append(nk); s_v.append(nv)
        hp = hp + mp
        hs = hs + ms
        hp = hp + swiglu(rms_norm(hp, g_ffn[l]), w_ffn_gate[l], w_ffn_up[l], w_ffn_down[l])
        hs = hs + swiglu(rms_norm(hs, g_ffn[l]), w_ffn_gate[l], w_ffn_up[l], w_ffn_down[l])
    y_prompt = rms_norm(hp, g_final)
    y_sample = rms_norm(hs, g_final)
    return (y_prompt, y_sample,
            jnp.stack(p_re), jnp.stack(p_im), jnp.stack(p_gla), jnp.stack(p_k), jnp.stack(p_v),
            jnp.stack(s_re), jnp.stack(s_im), jnp.stack(s_gla), jnp.stack(s_k), jnp.stack(s_v))
```

```python
import functools

import jax
import jax.numpy as jnp
from jax import lax
from jax.experimental import pallas as pl
from jax.experimental.pallas import tpu as pltpu

F32 = jnp.float32
BF16 = jnp.bfloat16

EPS = 1e-6
GLA_TAU = 16.0
GLA_CHUNK = 64
SWA_KV_HEADS = 2
SWA_HD = 64
WINDOW = 128
ROPE_THETA = 10000.0
PAST_LEN = 16384

SUBLANES = 8
LANES = 128
VMEM_LIMIT = 56 * 1024 * 1024

NT_DIMS = (((1,), (1,)), ((), ()))
TN_DIMS = (((0,), (0,)), ((), ()))


def _params(*sem):
    return pltpu.CompilerParams(dimension_semantics=sem, vmem_limit_bytes=VMEM_LIMIT)


def _dot(a, b):
    return jnp.dot(a, b, preferred_element_type=F32)


def _rms(x, g):
    return x * lax.rsqrt(jnp.mean(x * x, axis=-1, keepdims=True) + EPS) * g


def _full(shape):
    nd = len(shape)
    return pl.BlockSpec(shape, lambda *_: (0,) * nd)


def _inproj_kernel(x_ref, g_ref, w_ref, *o_refs):
    xb = _rms(x_ref[...], g_ref[...]).astype(BF16)
    off = 0
    for o_ref in o_refs:
        n = o_ref.shape[-1]
        for c0 in range(0, n, 512):
            c1 = min(c0 + 512, n)
            o_ref[:, c0:c1] = _dot(xb, w_ref[:, off + c0:off + c1])
        off += n


def _inproj(x, g, w, widths, tm):
    n, d = x.shape
    return pl.pallas_call(
        _inproj_kernel,
        grid=(n // tm,),
        in_specs=[pl.BlockSpec((tm, d), lambda i: (i, 0)), _full(g.shape), _full(w.shape)],
        out_specs=[pl.BlockSpec((tm, wd), lambda i: (i, 0)) for wd in widths],
        out_shape=[jax.ShapeDtypeStruct((n, wd), F32) for wd in widths],
        compiler_params=_params("parallel"),
        name="inproj",
    )(x, g, w)


def _s5_prep_kernel(lr_ref, li_ref, ldt_ref, br_ref, bi_ref, ar_ref, ai_ref, bbr_ref, bbi_ref):
    lr, li = lr_ref[...], li_ref[...]
    dt = jnp.exp(ldt_ref[...])
    mag = jnp.exp(lr * dt)
    ar = mag * jnp.cos(li * dt)
    ai = mag * jnp.sin(li * dt)
    den = lr * lr + li * li
    zr, zi = ar - 1.0, ai
    er = (zr * lr + zi * li) / den
    ei = (zi * lr - zr * li) / den
    ar_ref[...] = ar
    ai_ref[...] = ai
    br, bi = br_ref[...], bi_ref[...]
    bbr_ref[...] = er[None] * br - ei[None] * bi
    bbi_ref[...] = er[None] * bi + ei[None] * br


def _s5_prep(lam_re, lam_im, log_dt, b_re, b_im):
    g, c, n = b_re.shape
    brt, bit = jnp.transpose(b_re, (1, 0, 2)), jnp.transpose(b_im, (1, 0, 2))
    ar, ai, bbr, bbi = pl.pallas_call(
        _s5_prep_kernel,
        out_shape=[jax.ShapeDtypeStruct((g, n), F32)] * 2 + [jax.ShapeDtypeStruct((c, g, n), F32)] * 2,
        name="s5_prep",
    )(lam_re, lam_im, log_dt.reshape(g, 1), brt, bit)
    return ar, ai, jnp.transpose(bbr, (1, 0, 2)), jnp.transpose(bbi, (1, 0, 2))


def _s5_pack(bbr, bbi, c_re, c_im):
    g, c, n = bbr.shape
    gp = LANES // c
    ks = g // gp
    eye = jnp.eye(gp, dtype=F32)

    def pack_b(b):
        return jnp.einsum("kgcn,gh->kgchn", b.reshape(ks, gp, c, n), eye).reshape(ks, gp * c, gp * n).astype(BF16)

    def pack_c(cm):
        return jnp.einsum("kgnc,gh->kgnhc", cm.reshape(ks, gp, n, c), eye).reshape(ks, gp * n, gp * c).astype(BF16)

    return pack_b(bbr), pack_b(bbi), pack_c(c_re), pack_c(c_im)


S5_COL_BLOCK = 512


def _s5_kernel(u_ref, h0r_ref, h0i_ref, ar_ref, ai_ref, bre_ref, bim_ref, cre_ref, cim_ref, d_ref, wglu_ref,
               bglu_ref, y_ref, hr_out, hi_out, xr_s, xi_s, hr_s, hi_s, *, nb, steps):
    c = pl.program_id(0)

    @pl.when(c == 0)
    def _():
        hr_s[...] = h0r_ref[...]
        hi_s[...] = h0i_ref[...]

    u = u_ref[...]
    ub = u.astype(BF16)
    ks, kw, nw = bre_ref.shape
    for k in range(ks):
        uk = ub[:, k * kw:(k + 1) * kw]
        xr_s[:, k * nw:(k + 1) * nw] = _dot(uk, bre_ref[k])
        xi_s[:, k * nw:(k + 1) * nw] = _dot(uk, bim_ref[k])

    ncols = xr_s.shape[1]
    for rb in range(nb // SUBLANES):
        rows = slice(rb * SUBLANES, (rb + 1) * SUBLANES)
        for cb in range(ncols // S5_COL_BLOCK):
            cs = slice(cb * S5_COL_BLOCK, (cb + 1) * S5_COL_BLOCK)
            a_r = jnp.broadcast_to(ar_ref[:, cs], (SUBLANES, S5_COL_BLOCK))
            a_i = jnp.broadcast_to(ai_ref[:, cs], (SUBLANES, S5_COL_BLOCK))

            def step(t, carry, rb=rb, cs=cs, a_r=a_r, a_i=a_i):
                hr, hi = carry
                r0 = pl.multiple_of(t * nb + rb * SUBLANES, SUBLANES)
                nhr = a_r * hr - a_i * hi + xr_s[pl.ds(r0, SUBLANES), cs]
                nhi = a_r * hi + a_i * hr + xi_s[pl.ds(r0, SUBLANES), cs]
                xr_s[pl.ds(r0, SUBLANES), cs] = nhr
                xi_s[pl.ds(r0, SUBLANES), cs] = nhi
                return nhr, nhi

            hr, hi = lax.fori_loop(0, steps, step, (hr_s[rows, cs], hi_s[rows, cs]), unroll=min(steps, 8))
            hr_s[rows, cs] = hr
            hi_s[rows, cs] = hi

    for k in range(ks):
        ss = slice(k * nw, (k + 1) * nw)
        us = slice(k * kw, (k + 1) * kw)
        yk = _dot(xr_s[:, ss].astype(BF16), cre_ref[k]) - _dot(xi_s[:, ss].astype(BF16), cim_ref[k])
        y_ref[:, us] = jax.nn.gelu(yk + d_ref[:, us] * u[:, us])
    y = y_ref[...]
    y_ref[...] = y * jax.nn.sigmoid(_dot(y.astype(BF16), wglu_ref[...]) + bglu_ref[...])

    @pl.when(c == pl.num_programs(0) - 1)
    def _():
        hr_out[...] = hr_s[...]
        hi_out[...] = hi_s[...]


def _s5(u_t, h0r, h0i, ar, ai, bre, bim, cre, cim, d, wglu, bglu, steps):
    rows, w = u_t.shape
    nb, ns = h0r.shape
    blk = steps * nb
    kern = functools.partial(_s5_kernel, nb=nb, steps=steps)
    return pl.pallas_call(
        kern,
        grid=(rows // blk,),
        in_specs=[pl.BlockSpec((blk, w), lambda i: (i, 0)), _full(h0r.shape), _full(h0i.shape), _full(ar.shape),
                  _full(ai.shape), _full(bre.shape), _full(bim.shape), _full(cre.shape), _full(cim.shape),
                  _full(d.shape), _full(wglu.shape), _full(bglu.shape)],
        out_specs=[pl.BlockSpec((blk, w), lambda i: (i, 0)), _full((nb, ns)), _full((nb, ns))],
        out_shape=[jax.ShapeDtypeStruct((rows, w), F32), jax.ShapeDtypeStruct((nb, ns), F32),
                   jax.ShapeDtypeStruct((nb, ns), F32)],
        scratch_shapes=[pltpu.VMEM((blk, ns), F32), pltpu.VMEM((blk, ns), F32), pltpu.VMEM((nb, ns), F32),
                        pltpu.VMEM((nb, ns), F32)],
        compiler_params=_params("arbitrary"),
        name="s5_scan",
    )(u_t, h0r, h0i, ar, ai, bre, bim, cre, cim, d, wglu, bglu)


def _log2(n):
    assert n & (n - 1) == 0, n
    return n.bit_length() - 1


def _chunk_masks(rows, chunk):
    sh = _log2(chunk)
    rr = lax.broadcasted_iota(jnp.int32, (rows, rows), 0)
    cc = lax.broadcasted_iota(jnp.int32, (rows, rows), 1)
    same = lax.shift_right_logical(rr, sh) == lax.shift_right_logical(cc, sh)
    return same, same & (cc <= rr)


def _gla_prologue(p_ref, wa_ref, ba_ref, chunk, qk):
    rows = p_ref.shape[0]
    q = p_ref[:, 0:qk]
    k = p_ref[:, qk:2 * qk]
    alow = p_ref[:, p_ref.shape[1] - LANES:].astype(BF16)
    z = _dot(alow, wa_ref[...]) + ba_ref[...]
    la = (jnp.minimum(z, 0.0) - jnp.log(1.0 + jnp.exp(-jnp.abs(z)))) * (1.0 / GLA_TAU)
    same, causal = _chunk_masks(rows, chunk)
    tri = jnp.where(causal, 1.0, 0.0).astype(BF16)
    blk = jnp.where(same, 1.0, 0.0).astype(BF16)
    la_hi = la.astype(BF16)
    la_lo = (la - la_hi.astype(F32)).astype(BF16)
    b = _dot(tri, la_hi) + _dot(tri, la_lo)
    bl = _dot(blk, la_hi) + _dot(blk, la_lo)
    return q, k, b, bl


def _gla_out(o, gn, r):
    on = o * lax.rsqrt(jnp.mean(o * o, axis=-1, keepdims=True) + EPS) * gn
    return on * (r * jax.nn.sigmoid(r))


def _gla_prompt_kernel(p_ref, wa_ref, ba_ref, gn_ref, y_ref, s_ref, st_s, *, heads, dk, dv, chunk):
    c = pl.program_id(1)

    @pl.when(c == 0)
    def _():
        st_s[...] = jnp.zeros_like(st_s)

    rows = p_ref.shape[0]
    qk = heads * dk
    q, k, b, bl = _gla_prologue(p_ref, wa_ref, ba_ref, chunk, qk)
    _, cmask = _chunk_masks(chunk, chunk)
    v_off, r_off = 2 * qk, 2 * qk + heads * dv
    qd = (q * (dk ** -0.5) * jnp.exp(b)).astype(BF16)
    kd = (k * jnp.exp(-b)).astype(BF16)
    kst = (k * jnp.exp(bl - b)).astype(BF16)
    dec = jnp.exp(bl)
    gn = gn_ref[...]
    for ci in range(rows // chunk):
        rs = slice(ci * chunk, (ci + 1) * chunk)
        for h in range(heads):
            hs = slice(h * dk, (h + 1) * dk)
            vh = p_ref[rs, v_off + h * dv:v_off + (h + 1) * dv].astype(BF16)
            att = lax.dot_general(qd[rs, hs], kd[rs, hs], NT_DIMS, preferred_element_type=F32)
            att = jnp.where(cmask, att, 0.0).astype(BF16)
            st = st_s[h]
            o = _dot(att, vh) + lax.dot_general(qd[rs, hs], st.astype(BF16), NT_DIMS, preferred_element_type=F32)
            ds_t = lax.dot_general(vh, kst[rs, hs], TN_DIMS, preferred_element_type=F32)
            st_s[h] = dec[ci * chunk:ci * chunk + 1, hs] * st + ds_t
            r = p_ref[rs, r_off + h * dv:r_off + (h + 1) * dv]
            y_ref[rs, h * dv:(h + 1) * dv] = _gla_out(o, gn, r)

    @pl.when(c == pl.num_programs(1) - 1)
    def _():
        for h in range(heads):
            s_ref[0, h] = st_s[h].T


def _gla_prompt(p, wa, ba, gn, bsz, heads, dk, dv, rows):
    n, pw = p.shape
    t = n // bsz
    nblk = t // rows
    kern = functools.partial(_gla_prompt_kernel, heads=heads, dk=dk, dv=dv, chunk=GLA_CHUNK)
    return pl.pallas_call(
        kern,
        grid=(bsz, nblk),
        in_specs=[pl.BlockSpec((rows, pw), lambda b, c: (b * nblk + c, 0)), _full(wa.shape), _full(ba.shape),
                  _full(gn.shape)],
        out_specs=[pl.BlockSpec((rows, heads * dv), lambda b, c: (b * nblk + c, 0)),
                   pl.BlockSpec((1, heads, dk, dv), lambda b, c: (b, 0, 0, 0))],
        out_shape=[jax.ShapeDtypeStruct((n, heads * dv), F32), jax.ShapeDtypeStruct((bsz, heads, dk, dv), F32)],
        scratch_shapes=[pltpu.VMEM((heads, dv, dk), F32)],
        compiler_params=_params("parallel", "arbitrary"),
        name="gla_prompt",
    )(p, wa, ba, gn)


def _gla_sample_kernel(p_ref, s0_ref, wa_ref, ba_ref, gn_ref, y_ref, s_ref, *, heads, dk, dv, t):
    rows = p_ref.shape[0]
    qk = heads * dk
    q, k, b, bl = _gla_prologue(p_ref, wa_ref, ba_ref, t, qk)
    _, cmask = _chunk_masks(SUBLANES, t)
    v_off, r_off = 2 * qk, 2 * qk + heads * dv
    qd = (q * (dk ** -0.5) * jnp.exp(b)).astype(BF16)
    kd = (k * jnp.exp(-b)).astype(BF16)
    kst = k * jnp.exp(bl - b)
    dec = jnp.exp(bl)
    gn = gn_ref[...]
    per_tile = SUBLANES // t
    row_id = lax.broadcasted_iota(jnp.int32, (SUBLANES, 1), 0)
    for ti in range(rows // SUBLANES):
        rs = slice(ti * SUBLANES, (ti + 1) * SUBLANES)
        dcols = []
        for j in range(per_tile):
            drow = dec[ti * SUBLANES + j * t:ti * SUBLANES + j * t + 1, :]
            dcols.append([jnp.broadcast_to(drow[:, g * LANES:(g + 1) * LANES], (LANES, LANES)).T
                          for g in range(qk // LANES)])
        for h in range(heads):
            hs = slice(h * dk, (h + 1) * dk)
            vh = p_ref[rs, v_off + h * dv:v_off + (h + 1) * dv].astype(BF16)
            att = lax.dot_general(qd[rs, hs], kd[rs, hs], NT_DIMS, preferred_element_type=F32)
            att = jnp.where(cmask, att, 0.0).astype(BF16)
            o = _dot(att, vh)
            for j in range(per_tile):
                bidx = ti * per_tile + j
                in_seq = (row_id >= j * t) & (row_id < (j + 1) * t)
                s0 = s0_ref[bidx, h]
                oj = _dot(qd[rs, hs], s0.astype(BF16))
                o = o + jnp.where(in_seq, oj, 0.0)
                kj = jnp.where(in_seq, kst[rs, hs], 0.0).astype(BF16)
                ds = lax.dot_general(kj, vh, TN_DIMS, preferred_element_type=F32)
                g, lo = (h * dk) // LANES, (h * dk) % LANES
                dcol = dcols[j][g][lo:lo + dk, 0:dv]
                s_ref[bidx, h] = dcol * s0 + ds
            r = p_ref[rs, r_off + h * dv:r_off + (h + 1) * dv]
            y_ref[rs, h * dv:(h + 1) * dv] = _gla_out(o, gn, r)


def _gla_sample(p, s0, wa, ba, gn, t, nb):
    n, pw = p.shape
    bsz, heads, dk, dv = s0.shape
    rows = nb * t
    kern = functools.partial(_gla_sample_kernel, heads=heads, dk=dk, dv=dv, t=t)
    return pl.pallas_call(
        kern,
        grid=(bsz // nb,),
        in_specs=[pl.BlockSpec((rows, pw), lambda i: (i, 0)),
                  pl.BlockSpec((nb, heads, dk, dv), lambda i: (i, 0, 0, 0)),
                  _full(wa.shape), _full(ba.shape), _full(gn.shape)],
        out_specs=[pl.BlockSpec((rows, heads * dv), lambda i: (i, 0)),
                   pl.BlockSpec((nb, heads, dk, dv), lambda i: (i, 0, 0, 0))],
        out_shape=[jax.ShapeDtypeStruct((n, heads * dv), F32), jax.ShapeDtypeStruct(s0.shape, F32)],
        compiler_params=_params("parallel"),
        name="gla_sample",
    )(p, s0, wa, ba, gn)


def _rope_tables(pos_f, inv_ref, width):
    ang = pos_f * inv_ref[:, 0:width]
    first_half = _first_half(ang.shape)
    return jnp.cos(ang), jnp.where(first_half, -jnp.sin(ang), jnp.sin(ang)), first_half


def _first_half(shape):
    lane = lax.broadcasted_iota(jnp.int32, shape, 1)
    return (lane & (SWA_HD - 1)) < SWA_HD // 2


def _rope(x, cos, sin, first_half):
    w = x.shape[1]
    half = SWA_HD // 2
    swapped = jnp.where(first_half, pltpu.roll(x, w - half, 1), pltpu.roll(x, half, 1))
    return x * cos + swapped * sin


def _swa_prompt_kernel(sink_ref, cur_ref, prev_ref, inv_ref, y_ref, nk_ref, nv_ref, *, heads):
    i = pl.program_id(1)
    w = WINDOW
    qw = heads * SWA_HD
    kvw = SWA_KV_HEADS * SWA_HD
    gq = heads // SWA_KV_HEADS
    pos = (i * w + lax.broadcasted_iota(jnp.int32, (w, 1), 0)).astype(F32)
    cos_q, sin_q, fh_q = _rope_tables(pos, inv_ref, qw)
    q = _rope(cur_ref[:, 0:qw], cos_q, sin_q, fh_q) * (SWA_HD ** -0.5)
    cos_k, sin_k, fh_k = cos_q[:, 0:kvw], sin_q[:, 0:kvw], _first_half((w, kvw))
    k_cur = _rope(cur_ref[:, qw:qw + kvw], cos_k, sin_k, fh_k)
    cos_p, sin_p, _ = _rope_tables(pos - float(w), inv_ref, kvw)
    k_prev = _rope(prev_ref[:, 0:kvw], cos_p, sin_p, fh_k)
    v_cur = cur_ref[:, qw + kvw:qw + 2 * kvw]
    v_prev = prev_ref[:, kvw:2 * kvw]
    kcat = jnp.concatenate([k_prev, k_cur], axis=0).astype(BF16)
    vcat = jnp.concatenate([v_prev, v_cur], axis=0).astype(BF16)
    qb = q.astype(BF16)
    rr = lax.broadcasted_iota(jnp.int32, (w, 2 * w), 0)
    jj = lax.broadcasted_iota(jnp.int32, (w, 2 * w), 1)
    first_key = jnp.where(i > 0, 0, w)
    mask = (jj <= rr + w) & (jj > rr) & (jj >= first_key)
    for h in range(heads):
        kv = h // gq
        ks = slice(kv * SWA_HD, (kv + 1) * SWA_HD)
        s = lax.dot_general(qb[:, h * SWA_HD:(h + 1) * SWA_HD], kcat[:, ks], NT_DIMS, preferred_element_type=F32)
        s = jnp.where(mask, s, -jnp.inf)
        sink = sink_ref[h]
        m = jnp.maximum(jnp.max(s, axis=-1, keepdims=True), sink)
        p = jnp.exp(s - m)
        den = jnp.sum(p, axis=-1, keepdims=True) + jnp.exp(sink - m)
        pn = (p * (1.0 / den)).astype(BF16)
        y_ref[:, h * SWA_HD:(h + 1) * SWA_HD] = _dot(pn, vcat[:, ks])

    @pl.when(i == pl.num_programs(1) - 1)
    def _():
        nk_ref[0] = k_cur
        nv_ref[0] = v_cur


def _swa_prompt(p, sinks, inv, bsz, heads):
    n, pw = p.shape
    t = n // bsz
    nblk = t // WINDOW
    qw, kvw = heads * SWA_HD, SWA_KV_HEADS * SWA_HD
    kv_blk = qw // (2 * kvw)
    kern = functools.partial(_swa_prompt_kernel, heads=heads)
    return pl.pallas_call(
        kern,
        grid=(bsz, nblk),
        in_specs=[pl.BlockSpec(memory_space=pltpu.SMEM),
                  pl.BlockSpec((WINDOW, pw), lambda b, i: (b * nblk + i, 0)),
                  pl.BlockSpec((WINDOW, 2 * kvw), lambda b, i: (b * nblk + jnp.maximum(i - 1, 0), kv_blk)),
                  _full(inv.shape)],
        out_specs=[pl.BlockSpec((WINDOW, qw), lambda b, i: (b * nblk + i, 0)),
                   pl.BlockSpec((1, WINDOW, kvw), lambda b, i: (b, 0, 0)),
                   pl.BlockSpec((1, WINDOW, kvw), lambda b, i: (b, 0, 0))],
        out_shape=[jax.ShapeDtypeStruct((n, qw), F32), jax.ShapeDtypeStruct((bsz, WINDOW, kvw), F32),
                   jax.ShapeDtypeStruct((bsz, WINDOW, kvw), F32)],
        compiler_params=_params("parallel", "arbitrary"),
        name="swa_prompt",
    )(sinks, p, p, inv)


def _expand_kv(x, lane):
    other = pltpu.roll(x, SWA_HD, 1)
    first = lane < SWA_HD
    kv0 = jnp.where(first, x, other)
    kv1 = jnp.where(first, other, x)
    return kv0, kv1


def _swa_sample_kernel(sinkrow_ref, p_ref, ck_ref, cv_ref, inv_ref, y_ref, nk_ref, nv_ref, *, heads, t, nb):
    w = ck_ref.shape[1]
    qw = heads * SWA_HD
    kvw = SWA_KV_HEADS * SWA_HD
    gq = heads // SWA_KV_HEADS
    per_tile = SUBLANES // t
    ncol = heads * t
    row8 = lax.broadcasted_iota(jnp.int32, (SUBLANES, 1), 0)
    pos8 = (PAST_LEN + (row8 & (t - 1))).astype(F32)
    cos_q, sin_q, fh_q = _rope_tables(pos8, inv_ref, qw)
    cos_k, sin_k, fh_k = cos_q[:, 0:kvw], sin_q[:, 0:kvw], _first_half((SUBLANES, kvw))
    lane_kv = lax.broadcasted_iota(jnp.int32, (1, kvw), 1)
    qm_row = lax.broadcasted_iota(jnp.int32, (ncol, qw), 0)
    qm_lane = lax.broadcasted_iota(jnp.int32, (ncol, qw), 1)
    qmask = lax.shift_right_logical(qm_row, _log2(t)) == lax.shift_right_logical(qm_lane, _log2(SWA_HD))
    ck_row = lax.broadcasted_iota(jnp.int32, (w, ncol), 0)
    col_step = lax.broadcasted_iota(jnp.int32, (w, ncol), 1) & (t - 1)
    cache_ok = ck_row > col_step
    nk_row = lax.broadcasted_iota(jnp.int32, (SUBLANES, ncol), 0)
    nk_col_step = lax.broadcasted_iota(jnp.int32, (SUBLANES, ncol), 1) & (t - 1)
    sink_row = sinkrow_ref[...]
    for ti in range(nb // per_tile):
        rs = slice(ti * SUBLANES, (ti + 1) * SUBLANES)
        q8 = _rope(p_ref[rs, 0:qw], cos_q, sin_q, fh_q) * (SWA_HD ** -0.5)
        kn8 = _rope(p_ref[rs, qw:qw + kvw], cos_k, sin_k, fh_k)
        vn8 = p_ref[rs, qw + kvw:qw + 2 * kvw]
        kn0, kn1 = _expand_kv(kn8, lane_kv)
        vn0, vn1 = _expand_kv(vn8, lane_kv)
        kn_x = jnp.concatenate([kn0] * (gq // 2) + [kn1] * (gq // 2), axis=1).astype(BF16)
        vn_x = jnp.concatenate([vn0] * (gq // 2) + [vn1] * (gq // 2), axis=1).astype(BF16)
        for j in range(per_tile):
            bidx = ti * per_tile + j
            qj = q8[j * t:(j + 1) * t, :]
            qm = jnp.where(qmask, jnp.concatenate([qj] * heads, axis=0), 0.0).astype(BF16)
            ck = ck_ref[bidx]
            cv = cv_ref[bidx]
            ck0, ck1 = _expand_kv(ck, lane_kv)
            cv0, cv1 = _expand_kv(cv, lane_kv)
            ck_x = jnp.concatenate([ck0] * (gq // 2) + [ck1] * (gq // 2), axis=1).astype(BF16)
            cv_x = jnp.concatenate([cv0] * (gq // 2) + [cv1] * (gq // 2), axis=1).astype(BF16)
            s_c = lax.dot_general(ck_x, qm, NT_DIMS, preferred_element_type=F32)
            s_n = lax.dot_general(kn_x, qm, NT_DIMS, preferred_element_type=F32)
            s_c = jnp.where(cache_ok, s_c, -jnp.inf)
            new_ok = (nk_row >= j * t) & (nk_row - j * t <= nk_col_step) & (nk_row < (j + 1) * t)
            s_n = jnp.where(new_ok, s_n, -jnp.inf)
            m = jnp.maximum(jnp.maximum(jnp.max(s_c, axis=0, keepdims=True), jnp.max(s_n, axis=0, keepdims=True)),
                            sink_row)
            p_c = jnp.exp(s_c - m)
            p_n = jnp.exp(s_n - m)
            den = jnp.sum(p_c, axis=0, keepdims=True) + jnp.sum(p_n, axis=0, keepdims=True) + jnp.exp(sink_row - m)
            rden = 1.0 / den
            o = (lax.dot_general((p_c * rden).astype(BF16), cv_x, TN_DIMS, preferred_element_type=F32)
                 + lax.dot_general((p_n * rden).astype(BF16), vn_x, TN_DIMS, preferred_element_type=F32))
            o = jnp.where(qmask, o, 0.0)
            acc = o[0:SUBLANES]
            for g in range(1, ncol // SUBLANES):
                acc = acc + o[g * SUBLANES:(g + 1) * SUBLANES]
            for g in range(1, per_tile):
                acc = acc + pltpu.roll(acc, SUBLANES - g * t, 0)
            y_ref[ti * SUBLANES + j * t:ti * SUBLANES + (j + 1) * t, :] = acc[0:t]
            kn_tail = pltpu.roll(kn8, (SUBLANES - t - j * t) % SUBLANES, 0)
            vn_tail = pltpu.roll(vn8, (SUBLANES - t - j * t) % SUBLANES, 0)
            k_shift = pltpu.roll(ck, w - t, 0)
            v_shift = pltpu.roll(cv, w - t, 0)
            body = w - SUBLANES
            nk_ref[bidx, 0:body] = k_shift[0:body]
            nv_ref[bidx, 0:body] = v_shift[0:body]
            nk_ref[bidx, body:w] = jnp.where(row8 < SUBLANES - t, k_shift[body:w], kn_tail)
            nv_ref[bidx, body:w] = jnp.where(row8 < SUBLANES - t, v_shift[body:w], vn_tail)


def _swa_sample(p, ck, cv, sinks, inv, heads, t, nb):
    n, pw = p.shape
    bsz, w, kvw = ck.shape
    qw = heads * SWA_HD
    sink_row = jnp.repeat(sinks, t).reshape(1, heads * t)
    kern = functools.partial(_swa_sample_kernel, heads=heads, t=t, nb=nb)
    return pl.pallas_call(
        kern,
        grid=(bsz // nb,),
        in_specs=[_full(sink_row.shape), pl.BlockSpec((nb * t, pw), lambda i: (i, 0)),
                  pl.BlockSpec((nb, w, kvw), lambda i: (i, 0, 0)), pl.BlockSpec((nb, w, kvw), lambda i: (i, 0, 0)),
                  _full(inv.shape)],
        out_specs=[pl.BlockSpec((nb * t, qw), lambda i: (i, 0)),
                   pl.BlockSpec((nb, w, kvw), lambda i: (i, 0, 0)), pl.BlockSpec((nb, w, kvw), lambda i: (i, 0, 0))],
        out_shape=[jax.ShapeDtypeStruct((n, qw), F32), jax.ShapeDtypeStruct(ck.shape, F32),
                   jax.ShapeDtypeStruct(cv.shape, F32)],
        compiler_params=_params("parallel"),
        name="swa_sample",
    )(sink_row, p, ck, cv, inv)


def _merge_kernel(x_ref, ya_ref, yb_ref, yc_ref, g_ref, wb_ref, wo_ref, o_ref):
    d = x_ref.shape[1]
    merged = None
    off = 0
    for i, y_ref in enumerate((ya_ref, yb_ref, yc_ref)):
        wdt = y_ref.shape[1]
        br = _dot(y_ref[...].astype(BF16), wb_ref[off:off + wdt, :])
        term = jax.nn.sigmoid(g_ref[:, i * d:(i + 1) * d]) * br
        merged = term if merged is None else merged + term
        off += wdt
    o_ref[...] = x_ref[...] + _dot(merged.astype(BF16), wo_ref[...])


def _merge(x, ya, yb, yc, gates, wb, wo, tm):
    n, d = x.shape
    row = lambda a: pl.BlockSpec((tm, a.shape[1]), lambda i: (i, 0))
    return pl.pallas_call(
        _merge_kernel,
        grid=(n // tm,),
        in_specs=[row(x), row(ya), row(yb), row(yc), row(gates), _full(wb.shape), _full(wo.shape)],
        out_specs=pl.BlockSpec((tm, d), lambda i: (i, 0)),
        out_shape=jax.ShapeDtypeStruct((n, d), F32),
        compiler_params=_params("parallel"),
        name="merge",
    )(x, ya, yb, yc, gates, wb, wo)


FFN_HIDDEN_CHUNK = 1024


def _ffn_kernel(x_ref, g_ref, wg_ref, wu_ref, wd_ref, gf_ref, o_ref, *, final_norm):
    x = x_ref[...]
    hb = _rms(x, g_ref[...]).astype(BF16)
    hidden = wg_ref.shape[1]
    acc = x
    for c0 in range(0, hidden, FFN_HIDDEN_CHUNK):
        c1 = min(c0 + FFN_HIDDEN_CHUNK, hidden)
        gate = _dot(hb, wg_ref[:, c0:c1])
        up = _dot(hb, wu_ref[:, c0:c1])
        act = (gate * jax.nn.sigmoid(gate) * up).astype(BF16)
        acc = acc + _dot(act, wd_ref[c0:c1, :])
    o_ref[...] = _rms(acc, gf_ref[...]) if final_norm else acc


def _ffn(x, g, wg, wu, wd, gf, tm, final_norm):
    n, d = x.shape
    return pl.pallas_call(
        functools.partial(_ffn_kernel, final_norm=final_norm),
        grid=(n // tm,),
        in_specs=[pl.BlockSpec((tm, d), lambda i: (i, 0)), _full(g.shape), _full(wg.shape), _full(wu.shape),
                  _full(wd.shape), _full(gf.shape)],
        out_specs=pl.BlockSpec((tm, d), lambda i: (i, 0)),
        out_shape=jax.ShapeDtypeStruct((n, d), F32),
        compiler_params=_params("parallel"),
        name="ffn",
    )(x, g, wg, wu, wd, gf)


def _tile(n, want):
    while n % want:
        want //= 2
    return want


def kernel(x_prompt, x_sample, state_s5_re, state_s5_im, state_gla, cache_swa_k, cache_swa_v, g_mix, w_in, s5_lambda_re, s5_lambda_im, s5_log_dt, s5_b_re, s5_b_im, s5_c_re, s5_c_im, s5_d, w_glu, b_glu, w_gla_a2, b_gla_a, g_gla_norm, swa_sinks, w_branch, w_out, g_ffn, w_ffn_gate, w_ffn_up, w_ffn_down, g_final):
    bp, tp, d = x_prompt.shape
    bs, ts, _ = x_sample.shape
    depth = w_in.shape[0]
    s5_w = s5_d.shape[1]
    groups, s5_n = s5_lambda_re.shape[1:]
    gla_qk = w_gla_a2.shape[2]
    gla_rank = w_gla_a2.shape[1]
    gla_heads, gla_dk, gla_dv = state_gla.shape[2:]
    gla_v = gla_heads * gla_dv
    swa_heads = swa_sinks.shape[1]
    swa_q = swa_heads * SWA_HD
    swa_kv = SWA_KV_HEADS * SWA_HD
    assert SWA_KV_HEADS * SWA_HD == LANES and bp % SUBLANES == 0 and bs % SUBLANES == 0 and SUBLANES % ts == 0
    assert tp % WINDOW == 0 and cache_swa_k.shape[2] == WINDOW

    o_a, o_b = 0, s5_w
    o_low = o_b + 2 * gla_qk + 2 * gla_v
    o_c = o_low + gla_rank
    o_g = o_c + swa_q + 2 * swa_kv
    widths = (s5_w, 2 * gla_qk + 2 * gla_v + LANES, swa_q + 2 * swa_kv, 3 * d)

    hp = x_prompt.reshape(bp * tp, d)
    hs = x_sample.reshape(bs * ts, d)
    inv = (ROPE_THETA ** (-jnp.arange(SWA_HD // 2, dtype=F32) * (2.0 / SWA_HD)))
    inv = jnp.tile(inv, 2 * swa_heads).reshape(1, swa_q)
    zeros_s5 = jnp.zeros((bp, groups * s5_n), F32)
    tm_p, tm_s = _tile(bp * tp, 256), _tile(bs * ts, 256)
    s5_steps = _tile(tp, 128)
    gla_rows = _tile(tp, 256)

    outs_p = [[] for _ in range(5)]
    outs_s = [[] for _ in range(5)]
    for l in range(depth):
        wl = w_in[l]
        w_r = jnp.concatenate(
            [wl[:, o_a:o_b], wl[:, o_b:o_low], wl[:, o_low:o_c], jnp.zeros((d, LANES - gla_rank), F32),
             wl[:, o_c:o_g], wl[:, o_g:]], axis=1).astype(BF16)
        wa2 = jnp.concatenate([w_gla_a2[l], jnp.zeros((LANES - gla_rank, gla_qk), F32)], axis=0).astype(BF16)
        ba = b_gla_a[l].reshape(1, gla_qk)
        gn = g_gla_norm[l].reshape(1, gla_dv)
        ar, ai, bbr, bbi = _s5_prep(s5_lambda_re[l], s5_lambda_im[l], s5_log_dt[l], s5_b_re[l], s5_b_im[l])
        bre, bim, cre, cim = _s5_pack(bbr, bbi, s5_c_re[l], s5_c_im[l])
        ar, ai = ar.reshape(1, groups * s5_n), ai.reshape(1, groups * s5_n)
        s5_args = (ar, ai, bre, bim, cre, cim, s5_d[l].reshape(1, s5_w), w_glu[l].astype(BF16),
                   b_glu[l].reshape(1, s5_w))
        g_mix_l, g_ffn_l = g_mix[l].reshape(1, d), g_ffn[l].reshape(1, d)
        wb, wo = w_branch[l].astype(BF16), w_out[l].astype(BF16)
        wg, wu, wd = w_ffn_gate[l].astype(BF16), w_ffn_up[l].astype(BF16), w_ffn_down[l].astype(BF16)
        gf = g_final.reshape(1, d)
        last = l == depth - 1

        pa, pb, pc, pg = _inproj(hp, g_mix_l, w_r, widths, tm_p)
        u_t = pa.reshape(bp, tp, s5_w).transpose(1, 0, 2).reshape(tp * bp, s5_w)
        ya_t, hr, hi = _s5(u_t, zeros_s5, zeros_s5, *s5_args, steps=s5_steps)
        ya = ya_t.reshape(tp, bp, s5_w).transpose(1, 0, 2).reshape(bp * tp, s5_w)
        yb, sg = _gla_prompt(pb, wa2, ba, gn, bp, gla_heads, gla_dk, gla_dv, gla_rows)
        yc, nk, nv = _swa_prompt(pc, swa_sinks[l], inv, bp, swa_heads)
        hp = _merge(hp, ya, yb, yc, pg, wb, wo, tm_p)
        hp = _ffn(hp, g_ffn_l, wg, wu, wd, gf, tm_p, last)
        for lst, val in zip(outs_p, (hr.reshape(bp, groups, s5_n), hi.reshape(bp, groups, s5_n), sg,
                                     nk.reshape(bp, WINDOW, SWA_KV_HEADS, SWA_HD),
                                     nv.reshape(bp, WINDOW, SWA_KV_HEADS, SWA_HD))):
            lst.append(val)

        pa, pb, pc, pg = _inproj(hs, g_mix_l, w_r, widths, tm_s)
        u_t = pa.reshape(bs, ts, s5_w).transpose(1, 0, 2).reshape(ts * bs, s5_w)
        ya_t, hr, hi = _s5(u_t, state_s5_re[l].reshape(bs, groups * s5_n), state_s5_im[l].reshape(bs, groups * s5_n),
                           *s5_args, steps=ts)
        ya = ya_t.reshape(ts, bs, s5_w).transpose(1, 0, 2).reshape(bs * ts, s5_w)
        yb, sg = _gla_sample(pb, state_gla[l], wa2, ba, gn, ts, _tile(bs, 16))
        yc, nk, nv = _swa_sample(pc, cache_swa_k[l].reshape(bs, WINDOW, swa_kv), cache_swa_v[l].reshape(bs, WINDOW, swa_kv),
                                 swa_sinks[l], inv, swa_heads, ts, _tile(bs, 8))
        hs = _merge(hs, ya, yb, yc, pg, wb, wo, tm_s)
        hs = _ffn(hs, g_ffn_l, wg, wu, wd, gf, tm_s, last)
        for lst, val in zip(outs_s, (hr.reshape(bs, groups, s5_n), hi.reshape(bs, groups, s5_n), sg,
                                     nk.reshape(bs, WINDOW, SWA_KV_HEADS, SWA_HD),
                                     nv.reshape(bs, WINDOW, SWA_KV_HEADS, SWA_HD))):
            lst.append(val)

    return (hp.reshape(bp, tp, d), hs.reshape(bs, ts, d),
            *[jnp.stack(o) for o in outs_p], *[jnp.stack(o) for o in outs_s])
```

```python
import functools

import jax
import jax.numpy as jnp
from jax import lax
from jax.experimental import pallas as pl
from jax.experimental.pallas import tpu as pltpu

F32 = jnp.float32
BF16 = jnp.bfloat16

EPS = 1e-6
GLA_TAU = 16.0
GLA_CHUNK = 64
SWA_KV_HEADS = 2
SWA_HD = 64
WINDOW = 128
ROPE_THETA = 10000.0
PAST_LEN = 16384

SUBLANES = 8
LANES = 128
VMEM_LIMIT = 56 * 1024 * 1024

NT_DIMS = (((1,), (1,)), ((), ()))
TN_DIMS = (((0,), (0,)), ((), ()))


def _params(*sem):
    return pltpu.CompilerParams(dimension_semantics=sem, vmem_limit_bytes=VMEM_LIMIT)


def _dot(a, b):
    return jnp.dot(a, b, preferred_element_type=F32)


def _rms(x, g):
    return x * lax.rsqrt(jnp.mean(x * x, axis=-1, keepdims=True) + EPS) * g


def _full(shape):
    nd = len(shape)
    return pl.BlockSpec(shape, lambda *_: (0,) * nd)


def _layer(a, l):
    nd = a.ndim - 1
    return pl.BlockSpec((None,) + a.shape[1:], lambda *_: (l,) + (0,) * nd, pipeline_mode=pl.Buffered(1))


def _inproj_kernel(x_ref, g_ref, w_ref, *o_refs):
    xb = _rms(x_ref[...], g_ref[...]).astype(BF16)
    off = 0
    for o_ref in o_refs:
        n = o_ref.shape[-1]
        for c0 in range(0, n, 512):
            c1 = min(c0 + 512, n)
            o_ref[:, c0:c1] = _dot(xb, w_ref[:, off + c0:off + c1])
        off += n


def _inproj(x, g, w, l, widths, tm):
    n, d = x.shape
    return pl.pallas_call(
        _inproj_kernel,
        grid=(n // tm,),
        in_specs=[pl.BlockSpec((tm, d), lambda i: (i, 0)), _layer(g, l), _layer(w, l)],
        out_specs=[pl.BlockSpec((tm, wd), lambda i: (i, 0)) for wd in widths],
        out_shape=[jax.ShapeDtypeStruct((n, wd), F32) for wd in widths],
        compiler_params=_params("parallel"),
        name="inproj",
    )(x, g, w)


def _s5_prep_kernel(lr_ref, li_ref, ldt_ref, br_ref, bi_ref, ar_ref, ai_ref, bbr_ref, bbi_ref):
    lr, li = lr_ref[...], li_ref[...]
    dt = jnp.exp(ldt_ref[...])
    mag = jnp.exp(lr * dt)
    ar = mag * jnp.cos(li * dt)
    ai = mag * jnp.sin(li * dt)
    den = lr * lr + li * li
    zr, zi = ar - 1.0, ai
    er = (zr * lr + zi * li) / den
    ei = (zi * lr - zr * li) / den
    ar_ref[...] = ar
    ai_ref[...] = ai
    br, bi = br_ref[...], bi_ref[...]
    bbr_ref[...] = er[None] * br - ei[None] * bi
    bbi_ref[...] = er[None] * bi + ei[None] * br


def _s5_prep(lam_re, lam_im, log_dt, b_re, b_im):
    depth, g, c, n = b_re.shape
    dg = depth * g
    brt = jnp.transpose(b_re.reshape(dg, c, n), (1, 0, 2))
    bit = jnp.transpose(b_im.reshape(dg, c, n), (1, 0, 2))
    ar, ai, bbr, bbi = pl.pallas_call(
        _s5_prep_kernel,
        out_shape=[jax.ShapeDtypeStruct((dg, n), F32)] * 2 + [jax.ShapeDtypeStruct((c, dg, n), F32)] * 2,
        name="s5_prep",
    )(lam_re.reshape(dg, n), lam_im.reshape(dg, n), log_dt.reshape(dg, 1), brt, bit)
    unt = lambda a: jnp.transpose(a, (1, 0, 2)).reshape(depth, g, c, n)
    return ar.reshape(depth, 1, g * n), ai.reshape(depth, 1, g * n), unt(bbr), unt(bbi)


def _s5_pack(bbr, bbi, c_re, c_im):
    depth, g, c, n = bbr.shape
    gp = LANES // c
    ks = g // gp
    eye = jnp.eye(gp, dtype=F32)

    def pack_b(b):
        return jnp.einsum("lkgcn,gh->lkgchn", b.reshape(depth, ks, gp, c, n), eye).reshape(
            depth, ks, gp * c, gp * n).astype(BF16)

    def pack_c(cm):
        return jnp.einsum("lkgnc,gh->lkgnhc", cm.reshape(depth, ks, gp, n, c), eye).reshape(
            depth, ks, gp * n, gp * c).astype(BF16)

    return pack_b(bbr), pack_b(bbi), pack_c(c_re), pack_c(c_im)


S5_COL_BLOCK = 512


def _s5_kernel(u_ref, h0r_ref, h0i_ref, ar_ref, ai_ref, bre_ref, bim_ref, cre_ref, cim_ref, d_ref, wglu_ref,
               bglu_ref, y_ref, hr_out, hi_out, us_s, ys_s, xr_s, xi_s, hr_s, hi_s, *tmp_s, nb, steps):
    c = pl.program_id(0)

    @pl.when(c == 0)
    def _():
        hr_s[...] = h0r_ref[...]
        hi_s[...] = h0i_ref[...]

    ks, kw, nw = bre_ref.shape
    by_sequence = u_ref.ndim == 3
    if by_sequence:
        for b in range(nb):
            for k in range(ks):
                us_s[k, pl.ds(b, steps, stride=nb), :] = u_ref[b, :, k * kw:(k + 1) * kw]
    else:
        for k in range(ks):
            tmp_s[0][k] = u_ref[:, k * kw:(k + 1) * kw]
        for t in range(steps):
            for k in range(ks):
                us_s[k, t * nb:(t + 1) * nb, :] = tmp_s[0][k, pl.ds(t, nb, stride=steps), :]

    for k in range(ks):
        uk = us_s[k].astype(BF16)
        xr_s[:, k * nw:(k + 1) * nw] = _dot(uk, bre_ref[k])
        xi_s[:, k * nw:(k + 1) * nw] = _dot(uk, bim_ref[k])

    ncols = xr_s.shape[1]
    for rb in range(nb // SUBLANES):
        rows = slice(rb * SUBLANES, (rb + 1) * SUBLANES)
        for cb in range(ncols // S5_COL_BLOCK):
            cs = slice(cb * S5_COL_BLOCK, (cb + 1) * S5_COL_BLOCK)
            a_r = jnp.broadcast_to(ar_ref[:, cs], (SUBLANES, S5_COL_BLOCK))
            a_i = jnp.broadcast_to(ai_ref[:, cs], (SUBLANES, S5_COL_BLOCK))

            def step(t, carry, rb=rb, cs=cs, a_r=a_r, a_i=a_i):
                hr, hi = carry
                r0 = pl.multiple_of(t * nb + rb * SUBLANES, SUBLANES)
                nhr = a_r * hr - a_i * hi + xr_s[pl.ds(r0, SUBLANES), cs]
                nhi = a_r * hi + a_i * hr + xi_s[pl.ds(r0, SUBLANES), cs]
                xr_s[pl.ds(r0, SUBLANES), cs] = nhr
                xi_s[pl.ds(r0, SUBLANES), cs] = nhi
                return nhr, nhi

            hr, hi = lax.fori_loop(0, steps, step, (hr_s[rows, cs], hi_s[rows, cs]), unroll=min(steps, 8))
            hr_s[rows, cs] = hr
            hi_s[rows, cs] = hi

    for k in range(ks):
        ss = slice(k * nw, (k + 1) * nw)
        yk = _dot(xr_s[:, ss].astype(BF16), cre_ref[k]) - _dot(xi_s[:, ss].astype(BF16), cim_ref[k])
        ys_s[k] = jax.nn.gelu(yk + d_ref[:, k * kw:(k + 1) * kw] * us_s[k])
    y = jnp.concatenate([ys_s[k] for k in range(ks)], axis=1)
    y = y * jax.nn.sigmoid(_dot(y.astype(BF16), wglu_ref[...]) + bglu_ref[...])
    for k in range(ks):
        ys_s[k] = y[:, k * kw:(k + 1) * kw]
    if by_sequence:
        for b in range(nb):
            for k in range(ks):
                y_ref[b, :, k * kw:(k + 1) * kw] = ys_s[k, pl.ds(b, steps, stride=nb), :]
    else:
        for t in range(steps):
            for k in range(ks):
                tmp_s[0][k, pl.ds(t, nb, stride=steps), :] = ys_s[k, t * nb:(t + 1) * nb, :]
        for k in range(ks):
            y_ref[:, k * kw:(k + 1) * kw] = tmp_s[0][k]

    @pl.when(c == pl.num_programs(0) - 1)
    def _():
        hr_out[...] = hr_s[...]
        hi_out[...] = hi_s[...]


def _s5(u, h0r, h0i, l0, params, l, steps):
    _, nb, ns = h0r.shape
    w = u.shape[-1]
    blk = steps * nb
    ks = params[2].shape[1]
    if u.ndim == 3:
        grid = (u.shape[1] // steps,)
        u_spec = pl.BlockSpec((nb, steps, w), lambda i: (0, i, 0))
        tmp = []
    else:
        assert u.shape[0] == blk
        grid = (1,)
        u_spec = _full(u.shape)
        tmp = [pltpu.VMEM((ks, blk, LANES), F32)]
    kern = functools.partial(_s5_kernel, nb=nb, steps=steps)
    return pl.pallas_call(
        kern,
        grid=grid,
        in_specs=[u_spec, _layer(h0r, l0), _layer(h0i, l0)] + [_layer(a, l) for a in params],
        out_specs=[u_spec, _full((nb, ns)), _full((nb, ns))],
        out_shape=[jax.ShapeDtypeStruct(u.shape, F32), jax.ShapeDtypeStruct((nb, ns), F32),
                   jax.ShapeDtypeStruct((nb, ns), F32)],
        scratch_shapes=[pltpu.VMEM((ks, blk, LANES), F32), pltpu.VMEM((ks, blk, LANES), F32),
                        pltpu.VMEM((blk, ns), F32), pltpu.VMEM((blk, ns), F32), pltpu.VMEM((nb, ns), F32),
                        pltpu.VMEM((nb, ns), F32)] + tmp,
        compiler_params=_params("arbitrary"),
        name="s5_scan",
    )(u, h0r, h0i, *params)


def _log2(n):
    assert n & (n - 1) == 0, n
    return n.bit_length() - 1


def _chunk_masks(rows, chunk):
    sh = _log2(chunk)
    rr = lax.broadcasted_iota(jnp.int32, (rows, rows), 0)
    cc = lax.broadcasted_iota(jnp.int32, (rows, rows), 1)
    same = lax.shift_right_logical(rr, sh) == lax.shift_right_logical(cc, sh)
    return same, same & (cc <= rr)


def _gla_prologue(p_ref, wa_ref, ba_ref, chunk, qk):
    rows = p_ref.shape[0]
    q = p_ref[:, 0:qk]
    k = p_ref[:, qk:2 * qk]
    alow = p_ref[:, p_ref.shape[1] - LANES:].astype(BF16)
    z = _dot(alow, wa_ref[...]) + ba_ref[...]
    la = (jnp.minimum(z, 0.0) - jnp.log(1.0 + jnp.exp(-jnp.abs(z)))) * (1.0 / GLA_TAU)
    same, causal = _chunk_masks(rows, chunk)
    tri = jnp.where(causal, 1.0, 0.0).astype(BF16)
    blk = jnp.where(same, 1.0, 0.0).astype(BF16)
    la_hi = la.astype(BF16)
    la_lo = (la - la_hi.astype(F32)).astype(BF16)
    b = _dot(tri, la_hi) + _dot(tri, la_lo)
    bl = _dot(blk, la_hi) + _dot(blk, la_lo)
    return q, k, b, bl


def _gla_out(o, gn, r):
    on = o * lax.rsqrt(jnp.mean(o * o, axis=-1, keepdims=True) + EPS) * gn
    return on * (r * jax.nn.sigmoid(r))


def _gla_prompt_kernel(p_ref, wa_ref, ba_ref, gn_ref, y_ref, s_ref, st_s, *, heads, dk, dv, chunk):
    c = pl.program_id(1)

    @pl.when(c == 0)
    def _():
        st_s[...] = jnp.zeros_like(st_s)

    rows = p_ref.shape[0]
    qk = heads * dk
    q, k, b, bl = _gla_prologue(p_ref, wa_ref, ba_ref, chunk, qk)
    _, cmask = _chunk_masks(chunk, chunk)
    v_off, r_off = 2 * qk, 2 * qk + heads * dv
    qd = (q * (dk ** -0.5) * jnp.exp(b)).astype(BF16)
    kd = (k * jnp.exp(-b)).astype(BF16)
    kst = (k * jnp.exp(bl - b)).astype(BF16)
    dec = jnp.exp(bl)
    gn = gn_ref[...]
    for ci in range(rows // chunk):
        rs = slice(ci * chunk, (ci + 1) * chunk)
        for h in range(heads):
            hs = slice(h * dk, (h + 1) * dk)
            vh = p_ref[rs, v_off + h * dv:v_off + (h + 1) * dv].astype(BF16)
            att = lax.dot_general(qd[rs, hs], kd[rs, hs], NT_DIMS, preferred_element_type=F32)
            att = jnp.where(cmask, att, 0.0).astype(BF16)
            st = st_s[h]
            o = _dot(att, vh) + lax.dot_general(qd[rs, hs], st.astype(BF16), NT_DIMS, preferred_element_type=F32)
            ds_t = lax.dot_general(vh, kst[rs, hs], TN_DIMS, preferred_element_type=F32)
            st_s[h] = dec[ci * chunk:ci * chunk + 1, hs] * st + ds_t
            r = p_ref[rs, r_off + h * dv:r_off + (h + 1) * dv]
            y_ref[rs, h * dv:(h + 1) * dv] = _gla_out(o, gn, r)

    @pl.when(c == pl.num_programs(1) - 1)
    def _():
        for h in range(heads):
            s_ref[0, h] = st_s[h].T


def _gla_prompt(p, wa, ba, gn, l, bsz, heads, dk, dv, rows):
    n, pw = p.shape
    t = n // bsz
    nblk = t // rows
    kern = functools.partial(_gla_prompt_kernel, heads=heads, dk=dk, dv=dv, chunk=GLA_CHUNK)
    return pl.pallas_call(
        kern,
        grid=(bsz, nblk),
        in_specs=[pl.BlockSpec((rows, pw), lambda b, c: (b * nblk + c, 0)), _layer(wa, l), _layer(ba, l),
                  _layer(gn, l)],
        out_specs=[pl.BlockSpec((rows, heads * dv), lambda b, c: (b * nblk + c, 0)),
                   pl.BlockSpec((1, heads, dk, dv), lambda b, c: (b, 0, 0, 0))],
        out_shape=[jax.ShapeDtypeStruct((n, heads * dv), F32), jax.ShapeDtypeStruct((bsz, heads, dk, dv), F32)],
        scratch_shapes=[pltpu.VMEM((heads, dv, dk), F32)],
        compiler_params=_params("parallel", "arbitrary"),
        name="gla_prompt",
    )(p, wa, ba, gn)


def _gla_sample_kernel(p_ref, s0_ref, wa_ref, ba_ref, gn_ref, y_ref, s_ref, *, heads, dk, dv, t):
    rows = p_ref.shape[0]
    qk = heads * dk
    q, k, b, bl = _gla_prologue(p_ref, wa_ref, ba_ref, t, qk)
    _, cmask = _chunk_masks(SUBLANES, t)
    v_off, r_off = 2 * qk, 2 * qk + heads * dv
    qd = (q * (dk ** -0.5) * jnp.exp(b)).astype(BF16)
    kd = (k * jnp.exp(-b)).astype(BF16)
    kst = k * jnp.exp(bl - b)
    dec = jnp.exp(bl)
    gn = gn_ref[...]
    per_tile = SUBLANES // t
    row_id = lax.broadcasted_iota(jnp.int32, (SUBLANES, 1), 0)
    for ti in range(rows // SUBLANES):
        rs = slice(ti * SUBLANES, (ti + 1) * SUBLANES)
        dcols = []
        for j in range(per_tile):
            drow = dec[ti * SUBLANES + j * t:ti * SUBLANES + j * t + 1, :]
            dcols.append([jnp.broadcast_to(drow[:, g * LANES:(g + 1) * LANES], (LANES, LANES)).T
                          for g in range(qk // LANES)])
        for h in range(heads):
            hs = slice(h * dk, (h + 1) * dk)
            vh = p_ref[rs, v_off + h * dv:v_off + (h + 1) * dv].astype(BF16)
            att = lax.dot_general(qd[rs, hs], kd[rs, hs], NT_DIMS, preferred_element_type=F32)
            att = jnp.where(cmask, att, 0.0).astype(BF16)
            o = _dot(att, vh)
            for j in range(per_tile):
                bidx = ti * per_tile + j
                in_seq = (row_id >= j * t) & (row_id < (j + 1) * t)
                s0 = s0_ref[bidx, h]
                oj = _dot(qd[rs, hs], s0.astype(BF16))
                o = o + jnp.where(in_seq, oj, 0.0)
                kj = jnp.where(in_seq, kst[rs, hs], 0.0).astype(BF16)
                ds = lax.dot_general(kj, vh, TN_DIMS, preferred_element_type=F32)
                g, lo = (h * dk) // LANES, (h * dk) % LANES
                dcol = dcols[j][g][lo:lo + dk, 0:dv]
                s_ref[bidx, h] = dcol * s0 + ds
            r = p_ref[rs, r_off + h * dv:r_off + (h + 1) * dv]
            y_ref[rs, h * dv:(h + 1) * dv] = _gla_out(o, gn, r)


def _gla_sample(p, s0, wa, ba, gn, l, t, nb):
    n, pw = p.shape
    _, bsz, heads, dk, dv = s0.shape
    rows = nb * t
    kern = functools.partial(_gla_sample_kernel, heads=heads, dk=dk, dv=dv, t=t)
    return pl.pallas_call(
        kern,
        grid=(bsz // nb,),
        in_specs=[pl.BlockSpec((rows, pw), lambda i: (i, 0)),
                  pl.BlockSpec((None, nb, heads, dk, dv), lambda i: (l, i, 0, 0, 0)),
                  _layer(wa, l), _layer(ba, l), _layer(gn, l)],
        out_specs=[pl.BlockSpec((rows, heads * dv), lambda i: (i, 0)),
                   pl.BlockSpec((nb, heads, dk, dv), lambda i: (i, 0, 0, 0))],
        out_shape=[jax.ShapeDtypeStruct((n, heads * dv), F32), jax.ShapeDtypeStruct(s0.shape[1:], F32)],
        compiler_params=_params("parallel"),
        name="gla_sample",
    )(p, s0, wa, ba, gn)


def _first_half(shape):
    lane = lax.broadcasted_iota(jnp.int32, shape, 1)
    return (lane & (SWA_HD - 1)) < SWA_HD // 2


def _rope_table_kernel(pos_ref, inv_ref, cos_ref, sin_ref):
    ang = pos_ref[...] * inv_ref[...]
    s = jnp.sin(ang)
    cos_ref[...] = jnp.cos(ang)
    sin_ref[...] = jnp.where(_first_half(ang.shape), -s, s)


def _rope_table(pos, inv):
    rows = pos.shape[0]
    return pl.pallas_call(
        _rope_table_kernel,
        out_shape=[jax.ShapeDtypeStruct((rows, LANES), F32)] * 2,
        name="rope_table",
    )(pos.astype(F32).reshape(rows, 1), inv)


def _rope(x, cos, sin):
    half = SWA_HD // 2
    first_half = _first_half(cos.shape)
    out = []
    for g in range(x.shape[1] // LANES):
        xg = x[:, g * LANES:(g + 1) * LANES]
        swapped = jnp.where(first_half, pltpu.roll(xg, LANES - half, 1), pltpu.roll(xg, half, 1))
        out.append(xg * cos + swapped * sin)
    return out[0] if len(out) == 1 else jnp.concatenate(out, axis=1)


def _swa_prompt_kernel(sink_ref, cur_ref, prev_ref, cos_ref, sin_ref, cosp_ref, sinp_ref, y_ref, nk_ref, nv_ref, *,
                       heads, l):
    i = pl.program_id(1)
    w = WINDOW
    qw = heads * SWA_HD
    kvw = SWA_KV_HEADS * SWA_HD
    gq = heads // SWA_KV_HEADS
    cos, sin = cos_ref[...], sin_ref[...]
    q = _rope(cur_ref[:, 0:qw], cos, sin) * (SWA_HD ** -0.5)
    k_cur = _rope(cur_ref[:, qw:qw + kvw], cos, sin)
    k_prev = _rope(prev_ref[:, 0:kvw], cosp_ref[...], sinp_ref[...])
    v_cur = cur_ref[:, qw + kvw:qw + 2 * kvw]
    v_prev = prev_ref[:, kvw:2 * kvw]
    kcat = jnp.concatenate([k_prev, k_cur], axis=0).astype(BF16)
    vcat = jnp.concatenate([v_prev, v_cur], axis=0).astype(BF16)
    qb = q.astype(BF16)
    rr = lax.broadcasted_iota(jnp.int32, (w, 2 * w), 0)
    jj = lax.broadcasted_iota(jnp.int32, (w, 2 * w), 1)
    first_key = jnp.where(i > 0, 0, w)
    mask = (jj <= rr + w) & (jj > rr) & (jj >= first_key)
    for h in range(heads):
        kv = h // gq
        ks = slice(kv * SWA_HD, (kv + 1) * SWA_HD)
        s = lax.dot_general(qb[:, h * SWA_HD:(h + 1) * SWA_HD], kcat[:, ks], NT_DIMS, preferred_element_type=F32)
        s = jnp.where(mask, s, -jnp.inf)
        sink = sink_ref[l, h]
        m = jnp.maximum(jnp.max(s, axis=-1, keepdims=True), sink)
        p = jnp.exp(s - m)
        den = jnp.sum(p, axis=-1, keepdims=True) + jnp.exp(sink - m)
        pn = (p * (1.0 / den)).astype(BF16)
        y_ref[:, h * SWA_HD:(h + 1) * SWA_HD] = _dot(pn, vcat[:, ks])

    @pl.when(i == pl.num_programs(1) - 1)
    def _():
        nk_ref[0] = k_cur
        nv_ref[0] = v_cur


def _swa_prompt(p, sinks, cos, sin, l, bsz, heads):
    n, pw = p.shape
    t = n // bsz
    nblk = t // WINDOW
    qw, kvw = heads * SWA_HD, SWA_KV_HEADS * SWA_HD
    kv_blk = qw // (2 * kvw)
    kern = functools.partial(_swa_prompt_kernel, heads=heads, l=l)
    tab_cur = pl.BlockSpec((WINDOW, LANES), lambda b, i: (i, 0))
    tab_prev = pl.BlockSpec((WINDOW, LANES), lambda b, i: (jnp.maximum(i - 1, 0), 0))
    return pl.pallas_call(
        kern,
        grid=(bsz, nblk),
        in_specs=[pl.BlockSpec(memory_space=pltpu.SMEM),
                  pl.BlockSpec((WINDOW, pw), lambda b, i: (b * nblk + i, 0)),
                  pl.BlockSpec((WINDOW, 2 * kvw), lambda b, i: (b * nblk + jnp.maximum(i - 1, 0), kv_blk)),
                  tab_cur, tab_cur, tab_prev, tab_prev],
        out_specs=[pl.BlockSpec((WINDOW, qw), lambda b, i: (b * nblk + i, 0)),
                   pl.BlockSpec((1, WINDOW, kvw), lambda b, i: (b, 0, 0)),
                   pl.BlockSpec((1, WINDOW, kvw), lambda b, i: (b, 0, 0))],
        out_shape=[jax.ShapeDtypeStruct((n, qw), F32), jax.ShapeDtypeStruct((bsz, WINDOW, kvw), F32),
                   jax.ShapeDtypeStruct((bsz, WINDOW, kvw), F32)],
        compiler_params=_params("parallel", "arbitrary"),
        name="swa_prompt",
    )(sinks, p, p, cos, sin, cos, sin)


def _expand_kv(x, lane):
    other = pltpu.roll(x, SWA_HD, 1)
    first = lane < SWA_HD
    kv0 = jnp.where(first, x, other)
    kv1 = jnp.where(first, other, x)
    return kv0, kv1


def _swa_sample_kernel(sinkrow_ref, p_ref, ck_ref, cv_ref, cos_ref, sin_ref, y_ref, nk_ref, nv_ref, *, heads, t, nb):
    w = ck_ref.shape[1]
    qw = heads * SWA_HD
    kvw = SWA_KV_HEADS * SWA_HD
    gq = heads // SWA_KV_HEADS
    per_tile = SUBLANES // t
    ncol = heads * t
    row8 = lax.broadcasted_iota(jnp.int32, (SUBLANES, 1), 0)
    cos, sin = cos_ref[...], sin_ref[...]
    lane_kv = lax.broadcasted_iota(jnp.int32, (1, kvw), 1)
    qm_row = lax.broadcasted_iota(jnp.int32, (ncol, qw), 0)
    qm_lane = lax.broadcasted_iota(jnp.int32, (ncol, qw), 1)
    qmask = lax.shift_right_logical(qm_row, _log2(t)) == lax.shift_right_logical(qm_lane, _log2(SWA_HD))
    ck_row = lax.broadcasted_iota(jnp.int32, (w, ncol), 0)
    col_step = lax.broadcasted_iota(jnp.int32, (w, ncol), 1) & (t - 1)
    cache_ok = ck_row > col_step
    nk_row = lax.broadcasted_iota(jnp.int32, (SUBLANES, ncol), 0)
    nk_col_step = lax.broadcasted_iota(jnp.int32, (SUBLANES, ncol), 1) & (t - 1)
    sink_row = sinkrow_ref[...]
    for ti in range(nb // per_tile):
        rs = slice(ti * SUBLANES, (ti + 1) * SUBLANES)
        q8 = _rope(p_ref[rs, 0:qw], cos, sin) * (SWA_HD ** -0.5)
        kn8 = _rope(p_ref[rs, qw:qw + kvw], cos, sin)
        vn8 = p_ref[rs, qw + kvw:qw + 2 * kvw]
        kn0, kn1 = _expand_kv(kn8, lane_kv)
        vn0, vn1 = _expand_kv(vn8, lane_kv)
        kn_x = jnp.concatenate([kn0] * (gq // 2) + [kn1] * (gq // 2), axis=1).astype(BF16)
        vn_x = jnp.concatenate([vn0] * (gq // 2) + [vn1] * (gq // 2), axis=1).astype(BF16)
        for j in range(per_tile):
            bidx = ti * per_tile + j
            qj = q8[j * t:(j + 1) * t, :]
            qm = jnp.where(qmask, jnp.concatenate([qj] * heads, axis=0), 0.0).astype(BF16)
            ck = ck_ref[bidx]
            cv = cv_ref[bidx]
            ck0, ck1 = _expand_kv(ck, lane_kv)
            cv0, cv1 = _expand_kv(cv, lane_kv)
            ck_x = jnp.concatenate([ck0] * (gq // 2) + [ck1] * (gq // 2), axis=1).astype(BF16)
            cv_x = jnp.concatenate([cv0] * (gq // 2) + [cv1] * (gq // 2), axis=1).astype(BF16)
            s_c = lax.dot_general(ck_x, qm, NT_DIMS, preferred_element_type=F32)
            s_n = lax.dot_general(kn_x, qm, NT_DIMS, preferred_element_type=F32)
            s_c = jnp.where(cache_ok, s_c, -jnp.inf)
            new_ok = (nk_row >= j * t) & (nk_row - j * t <= nk_col_step) & (nk_row < (j + 1) * t)
            s_n = jnp.where(new_ok, s_n, -jnp.inf)
            m = jnp.maximum(jnp.maximum(jnp.max(s_c, axis=0, keepdims=True), jnp.max(s_n, axis=0, keepdims=True)),
                            sink_row)
            p_c = jnp.exp(s_c - m)
            p_n = jnp.exp(s_n - m)
            den = jnp.sum(p_c, axis=0, keepdims=True) + jnp.sum(p_n, axis=0, keepdims=True) + jnp.exp(sink_row - m)
            rden = 1.0 / den
            o = (lax.dot_general((p_c * rden).astype(BF16), cv_x, TN_DIMS, preferred_element_type=F32)
                 + lax.dot_general((p_n * rden).astype(BF16), vn_x, TN_DIMS, preferred_element_type=F32))
            o = jnp.where(qmask, o, 0.0)
            acc = o[0:SUBLANES]
            for g in range(1, ncol // SUBLANES):
                acc = acc + o[g * SUBLANES:(g + 1) * SUBLANES]
            for g in range(1, per_tile):
                acc = acc + pltpu.roll(acc, SUBLANES - g * t, 0)
            y_ref[ti * SUBLANES + j * t:ti * SUBLANES + (j + 1) * t, :] = acc[0:t]
            kn_tail = pltpu.roll(kn8, (SUBLANES - t - j * t) % SUBLANES, 0)
            vn_tail = pltpu.roll(vn8, (SUBLANES - t - j * t) % SUBLANES, 0)
            k_shift = pltpu.roll(ck, w - t, 0)
            v_shift = pltpu.roll(cv, w - t, 0)
            body = w - SUBLANES
            nk_ref[bidx, 0:body] = k_shift[0:body]
            nv_ref[bidx, 0:body] = v_shift[0:body]
            nk_ref[bidx, body:w] = jnp.where(row8 < SUBLANES - t, k_shift[body:w], kn_tail)
            nv_ref[bidx, body:w] = jnp.where(row8 < SUBLANES - t, v_shift[body:w], vn_tail)


def _swa_sample(p, ck, cv, sink_rows, cos, sin, l, heads, t, nb):
    n, pw = p.shape
    _, bsz, w, kvw = ck.shape
    qw = heads * SWA_HD
    kern = functools.partial(_swa_sample_kernel, heads=heads, t=t, nb=nb)
    cache = pl.BlockSpec((None, nb, w, kvw), lambda i: (l, i, 0, 0))
    return pl.pallas_call(
        kern,
        grid=(bsz // nb,),
        in_specs=[_layer(sink_rows, l), pl.BlockSpec((nb * t, pw), lambda i: (i, 0)), cache, cache,
                  _full(cos.shape), _full(sin.shape)],
        out_specs=[pl.BlockSpec((nb * t, qw), lambda i: (i, 0)),
                   pl.BlockSpec((nb, w, kvw), lambda i: (i, 0, 0)), pl.BlockSpec((nb, w, kvw), lambda i: (i, 0, 0))],
        out_shape=[jax.ShapeDtypeStruct((n, qw), F32), jax.ShapeDtypeStruct(ck.shape[1:], F32),
                   jax.ShapeDtypeStruct(cv.shape[1:], F32)],
        compiler_params=_params("parallel"),
        name="swa_sample",
    )(sink_rows, p, ck, cv, cos, sin)


FFN_HIDDEN_CHUNK = 1024


def _merge_ffn_kernel(x_ref, ya_ref, yb_ref, yc_ref, gt_ref, wb_ref, wo_ref, g_ref, wg_ref, wu_ref, wd_ref, gf_ref,
                      o_ref, *, final_norm):
    d = x_ref.shape[1]
    merged = None
    off = 0
    for i, y_ref in enumerate((ya_ref, yb_ref, yc_ref)):
        wdt = y_ref.shape[1]
        br = _dot(y_ref[...].astype(BF16), wb_ref[off:off + wdt, :])
        term = jax.nn.sigmoid(gt_ref[:, i * d:(i + 1) * d]) * br
        merged = term if merged is None else merged + term
        off += wdt
    x = x_ref[...] + _dot(merged.astype(BF16), wo_ref[...])
    hb = _rms(x, g_ref[...]).astype(BF16)
    hidden = wg_ref.shape[1]
    acc = x
    for c0 in range(0, hidden, FFN_HIDDEN_CHUNK):
        c1 = min(c0 + FFN_HIDDEN_CHUNK, hidden)
        gate = _dot(hb, wg_ref[:, c0:c1])
        up = _dot(hb, wu_ref[:, c0:c1])
        act = (gate * jax.nn.sigmoid(gate) * up).astype(BF16)
        acc = acc + _dot(act, wd_ref[c0:c1, :])
    o_ref[...] = _rms(acc, gf_ref[...]) if final_norm else acc


def _merge_ffn(x, ya, yb, yc, gates, wb, wo, g, wg, wu, wd, gf, l, tm, final_norm):
    n, d = x.shape
    row = lambda a: pl.BlockSpec((tm, a.shape[1]), lambda i: (i, 0))
    return pl.pallas_call(
        functools.partial(_merge_ffn_kernel, final_norm=final_norm),
        grid=(n // tm,),
        in_specs=[row(x), row(ya), row(yb), row(yc), row(gates), _layer(wb, l), _layer(wo, l), _layer(g, l),
                  _layer(wg, l), _layer(wu, l), _layer(wd, l), _full(gf.shape)],
        out_specs=pl.BlockSpec((tm, d), lambda i: (i, 0)),
        out_shape=jax.ShapeDtypeStruct((n, d), F32),
        compiler_params=_params("parallel"),
        name="merge_ffn",
    )(x, ya, yb, yc, gates, wb, wo, g, wg, wu, wd, gf)


def _tile(n, want):
    while n % want:
        want //= 2
    return want


def kernel(x_prompt, x_sample, state_s5_re, state_s5_im, state_gla, cache_swa_k, cache_swa_v, g_mix, w_in, s5_lambda_re, s5_lambda_im, s5_log_dt, s5_b_re, s5_b_im, s5_c_re, s5_c_im, s5_d, w_glu, b_glu, w_gla_a2, b_gla_a, g_gla_norm, swa_sinks, w_branch, w_out, g_ffn, w_ffn_gate, w_ffn_up, w_ffn_down, g_final):
    bp, tp, d = x_prompt.shape
    bs, ts, _ = x_sample.shape
    depth = w_in.shape[0]
    s5_w = s5_d.shape[1]
    groups, s5_n = s5_lambda_re.shape[1:]
    gla_qk = w_gla_a2.shape[2]
    gla_rank = w_gla_a2.shape[1]
    gla_heads, gla_dk, gla_dv = state_gla.shape[2:]
    gla_v = gla_heads * gla_dv
    swa_heads = swa_sinks.shape[1]
    swa_q = swa_heads * SWA_HD
    swa_kv = SWA_KV_HEADS * SWA_HD
    assert SWA_KV_HEADS * SWA_HD == LANES and bp % SUBLANES == 0 and bs % SUBLANES == 0 and SUBLANES % ts == 0
    assert tp % WINDOW == 0 and cache_swa_k.shape[2] == WINDOW

    o_a, o_b = 0, s5_w
    o_low = o_b + 2 * gla_qk + 2 * gla_v
    o_c = o_low + gla_rank
    o_g = o_c + swa_q + 2 * swa_kv
    widths = (s5_w, 2 * gla_qk + 2 * gla_v + LANES, swa_q + 2 * swa_kv, 3 * d)

    hp = x_prompt.reshape(bp * tp, d)
    hs = x_sample.reshape(bs * ts, d)
    inv = ROPE_THETA ** (-jnp.arange(SWA_HD // 2, dtype=F32) * (2.0 / SWA_HD))
    inv = jnp.tile(inv, 2 * LANES // SWA_HD).reshape(1, LANES)
    cos_p, sin_p = _rope_table(jnp.arange(tp, dtype=jnp.int32), inv)
    cos_s, sin_s = _rope_table(PAST_LEN + (jnp.arange(SUBLANES, dtype=jnp.int32) & (ts - 1)), inv)
    zeros_s5 = jnp.zeros((1, bp, groups * s5_n), F32)
    tm_p, tm_s = _tile(bp * tp, 256), _tile(bs * ts, 256)
    s5_steps = _tile(tp, 128)
    gla_rows = _tile(tp, 256)

    pad_cols = jnp.zeros((depth, d, LANES - gla_rank), F32)
    w_r = jnp.concatenate([w_in[:, :, o_a:o_c], pad_cols, w_in[:, :, o_c:]], axis=2).astype(BF16)
    wa2 = jnp.concatenate([w_gla_a2, jnp.zeros((depth, LANES - gla_rank, gla_qk), F32)], axis=1).astype(BF16)
    ba = b_gla_a.reshape(depth, 1, gla_qk)
    gn = g_gla_norm.reshape(depth, 1, gla_dv)
    ar, ai, bbr, bbi = _s5_prep(s5_lambda_re, s5_lambda_im, s5_log_dt, s5_b_re, s5_b_im)
    bre, bim, cre, cim = _s5_pack(bbr, bbi, s5_c_re, s5_c_im)
    s5_params = (ar, ai, bre, bim, cre, cim, s5_d.reshape(depth, 1, s5_w), w_glu.astype(BF16),
                 b_glu.reshape(depth, 1, s5_w))
    h0r_s = state_s5_re.reshape(depth, bs, groups * s5_n)
    h0i_s = state_s5_im.reshape(depth, bs, groups * s5_n)
    g_mix_r, g_ffn_r, gf = g_mix.reshape(depth, 1, d), g_ffn.reshape(depth, 1, d), g_final.reshape(1, d)
    wb, wo = w_branch.astype(BF16), w_out.astype(BF16)
    wg, wu, wd = w_ffn_gate.astype(BF16), w_ffn_up.astype(BF16), w_ffn_down.astype(BF16)
    sink_rows = jnp.repeat(swa_sinks, ts, axis=1).reshape(depth, 1, swa_heads * ts)
    ck = cache_swa_k.reshape(depth, bs, WINDOW, swa_kv)
    cv = cache_swa_v.reshape(depth, bs, WINDOW, swa_kv)

    outs_p = [[] for _ in range(5)]
    outs_s = [[] for _ in range(5)]
    for l in range(depth):
        last = l == depth - 1

        pa, pb, pc, pg = _inproj(hp, g_mix_r, w_r, l, widths, tm_p)
        ya, hr, hi = _s5(pa.reshape(bp, tp, s5_w), zeros_s5, zeros_s5, 0, s5_params, l, s5_steps)
        ya = ya.reshape(bp * tp, s5_w)
        yb, sg = _gla_prompt(pb, wa2, ba, gn, l, bp, gla_heads, gla_dk, gla_dv, gla_rows)
        yc, nk, nv = _swa_prompt(pc, swa_sinks, cos_p, sin_p, l, bp, swa_heads)
        hp = _merge_ffn(hp, ya, yb, yc, pg, wb, wo, g_ffn_r, wg, wu, wd, gf, l, tm_p, last)
        for lst, val in zip(outs_p, (hr.reshape(bp, groups, s5_n), hi.reshape(bp, groups, s5_n), sg,
                                     nk.reshape(bp, WINDOW, SWA_KV_HEADS, SWA_HD),
                                     nv.reshape(bp, WINDOW, SWA_KV_HEADS, SWA_HD))):
            lst.append(val)

        pa, pb, pc, pg = _inproj(hs, g_mix_r, w_r, l, widths, tm_s)
        ya, hr, hi = _s5(pa, h0r_s, h0i_s, l, s5_params, l, ts)
        yb, sg = _gla_sample(pb, state_gla, wa2, ba, gn, l, ts, _tile(bs, 16))
        yc, nk, nv = _swa_sample(pc, ck, cv, sink_rows, cos_s, sin_s, l, swa_heads, ts, _tile(bs, 8))
        hs = _merge_ffn(hs, ya, yb, yc, pg, wb, wo, g_ffn_r, wg, wu, wd, gf, l, tm_s, last)
        for lst, val in zip(outs_s, (hr.reshape(bs, groups, s5_n), hi.reshape(bs, groups, s5_n), sg,
                                     nk.reshape(bs, WINDOW, SWA_KV_HEADS, SWA_HD),
                                     nv.reshape(bs, WINDOW, SWA_KV_HEADS, SWA_HD))):
            lst.append(val)

    return (hp.reshape(bp, tp, d), hs.reshape(bs, ts, d),
            *[jnp.stack(o) for o in outs_p], *[jnp.stack(o) for o in outs_s])
```

```python
import functools

import jax
import jax.numpy as jnp
from jax import lax
from jax.experimental import pallas as pl
from jax.experimental.pallas import tpu as pltpu

F32 = jnp.float32
BF16 = jnp.bfloat16

EPS = 1e-6
GLA_TAU = 16.0
GLA_CHUNK = 64
SWA_KV_HEADS = 2
SWA_HD = 64
WINDOW = 128
ROPE_THETA = 10000.0
PAST_LEN = 16384

SUBLANES = 8
LANES = 128
VMEM_LIMIT = 56 * 1024 * 1024

NT_DIMS = (((1,), (1,)), ((), ()))
TN_DIMS = (((0,), (0,)), ((), ()))


def _params(*sem):
    return pltpu.CompilerParams(dimension_semantics=sem, vmem_limit_bytes=VMEM_LIMIT)


def _dot(a, b):
    return jnp.dot(a, b, preferred_element_type=F32)


def _rms(x, g):
    return x * lax.rsqrt(jnp.mean(x * x, axis=-1, keepdims=True) + EPS) * g


def _full(shape):
    nd = len(shape)
    return pl.BlockSpec(shape, lambda *_: (0,) * nd)


def _layer(a, l):
    nd = a.ndim - 1
    return pl.BlockSpec((None,) + a.shape[1:], lambda *_: (l,) + (0,) * nd, pipeline_mode=pl.Buffered(1))


def _win_prep_kernel(w_ref, o_ref, *, split, pad):
    o_ref[:, 0:split] = w_ref[:, 0:split].astype(BF16)
    o_ref[:, split:split + pad] = jnp.zeros((o_ref.shape[0], pad), BF16)
    o_ref[:, split + pad:] = w_ref[:, split:].astype(BF16)


def _win_prep(w, split, pad, rows):
    depth, d, n = w.shape
    return pl.pallas_call(
        functools.partial(_win_prep_kernel, split=split, pad=pad),
        grid=(depth, d // rows),
        in_specs=[pl.BlockSpec((None, rows, n), lambda l, i: (l, i, 0))],
        out_specs=pl.BlockSpec((None, rows, n + pad), lambda l, i: (l, i, 0)),
        out_shape=jax.ShapeDtypeStruct((depth, d, n + pad), BF16),
        compiler_params=_params("parallel", "parallel"),
        name="win_prep",
    )(w)


def _inproj_kernel(x_ref, g_ref, w_ref, *o_refs):
    xb = _rms(x_ref[...], g_ref[...]).astype(BF16)
    off = 0
    for o_ref in o_refs:
        n = o_ref.shape[-1]
        for c0 in range(0, n, 512):
            c1 = min(c0 + 512, n)
            o_ref[:, c0:c1] = _dot(xb, w_ref[:, off + c0:off + c1])
        off += n


def _inproj(x, g, w, l, widths, tm):
    n, d = x.shape
    return pl.pallas_call(
        _inproj_kernel,
        grid=(n // tm,),
        in_specs=[pl.BlockSpec((tm, d), lambda i: (i, 0)), _layer(g, l), _layer(w, l)],
        out_specs=[pl.BlockSpec((tm, wd), lambda i: (i, 0)) for wd in widths],
        out_shape=[jax.ShapeDtypeStruct((n, wd), F32) for wd in widths],
        compiler_params=_params("parallel"),
        name="inproj",
    )(x, g, w)


def _s5_prep_kernel(lr_ref, li_ref, ldt_ref, br_ref, bi_ref, ar_ref, ai_ref, bbr_ref, bbi_ref):
    lr, li = lr_ref[...], li_ref[...]
    dt = jnp.exp(ldt_ref[...])
    mag = jnp.exp(lr * dt)
    ar = mag * jnp.cos(li * dt)
    ai = mag * jnp.sin(li * dt)
    den = lr * lr + li * li
    zr, zi = ar - 1.0, ai
    er = (zr * lr + zi * li) / den
    ei = (zi * lr - zr * li) / den
    ar_ref[...] = ar
    ai_ref[...] = ai
    br, bi = br_ref[...], bi_ref[...]
    bbr_ref[...] = er[None] * br - ei[None] * bi
    bbi_ref[...] = er[None] * bi + ei[None] * br


def _s5_prep(lam_re, lam_im, log_dt, b_re, b_im):
    depth, g, c, n = b_re.shape
    dg = depth * g
    brt = jnp.transpose(b_re.reshape(dg, c, n), (1, 0, 2))
    bit = jnp.transpose(b_im.reshape(dg, c, n), (1, 0, 2))
    ar, ai, bbr, bbi = pl.pallas_call(
        _s5_prep_kernel,
        out_shape=[jax.ShapeDtypeStruct((dg, n), F32)] * 2 + [jax.ShapeDtypeStruct((c, dg, n), F32)] * 2,
        name="s5_prep",
    )(lam_re.reshape(dg, n), lam_im.reshape(dg, n), log_dt.reshape(dg, 1), brt, bit)
    unt = lambda a: jnp.transpose(a, (1, 0, 2)).reshape(depth, g, c, n)
    return ar.reshape(depth, 1, g * n), ai.reshape(depth, 1, g * n), unt(bbr), unt(bbi)


def _s5_pack(bbr, bbi, c_re, c_im):
    depth, g, c, n = bbr.shape
    gp = LANES // c
    ks = g // gp
    eye = jnp.eye(gp, dtype=F32)

    def pack_b(b):
        return jnp.einsum("lkgcn,gh->lkgchn", b.reshape(depth, ks, gp, c, n), eye).reshape(
            depth, ks, gp * c, gp * n).astype(BF16)

    def pack_c(cm):
        return jnp.einsum("lkgnc,gh->lkgnhc", cm.reshape(depth, ks, gp, n, c), eye).reshape(
            depth, ks, gp * n, gp * c).astype(BF16)

    return pack_b(bbr), pack_b(bbi), pack_c(c_re), pack_c(c_im)


S5_COL_BLOCK = 512


def _s5_kernel(u_ref, h0r_ref, h0i_ref, ar_ref, ai_ref, bre_ref, bim_ref, cre_ref, cim_ref, d_ref, wglu_ref,
               bglu_ref, y_ref, hr_out, hi_out, us_s, ys_s, xr_s, xi_s, hr_s, hi_s, *tmp_s, nb, steps):
    c = pl.program_id(0)

    @pl.when(c == 0)
    def _():
        hr_s[...] = h0r_ref[...]
        hi_s[...] = h0i_ref[...]

    ks, kw, nw = bre_ref.shape
    by_sequence = u_ref.ndim == 3
    if by_sequence:
        for b in range(nb):
            for k in range(ks):
                us_s[k, pl.ds(b, steps, stride=nb), :] = u_ref[b, :, k * kw:(k + 1) * kw]
    else:
        for k in range(ks):
            tmp_s[0][k] = u_ref[:, k * kw:(k + 1) * kw]
        for t in range(steps):
            for k in range(ks):
                us_s[k, t * nb:(t + 1) * nb, :] = tmp_s[0][k, pl.ds(t, nb, stride=steps), :]

    for k in range(ks):
        uk = us_s[k].astype(BF16)
        xr_s[:, k * nw:(k + 1) * nw] = _dot(uk, bre_ref[k])
        xi_s[:, k * nw:(k + 1) * nw] = _dot(uk, bim_ref[k])

    ncols = xr_s.shape[1]
    for rb in range(nb // SUBLANES):
        rows = slice(rb * SUBLANES, (rb + 1) * SUBLANES)
        for cb in range(ncols // S5_COL_BLOCK):
            cs = slice(cb * S5_COL_BLOCK, (cb + 1) * S5_COL_BLOCK)
            a_r = jnp.broadcast_to(ar_ref[:, cs], (SUBLANES, S5_COL_BLOCK))
            a_i = jnp.broadcast_to(ai_ref[:, cs], (SUBLANES, S5_COL_BLOCK))

            def step(t, carry, rb=rb, cs=cs, a_r=a_r, a_i=a_i):
                hr, hi = carry
                r0 = pl.multiple_of(t * nb + rb * SUBLANES, SUBLANES)
                nhr = a_r * hr - a_i * hi + xr_s[pl.ds(r0, SUBLANES), cs]
                nhi = a_r * hi + a_i * hr + xi_s[pl.ds(r0, SUBLANES), cs]
                xr_s[pl.ds(r0, SUBLANES), cs] = nhr
                xi_s[pl.ds(r0, SUBLANES), cs] = nhi
                return nhr, nhi

            hr, hi = lax.fori_loop(0, steps, step, (hr_s[rows, cs], hi_s[rows, cs]), unroll=min(steps, 8))
            hr_s[rows, cs] = hr
            hi_s[rows, cs] = hi

    for k in range(ks):
        ss = slice(k * nw, (k + 1) * nw)
        yk = _dot(xr_s[:, ss].astype(BF16), cre_ref[k]) - _dot(xi_s[:, ss].astype(BF16), cim_ref[k])
        ys_s[k] = jax.nn.gelu(yk + d_ref[:, k * kw:(k + 1) * kw] * us_s[k])
    y = jnp.concatenate([ys_s[k] for k in range(ks)], axis=1)
    y = y * jax.nn.sigmoid(_dot(y.astype(BF16), wglu_ref[...]) + bglu_ref[...])
    for k in range(ks):
        ys_s[k] = y[:, k * kw:(k + 1) * kw]
    if by_sequence:
        for b in range(nb):
            for k in range(ks):
                y_ref[b, :, k * kw:(k + 1) * kw] = ys_s[k, pl.ds(b, steps, stride=nb), :]
    else:
        for t in range(steps):
            for k in range(ks):
                tmp_s[0][k, pl.ds(t, nb, stride=steps), :] = ys_s[k, t * nb:(t + 1) * nb, :]
        for k in range(ks):
            y_ref[:, k * kw:(k + 1) * kw] = tmp_s[0][k]

    @pl.when(c == pl.num_programs(0) - 1)
    def _():
        hr_out[...] = hr_s[...]
        hi_out[...] = hi_s[...]


def _s5(u, h0r, h0i, l0, params, l, steps):
    _, nb, ns = h0r.shape
    w = u.shape[-1]
    blk = steps * nb
    ks = params[2].shape[1]
    if u.ndim == 3:
        grid = (u.shape[1] // steps,)
        u_spec = pl.BlockSpec((nb, steps, w), lambda i: (0, i, 0))
        tmp = []
    else:
        assert u.shape[0] == blk
        grid = (1,)
        u_spec = _full(u.shape)
        tmp = [pltpu.VMEM((ks, blk, LANES), F32)]
    kern = functools.partial(_s5_kernel, nb=nb, steps=steps)
    return pl.pallas_call(
        kern,
        grid=grid,
        in_specs=[u_spec, _layer(h0r, l0), _layer(h0i, l0)] + [_layer(a, l) for a in params],
        out_specs=[u_spec, _full((nb, ns)), _full((nb, ns))],
        out_shape=[jax.ShapeDtypeStruct(u.shape, F32), jax.ShapeDtypeStruct((nb, ns), F32),
                   jax.ShapeDtypeStruct((nb, ns), F32)],
        scratch_shapes=[pltpu.VMEM((ks, blk, LANES), F32), pltpu.VMEM((ks, blk, LANES), F32),
                        pltpu.VMEM((blk, ns), F32), pltpu.VMEM((blk, ns), F32), pltpu.VMEM((nb, ns), F32),
                        pltpu.VMEM((nb, ns), F32)] + tmp,
        compiler_params=_params("arbitrary"),
        name="s5_scan",
    )(u, h0r, h0i, *params)


def _log2(n):
    assert n & (n - 1) == 0, n
    return n.bit_length() - 1


def _chunk_masks(rows, chunk):
    sh = _log2(chunk)
    rr = lax.broadcasted_iota(jnp.int32, (rows, rows), 0)
    cc = lax.broadcasted_iota(jnp.int32, (rows, rows), 1)
    same = lax.shift_right_logical(rr, sh) == lax.shift_right_logical(cc, sh)
    return same, same & (cc <= rr)


def _gla_prologue(p_ref, wa_ref, ba_ref, chunk, qk):
    rows = p_ref.shape[0]
    q = p_ref[:, 0:qk]
    k = p_ref[:, qk:2 * qk]
    alow = p_ref[:, p_ref.shape[1] - LANES:].astype(BF16)
    z = _dot(alow, wa_ref[...]) + ba_ref[...]
    la = (jnp.minimum(z, 0.0) - jnp.log(1.0 + jnp.exp(-jnp.abs(z)))) * (1.0 / GLA_TAU)
    same, causal = _chunk_masks(rows, chunk)
    tri = jnp.where(causal, 1.0, 0.0).astype(BF16)
    blk = jnp.where(same, 1.0, 0.0).astype(BF16)
    la_hi = la.astype(BF16)
    la_lo = (la - la_hi.astype(F32)).astype(BF16)
    b = _dot(tri, la_hi) + _dot(tri, la_lo)
    bl = _dot(blk, la_hi) + _dot(blk, la_lo)
    return q, k, b, bl


def _gla_out(o, gn, r):
    on = o * lax.rsqrt(jnp.mean(o * o, axis=-1, keepdims=True) + EPS) * gn
    return on * (r * jax.nn.sigmoid(r))


def _gla_prompt_kernel(p_ref, wa_ref, ba_ref, gn_ref, y_ref, s_ref, st_s, *, heads, dk, dv, chunk):
    c = pl.program_id(1)

    @pl.when(c == 0)
    def _():
        st_s[...] = jnp.zeros_like(st_s)

    rows = p_ref.shape[0]
    qk, vw = heads * dk, heads * dv
    q, k, b, bl = _gla_prologue(p_ref, wa_ref, ba_ref, chunk, qk)
    v_off, r_off = 2 * qk, 2 * qk + vw
    qd = (q * (dk ** -0.5) * jnp.exp(b)).astype(BF16)
    kd = (k * jnp.exp(-b)).astype(BF16)
    kst = (k * jnp.exp(bl - b)).astype(BF16)
    dec = jnp.exp(bl)
    gn = gn_ref[...]
    zero = jnp.zeros((), BF16)
    qk_head = lax.shift_right_logical(lax.broadcasted_iota(jnp.int32, (chunk, qk), 1), _log2(dk))
    v_head = lax.shift_right_logical(lax.broadcasted_iota(jnp.int32, (chunk, vw), 1), _log2(dv))
    rr = lax.broadcasted_iota(jnp.int32, (chunk, heads * chunk), 0)
    cc = lax.broadcasted_iota(jnp.int32, (chunk, heads * chunk), 1) & (chunk - 1)
    causal = cc <= rr
    nchunk = rows // chunk
    by_head = lambda x, head: jnp.concatenate([jnp.where(head == h, x, zero) for h in range(heads)], axis=0)
    att, ds_t, v_diag, qd_rows = [], [], [], []
    for ci in range(nchunk):
        rs = slice(ci * chunk, (ci + 1) * chunk)
        v_c = p_ref[rs, v_off:v_off + vw].astype(BF16)
        v_rows = jnp.concatenate([v_c[:, h * dv:(h + 1) * dv] for h in range(heads)], axis=0)
        v_diag.append(by_head(v_c, v_head))
        qd_rows.append(by_head(qd[rs], qk_head))
        att.append(lax.dot_general(qd[rs], by_head(kd[rs], qk_head), NT_DIMS, preferred_element_type=F32))
        ds_t.append(lax.dot_general(v_rows, by_head(kst[rs], qk_head), TN_DIMS, preferred_element_type=F32))
    o_intra = [_dot(jnp.where(causal, att[ci], 0.0).astype(BF16), v_diag[ci]) for ci in range(nchunk)]
    st = st_s[...]
    o_inter = []
    for ci in range(nchunk):
        o_inter.append(lax.dot_general(qd_rows[ci], st.astype(BF16), NT_DIMS, preferred_element_type=F32))
        st = dec[ci * chunk:ci * chunk + 1, :] * st + ds_t[ci]
    st_s[...] = st
    for ci in range(nchunk):
        rs = slice(ci * chunk, (ci + 1) * chunk)
        for h in range(heads):
            o = o_intra[ci][:, h * dv:(h + 1) * dv] + o_inter[ci][h * chunk:(h + 1) * chunk]
            r = p_ref[rs, r_off + h * dv:r_off + (h + 1) * dv]
            y_ref[rs, h * dv:(h + 1) * dv] = _gla_out(o, gn, r)

    @pl.when(c == pl.num_programs(1) - 1)
    def _():
        for h in range(heads):
            s_ref[0, h] = st_s[:, h * dk:(h + 1) * dk].T


def _gla_prompt(p, wa, ba, gn, l, bsz, heads, dk, dv, rows):
    n, pw = p.shape
    t = n // bsz
    nblk = t // rows
    kern = functools.partial(_gla_prompt_kernel, heads=heads, dk=dk, dv=dv, chunk=GLA_CHUNK)
    return pl.pallas_call(
        kern,
        grid=(bsz, nblk),
        in_specs=[pl.BlockSpec((rows, pw), lambda b, c: (b * nblk + c, 0)), _layer(wa, l), _layer(ba, l),
                  _layer(gn, l)],
        out_specs=[pl.BlockSpec((rows, heads * dv), lambda b, c: (b * nblk + c, 0)),
                   pl.BlockSpec((1, heads, dk, dv), lambda b, c: (b, 0, 0, 0))],
        out_shape=[jax.ShapeDtypeStruct((n, heads * dv), F32), jax.ShapeDtypeStruct((bsz, heads, dk, dv), F32)],
        scratch_shapes=[pltpu.VMEM((dv, heads * dk), F32)],
        compiler_params=_params("parallel", "arbitrary"),
        name="gla_prompt",
    )(p, wa, ba, gn)


def _gla_sample_kernel(p_ref, s0_ref, wa_ref, ba_ref, gn_ref, y_ref, s_ref, *, heads, dk, dv, t):
    rows = p_ref.shape[0]
    qk = heads * dk
    q, k, b, bl = _gla_prologue(p_ref, wa_ref, ba_ref, t, qk)
    _, cmask = _chunk_masks(SUBLANES, t)
    v_off, r_off = 2 * qk, 2 * qk + heads * dv
    qd = (q * (dk ** -0.5) * jnp.exp(b)).astype(BF16)
    kd = (k * jnp.exp(-b)).astype(BF16)
    kst = k * jnp.exp(bl - b)
    dec = jnp.exp(bl)
    gn = gn_ref[...]
    per_tile = SUBLANES // t
    row_id = lax.broadcasted_iota(jnp.int32, (SUBLANES, 1), 0)
    att, o_state = {}, {}
    for ti in range(rows // SUBLANES):
        rs = slice(ti * SUBLANES, (ti + 1) * SUBLANES)
        dcols = []
        for j in range(per_tile):
            drow = dec[ti * SUBLANES + j * t:ti * SUBLANES + j * t + 1, :]
            dcols.append([jnp.broadcast_to(drow[:, g * LANES:(g + 1) * LANES], (LANES, LANES)).T
                          for g in range(qk // LANES)])
        for h in range(heads):
            hs = slice(h * dk, (h + 1) * dk)
            vh = p_ref[rs, v_off + h * dv:v_off + (h + 1) * dv].astype(BF16)
            att[ti, h] = lax.dot_general(qd[rs, hs], kd[rs, hs], NT_DIMS, preferred_element_type=F32)
            o = None
            for j in range(per_tile):
                bidx = ti * per_tile + j
                in_seq = (row_id >= j * t) & (row_id < (j + 1) * t)
                s0 = s0_ref[bidx, h]
                oj = jnp.where(in_seq, _dot(qd[rs, hs], s0.astype(BF16)), 0.0)
                o = oj if o is None else o + oj
                kj = jnp.where(in_seq, kst[rs, hs], 0.0).astype(BF16)
                ds = lax.dot_general(kj, vh, TN_DIMS, preferred_element_type=F32)
                g, lo = (h * dk) // LANES, (h * dk) % LANES
                dcol = dcols[j][g][lo:lo + dk, 0:dv]
                s_ref[bidx, h] = dcol * s0 + ds
            o_state[ti, h] = o
    for ti in range(rows // SUBLANES):
        rs = slice(ti * SUBLANES, (ti + 1) * SUBLANES)
        for h in range(heads):
            vh = p_ref[rs, v_off + h * dv:v_off + (h + 1) * dv].astype(BF16)
            o = _dot(jnp.where(cmask, att[ti, h], 0.0).astype(BF16), vh) + o_state[ti, h]
            r = p_ref[rs, r_off + h * dv:r_off + (h + 1) * dv]
            y_ref[rs, h * dv:(h + 1) * dv] = _gla_out(o, gn, r)


def _gla_sample(p, s0, wa, ba, gn, l, t, nb):
    n, pw = p.shape
    _, bsz, heads, dk, dv = s0.shape
    rows = nb * t
    kern = functools.partial(_gla_sample_kernel, heads=heads, dk=dk, dv=dv, t=t)
    return pl.pallas_call(
        kern,
        grid=(bsz // nb,),
        in_specs=[pl.BlockSpec((rows, pw), lambda i: (i, 0)),
                  pl.BlockSpec((None, nb, heads, dk, dv), lambda i: (l, i, 0, 0, 0)),
                  _layer(wa, l), _layer(ba, l), _layer(gn, l)],
        out_specs=[pl.BlockSpec((rows, heads * dv), lambda i: (i, 0)),
                   pl.BlockSpec((nb, heads, dk, dv), lambda i: (i, 0, 0, 0))],
        out_shape=[jax.ShapeDtypeStruct((n, heads * dv), F32), jax.ShapeDtypeStruct(s0.shape[1:], F32)],
        compiler_params=_params("parallel"),
        name="gla_sample",
    )(p, s0, wa, ba, gn)


def _first_half(shape):
    lane = lax.broadcasted_iota(jnp.int32, shape, 1)
    return (lane & (SWA_HD - 1)) < SWA_HD // 2


def _rope_table_kernel(pos_ref, inv_ref, cos_ref, sin_ref):
    ang = pos_ref[...] * inv_ref[...]
    s = jnp.sin(ang)
    cos_ref[...] = jnp.cos(ang)
    sin_ref[...] = jnp.where(_first_half(ang.shape), -s, s)


def _rope_table(pos, inv):
    rows = pos.shape[0]
    return pl.pallas_call(
        _rope_table_kernel,
        out_shape=[jax.ShapeDtypeStruct((rows, LANES), F32)] * 2,
        name="rope_table",
    )(pos.astype(F32).reshape(rows, 1), inv)


def _rope(x, cos, sin):
    half = SWA_HD // 2
    first_half = _first_half(cos.shape)
    out = []
    for g in range(x.shape[1] // LANES):
        xg = x[:, g * LANES:(g + 1) * LANES]
        swapped = jnp.where(first_half, pltpu.roll(xg, LANES - half, 1), pltpu.roll(xg, half, 1))
        out.append(xg * cos + swapped * sin)
    return out[0] if len(out) == 1 else jnp.concatenate(out, axis=1)


def _swa_prompt_kernel(sink_ref, cur_ref, cos_ref, sin_ref, y_ref, nk_ref, nv_ref, prev_s, s_s, pc_s, *, heads, l):
    i = pl.program_id(1)
    w = WINDOW
    qw = heads * SWA_HD
    kvw = SWA_KV_HEADS * SWA_HD
    gq = heads // SWA_KV_HEADS

    @pl.when(i == 0)
    def _():
        prev_s[...] = jnp.zeros_like(prev_s)

    cos, sin = cos_ref[...], sin_ref[...]
    q = (_rope(cur_ref[:, 0:qw], cos, sin) * (SWA_HD ** -0.5)).astype(BF16)
    k_cur = _rope(cur_ref[:, qw:qw + kvw], cos, sin)
    v_cur = cur_ref[:, qw + kvw:qw + 2 * kvw]
    new = [x.astype(BF16) for x in (k_cur, pltpu.roll(k_cur, SWA_HD, 1), v_cur, pltpu.roll(v_cur, SWA_HD, 1))]
    k2, k2r, v2, v2r = [jnp.concatenate([prev_s[j], new[j]], axis=0) for j in range(4)]
    zero = jnp.zeros((), BF16)
    lo2 = lax.broadcasted_iota(jnp.int32, (2 * w, LANES), 1) < SWA_HD
    lo = lax.broadcasted_iota(jnp.int32, (w, LANES), 1) < SWA_HD
    k_both = (jnp.where(lo2, k2, k2r), jnp.where(lo2, k2r, k2))
    v_half = {(0, 0): jnp.where(lo2, v2, zero), (1, 1): jnp.where(lo2, zero, v2),
              (0, 1): jnp.where(lo2, zero, v2r), (1, 0): jnp.where(lo2, v2r, zero)}
    upper = lax.broadcasted_iota(jnp.int32, (w, w), 1) > lax.broadcasted_iota(jnp.int32, (w, w), 0)
    no_prev = jnp.where(i > 0, 0.0, -jnp.inf)
    for kv in range(SWA_KV_HEADS):
        qm = []
        for h in range(kv * gq, (kv + 1) * gq):
            qg = q[:, (h // 2) * LANES:(h // 2 + 1) * LANES]
            qm.append(jnp.where(lo, qg, zero) if h % 2 == 0 else jnp.where(lo, zero, qg))
        s_s[kv * gq * w:(kv + 1) * gq * w, :] = lax.dot_general(
            jnp.concatenate(qm, axis=0), k_both[kv], NT_DIMS, preferred_element_type=F32)
    sink_term = []
    for h in range(heads):
        rows = slice(h * w, (h + 1) * w)
        sp = jnp.where(upper, s_s[rows, 0:w] + no_prev, s_s[rows, w:2 * w])
        sink = sink_ref[l, h]
        m = jnp.maximum(jnp.max(sp, axis=-1, keepdims=True), sink)
        p = jnp.exp(sp - m)
        pc_s[rows, 0:w] = jnp.where(upper, p, 0.0).astype(BF16)
        pc_s[rows, w:2 * w] = jnp.where(upper, 0.0, p).astype(BF16)
        sink_term.append(jnp.exp(sink - m))
    num = [_dot(pc_s[h * w:(h + 1) * w, :], v_half[(h // gq, h % 2)]) for h in range(heads)]
    den_all = _dot(pc_s[...], jnp.ones((2 * w, LANES), BF16))
    for g in range(heads // 2):
        he, ho = 2 * g, 2 * g + 1
        den = jnp.where(lo, den_all[he * w:(he + 1) * w] + sink_term[he], den_all[ho * w:(ho + 1) * w] + sink_term[ho])
        y_ref[:, g * LANES:(g + 1) * LANES] = (num[he] + num[ho]) * (1.0 / den)
    for j in range(4):
        prev_s[j] = new[j]

    @pl.when(i == pl.num_programs(1) - 1)
    def _():
        nk_ref[0] = k_cur
        nv_ref[0] = v_cur


def _swa_prompt(p, sinks, cos, sin, l, bsz, heads):
    n, pw = p.shape
    t = n // bsz
    nblk = t // WINDOW
    qw, kvw = heads * SWA_HD, SWA_KV_HEADS * SWA_HD
    kern = functools.partial(_swa_prompt_kernel, heads=heads, l=l)
    tab = pl.BlockSpec((WINDOW, LANES), lambda b, i: (i, 0))
    return pl.pallas_call(
        kern,
        grid=(bsz, nblk),
        in_specs=[pl.BlockSpec(memory_space=pltpu.SMEM),
                  pl.BlockSpec((WINDOW, pw), lambda b, i: (b * nblk + i, 0)), tab, tab],
        out_specs=[pl.BlockSpec((WINDOW, qw), lambda b, i: (b * nblk + i, 0)),
                   pl.BlockSpec((1, WINDOW, kvw), lambda b, i: (b, 0, 0)),
                   pl.BlockSpec((1, WINDOW, kvw), lambda b, i: (b, 0, 0))],
        out_shape=[jax.ShapeDtypeStruct((n, qw), F32), jax.ShapeDtypeStruct((bsz, WINDOW, kvw), F32),
                   jax.ShapeDtypeStruct((bsz, WINDOW, kvw), F32)],
        scratch_shapes=[pltpu.VMEM((4, WINDOW, LANES), BF16), pltpu.VMEM((heads * WINDOW, 2 * WINDOW), F32),
                        pltpu.VMEM((heads * WINDOW, 2 * WINDOW), BF16)],
        compiler_params=_params("parallel", "arbitrary"),
        name="swa_prompt",
    )(sinks, p, cos, sin)


def _expand_kv(x, lane):
    other = pltpu.roll(x, SWA_HD, 1)
    first = lane < SWA_HD
    kv0 = jnp.where(first, x, other)
    kv1 = jnp.where(first, other, x)
    return kv0, kv1


def _swa_sample_kernel(sinkrow_ref, p_ref, ck_ref, cv_ref, cos_ref, sin_ref, y_ref, nk_ref, nv_ref, *, heads, t, nb):
    w = ck_ref.shape[1]
    qw = heads * SWA_HD
    kvw = SWA_KV_HEADS * SWA_HD
    gq = heads // SWA_KV_HEADS
    per_tile = SUBLANES // t
    ncol = heads * t
    row8 = lax.broadcasted_iota(jnp.int32, (SUBLANES, 1), 0)
    cos, sin = cos_ref[...], sin_ref[...]
    lane_kv = lax.broadcasted_iota(jnp.int32, (1, kvw), 1)
    qm_row = lax.broadcasted_iota(jnp.int32, (ncol, qw), 0)
    qm_lane = lax.broadcasted_iota(jnp.int32, (ncol, qw), 1)
    qmask = lax.shift_right_logical(qm_row, _log2(t)) == lax.shift_right_logical(qm_lane, _log2(SWA_HD))
    ck_row = lax.broadcasted_iota(jnp.int32, (w, ncol), 0)
    col_step = lax.broadcasted_iota(jnp.int32, (w, ncol), 1) & (t - 1)
    cache_ok = ck_row > col_step
    nk_row = lax.broadcasted_iota(jnp.int32, (SUBLANES, ncol), 0)
    nk_col_step = lax.broadcasted_iota(jnp.int32, (SUBLANES, ncol), 1) & (t - 1)
    sink_row = sinkrow_ref[...]
    widen = lambda a, b: jnp.concatenate([a] * (gq // 2) + [b] * (gq // 2), axis=1).astype(BF16)
    body = w - SUBLANES
    new_k, new_v, scores = [], [], []
    for ti in range(nb // per_tile):
        rs = slice(ti * SUBLANES, (ti + 1) * SUBLANES)
        q8 = _rope(p_ref[rs, 0:qw], cos, sin) * (SWA_HD ** -0.5)
        kn8 = _rope(p_ref[rs, qw:qw + kvw], cos, sin)
        vn8 = p_ref[rs, qw + kvw:qw + 2 * kvw]
        kn_x = widen(*_expand_kv(kn8, lane_kv))
        new_k.append(kn8)
        new_v.append(vn8)
        for j in range(per_tile):
            bidx = ti * per_tile + j
            qj = q8[j * t:(j + 1) * t, :]
            qm = jnp.where(qmask, jnp.concatenate([qj] * heads, axis=0), 0.0).astype(BF16)
            ck = ck_ref[bidx]
            ck_x = widen(*_expand_kv(ck, lane_kv))
            s_c = lax.dot_general(ck_x, qm, NT_DIMS, preferred_element_type=F32)
            s_n = lax.dot_general(kn_x, qm, NT_DIMS, preferred_element_type=F32)
            scores.append((s_c, s_n))
            kn_tail = pltpu.roll(kn8, (SUBLANES - t - j * t) % SUBLANES, 0)
            k_shift = pltpu.roll(ck, w - t, 0)
            nk_ref[bidx, 0:body] = k_shift[0:body]
            nk_ref[bidx, body:w] = jnp.where(row8 < SUBLANES - t, k_shift[body:w], kn_tail)
    probs = []
    for bidx, (s_c, s_n) in enumerate(scores):
        j = bidx % per_tile
        s_c = jnp.where(cache_ok, s_c, -jnp.inf)
        new_ok = (nk_row >= j * t) & (nk_row - j * t <= nk_col_step) & (nk_row < (j + 1) * t)
        s_n = jnp.where(new_ok, s_n, -jnp.inf)
        m = jnp.maximum(jnp.maximum(jnp.max(s_c, axis=0, keepdims=True), jnp.max(s_n, axis=0, keepdims=True)),
                        sink_row)
        p_c = jnp.exp(s_c - m)
        p_n = jnp.exp(s_n - m)
        den = jnp.sum(p_c, axis=0, keepdims=True) + jnp.sum(p_n, axis=0, keepdims=True) + jnp.exp(sink_row - m)
        rden = 1.0 / den
        probs.append(((p_c * rden).astype(BF16), (p_n * rden).astype(BF16)))
    outs = []
    for bidx, (p_c, p_n) in enumerate(probs):
        ti, j = bidx // per_tile, bidx % per_tile
        cv = cv_ref[bidx]
        cv_x = widen(*_expand_kv(cv, lane_kv))
        vn_x = widen(*_expand_kv(new_v[ti], lane_kv))
        outs.append(lax.dot_general(p_c, cv_x, TN_DIMS, preferred_element_type=F32)
                    + lax.dot_general(p_n, vn_x, TN_DIMS, preferred_element_type=F32))
        vn_tail = pltpu.roll(new_v[ti], (SUBLANES - t - j * t) % SUBLANES, 0)
        v_shift = pltpu.roll(cv, w - t, 0)
        nv_ref[bidx, 0:body] = v_shift[0:body]
        nv_ref[bidx, body:w] = jnp.where(row8 < SUBLANES - t, v_shift[body:w], vn_tail)
    for bidx, o in enumerate(outs):
        ti, j = bidx // per_tile, bidx % per_tile
        o = jnp.where(qmask, o, 0.0)
        acc = o[0:SUBLANES]
        for g in range(1, ncol // SUBLANES):
            acc = acc + o[g * SUBLANES:(g + 1) * SUBLANES]
        for g in range(1, per_tile):
            acc = acc + pltpu.roll(acc, SUBLANES - g * t, 0)
        y_ref[ti * SUBLANES + j * t:ti * SUBLANES + (j + 1) * t, :] = acc[0:t]


def _swa_sample(p, ck, cv, sink_rows, cos, sin, l, heads, t, nb):
    n, pw = p.shape
    _, bsz, w, kvw = ck.shape
    qw = heads * SWA_HD
    kern = functools.partial(_swa_sample_kernel, heads=heads, t=t, nb=nb)
    cache = pl.BlockSpec((None, nb, w, kvw), lambda i: (l, i, 0, 0))
    return pl.pallas_call(
        kern,
        grid=(bsz // nb,),
        in_specs=[_layer(sink_rows, l), pl.BlockSpec((nb * t, pw), lambda i: (i, 0)), cache, cache,
                  _full(cos.shape), _full(sin.shape)],
        out_specs=[pl.BlockSpec((nb * t, qw), lambda i: (i, 0)),
                   pl.BlockSpec((nb, w, kvw), lambda i: (i, 0, 0)), pl.BlockSpec((nb, w, kvw), lambda i: (i, 0, 0))],
        out_shape=[jax.ShapeDtypeStruct((n, qw), F32), jax.ShapeDtypeStruct(ck.shape[1:], F32),
                   jax.ShapeDtypeStruct(cv.shape[1:], F32)],
        compiler_params=_params("parallel"),
        name="swa_sample",
    )(sink_rows, p, ck, cv, cos, sin)


FFN_HIDDEN_CHUNK = 1024


def _merge_ffn_kernel(x_ref, ya_ref, yb_ref, yc_ref, gt_ref, wb_ref, wo_ref, g_ref, wg_ref, wu_ref, wd_ref, gf_ref,
                      o_ref, *, final_norm):
    d = x_ref.shape[1]
    merged = None
    off = 0
    for i, y_ref in enumerate((ya_ref, yb_ref, yc_ref)):
        wdt = y_ref.shape[1]
        br = _dot(y_ref[...].astype(BF16), wb_ref[off:off + wdt, :])
        term = jax.nn.sigmoid(gt_ref[:, i * d:(i + 1) * d]) * br
        merged = term if merged is None else merged + term
        off += wdt
    x = x_ref[...] + _dot(merged.astype(BF16), wo_ref[...])
    hb = _rms(x, g_ref[...]).astype(BF16)
    hidden = wg_ref.shape[1]
    acc = x
    for c0 in range(0, hidden, FFN_HIDDEN_CHUNK):
        c1 = min(c0 + FFN_HIDDEN_CHUNK, hidden)
        gate = _dot(hb, wg_ref[:, c0:c1])
        up = _dot(hb, wu_ref[:, c0:c1])
        act = (gate * jax.nn.sigmoid(gate) * up).astype(BF16)
        acc = acc + _dot(act, wd_ref[c0:c1, :])
    o_ref[...] = _rms(acc, gf_ref[...]) if final_norm else acc


def _merge_ffn(x, ya, yb, yc, gates, wb, wo, g, wg, wu, wd, gf, l, tm, final_norm):
    n, d = x.shape
    row = lambda a: pl.BlockSpec((tm, a.shape[1]), lambda i: (i, 0))
    return pl.pallas_call(
        functools.partial(_merge_ffn_kernel, final_norm=final_norm),
        grid=(n // tm,),
        in_specs=[row(x), row(ya), row(yb), row(yc), row(gates), _layer(wb, l), _layer(wo, l), _layer(g, l),
                  _layer(wg, l), _layer(wu, l), _layer(wd, l), _full(gf.shape)],
        out_specs=pl.BlockSpec((tm, d), lambda i: (i, 0)),
        out_shape=jax.ShapeDtypeStruct((n, d), F32),
        compiler_params=_params("parallel"),
        name="merge_ffn",
    )(x, ya, yb, yc, gates, wb, wo, g, wg, wu, wd, gf)


def _tile(n, want):
    while n % want:
        want //= 2
    return want


def kernel(x_prompt, x_sample, state_s5_re, state_s5_im, state_gla, cache_swa_k, cache_swa_v, g_mix, w_in, s5_lambda_re, s5_lambda_im, s5_log_dt, s5_b_re, s5_b_im, s5_c_re, s5_c_im, s5_d, w_glu, b_glu, w_gla_a2, b_gla_a, g_gla_norm, swa_sinks, w_branch, w_out, g_ffn, w_ffn_gate, w_ffn_up, w_ffn_down, g_final):
    bp, tp, d = x_prompt.shape
    bs, ts, _ = x_sample.shape
    depth = w_in.shape[0]
    s5_w = s5_d.shape[1]
    groups, s5_n = s5_lambda_re.shape[1:]
    gla_qk = w_gla_a2.shape[2]
    gla_rank = w_gla_a2.shape[1]
    gla_heads, gla_dk, gla_dv = state_gla.shape[2:]
    gla_v = gla_heads * gla_dv
    swa_heads = swa_sinks.shape[1]
    swa_q = swa_heads * SWA_HD
    swa_kv = SWA_KV_HEADS * SWA_HD
    assert SWA_KV_HEADS * SWA_HD == LANES and bp % SUBLANES == 0 and bs % SUBLANES == 0 and SUBLANES % ts == 0
    assert tp % WINDOW == 0 and cache_swa_k.shape[2] == WINDOW

    o_a, o_b = 0, s5_w
    o_low = o_b + 2 * gla_qk + 2 * gla_v
    o_c = o_low + gla_rank
    o_g = o_c + swa_q + 2 * swa_kv
    widths = (s5_w, 2 * gla_qk + 2 * gla_v + LANES, swa_q + 2 * swa_kv, 3 * d)

    hp = x_prompt.reshape(bp * tp, d)
    hs = x_sample.reshape(bs * ts, d)
    inv = ROPE_THETA ** (-jnp.arange(SWA_HD // 2, dtype=F32) * (2.0 / SWA_HD))
    inv = jnp.tile(inv, 2 * LANES // SWA_HD).reshape(1, LANES)
    cos_p, sin_p = _rope_table(jnp.arange(tp, dtype=jnp.int32), inv)
    cos_s, sin_s = _rope_table(PAST_LEN + (jnp.arange(SUBLANES, dtype=jnp.int32) & (ts - 1)), inv)
    zeros_s5 = jnp.zeros((1, bp, groups * s5_n), F32)
    tm_p, tm_s = _tile(bp * tp, 256), _tile(bs * ts, 256)
    s5_steps = _tile(tp, 128)
    gla_rows = _tile(tp, 512)

    w_r = _win_prep(w_in, o_c, LANES - gla_rank, _tile(d, 256))
    wa2 = jnp.concatenate([w_gla_a2, jnp.zeros((depth, LANES - gla_rank, gla_qk), F32)], axis=1).astype(BF16)
    ba = b_gla_a.reshape(depth, 1, gla_qk)
    gn = g_gla_norm.reshape(depth, 1, gla_dv)
    ar, ai, bbr, bbi = _s5_prep(s5_lambda_re, s5_lambda_im, s5_log_dt, s5_b_re, s5_b_im)
    bre, bim, cre, cim = _s5_pack(bbr, bbi, s5_c_re, s5_c_im)
    s5_params = (ar, ai, bre, bim, cre, cim, s5_d.reshape(depth, 1, s5_w), w_glu.astype(BF16),
                 b_glu.reshape(depth, 1, s5_w))
    h0r_s = state_s5_re.reshape(depth, bs, groups * s5_n)
    h0i_s = state_s5_im.reshape(depth, bs, groups * s5_n)
    g_mix_r, g_ffn_r, gf = g_mix.reshape(depth, 1, d), g_ffn.reshape(depth, 1, d), g_final.reshape(1, d)
    wb, wo = w_branch.astype(BF16), w_out.astype(BF16)
    wg, wu, wd = w_ffn_gate.astype(BF16), w_ffn_up.astype(BF16), w_ffn_down.astype(BF16)
    sink_rows = jnp.repeat(swa_sinks, ts, axis=1).reshape(depth, 1, swa_heads * ts)
    ck = cache_swa_k.reshape(depth, bs, WINDOW, swa_kv)
    cv = cache_swa_v.reshape(depth, bs, WINDOW, swa_kv)

    outs_p = [[] for _ in range(5)]
    outs_s = [[] for _ in range(5)]
    for l in range(depth):
        last = l == depth - 1

        pa, pb, pc, pg = _inproj(hp, g_mix_r, w_r, l, widths, tm_p)
        ya, hr, hi = _s5(pa.reshape(bp, tp, s5_w), zeros_s5, zeros_s5, 0, s5_params, l, s5_steps)
        ya = ya.reshape(bp * tp, s5_w)
        yb, sg = _gla_prompt(pb, wa2, ba, gn, l, bp, gla_heads, gla_dk, gla_dv, gla_rows)
        yc, nk, nv = _swa_prompt(pc, swa_sinks, cos_p, sin_p, l, bp, swa_heads)
        hp = _merge_ffn(hp, ya, yb, yc, pg, wb, wo, g_ffn_r, wg, wu, wd, gf, l, tm_p, last)
        for lst, val in zip(outs_p, (hr.reshape(bp, groups, s5_n), hi.reshape(bp, groups, s5_n), sg,
                                     nk.reshape(bp, WINDOW, SWA_KV_HEADS, SWA_HD),
                                     nv.reshape(bp, WINDOW, SWA_KV_HEADS, SWA_HD))):
            lst.append(val)

        pa, pb, pc, pg = _inproj(hs, g_mix_r, w_r, l, widths, tm_s)
        ya, hr, hi = _s5(pa, h0r_s, h0i_s, l, s5_params, l, ts)
        yb, sg = _gla_sample(pb, state_gla, wa2, ba, gn, l, ts, _tile(bs, 16))
        yc, nk, nv = _swa_sample(pc, ck, cv, sink_rows, cos_s, sin_s, l, swa_heads, ts, _tile(bs, 8))
        hs = _merge_ffn(hs, ya, yb, yc, pg, wb, wo, g_ffn_r, wg, wu, wd, gf, l, tm_s, last)
        for lst, val in zip(outs_s, (hr.reshape(bs, groups, s5_n), hi.reshape(bs, groups, s5_n), sg,
                                     nk.reshape(bs, WINDOW, SWA_KV_HEADS, SWA_HD),
                                     nv.reshape(bs, WINDOW, SWA_KV_HEADS, SWA_HD))):
            lst.append(val)

    return (hp.reshape(bp, tp, d), hs.reshape(bs, ts, d),
            *[jnp.stack(o) for o in outs_p], *[jnp.stack(o) for o in outs_s])
```

```python
import functools

import jax
import jax.numpy as jnp
from jax import lax
from jax.experimental import pallas as pl
from jax.experimental.pallas import tpu as pltpu

F32 = jnp.float32
BF16 = jnp.bfloat16

EPS = 1e-6
GLA_TAU = 16.0
GLA_CHUNK = 64
SWA_KV_HEADS = 2
SWA_HD = 64
WINDOW = 128
ROPE_THETA = 10000.0
PAST_LEN = 16384

SUBLANES = 8
LANES = 128
VMEM_LIMIT = 56 * 1024 * 1024

NT_DIMS = (((1,), (1,)), ((), ()))
TN_DIMS = (((0,), (0,)), ((), ()))


def _params(*sem):
    return pltpu.CompilerParams(dimension_semantics=sem, vmem_limit_bytes=VMEM_LIMIT)


def _dot(a, b):
    return jnp.dot(a, b, preferred_element_type=F32)


def _rms(x, g):
    return x * lax.rsqrt(jnp.mean(x * x, axis=-1, keepdims=True) + EPS) * g


def _full(shape):
    nd = len(shape)
    return pl.BlockSpec(shape, lambda *_: (0,) * nd)


def _layer(a, l):
    nd = a.ndim - 1
    return pl.BlockSpec((None,) + a.shape[1:], lambda *_: (l,) + (0,) * nd, pipeline_mode=pl.Buffered(1))


def _inproj_kernel(x_ref, g_ref, wa_ref, wb_ref, *o_refs):
    xb = _rms(x_ref[...], g_ref[...]).astype(BF16)
    for w_ref, group in ((wa_ref, o_refs[:2]), (wb_ref, o_refs[2:])):
        off = 0
        for o_ref in group:
            n = o_ref.shape[-1]
            for c0 in range(0, n, 512):
                c1 = min(c0 + 512, n)
                o_ref[:, c0:c1] = lax.dot_general(xb, w_ref[off + c0:off + c1, :], NT_DIMS,
                                                  preferred_element_type=F32)
            off += n


def _inproj(x, g, wa, wb, l, widths, tm):
    n, d = x.shape
    assert wa.shape[1] == sum(widths[:2]) and wb.shape[1] == sum(widths[2:])
    return pl.pallas_call(
        _inproj_kernel,
        grid=(n // tm,),
        in_specs=[pl.BlockSpec((tm, d), lambda i: (i, 0)), _layer(g, l), _layer(wa, l), _layer(wb, l)],
        out_specs=[pl.BlockSpec((tm, wd), lambda i: (i, 0)) for wd in widths],
        out_shape=[jax.ShapeDtypeStruct((n, wd), F32) for wd in widths],
        compiler_params=_params("parallel"),
        name="inproj",
    )(x, g, wa, wb)


def _s5_prep_kernel(lr_ref, li_ref, ldt_ref, br_ref, bi_ref, ar_ref, ai_ref, bbr_ref, bbi_ref):
    lr, li = lr_ref[...], li_ref[...]
    dt = jnp.exp(ldt_ref[...])
    mag = jnp.exp(lr * dt)
    ar = mag * jnp.cos(li * dt)
    ai = mag * jnp.sin(li * dt)
    den = lr * lr + li * li
    zr, zi = ar - 1.0, ai
    er = (zr * lr + zi * li) / den
    ei = (zi * lr - zr * li) / den
    ar_ref[...] = ar
    ai_ref[...] = ai
    br, bi = br_ref[...], bi_ref[...]
    bbr_ref[...] = er[None] * br - ei[None] * bi
    bbi_ref[...] = er[None] * bi + ei[None] * br


def _s5_prep(lam_re, lam_im, log_dt, b_re, b_im):
    depth, g, c, n = b_re.shape
    dg = depth * g
    brt = jnp.transpose(b_re.reshape(dg, c, n), (1, 0, 2))
    bit = jnp.transpose(b_im.reshape(dg, c, n), (1, 0, 2))
    ar, ai, bbr, bbi = pl.pallas_call(
        _s5_prep_kernel,
        out_shape=[jax.ShapeDtypeStruct((dg, n), F32)] * 2 + [jax.ShapeDtypeStruct((c, dg, n), F32)] * 2,
        name="s5_prep",
    )(lam_re.reshape(dg, n), lam_im.reshape(dg, n), log_dt.reshape(dg, 1), brt, bit)
    unt = lambda a: jnp.transpose(a, (1, 0, 2)).reshape(depth, g, c, n)
    return ar.reshape(depth, 1, g * n), ai.reshape(depth, 1, g * n), unt(bbr), unt(bbi)


def _s5_pack(bbr, bbi, c_re, c_im):
    depth, g, c, n = bbr.shape
    gp = LANES // c
    ks = g // gp
    eye = jnp.eye(gp, dtype=F32)

    def pack_b(b):
        return jnp.einsum("lkgcn,gh->lkgchn", b.reshape(depth, ks, gp, c, n), eye).reshape(
            depth, ks, gp * c, gp * n).astype(BF16)

    def pack_c(cm):
        return jnp.einsum("lkgnc,gh->lkgnhc", cm.reshape(depth, ks, gp, n, c), eye).reshape(
            depth, ks, gp * n, gp * c).astype(BF16)

    return pack_b(bbr), pack_b(bbi), pack_c(c_re), pack_c(c_im)


S5_COL_BLOCK = 512


def _s5_kernel(u_ref, h0r_ref, h0i_ref, ar_ref, ai_ref, bre_ref, bim_ref, cre_ref, cim_ref, d_ref, wglu_ref,
               bglu_ref, y_ref, hr_out, hi_out, us_s, ys_s, xr_s, xi_s, hr_s, hi_s, *tmp_s, nb, steps):
    c = pl.program_id(0)

    @pl.when(c == 0)
    def _():
        hr_s[...] = h0r_ref[...]
        hi_s[...] = h0i_ref[...]

    ks, kw, nw = bre_ref.shape
    by_sequence = u_ref.ndim == 3
    if by_sequence:
        for b in range(nb):
            for k in range(ks):
                us_s[k, pl.ds(b, steps, stride=nb), :] = u_ref[b, :, k * kw:(k + 1) * kw]
    else:
        for k in range(ks):
            tmp_s[0][k] = u_ref[:, k * kw:(k + 1) * kw]
        for t in range(steps):
            for k in range(ks):
                us_s[k, t * nb:(t + 1) * nb, :] = tmp_s[0][k, pl.ds(t, nb, stride=steps), :]

    for k in range(ks):
        uk = us_s[k].astype(BF16)
        xr_s[:, k * nw:(k + 1) * nw] = _dot(uk, bre_ref[k])
        xi_s[:, k * nw:(k + 1) * nw] = _dot(uk, bim_ref[k])

    ncols = xr_s.shape[1]
    for rb in range(nb // SUBLANES):
        rows = slice(rb * SUBLANES, (rb + 1) * SUBLANES)
        for cb in range(ncols // S5_COL_BLOCK):
            cs = slice(cb * S5_COL_BLOCK, (cb + 1) * S5_COL_BLOCK)
            a_r = jnp.broadcast_to(ar_ref[:, cs], (SUBLANES, S5_COL_BLOCK))
            a_i = jnp.broadcast_to(ai_ref[:, cs], (SUBLANES, S5_COL_BLOCK))

            def step(t, carry, rb=rb, cs=cs, a_r=a_r, a_i=a_i):
                hr, hi = carry
                r0 = pl.multiple_of(t * nb + rb * SUBLANES, SUBLANES)
                nhr = a_r * hr - a_i * hi + xr_s[pl.ds(r0, SUBLANES), cs]
                nhi = a_r * hi + a_i * hr + xi_s[pl.ds(r0, SUBLANES), cs]
                xr_s[pl.ds(r0, SUBLANES), cs] = nhr
                xi_s[pl.ds(r0, SUBLANES), cs] = nhi
                return nhr, nhi

            hr, hi = lax.fori_loop(0, steps, step, (hr_s[rows, cs], hi_s[rows, cs]), unroll=min(steps, 8))
            hr_s[rows, cs] = hr
            hi_s[rows, cs] = hi

    for k in range(ks):
        ss = slice(k * nw, (k + 1) * nw)
        yk = _dot(xr_s[:, ss].astype(BF16), cre_ref[k]) - _dot(xi_s[:, ss].astype(BF16), cim_ref[k])
        ys_s[k] = jax.nn.gelu(yk + d_ref[:, k * kw:(k + 1) * kw] * us_s[k])
    y = jnp.concatenate([ys_s[k] for k in range(ks)], axis=1)
    y = y * jax.nn.sigmoid(_dot(y.astype(BF16), wglu_ref[...]) + bglu_ref[...])
    for k in range(ks):
        ys_s[k] = y[:, k * kw:(k + 1) * kw]
    if by_sequence:
        for b in range(nb):
            for k in range(ks):
                y_ref[b, :, k * kw:(k + 1) * kw] = ys_s[k, pl.ds(b, steps, stride=nb), :]
    else:
        for t in range(steps):
            for k in range(ks):
                tmp_s[0][k, pl.ds(t, nb, stride=steps), :] = ys_s[k, t * nb:(t + 1) * nb, :]
        for k in range(ks):
            y_ref[:, k * kw:(k + 1) * kw] = tmp_s[0][k]

    @pl.when(c == pl.num_programs(0) - 1)
    def _():
        hr_out[...] = hr_s[...]
        hi_out[...] = hi_s[...]


def _s5(u, h0r, h0i, l0, params, l, steps):
    _, nb, ns = h0r.shape
    w = u.shape[-1]
    blk = steps * nb
    ks = params[2].shape[1]
    if u.ndim == 3:
        grid = (u.shape[1] // steps,)
        u_spec = pl.BlockSpec((nb, steps, w), lambda i: (0, i, 0))
        tmp = []
    else:
        assert u.shape[0] == blk
        grid = (1,)
        u_spec = _full(u.shape)
        tmp = [pltpu.VMEM((ks, blk, LANES), F32)]
    kern = functools.partial(_s5_kernel, nb=nb, steps=steps)
    return pl.pallas_call(
        kern,
        grid=grid,
        in_specs=[u_spec, _layer(h0r, l0), _layer(h0i, l0)] + [_layer(a, l) for a in params],
        out_specs=[u_spec, _full((nb, ns)), _full((nb, ns))],
        out_shape=[jax.ShapeDtypeStruct(u.shape, F32), jax.ShapeDtypeStruct((nb, ns), F32),
                   jax.ShapeDtypeStruct((nb, ns), F32)],
        scratch_shapes=[pltpu.VMEM((ks, blk, LANES), F32), pltpu.VMEM((ks, blk, LANES), F32),
                        pltpu.VMEM((blk, ns), F32), pltpu.VMEM((blk, ns), F32), pltpu.VMEM((nb, ns), F32),
                        pltpu.VMEM((nb, ns), F32)] + tmp,
        compiler_params=_params("arbitrary"),
        name="s5_scan",
    )(u, h0r, h0i, *params)


def _log2(n):
    assert n & (n - 1) == 0, n
    return n.bit_length() - 1


def _chunk_masks(rows, chunk):
    sh = _log2(chunk)
    rr = lax.broadcasted_iota(jnp.int32, (rows, rows), 0)
    cc = lax.broadcasted_iota(jnp.int32, (rows, rows), 1)
    same = lax.shift_right_logical(rr, sh) == lax.shift_right_logical(cc, sh)
    return same, same & (cc <= rr)


def _gla_prologue(p_ref, wa_ref, ba_ref, chunk, qk):
    rows = p_ref.shape[0]
    q = p_ref[:, 0:qk]
    k = p_ref[:, qk:2 * qk]
    alow = p_ref[:, p_ref.shape[1] - LANES:].astype(BF16)
    z = _dot(alow, wa_ref[...]) + ba_ref[...]
    la = (jnp.minimum(z, 0.0) - jnp.log(1.0 + jnp.exp(-jnp.abs(z)))) * (1.0 / GLA_TAU)
    same, causal = _chunk_masks(rows, chunk)
    tri = jnp.where(causal, 1.0, 0.0).astype(BF16)
    blk = jnp.where(same, 1.0, 0.0).astype(BF16)
    la_hi = la.astype(BF16)
    la_lo = (la - la_hi.astype(F32)).astype(BF16)
    b = _dot(tri, la_hi) + _dot(tri, la_lo)
    bl = _dot(blk, la_hi) + _dot(blk, la_lo)
    return q, k, b, bl


def _gla_out(o, gn, r):
    on = o * lax.rsqrt(jnp.mean(o * o, axis=-1, keepdims=True) + EPS) * gn
    return on * (r * jax.nn.sigmoid(r))


def _gla_prompt_kernel(p_ref, wa_ref, ba_ref, gn_ref, y_ref, s_ref, st_s, *, heads, dk, dv, chunk):
    c = pl.program_id(1)

    @pl.when(c == 0)
    def _():
        st_s[...] = jnp.zeros_like(st_s)

    rows = p_ref.shape[0]
    qk, vw = heads * dk, heads * dv
    q, k, b, bl = _gla_prologue(p_ref, wa_ref, ba_ref, chunk, qk)
    v_off, r_off = 2 * qk, 2 * qk + vw
    qd = (q * (dk ** -0.5) * jnp.exp(b)).astype(BF16)
    kd = (k * jnp.exp(-b)).astype(BF16)
    kst = (k * jnp.exp(bl - b)).astype(BF16)
    dec = jnp.exp(bl)
    gn = gn_ref[...]
    zero = jnp.zeros((), BF16)
    qk_head = lax.shift_right_logical(lax.broadcasted_iota(jnp.int32, (chunk, qk), 1), _log2(dk))
    v_head = lax.shift_right_logical(lax.broadcasted_iota(jnp.int32, (chunk, vw), 1), _log2(dv))
    rr = lax.broadcasted_iota(jnp.int32, (chunk, heads * chunk), 0)
    cc = lax.broadcasted_iota(jnp.int32, (chunk, heads * chunk), 1) & (chunk - 1)
    causal = cc <= rr
    nchunk = rows // chunk
    by_head = lambda x, head: jnp.concatenate([jnp.where(head == h, x, zero) for h in range(heads)], axis=0)
    att, ds_t, v_diag, qd_rows = [], [], [], []
    for ci in range(nchunk):
        rs = slice(ci * chunk, (ci + 1) * chunk)
        v_c = p_ref[rs, v_off:v_off + vw].astype(BF16)
        v_rows = jnp.concatenate([v_c[:, h * dv:(h + 1) * dv] for h in range(heads)], axis=0)
        v_diag.append(by_head(v_c, v_head))
        qd_rows.append(by_head(qd[rs], qk_head))
        att.append(lax.dot_general(qd[rs], by_head(kd[rs], qk_head), NT_DIMS, preferred_element_type=F32))
        ds_t.append(lax.dot_general(v_rows, by_head(kst[rs], qk_head), TN_DIMS, preferred_element_type=F32))
    o_intra = [_dot(jnp.where(causal, att[ci], 0.0).astype(BF16), v_diag[ci]) for ci in range(nchunk)]
    st = st_s[...]
    o_inter = []
    for ci in range(nchunk):
        o_inter.append(lax.dot_general(qd_rows[ci], st.astype(BF16), NT_DIMS, preferred_element_type=F32))
        st = dec[ci * chunk:ci * chunk + 1, :] * st + ds_t[ci]
    st_s[...] = st
    for ci in range(nchunk):
        rs = slice(ci * chunk, (ci + 1) * chunk)
        for h in range(heads):
            o = o_intra[ci][:, h * dv:(h + 1) * dv] + o_inter[ci][h * chunk:(h + 1) * chunk]
            r = p_ref[rs, r_off + h * dv:r_off + (h + 1) * dv]
            y_ref[rs, h * dv:(h + 1) * dv] = _gla_out(o, gn, r)

    @pl.when(c == pl.num_programs(1) - 1)
    def _():
        for h in range(heads):
            s_ref[0, h] = st_s[:, h * dk:(h + 1) * dk].T


def _gla_prompt(p, wa, ba, gn, l, bsz, heads, dk, dv, rows):
    n, pw = p.shape
    t = n // bsz
    nblk = t // rows
    kern = functools.partial(_gla_prompt_kernel, heads=heads, dk=dk, dv=dv, chunk=GLA_CHUNK)
    return pl.pallas_call(
        kern,
        grid=(bsz, nblk),
        in_specs=[pl.BlockSpec((rows, pw), lambda b, c: (b * nblk + c, 0)), _layer(wa, l), _layer(ba, l),
                  _layer(gn, l)],
        out_specs=[pl.BlockSpec((rows, heads * dv), lambda b, c: (b * nblk + c, 0)),
                   pl.BlockSpec((1, heads, dk, dv), lambda b, c: (b, 0, 0, 0))],
        out_shape=[jax.ShapeDtypeStruct((n, heads * dv), F32), jax.ShapeDtypeStruct((bsz, heads, dk, dv), F32)],
        scratch_shapes=[pltpu.VMEM((dv, heads * dk), F32)],
        compiler_params=_params("parallel", "arbitrary"),
        name="gla_prompt",
    )(p, wa, ba, gn)


def _gla_sample_kernel(p_ref, s0_ref, wa_ref, ba_ref, gn_ref, y_ref, s_ref, *, heads, dk, dv, t):
    rows = p_ref.shape[0]
    qk = heads * dk
    q, k, b, bl = _gla_prologue(p_ref, wa_ref, ba_ref, t, qk)
    _, cmask = _chunk_masks(SUBLANES, t)
    v_off, r_off = 2 * qk, 2 * qk + heads * dv
    qd = (q * (dk ** -0.5) * jnp.exp(b)).astype(BF16)
    kd = (k * jnp.exp(-b)).astype(BF16)
    kst = k * jnp.exp(bl - b)
    dec = jnp.exp(bl)
    gn = gn_ref[...]
    per_tile = SUBLANES // t
    row_id = lax.broadcasted_iota(jnp.int32, (SUBLANES, 1), 0)
    att, o_state = {}, {}
    for ti in range(rows // SUBLANES):
        rs = slice(ti * SUBLANES, (ti + 1) * SUBLANES)
        dcols = []
        for j in range(per_tile):
            drow = dec[ti * SUBLANES + j * t:ti * SUBLANES + j * t + 1, :]
            dcols.append([jnp.broadcast_to(drow[:, g * LANES:(g + 1) * LANES], (LANES, LANES)).T
                          for g in range(qk // LANES)])
        for h in range(heads):
            hs = slice(h * dk, (h + 1) * dk)
            vh = p_ref[rs, v_off + h * dv:v_off + (h + 1) * dv].astype(BF16)
            att[ti, h] = lax.dot_general(qd[rs, hs], kd[rs, hs], NT_DIMS, preferred_element_type=F32)
            o = None
            for j in range(per_tile):
                bidx = ti * per_tile + j
                in_seq = (row_id >= j * t) & (row_id < (j + 1) * t)
                s0 = s0_ref[bidx, h]
                oj = jnp.where(in_seq, _dot(qd[rs, hs], s0.astype(BF16)), 0.0)
                o = oj if o is None else o + oj
                kj = jnp.where(in_seq, kst[rs, hs], 0.0).astype(BF16)
                ds = lax.dot_general(kj, vh, TN_DIMS, preferred_element_type=F32)
                g, lo = (h * dk) // LANES, (h * dk) % LANES
                dcol = dcols[j][g][lo:lo + dk, 0:dv]
                s_ref[bidx, h] = dcol * s0 + ds
            o_state[ti, h] = o
    for ti in range(rows // SUBLANES):
        rs = slice(ti * SUBLANES, (ti + 1) * SUBLANES)
        for h in range(heads):
            vh = p_ref[rs, v_off + h * dv:v_off + (h + 1) * dv].astype(BF16)
            o = _dot(jnp.where(cmask, att[ti, h], 0.0).astype(BF16), vh) + o_state[ti, h]
            r = p_ref[rs, r_off + h * dv:r_off + (h + 1) * dv]
            y_ref[rs, h * dv:(h + 1) * dv] = _gla_out(o, gn, r)


def _gla_sample(p, s0, wa, ba, gn, l, t, nb):
    n, pw = p.shape
    _, bsz, heads, dk, dv = s0.shape
    rows = nb * t
    kern = functools.partial(_gla_sample_kernel, heads=heads, dk=dk, dv=dv, t=t)
    return pl.pallas_call(
        kern,
        grid=(bsz // nb,),
        in_specs=[pl.BlockSpec((rows, pw), lambda i: (i, 0)),
                  pl.BlockSpec((None, nb, heads, dk, dv), lambda i: (l, i, 0, 0, 0)),
                  _layer(wa, l), _layer(ba, l), _layer(gn, l)],
        out_specs=[pl.BlockSpec((rows, heads * dv), lambda i: (i, 0)),
                   pl.BlockSpec((nb, heads, dk, dv), lambda i: (i, 0, 0, 0))],
        out_shape=[jax.ShapeDtypeStruct((n, heads * dv), F32), jax.ShapeDtypeStruct(s0.shape[1:], F32)],
        compiler_params=_params("parallel"),
        name="gla_sample",
    )(p, s0, wa, ba, gn)


def _first_half(shape):
    lane = lax.broadcasted_iota(jnp.int32, shape, 1)
    return (lane & (SWA_HD - 1)) < SWA_HD // 2


def _rope_table_kernel(pos_ref, inv_ref, cos_ref, sin_ref):
    ang = pos_ref[...] * inv_ref[...]
    s = jnp.sin(ang)
    cos_ref[...] = jnp.cos(ang)
    sin_ref[...] = jnp.where(_first_half(ang.shape), -s, s)


def _rope_table(pos, inv):
    rows = pos.shape[0]
    return pl.pallas_call(
        _rope_table_kernel,
        out_shape=[jax.ShapeDtypeStruct((rows, LANES), F32)] * 2,
        name="rope_table",
    )(pos.astype(F32).reshape(rows, 1), inv)


def _rope(x, cos, sin):
    half = SWA_HD // 2
    first_half = _first_half(cos.shape)
    out = []
    for g in range(x.shape[1] // LANES):
        xg = x[:, g * LANES:(g + 1) * LANES]
        swapped = jnp.where(first_half, pltpu.roll(xg, LANES - half, 1), pltpu.roll(xg, half, 1))
        out.append(xg * cos + swapped * sin)
    return out[0] if len(out) == 1 else jnp.concatenate(out, axis=1)


def _swa_prompt_kernel(sink_ref, cur_ref, cos_ref, sin_ref, y_ref, nk_ref, nv_ref, prev_s, s_s, pc_s, *, heads, l):
    i = pl.program_id(1)
    w = WINDOW
    qw = heads * SWA_HD
    kvw = SWA_KV_HEADS * SWA_HD
    gq = heads // SWA_KV_HEADS

    @pl.when(i == 0)
    def _():
        prev_s[...] = jnp.zeros_like(prev_s)

    cos, sin = cos_ref[...], sin_ref[...]
    q = (_rope(cur_ref[:, 0:qw], cos, sin) * (SWA_HD ** -0.5)).astype(BF16)
    k_cur = _rope(cur_ref[:, qw:qw + kvw], cos, sin)
    v_cur = cur_ref[:, qw + kvw:qw + 2 * kvw]
    new = [x.astype(BF16) for x in (k_cur, pltpu.roll(k_cur, SWA_HD, 1), v_cur, pltpu.roll(v_cur, SWA_HD, 1))]
    k2, k2r, v2, v2r = [jnp.concatenate([prev_s[j], new[j]], axis=0) for j in range(4)]
    zero = jnp.zeros((), BF16)
    lo2 = lax.broadcasted_iota(jnp.int32, (2 * w, LANES), 1) < SWA_HD
    lo = lax.broadcasted_iota(jnp.int32, (w, LANES), 1) < SWA_HD
    k_both = (jnp.where(lo2, k2, k2r), jnp.where(lo2, k2r, k2))
    v_half = {(0, 0): jnp.where(lo2, v2, zero), (1, 1): jnp.where(lo2, zero, v2),
              (0, 1): jnp.where(lo2, zero, v2r), (1, 0): jnp.where(lo2, v2r, zero)}
    upper = lax.broadcasted_iota(jnp.int32, (w, w), 1) > lax.broadcasted_iota(jnp.int32, (w, w), 0)
    no_prev = jnp.where(i > 0, 0.0, -jnp.inf)
    for kv in range(SWA_KV_HEADS):
        qm = []
        for h in range(kv * gq, (kv + 1) * gq):
            qg = q[:, (h // 2) * LANES:(h // 2 + 1) * LANES]
            qm.append(jnp.where(lo, qg, zero) if h % 2 == 0 else jnp.where(lo, zero, qg))
        s_s[kv * gq * w:(kv + 1) * gq * w, :] = lax.dot_general(
            jnp.concatenate(qm, axis=0), k_both[kv], NT_DIMS, preferred_element_type=F32)
    sink_term = []
    for h in range(heads):
        rows = slice(h * w, (h + 1) * w)
        sp = jnp.where(upper, s_s[rows, 0:w] + no_prev, s_s[rows, w:2 * w])
        sink = sink_ref[l, h]
        m = jnp.maximum(jnp.max(sp, axis=-1, keepdims=True), sink)
        p = jnp.exp(sp - m)
        pc_s[rows, 0:w] = jnp.where(upper, p, 0.0).astype(BF16)
        pc_s[rows, w:2 * w] = jnp.where(upper, 0.0, p).astype(BF16)
        sink_term.append(jnp.exp(sink - m))
    num = [_dot(pc_s[h * w:(h + 1) * w, :], v_half[(h // gq, h % 2)]) for h in range(heads)]
    den_all = _dot(pc_s[...], jnp.ones((2 * w, LANES), BF16))
    for g in range(heads // 2):
        he, ho = 2 * g, 2 * g + 1
        den = jnp.where(lo, den_all[he * w:(he + 1) * w] + sink_term[he], den_all[ho * w:(ho + 1) * w] + sink_term[ho])
        y_ref[:, g * LANES:(g + 1) * LANES] = (num[he] + num[ho]) * (1.0 / den)
    for j in range(4):
        prev_s[j] = new[j]

    @pl.when(i == pl.num_programs(1) - 1)
    def _():
        nk_ref[0] = k_cur
        nv_ref[0] = v_cur


def _swa_prompt(p, sinks, cos, sin, l, bsz, heads):
    n, pw = p.shape
    t = n // bsz
    nblk = t // WINDOW
    qw, kvw = heads * SWA_HD, SWA_KV_HEADS * SWA_HD
    kern = functools.partial(_swa_prompt_kernel, heads=heads, l=l)
    tab = pl.BlockSpec((WINDOW, LANES), lambda b, i: (i, 0))
    return pl.pallas_call(
        kern,
        grid=(bsz, nblk),
        in_specs=[pl.BlockSpec(memory_space=pltpu.SMEM),
                  pl.BlockSpec((WINDOW, pw), lambda b, i: (b * nblk + i, 0)), tab, tab],
        out_specs=[pl.BlockSpec((WINDOW, qw), lambda b, i: (b * nblk + i, 0)),
                   pl.BlockSpec((1, WINDOW, kvw), lambda b, i: (b, 0, 0)),
                   pl.BlockSpec((1, WINDOW, kvw), lambda b, i: (b, 0, 0))],
        out_shape=[jax.ShapeDtypeStruct((n, qw), F32), jax.ShapeDtypeStruct((bsz, WINDOW, kvw), F32),
                   jax.ShapeDtypeStruct((bsz, WINDOW, kvw), F32)],
        scratch_shapes=[pltpu.VMEM((4, WINDOW, LANES), BF16), pltpu.VMEM((heads * WINDOW, 2 * WINDOW), F32),
                        pltpu.VMEM((heads * WINDOW, 2 * WINDOW), BF16)],
        compiler_params=_params("parallel", "arbitrary"),
        name="swa_prompt",
    )(sinks, p, cos, sin)


def _expand_kv(x, lane):
    other = pltpu.roll(x, SWA_HD, 1)
    first = lane < SWA_HD
    kv0 = jnp.where(first, x, other)
    kv1 = jnp.where(first, other, x)
    return kv0, kv1


def _swa_sample_kernel(sinkcol_ref, p_ref, ck_ref, cv_ref, cos_ref, sin_ref, y_ref, nk_ref, nv_ref, *, heads, t, nb):
    w = ck_ref.shape[3]
    qw = heads * SWA_HD
    kvw = SWA_KV_HEADS * SWA_HD
    gq = heads // SWA_KV_HEADS
    per_tile = SUBLANES // t
    nrow = heads * t
    cos, sin = cos_ref[...], sin_ref[...]
    lane_kv = lax.broadcasted_iota(jnp.int32, (1, kvw), 1)
    qm_row = lax.broadcasted_iota(jnp.int32, (nrow, qw), 0)
    qm_lane = lax.broadcasted_iota(jnp.int32, (nrow, qw), 1)
    qmask = lax.shift_right_logical(qm_row, _log2(t)) == lax.shift_right_logical(qm_lane, _log2(SWA_HD))
    row_step = lax.broadcasted_iota(jnp.int32, (nrow, w), 0) & (t - 1)
    cache_ok = lax.broadcasted_iota(jnp.int32, (nrow, w), 1) > row_step
    new_col = lax.broadcasted_iota(jnp.int32, (nrow, SUBLANES), 1)
    new_row_step = lax.broadcasted_iota(jnp.int32, (nrow, SUBLANES), 0) & (t - 1)
    sink_col = sinkcol_ref[...]
    lane_w = lax.broadcasted_iota(jnp.int32, (SWA_HD, w), 1)
    widen = lambda a, b: jnp.concatenate([a] * (gq // 2) + [b] * (gq // 2), axis=1).astype(BF16)
    per_head = lambda c: jnp.concatenate([c[kv] for kv in range(SWA_KV_HEADS) for _ in range(gq)],
                                         axis=0).astype(BF16)
    pad_rows = jnp.zeros((LANES - 2 * SUBLANES, kvw), F32)
    new_k, new_v, scores = [], [], []
    for ti in range(nb // per_tile):
        rs = slice(ti * SUBLANES, (ti + 1) * SUBLANES)
        q8 = _rope(p_ref[rs, 0:qw], cos, sin) * (SWA_HD ** -0.5)
        kn8 = _rope(p_ref[rs, qw:qw + kvw], cos, sin)
        vn8 = p_ref[rs, qw + kvw:qw + 2 * kvw]
        kn_x = widen(*_expand_kv(kn8, lane_kv))
        new_k.append(kn8)
        new_v.append(vn8)
        for j in range(per_tile):
            bidx = ti * per_tile + j
            qj = q8[j * t:(j + 1) * t, :]
            qm = jnp.where(qmask, jnp.concatenate([qj] * heads, axis=0), 0.0).astype(BF16)
            s_c = _dot(qm, per_head(ck_ref[bidx]))
            s_n = lax.dot_general(qm, kn_x, NT_DIMS, preferred_element_type=F32)
            scores.append((s_c, s_n))
    probs = []
    for bidx, (s_c, s_n) in enumerate(scores):
        j = bidx % per_tile
        s_c = jnp.where(cache_ok, s_c, -jnp.inf)
        new_ok = (new_col >= j * t) & (new_col - j * t <= new_row_step) & (new_col < (j + 1) * t)
        s_n = jnp.where(new_ok, s_n, -jnp.inf)
        m = jnp.maximum(jnp.maximum(jnp.max(s_c, axis=1, keepdims=True), jnp.max(s_n, axis=1, keepdims=True)),
                        sink_col)
        p_c = jnp.exp(s_c - m)
        p_n = jnp.exp(s_n - m)
        den = jnp.sum(p_c, axis=1, keepdims=True) + jnp.sum(p_n, axis=1, keepdims=True) + jnp.exp(sink_col - m)
        rden = 1.0 / den
        probs.append(((p_c * rden).astype(BF16), (p_n * rden).astype(BF16)))
    outs = []
    for bidx, (p_c, p_n) in enumerate(probs):
        ti, j = bidx // per_tile, bidx % per_tile
        vn_x = widen(*_expand_kv(new_v[ti], lane_kv))
        outs.append(lax.dot_general(p_c, per_head(cv_ref[bidx]), NT_DIMS, preferred_element_type=F32)
                    + _dot(p_n, vn_x))
    for bidx, o in enumerate(outs):
        ti, j = bidx // per_tile, bidx % per_tile
        o = jnp.where(qmask, o, 0.0)
        acc = o[0:SUBLANES]
        for g in range(1, nrow // SUBLANES):
            acc = acc + o[g * SUBLANES:(g + 1) * SUBLANES]
        for g in range(1, per_tile):
            acc = acc + pltpu.roll(acc, SUBLANES - g * t, 0)
        y_ref[ti * SUBLANES + j * t:ti * SUBLANES + (j + 1) * t, :] = acc[0:t]
    for ti in range(nb // per_tile):
        kv_t = jnp.concatenate([new_k[ti], new_v[ti], pad_rows], axis=0).T
        for j in range(per_tile):
            bidx = ti * per_tile + j
            for c_ref, n_ref, col0 in ((ck_ref, nk_ref, 0), (cv_ref, nv_ref, SUBLANES)):
                for kv in range(SWA_KV_HEADS):
                    tail = pltpu.roll(kv_t[kv * SWA_HD:(kv + 1) * SWA_HD, :], w - t - col0 - j * t, 1)
                    n_ref[bidx, kv] = jnp.where(lane_w < w - t, pltpu.roll(c_ref[bidx, kv], w - t, 1), tail)


def _swa_sample(p, ck, cv, sink_cols, cos, sin, l, heads, t, nb):
    n, pw = p.shape
    _, bsz, kvh, hd, w = ck.shape
    qw = heads * SWA_HD
    kern = functools.partial(_swa_sample_kernel, heads=heads, t=t, nb=nb)
    cache = pl.BlockSpec((None, nb, kvh, hd, w), lambda i: (l, i, 0, 0, 0))
    new_cache = pl.BlockSpec((nb, kvh, hd, w), lambda i: (i, 0, 0, 0))
    return pl.pallas_call(
        kern,
        grid=(bsz // nb,),
        in_specs=[_layer(sink_cols, l), pl.BlockSpec((nb * t, pw), lambda i: (i, 0)), cache, cache,
                  _full(cos.shape), _full(sin.shape)],
        out_specs=[pl.BlockSpec((nb * t, qw), lambda i: (i, 0)), new_cache, new_cache],
        out_shape=[jax.ShapeDtypeStruct((n, qw), F32), jax.ShapeDtypeStruct(ck.shape[1:], F32),
                   jax.ShapeDtypeStruct(cv.shape[1:], F32)],
        compiler_params=_params("parallel"),
        name="swa_sample",
    )(sink_cols, p, ck, cv, cos, sin)


FFN_HIDDEN_CHUNK = 1024


def _merge_ffn_kernel(x_ref, ya_ref, yb_ref, yc_ref, gt_ref, wb_ref, wo_ref, g_ref, wg_ref, wu_ref, wd_ref, gf_ref,
                      o_ref, *, final_norm):
    d = x_ref.shape[1]
    merged = None
    off = 0
    for i, y_ref in enumerate((ya_ref, yb_ref, yc_ref)):
        wdt = y_ref.shape[1]
        br = _dot(y_ref[...].astype(BF16), wb_ref[off:off + wdt, :])
        term = jax.nn.sigmoid(gt_ref[:, i * d:(i + 1) * d]) * br
        merged = term if merged is None else merged + term
        off += wdt
    x = x_ref[...] + _dot(merged.astype(BF16), wo_ref[...])
    hb = _rms(x, g_ref[...]).astype(BF16)
    hidden = wg_ref.shape[1]
    acc = x
    for c0 in range(0, hidden, FFN_HIDDEN_CHUNK):
        c1 = min(c0 + FFN_HIDDEN_CHUNK, hidden)
        gate = _dot(hb, wg_ref[:, c0:c1])
        up = _dot(hb, wu_ref[:, c0:c1])
        act = (gate * jax.nn.sigmoid(gate) * up).astype(BF16)
        acc = acc + _dot(act, wd_ref[c0:c1, :])
    o_ref[...] = _rms(acc, gf_ref[...]) if final_norm else acc


def _merge_ffn(x, ya, yb, yc, gates, wb, wo, g, wg, wu, wd, gf, l, tm, final_norm):
    n, d = x.shape
    row = lambda a: pl.BlockSpec((tm, a.shape[1]), lambda i: (i, 0))
    return pl.pallas_call(
        functools.partial(_merge_ffn_kernel, final_norm=final_norm),
        grid=(n // tm,),
        in_specs=[row(x), row(ya), row(yb), row(yc), row(gates), _layer(wb, l), _layer(wo, l), _layer(g, l),
                  _layer(wg, l), _layer(wu, l), _layer(wd, l), _full(gf.shape)],
        out_specs=pl.BlockSpec((tm, d), lambda i: (i, 0)),
        out_shape=jax.ShapeDtypeStruct((n, d), F32),
        compiler_params=_params("parallel"),
        name="merge_ffn",
    )(x, ya, yb, yc, gates, wb, wo, g, wg, wu, wd, gf)


def _tile(n, want):
    while n % want:
        want //= 2
    return want


def kernel(x_prompt, x_sample, state_s5_re, state_s5_im, state_gla, cache_swa_k, cache_swa_v, g_mix, w_in, s5_lambda_re, s5_lambda_im, s5_log_dt, s5_b_re, s5_b_im, s5_c_re, s5_c_im, s5_d, w_glu, b_glu, w_gla_a2, b_gla_a, g_gla_norm, swa_sinks, w_branch, w_out, g_ffn, w_ffn_gate, w_ffn_up, w_ffn_down, g_final):
    bp, tp, d = x_prompt.shape
    bs, ts, _ = x_sample.shape
    depth = w_in.shape[0]
    s5_w = s5_d.shape[1]
    groups, s5_n = s5_lambda_re.shape[1:]
    gla_qk = w_gla_a2.shape[2]
    gla_rank = w_gla_a2.shape[1]
    gla_heads, gla_dk, gla_dv = state_gla.shape[2:]
    gla_v = gla_heads * gla_dv
    swa_heads = swa_sinks.shape[1]
    swa_q = swa_heads * SWA_HD
    swa_kv = SWA_KV_HEADS * SWA_HD
    assert SWA_KV_HEADS * SWA_HD == LANES and bp % SUBLANES == 0 and bs % SUBLANES == 0 and SUBLANES % ts == 0
    assert tp % WINDOW == 0 and cache_swa_k.shape[2] == WINDOW

    o_a, o_b = 0, s5_w
    o_low = o_b + 2 * gla_qk + 2 * gla_v
    o_c = o_low + gla_rank
    o_g = o_c + swa_q + 2 * swa_kv
    widths = (s5_w, 2 * gla_qk + 2 * gla_v + LANES, swa_q + 2 * swa_kv, 3 * d)

    hp = x_prompt.reshape(bp * tp, d)
    hs = x_sample.reshape(bs * ts, d)
    inv = ROPE_THETA ** (-jnp.arange(SWA_HD // 2, dtype=F32) * (2.0 / SWA_HD))
    inv = jnp.tile(inv, 2 * LANES // SWA_HD).reshape(1, LANES)
    cos_p, sin_p = _rope_table(jnp.arange(tp, dtype=jnp.int32), inv)
    cos_s, sin_s = _rope_table(PAST_LEN + (jnp.arange(SUBLANES, dtype=jnp.int32) & (ts - 1)), inv)
    zeros_s5 = jnp.zeros((1, bp, groups * s5_n), F32)
    tm_p, tm_s = _tile(bp * tp, 256), _tile(bs * ts, 256)
    s5_steps = _tile(tp, 128)
    gla_rows = _tile(tp, 512)

    w_t = jnp.swapaxes(w_in, 1, 2)
    w_a = jnp.pad(w_t[:, :o_c].astype(BF16), ((0, 0), (0, LANES - gla_rank), (0, 0)))
    w_b = w_t[:, o_c:].astype(BF16)
    wa2 = jnp.concatenate([w_gla_a2, jnp.zeros((depth, LANES - gla_rank, gla_qk), F32)], axis=1).astype(BF16)
    ba = b_gla_a.reshape(depth, 1, gla_qk)
    gn = g_gla_norm.reshape(depth, 1, gla_dv)
    ar, ai, bbr, bbi = _s5_prep(s5_lambda_re, s5_lambda_im, s5_log_dt, s5_b_re, s5_b_im)
    bre, bim, cre, cim = _s5_pack(bbr, bbi, s5_c_re, s5_c_im)
    s5_params = (ar, ai, bre, bim, cre, cim, s5_d.reshape(depth, 1, s5_w), w_glu.astype(BF16),
                 b_glu.reshape(depth, 1, s5_w))
    h0r_s = state_s5_re.reshape(depth, bs, groups * s5_n)
    h0i_s = state_s5_im.reshape(depth, bs, groups * s5_n)
    g_mix_r, g_ffn_r, gf = g_mix.reshape(depth, 1, d), g_ffn.reshape(depth, 1, d), g_final.reshape(1, d)
    wb, wo = w_branch.astype(BF16), w_out.astype(BF16)
    wg, wu, wd = w_ffn_gate.astype(BF16), w_ffn_up.astype(BF16), w_ffn_down.astype(BF16)
    sink_cols = jnp.repeat(swa_sinks, ts, axis=1).reshape(depth, swa_heads * ts, 1)
    to_lanes = lambda c: jnp.transpose(c, (0, 1, 3, 4, 2))
    from_lanes = lambda c: jnp.transpose(c, (0, 1, 4, 2, 3))
    ck, cv = to_lanes(cache_swa_k), to_lanes(cache_swa_v)

    outs_p = [[] for _ in range(5)]
    outs_s = [[] for _ in range(5)]
    for l in range(depth):
        last = l == depth - 1

        pa, pb, pc, pg = _inproj(hp, g_mix_r, w_a, w_b, l, widths, tm_p)
        ya, hr, hi = _s5(pa.reshape(bp, tp, s5_w), zeros_s5, zeros_s5, 0, s5_params, l, s5_steps)
        ya = ya.reshape(bp * tp, s5_w)
        yb, sg = _gla_prompt(pb, wa2, ba, gn, l, bp, gla_heads, gla_dk, gla_dv, gla_rows)
        yc, nk, nv = _swa_prompt(pc, swa_sinks, cos_p, sin_p, l, bp, swa_heads)
        hp = _merge_ffn(hp, ya, yb, yc, pg, wb, wo, g_ffn_r, wg, wu, wd, gf, l, tm_p, last)
        for lst, val in zip(outs_p, (hr.reshape(bp, groups, s5_n), hi.reshape(bp, groups, s5_n), sg,
                                     nk.reshape(bp, WINDOW, SWA_KV_HEADS, SWA_HD),
                                     nv.reshape(bp, WINDOW, SWA_KV_HEADS, SWA_HD))):
            lst.append(val)

        pa, pb, pc, pg = _inproj(hs, g_mix_r, w_a, w_b, l, widths, tm_s)
        ya, hr, hi = _s5(pa, h0r_s, h0i_s, l, s5_params, l, ts)
        yb, sg = _gla_sample(pb, state_gla, wa2, ba, gn, l, ts, _tile(bs, 16))
        yc, nk, nv = _swa_sample(pc, ck, cv, sink_cols, cos_s, sin_s, l, swa_heads, ts, _tile(bs, 8))
        hs = _merge_ffn(hs, ya, yb, yc, pg, wb, wo, g_ffn_r, wg, wu, wd, gf, l, tm_s, last)
        for lst, val in zip(outs_s, (hr.reshape(bs, groups, s5_n), hi.reshape(bs, groups, s5_n), sg, nk, nv)):
            lst.append(val)

    outs_s = [jnp.stack(o) for o in outs_s]
    outs_s[3], outs_s[4] = from_lanes(outs_s[3]), from_lanes(outs_s[4])
    return (hp.reshape(bp, tp, d), hs.reshape(bs, ts, d), *[jnp.stack(o) for o in outs_p], *outs_s)
```

```python
import functools

import jax
import jax.numpy as jnp
from jax import lax
from jax.experimental import pallas as pl
from jax.experimental.pallas import tpu as pltpu

F32 = jnp.float32
BF16 = jnp.bfloat16

EPS = 1e-6
GLA_TAU = 16.0
GLA_CHUNK = 64
SWA_KV_HEADS = 2
SWA_HD = 64
WINDOW = 128
ROPE_THETA = 10000.0
PAST_LEN = 16384

SUBLANES = 8
LANES = 128
VMEM_LIMIT = 56 * 1024 * 1024

NT_DIMS = (((1,), (1,)), ((), ()))
TN_DIMS = (((0,), (0,)), ((), ()))


def _params(*sem):
    return pltpu.CompilerParams(dimension_semantics=sem, vmem_limit_bytes=VMEM_LIMIT)


def _dot(a, b):
    return jnp.dot(a, b, preferred_element_type=F32)


def _rms(x, g):
    return x * lax.rsqrt(jnp.mean(x * x, axis=-1, keepdims=True) + EPS) * g


def _full(shape):
    nd = len(shape)
    return pl.BlockSpec(shape, lambda *_: (0,) * nd)


def _layer(a, l):
    nd = a.ndim - 1
    return pl.BlockSpec((None,) + a.shape[1:], lambda *_: (l,) + (0,) * nd, pipeline_mode=pl.Buffered(1))


def _inproj_kernel(x_ref, g_ref, wa_ref, wb_ref, *o_refs):
    xb = _rms(x_ref[...], g_ref[...]).astype(BF16)
    for w_ref, group in ((wa_ref, o_refs[:2]), (wb_ref, o_refs[2:])):
        off = 0
        for o_ref in group:
            n = o_ref.shape[-1]
            for c0 in range(0, n, 512):
                c1 = min(c0 + 512, n)
                o_ref[:, c0:c1] = lax.dot_general(xb, w_ref[off + c0:off + c1, :], NT_DIMS,
                                                  preferred_element_type=F32)
            off += n


def _inproj(x, g, wa, wb, l, widths, tm):
    n, d = x.shape
    assert wa.shape[1] == sum(widths[:2]) and wb.shape[1] == sum(widths[2:])
    return pl.pallas_call(
        _inproj_kernel,
        grid=(n // tm,),
        in_specs=[pl.BlockSpec((tm, d), lambda i: (i, 0)), _layer(g, l), _layer(wa, l), _layer(wb, l)],
        out_specs=[pl.BlockSpec((tm, wd), lambda i: (i, 0)) for wd in widths],
        out_shape=[jax.ShapeDtypeStruct((n, wd), F32) for wd in widths],
        compiler_params=_params("parallel"),
        name="inproj",
    )(x, g, wa, wb)


def _s5_prep_kernel(lr_ref, li_ref, ldt_ref, br_ref, bi_ref, ar_ref, ai_ref, bbr_ref, bbi_ref):
    lr, li = lr_ref[...], li_ref[...]
    dt = jnp.exp(ldt_ref[...])
    mag = jnp.exp(lr * dt)
    ar = mag * jnp.cos(li * dt)
    ai = mag * jnp.sin(li * dt)
    den = lr * lr + li * li
    zr, zi = ar - 1.0, ai
    er = (zr * lr + zi * li) / den
    ei = (zi * lr - zr * li) / den
    ar_ref[...] = ar
    ai_ref[...] = ai
    br, bi = br_ref[...], bi_ref[...]
    bbr_ref[...] = er[None] * br - ei[None] * bi
    bbi_ref[...] = er[None] * bi + ei[None] * br


def _s5_prep(lam_re, lam_im, log_dt, b_re, b_im):
    depth, g, c, n = b_re.shape
    dg = depth * g
    brt = jnp.transpose(b_re.reshape(dg, c, n), (1, 0, 2))
    bit = jnp.transpose(b_im.reshape(dg, c, n), (1, 0, 2))
    ar, ai, bbr, bbi = pl.pallas_call(
        _s5_prep_kernel,
        out_shape=[jax.ShapeDtypeStruct((dg, n), F32)] * 2 + [jax.ShapeDtypeStruct((c, dg, n), F32)] * 2,
        name="s5_prep",
    )(lam_re.reshape(dg, n), lam_im.reshape(dg, n), log_dt.reshape(dg, 1), brt, bit)
    unt = lambda a: jnp.transpose(a, (1, 0, 2)).reshape(depth, g, c, n)
    return ar.reshape(depth, 1, g * n), ai.reshape(depth, 1, g * n), unt(bbr), unt(bbi)


def _s5_pack(bbr, bbi, c_re, c_im):
    depth, g, c, n = bbr.shape
    gp = LANES // c
    ks = g // gp
    eye = jnp.eye(gp, dtype=F32)

    def pack_b(b):
        return jnp.einsum("lkgcn,gh->lkgchn", b.reshape(depth, ks, gp, c, n), eye).reshape(
            depth, ks, gp * c, gp * n).astype(BF16)

    def pack_c(cm):
        return jnp.einsum("lkgnc,gh->lkgnhc", cm.reshape(depth, ks, gp, n, c), eye).reshape(
            depth, ks, gp * n, gp * c).astype(BF16)

    return pack_b(bbr), pack_b(bbi), pack_c(c_re), pack_c(c_im)


S5_COL_BLOCK = 1024
S5_UNROLL = 8


def _s5_kernel(u_ref, h0r_ref, h0i_ref, ar_ref, ai_ref, bre_ref, bim_ref, cre_ref, cim_ref, d_ref, wglu_ref,
               bglu_ref, y_ref, hr_out, hi_out, us_s, ys_s, xr_s, xi_s, hr_s, hi_s, *tmp_s, nb, steps):
    c = pl.program_id(0)

    @pl.when(c == 0)
    def _():
        hr_s[...] = h0r_ref[...]
        hi_s[...] = h0i_ref[...]

    ks, kw, nw = bre_ref.shape
    by_sequence = u_ref.ndim == 3
    if by_sequence:
        for b in range(nb):
            for k in range(ks):
                us_s[k, pl.ds(b, steps, stride=nb), :] = u_ref[b, :, k * kw:(k + 1) * kw]
    else:
        for k in range(ks):
            tmp_s[0][k] = u_ref[:, k * kw:(k + 1) * kw]
        for t in range(steps):
            for k in range(ks):
                us_s[k, t * nb:(t + 1) * nb, :] = tmp_s[0][k, pl.ds(t, nb, stride=steps), :]

    for k in range(ks):
        uk = us_s[k].astype(BF16)
        xr_s[:, k * nw:(k + 1) * nw] = _dot(uk, bre_ref[k])
        xi_s[:, k * nw:(k + 1) * nw] = _dot(uk, bim_ref[k])

    ncols = xr_s.shape[1]
    for rb in range(nb // SUBLANES):
        rows = slice(rb * SUBLANES, (rb + 1) * SUBLANES)
        for cb in range(ncols // S5_COL_BLOCK):
            cs = slice(cb * S5_COL_BLOCK, (cb + 1) * S5_COL_BLOCK)
            a_r = jnp.broadcast_to(ar_ref[:, cs], (SUBLANES, S5_COL_BLOCK))
            a_i = jnp.broadcast_to(ai_ref[:, cs], (SUBLANES, S5_COL_BLOCK))

            def step(t, carry, rb=rb, cs=cs, a_r=a_r, a_i=a_i):
                hr, hi = carry
                r0 = pl.multiple_of(t * nb + rb * SUBLANES, SUBLANES)
                nhr = a_r * hr - a_i * hi + xr_s[pl.ds(r0, SUBLANES), cs]
                nhi = a_r * hi + a_i * hr + xi_s[pl.ds(r0, SUBLANES), cs]
                xr_s[pl.ds(r0, SUBLANES), cs] = nhr
                xi_s[pl.ds(r0, SUBLANES), cs] = nhi
                return nhr, nhi

            hr, hi = lax.fori_loop(0, steps, step, (hr_s[rows, cs], hi_s[rows, cs]),
                                   unroll=min(steps, S5_UNROLL))
            hr_s[rows, cs] = hr
            hi_s[rows, cs] = hi

    for k in range(ks):
        ss = slice(k * nw, (k + 1) * nw)
        yk = _dot(xr_s[:, ss].astype(BF16), cre_ref[k]) - _dot(xi_s[:, ss].astype(BF16), cim_ref[k])
        ys_s[k] = jax.nn.gelu(yk + d_ref[:, k * kw:(k + 1) * kw] * us_s[k])
    y = jnp.concatenate([ys_s[k] for k in range(ks)], axis=1)
    y = y * jax.nn.sigmoid(_dot(y.astype(BF16), wglu_ref[...]) + bglu_ref[...])
    for k in range(ks):
        ys_s[k] = y[:, k * kw:(k + 1) * kw]
    if by_sequence:
        for b in range(nb):
            for k in range(ks):
                y_ref[b, :, k * kw:(k + 1) * kw] = ys_s[k, pl.ds(b, steps, stride=nb), :]
    else:
        for t in range(steps):
            for k in range(ks):
                tmp_s[0][k, pl.ds(t, nb, stride=steps), :] = ys_s[k, t * nb:(t + 1) * nb, :]
        for k in range(ks):
            y_ref[:, k * kw:(k + 1) * kw] = tmp_s[0][k]

    @pl.when(c == pl.num_programs(0) - 1)
    def _():
        hr_out[...] = hr_s[...]
        hi_out[...] = hi_s[...]


def _s5(u, h0r, h0i, l0, params, l, steps):
    _, nb, ns = h0r.shape
    w = u.shape[-1]
    blk = steps * nb
    ks = params[2].shape[1]
    if u.ndim == 3:
        grid = (u.shape[1] // steps,)
        u_spec = pl.BlockSpec((nb, steps, w), lambda i: (0, i, 0))
        tmp = []
    else:
        assert u.shape[0] == blk
        grid = (1,)
        u_spec = _full(u.shape)
        tmp = [pltpu.VMEM((ks, blk, LANES), F32)]
    kern = functools.partial(_s5_kernel, nb=nb, steps=steps)
    return pl.pallas_call(
        kern,
        grid=grid,
        in_specs=[u_spec, _layer(h0r, l0), _layer(h0i, l0)] + [_layer(a, l) for a in params],
        out_specs=[u_spec, _full((nb, ns)), _full((nb, ns))],
        out_shape=[jax.ShapeDtypeStruct(u.shape, F32), jax.ShapeDtypeStruct((nb, ns), F32),
                   jax.ShapeDtypeStruct((nb, ns), F32)],
        scratch_shapes=[pltpu.VMEM((ks, blk, LANES), F32), pltpu.VMEM((ks, blk, LANES), F32),
                        pltpu.VMEM((blk, ns), F32), pltpu.VMEM((blk, ns), F32), pltpu.VMEM((nb, ns), F32),
                        pltpu.VMEM((nb, ns), F32)] + tmp,
        compiler_params=_params("arbitrary"),
        name="s5_scan",
    )(u, h0r, h0i, *params)


def _log2(n):
    assert n & (n - 1) == 0, n
    return n.bit_length() - 1


def _chunk_masks(rows, chunk):
    sh = _log2(chunk)
    rr = lax.broadcasted_iota(jnp.int32, (rows, rows), 0)
    cc = lax.broadcasted_iota(jnp.int32, (rows, rows), 1)
    same = lax.shift_right_logical(rr, sh) == lax.shift_right_logical(cc, sh)
    return same, same & (cc <= rr)


def _gla_prologue(p_ref, wa_ref, ba_ref, chunk, qk):
    rows = p_ref.shape[0]
    q = p_ref[:, 0:qk]
    k = p_ref[:, qk:2 * qk]
    alow = p_ref[:, p_ref.shape[1] - LANES:].astype(BF16)
    z = _dot(alow, wa_ref[...]) + ba_ref[...]
    la = (jnp.minimum(z, 0.0) - jnp.log(1.0 + jnp.exp(-jnp.abs(z)))) * (1.0 / GLA_TAU)
    same, causal = _chunk_masks(rows, chunk)
    tri = jnp.where(causal, 1.0, 0.0).astype(BF16)
    blk = jnp.where(same, 1.0, 0.0).astype(BF16)
    la_hi = la.astype(BF16)
    la_lo = (la - la_hi.astype(F32)).astype(BF16)
    b = _dot(tri, la_hi) + _dot(tri, la_lo)
    bl = _dot(blk, la_hi) + _dot(blk, la_lo)
    return q, k, b, bl


def _gla_out(o, gn, r):
    on = o * lax.rsqrt(jnp.mean(o * o, axis=-1, keepdims=True) + EPS) * gn
    return on * (r * jax.nn.sigmoid(r))


def _gla_prompt_kernel(p_ref, wa_ref, ba_ref, gn_ref, y_ref, s_ref, st_s, *, heads, dk, dv, chunk):
    c = pl.program_id(1)

    @pl.when(c == 0)
    def _():
        st_s[...] = jnp.zeros_like(st_s)

    rows = p_ref.shape[0]
    qk, vw = heads * dk, heads * dv
    q, k, b, bl = _gla_prologue(p_ref, wa_ref, ba_ref, chunk, qk)
    v_off, r_off = 2 * qk, 2 * qk + vw
    qd = (q * (dk ** -0.5) * jnp.exp(b)).astype(BF16)
    kd = (k * jnp.exp(-b)).astype(BF16)
    kst = (k * jnp.exp(bl - b)).astype(BF16)
    dec = jnp.exp(bl)
    gn = gn_ref[...]
    zero = jnp.zeros((), BF16)
    qk_head = lax.shift_right_logical(lax.broadcasted_iota(jnp.int32, (chunk, qk), 1), _log2(dk))
    v_head = lax.shift_right_logical(lax.broadcasted_iota(jnp.int32, (chunk, vw), 1), _log2(dv))
    rr = lax.broadcasted_iota(jnp.int32, (chunk, heads * chunk), 0)
    cc = lax.broadcasted_iota(jnp.int32, (chunk, heads * chunk), 1) & (chunk - 1)
    causal = cc <= rr
    nchunk = rows // chunk
    by_head = lambda x, head: jnp.concatenate([jnp.where(head == h, x, zero) for h in range(heads)], axis=0)
    att, ds_t, v_diag, qd_rows = [], [], [], []
    for ci in range(nchunk):
        rs = slice(ci * chunk, (ci + 1) * chunk)
        v_c = p_ref[rs, v_off:v_off + vw].astype(BF16)
        v_rows = jnp.concatenate([v_c[:, h * dv:(h + 1) * dv] for h in range(heads)], axis=0)
        v_diag.append(by_head(v_c, v_head))
        qd_rows.append(by_head(qd[rs], qk_head))
        att.append(lax.dot_general(qd[rs], by_head(kd[rs], qk_head), NT_DIMS, preferred_element_type=F32))
        ds_t.append(lax.dot_general(v_rows, by_head(kst[rs], qk_head), TN_DIMS, preferred_element_type=F32))
    o_intra = [_dot(jnp.where(causal, att[ci], 0.0).astype(BF16), v_diag[ci]) for ci in range(nchunk)]
    st = st_s[...]
    o_inter = []
    for ci in range(nchunk):
        o_inter.append(lax.dot_general(qd_rows[ci], st.astype(BF16), NT_DIMS, preferred_element_type=F32))
        st = dec[ci * chunk:ci * chunk + 1, :] * st + ds_t[ci]
    st_s[...] = st
    for ci in range(nchunk):
        rs = slice(ci * chunk, (ci + 1) * chunk)
        for h in range(heads):
            o = o_intra[ci][:, h * dv:(h + 1) * dv] + o_inter[ci][h * chunk:(h + 1) * chunk]
            r = p_ref[rs, r_off + h * dv:r_off + (h + 1) * dv]
            y_ref[rs, h * dv:(h + 1) * dv] = _gla_out(o, gn, r)

    @pl.when(c == pl.num_programs(1) - 1)
    def _():
        for h in range(heads):
            s_ref[0, h] = st_s[:, h * dk:(h + 1) * dk].T


def _gla_prompt(p, wa, ba, gn, l, bsz, heads, dk, dv, rows):
    n, pw = p.shape
    t = n // bsz
    nblk = t // rows
    kern = functools.partial(_gla_prompt_kernel, heads=heads, dk=dk, dv=dv, chunk=GLA_CHUNK)
    return pl.pallas_call(
        kern,
        grid=(bsz, nblk),
        in_specs=[pl.BlockSpec((rows, pw), lambda b, c: (b * nblk + c, 0)), _layer(wa, l), _layer(ba, l),
                  _layer(gn, l)],
        out_specs=[pl.BlockSpec((rows, heads * dv), lambda b, c: (b * nblk + c, 0)),
                   pl.BlockSpec((1, heads, dk, dv), lambda b, c: (b, 0, 0, 0))],
        out_shape=[jax.ShapeDtypeStruct((n, heads * dv), F32), jax.ShapeDtypeStruct((bsz, heads, dk, dv), F32)],
        scratch_shapes=[pltpu.VMEM((dv, heads * dk), F32)],
        compiler_params=_params("parallel", "arbitrary"),
        name="gla_prompt",
    )(p, wa, ba, gn)


def _gla_sample_kernel(p_ref, s0_ref, wa_ref, ba_ref, gn_ref, y_ref, s_ref, *, heads, dk, dv, t):
    rows = p_ref.shape[0]
    qk = heads * dk
    q, k, b, bl = _gla_prologue(p_ref, wa_ref, ba_ref, t, qk)
    _, cmask = _chunk_masks(SUBLANES, t)
    v_off, r_off = 2 * qk, 2 * qk + heads * dv
    qd = (q * (dk ** -0.5) * jnp.exp(b)).astype(BF16)
    kd = (k * jnp.exp(-b)).astype(BF16)
    kst = k * jnp.exp(bl - b)
    dec = jnp.exp(bl)
    gn = gn_ref[...]
    per_tile = SUBLANES // t
    row_id = lax.broadcasted_iota(jnp.int32, (SUBLANES, 1), 0)
    att, o_state = {}, {}
    for ti in range(rows // SUBLANES):
        rs = slice(ti * SUBLANES, (ti + 1) * SUBLANES)
        dcols = []
        for j in range(per_tile):
            drow = dec[ti * SUBLANES + j * t:ti * SUBLANES + j * t + 1, :]
            dcols.append([jnp.broadcast_to(drow[:, g * LANES:(g + 1) * LANES], (LANES, LANES)).T
                          for g in range(qk // LANES)])
        for h in range(heads):
            hs = slice(h * dk, (h + 1) * dk)
            vh = p_ref[rs, v_off + h * dv:v_off + (h + 1) * dv].astype(BF16)
            att[ti, h] = lax.dot_general(qd[rs, hs], kd[rs, hs], NT_DIMS, preferred_element_type=F32)
            o = None
            for j in range(per_tile):
                bidx = ti * per_tile + j
                in_seq = (row_id >= j * t) & (row_id < (j + 1) * t)
                s0 = s0_ref[bidx, h]
                oj = jnp.where(in_seq, _dot(qd[rs, hs], s0.astype(BF16)), 0.0)
                o = oj if o is None else o + oj
                kj = jnp.where(in_seq, kst[rs, hs], 0.0).astype(BF16)
                ds = lax.dot_general(kj, vh, TN_DIMS, preferred_element_type=F32)
                g, lo = (h * dk) // LANES, (h * dk) % LANES
                dcol = dcols[j][g][lo:lo + dk, 0:dv]
                s_ref[bidx, h] = dcol * s0 + ds
            o_state[ti, h] = o
    for ti in range(rows // SUBLANES):
        rs = slice(ti * SUBLANES, (ti + 1) * SUBLANES)
        for h in range(heads):
            vh = p_ref[rs, v_off + h * dv:v_off + (h + 1) * dv].astype(BF16)
            o = _dot(jnp.where(cmask, att[ti, h], 0.0).astype(BF16), vh) + o_state[ti, h]
            r = p_ref[rs, r_off + h * dv:r_off + (h + 1) * dv]
            y_ref[rs, h * dv:(h + 1) * dv] = _gla_out(o, gn, r)


def _gla_sample(p, s0, wa, ba, gn, l, t, nb):
    n, pw = p.shape
    _, bsz, heads, dk, dv = s0.shape
    rows = nb * t
    kern = functools.partial(_gla_sample_kernel, heads=heads, dk=dk, dv=dv, t=t)
    return pl.pallas_call(
        kern,
        grid=(bsz // nb,),
        in_specs=[pl.BlockSpec((rows, pw), lambda i: (i, 0)),
                  pl.BlockSpec((None, nb, heads, dk, dv), lambda i: (l, i, 0, 0, 0)),
                  _layer(wa, l), _layer(ba, l), _layer(gn, l)],
        out_specs=[pl.BlockSpec((rows, heads * dv), lambda i: (i, 0)),
                   pl.BlockSpec((nb, heads, dk, dv), lambda i: (i, 0, 0, 0))],
        out_shape=[jax.ShapeDtypeStruct((n, heads * dv), F32), jax.ShapeDtypeStruct(s0.shape[1:], F32)],
        compiler_params=_params("parallel"),
        name="gla_sample",
    )(p, s0, wa, ba, gn)


def _first_half(shape):
    lane = lax.broadcasted_iota(jnp.int32, shape, 1)
    return (lane & (SWA_HD - 1)) < SWA_HD // 2


def _rope_table_kernel(pos_ref, inv_ref, cos_ref, sin_ref):
    ang = pos_ref[...] * inv_ref[...]
    s = jnp.sin(ang)
    cos_ref[...] = jnp.cos(ang)
    sin_ref[...] = jnp.where(_first_half(ang.shape), -s, s)


def _rope_table(pos, inv):
    rows = pos.shape[0]
    return pl.pallas_call(
        _rope_table_kernel,
        out_shape=[jax.ShapeDtypeStruct((rows, LANES), F32)] * 2,
        name="rope_table",
    )(pos.astype(F32).reshape(rows, 1), inv)


def _rope(x, cos, sin):
    half = SWA_HD // 2
    first_half = _first_half(cos.shape)
    out = []
    for g in range(x.shape[1] // LANES):
        xg = x[:, g * LANES:(g + 1) * LANES]
        swapped = jnp.where(first_half, pltpu.roll(xg, LANES - half, 1), pltpu.roll(xg, half, 1))
        out.append(xg * cos + swapped * sin)
    return out[0] if len(out) == 1 else jnp.concatenate(out, axis=1)


def _swa_prompt_kernel(sink_ref, cur_ref, cos_ref, sin_ref, y_ref, nk_ref, nv_ref, prev_s, s_s, pc_s, *, heads, l):
    i = pl.program_id(1)
    w = WINDOW
    qw = heads * SWA_HD
    kvw = SWA_KV_HEADS * SWA_HD
    gq = heads // SWA_KV_HEADS
    nq = cur_ref.shape[0] // w

    @pl.when(i == 0)
    def _():
        prev_s[...] = jnp.zeros_like(prev_s)

    cos, sin = cos_ref[...], sin_ref[...]
    q = (_rope(cur_ref[:, 0:qw], cos, sin) * (SWA_HD ** -0.5)).astype(BF16)
    k_cur = _rope(cur_ref[:, qw:qw + kvw], cos, sin)
    v_cur = cur_ref[:, qw + kvw:qw + 2 * kvw]
    new = [x.astype(BF16) for x in (k_cur, pltpu.roll(k_cur, SWA_HD, 1), v_cur, pltpu.roll(v_cur, SWA_HD, 1))]
    k2, k2r, v2, v2r = [jnp.concatenate([prev_s[j], new[j]], axis=0) for j in range(4)]
    zero = jnp.zeros((), BF16)
    lo2 = lax.broadcasted_iota(jnp.int32, ((nq + 1) * w, LANES), 1) < SWA_HD
    lo = lax.broadcasted_iota(jnp.int32, (w, LANES), 1) < SWA_HD
    k_both = (jnp.where(lo2, k2, k2r), jnp.where(lo2, k2r, k2))
    v_half = {(0, 0): jnp.where(lo2, v2, zero), (1, 1): jnp.where(lo2, zero, v2),
              (0, 1): jnp.where(lo2, zero, v2r), (1, 0): jnp.where(lo2, v2r, zero)}
    upper = lax.broadcasted_iota(jnp.int32, (w, w), 1) > lax.broadcasted_iota(jnp.int32, (w, w), 0)
    no_prev = jnp.where(i > 0, 0.0, -jnp.inf)
    slot = lambda c, h: slice((c * heads + h) * w, (c * heads + h + 1) * w)
    for c in range(nq):
        for kv in range(SWA_KV_HEADS):
            qm = []
            for h in range(kv * gq, (kv + 1) * gq):
                qg = q[c * w:(c + 1) * w, (h // 2) * LANES:(h // 2 + 1) * LANES]
                qm.append(jnp.where(lo, qg, zero) if h % 2 == 0 else jnp.where(lo, zero, qg))
            s_s[(c * heads + kv * gq) * w:(c * heads + (kv + 1) * gq) * w, :] = lax.dot_general(
                jnp.concatenate(qm, axis=0), k_both[kv][c * w:(c + 2) * w], NT_DIMS,
                preferred_element_type=F32)
    sink_term = {}
    for c in range(nq):
        for h in range(heads):
            s_prev = s_s[slot(c, h), 0:w] + no_prev if c == 0 else s_s[slot(c, h), 0:w]
            sp = jnp.where(upper, s_prev, s_s[slot(c, h), w:2 * w])
            sink = sink_ref[l, h]
            m = jnp.maximum(jnp.max(sp, axis=-1, keepdims=True), sink)
            p = jnp.exp(sp - m)
            pc_s[slot(c, h), 0:w] = jnp.where(upper, p, 0.0).astype(BF16)
            pc_s[slot(c, h), w:2 * w] = jnp.where(upper, 0.0, p).astype(BF16)
            sink_term[c, h] = jnp.exp(sink - m)
    num = {(c, h): _dot(pc_s[slot(c, h), :], v_half[(h // gq, h % 2)][c * w:(c + 2) * w])
           for c in range(nq) for h in range(heads)}
    den_all = _dot(pc_s[...], jnp.ones((2 * w, LANES), BF16))
    for c in range(nq):
        for g in range(heads // 2):
            he, ho = 2 * g, 2 * g + 1
            den = jnp.where(lo, den_all[slot(c, he)] + sink_term[c, he], den_all[slot(c, ho)] + sink_term[c, ho])
            y_ref[c * w:(c + 1) * w, g * LANES:(g + 1) * LANES] = (num[c, he] + num[c, ho]) * (1.0 / den)
    for j in range(4):
        prev_s[j] = new[j][(nq - 1) * w:]

    @pl.when(i == pl.num_programs(1) - 1)
    def _():
        nk_ref[0] = k_cur[(nq - 1) * w:]
        nv_ref[0] = v_cur[(nq - 1) * w:]


def _swa_prompt(p, sinks, cos, sin, l, bsz, heads, nq):
    n, pw = p.shape
    t = n // bsz
    rows = nq * WINDOW
    nblk = t // rows
    qw, kvw = heads * SWA_HD, SWA_KV_HEADS * SWA_HD
    kern = functools.partial(_swa_prompt_kernel, heads=heads, l=l)
    tab = pl.BlockSpec((rows, LANES), lambda b, i: (i, 0))
    return pl.pallas_call(
        kern,
        grid=(bsz, nblk),
        in_specs=[pl.BlockSpec(memory_space=pltpu.SMEM),
                  pl.BlockSpec((rows, pw), lambda b, i: (b * nblk + i, 0)), tab, tab],
        out_specs=[pl.BlockSpec((rows, qw), lambda b, i: (b * nblk + i, 0)),
                   pl.BlockSpec((1, WINDOW, kvw), lambda b, i: (b, 0, 0)),
                   pl.BlockSpec((1, WINDOW, kvw), lambda b, i: (b, 0, 0))],
        out_shape=[jax.ShapeDtypeStruct((n, qw), F32), jax.ShapeDtypeStruct((bsz, WINDOW, kvw), F32),
                   jax.ShapeDtypeStruct((bsz, WINDOW, kvw), F32)],
        scratch_shapes=[pltpu.VMEM((4, WINDOW, LANES), BF16),
                        pltpu.VMEM((nq * heads * WINDOW, 2 * WINDOW), F32),
                        pltpu.VMEM((nq * heads * WINDOW, 2 * WINDOW), BF16)],
        compiler_params=_params("parallel", "arbitrary"),
        name="swa_prompt",
    )(sinks, p, cos, sin)


def _expand_kv(x, lane):
    other = pltpu.roll(x, SWA_HD, 1)
    first = lane < SWA_HD
    kv0 = jnp.where(first, x, other)
    kv1 = jnp.where(first, other, x)
    return kv0, kv1


def _swa_sample_kernel(sinkcol_ref, p_ref, ck_ref, cv_ref, cos_ref, sin_ref, y_ref, nk_ref, nv_ref, *, heads, t, nb):
    w = ck_ref.shape[3]
    qw = heads * SWA_HD
    kvw = SWA_KV_HEADS * SWA_HD
    gq = heads // SWA_KV_HEADS
    per_tile = SUBLANES // t
    nrow = heads * t
    cos, sin = cos_ref[...], sin_ref[...]
    lane_kv = lax.broadcasted_iota(jnp.int32, (1, kvw), 1)
    qm_row = lax.broadcasted_iota(jnp.int32, (nrow, qw), 0)
    qm_lane = lax.broadcasted_iota(jnp.int32, (nrow, qw), 1)
    qmask = lax.shift_right_logical(qm_row, _log2(t)) == lax.shift_right_logical(qm_lane, _log2(SWA_HD))
    row_step = lax.broadcasted_iota(jnp.int32, (nrow, w), 0) & (t - 1)
    cache_ok = lax.broadcasted_iota(jnp.int32, (nrow, w), 1) > row_step
    new_col = lax.broadcasted_iota(jnp.int32, (nrow, SUBLANES), 1)
    new_row_step = lax.broadcasted_iota(jnp.int32, (nrow, SUBLANES), 0) & (t - 1)
    sink_col = sinkcol_ref[...]
    lane_w = lax.broadcasted_iota(jnp.int32, (SWA_HD, w), 1)
    widen = lambda a, b: jnp.concatenate([a] * (gq // 2) + [b] * (gq // 2), axis=1).astype(BF16)
    per_head = lambda c: jnp.concatenate([c[kv] for kv in range(SWA_KV_HEADS) for _ in range(gq)],
                                         axis=0).astype(BF16)
    pad_rows = jnp.zeros((LANES - 2 * SUBLANES, kvw), F32)
    new_k, new_v, scores = [], [], []
    for ti in range(nb // per_tile):
        rs = slice(ti * SUBLANES, (ti + 1) * SUBLANES)
        q8 = _rope(p_ref[rs, 0:qw], cos, sin) * (SWA_HD ** -0.5)
        kn8 = _rope(p_ref[rs, qw:qw + kvw], cos, sin)
        vn8 = p_ref[rs, qw + kvw:qw + 2 * kvw]
        kn_x = widen(*_expand_kv(kn8, lane_kv))
        new_k.append(kn8)
        new_v.append(vn8)
        for j in range(per_tile):
            bidx = ti * per_tile + j
            qj = q8[j * t:(j + 1) * t, :]
            qm = jnp.where(qmask, jnp.concatenate([qj] * heads, axis=0), 0.0).astype(BF16)
            s_c = _dot(qm, per_head(ck_ref[bidx]))
            s_n = lax.dot_general(qm, kn_x, NT_DIMS, preferred_element_type=F32)
            scores.append((s_c, s_n))
    probs = []
    for bidx, (s_c, s_n) in enumerate(scores):
        j = bidx % per_tile
        s_c = jnp.where(cache_ok, s_c, -jnp.inf)
        new_ok = (new_col >= j * t) & (new_col - j * t <= new_row_step) & (new_col < (j + 1) * t)
        s_n = jnp.where(new_ok, s_n, -jnp.inf)
        m = jnp.maximum(jnp.maximum(jnp.max(s_c, axis=1, keepdims=True), jnp.max(s_n, axis=1, keepdims=True)),
                        sink_col)
        p_c = jnp.exp(s_c - m)
        p_n = jnp.exp(s_n - m)
        den = jnp.sum(p_c, axis=1, keepdims=True) + jnp.sum(p_n, axis=1, keepdims=True) + jnp.exp(sink_col - m)
        rden = 1.0 / den
        probs.append(((p_c * rden).astype(BF16), (p_n * rden).astype(BF16)))
    outs = []
    for bidx, (p_c, p_n) in enumerate(probs):
        ti, j = bidx // per_tile, bidx % per_tile
        vn_x = widen(*_expand_kv(new_v[ti], lane_kv))
        outs.append(lax.dot_general(p_c, per_head(cv_ref[bidx]), NT_DIMS, preferred_element_type=F32)
                    + _dot(p_n, vn_x))
    for bidx, o in enumerate(outs):
        ti, j = bidx // per_tile, bidx % per_tile
        o = jnp.where(qmask, o, 0.0)
        acc = o[0:SUBLANES]
        for g in range(1, nrow // SUBLANES):
            acc = acc + o[g * SUBLANES:(g + 1) * SUBLANES]
        for g in range(1, per_tile):
            acc = acc + pltpu.roll(acc, SUBLANES - g * t, 0)
        y_ref[ti * SUBLANES + j * t:ti * SUBLANES + (j + 1) * t, :] = acc[0:t]
    for ti in range(nb // per_tile):
        kv_t = jnp.concatenate([new_k[ti], new_v[ti], pad_rows], axis=0).T
        for j in range(per_tile):
            bidx = ti * per_tile + j
            for c_ref, n_ref, col0 in ((ck_ref, nk_ref, 0), (cv_ref, nv_ref, SUBLANES)):
                for kv in range(SWA_KV_HEADS):
                    tail = pltpu.roll(kv_t[kv * SWA_HD:(kv + 1) * SWA_HD, :], w - t - col0 - j * t, 1)
                    n_ref[bidx, kv] = jnp.where(lane_w < w - t, pltpu.roll(c_ref[bidx, kv], w - t, 1), tail)


def _swa_sample(p, ck, cv, sink_cols, cos, sin, l, heads, t, nb):
    n, pw = p.shape
    _, bsz, kvh, hd, w = ck.shape
    qw = heads * SWA_HD
    kern = functools.partial(_swa_sample_kernel, heads=heads, t=t, nb=nb)
    cache = pl.BlockSpec((None, nb, kvh, hd, w), lambda i: (l, i, 0, 0, 0))
    new_cache = pl.BlockSpec((nb, kvh, hd, w), lambda i: (i, 0, 0, 0))
    return pl.pallas_call(
        kern,
        grid=(bsz // nb,),
        in_specs=[_layer(sink_cols, l), pl.BlockSpec((nb * t, pw), lambda i: (i, 0)), cache, cache,
                  _full(cos.shape), _full(sin.shape)],
        out_specs=[pl.BlockSpec((nb * t, qw), lambda i: (i, 0)), new_cache, new_cache],
        out_shape=[jax.ShapeDtypeStruct((n, qw), F32), jax.ShapeDtypeStruct(ck.shape[1:], F32),
                   jax.ShapeDtypeStruct(cv.shape[1:], F32)],
        compiler_params=_params("parallel"),
        name="swa_sample",
    )(sink_cols, p, ck, cv, cos, sin)


FFN_HIDDEN_CHUNK = 1024


def _merge_ffn_kernel(x_ref, ya_ref, yb_ref, yc_ref, gt_ref, wb_ref, wo_ref, g_ref, wg_ref, wu_ref, wd_ref, gf_ref,
                      o_ref, *, final_norm):
    d = x_ref.shape[1]
    merged = None
    off = 0
    for i, y_ref in enumerate((ya_ref, yb_ref, yc_ref)):
        wdt = y_ref.shape[1]
        br = _dot(y_ref[...].astype(BF16), wb_ref[off:off + wdt, :])
        term = jax.nn.sigmoid(gt_ref[:, i * d:(i + 1) * d]) * br
        merged = term if merged is None else merged + term
        off += wdt
    x = x_ref[...] + _dot(merged.astype(BF16), wo_ref[...])
    hb = _rms(x, g_ref[...]).astype(BF16)
    hidden = wg_ref.shape[1]
    acc = x
    for c0 in range(0, hidden, FFN_HIDDEN_CHUNK):
        c1 = min(c0 + FFN_HIDDEN_CHUNK, hidden)
        gate = _dot(hb, wg_ref[:, c0:c1])
        up = _dot(hb, wu_ref[:, c0:c1])
        act = (gate * jax.nn.sigmoid(gate) * up).astype(BF16)
        acc = acc + _dot(act, wd_ref[c0:c1, :])
    o_ref[...] = _rms(acc, gf_ref[...]) if final_norm else acc


def _merge_ffn(x, ya, yb, yc, gates, wb, wo, g, wg, wu, wd, gf, l, tm, final_norm):
    n, d = x.shape
    row = lambda a: pl.BlockSpec((tm, a.shape[1]), lambda i: (i, 0))
    return pl.pallas_call(
        functools.partial(_merge_ffn_kernel, final_norm=final_norm),
        grid=(n // tm,),
        in_specs=[row(x), row(ya), row(yb), row(yc), row(gates), _layer(wb, l), _layer(wo, l), _layer(g, l),
                  _layer(wg, l), _layer(wu, l), _layer(wd, l), _full(gf.shape)],
        out_specs=pl.BlockSpec((tm, d), lambda i: (i, 0)),
        out_shape=jax.ShapeDtypeStruct((n, d), F32),
        compiler_params=_params("parallel"),
        name="merge_ffn",
    )(x, ya, yb, yc, gates, wb, wo, g, wg, wu, wd, gf)


def _tile(n, want):
    while n % want:
        want //= 2
    return want


def kernel(x_prompt, x_sample, state_s5_re, state_s5_im, state_gla, cache_swa_k, cache_swa_v, g_mix, w_in, s5_lambda_re, s5_lambda_im, s5_log_dt, s5_b_re, s5_b_im, s5_c_re, s5_c_im, s5_d, w_glu, b_glu, w_gla_a2, b_gla_a, g_gla_norm, swa_sinks, w_branch, w_out, g_ffn, w_ffn_gate, w_ffn_up, w_ffn_down, g_final):
    bp, tp, d = x_prompt.shape
    bs, ts, _ = x_sample.shape
    depth = w_in.shape[0]
    s5_w = s5_d.shape[1]
    groups, s5_n = s5_lambda_re.shape[1:]
    gla_qk = w_gla_a2.shape[2]
    gla_rank = w_gla_a2.shape[1]
    gla_heads, gla_dk, gla_dv = state_gla.shape[2:]
    gla_v = gla_heads * gla_dv
    swa_heads = swa_sinks.shape[1]
    swa_q = swa_heads * SWA_HD
    swa_kv = SWA_KV_HEADS * SWA_HD
    assert SWA_KV_HEADS * SWA_HD == LANES and bp % SUBLANES == 0 and bs % SUBLANES == 0 and SUBLANES % ts == 0
    assert tp % WINDOW == 0 and cache_swa_k.shape[2] == WINDOW

    o_a, o_b = 0, s5_w
    o_low = o_b + 2 * gla_qk + 2 * gla_v
    o_c = o_low + gla_rank
    o_g = o_c + swa_q + 2 * swa_kv
    widths = (s5_w, 2 * gla_qk + 2 * gla_v + LANES, swa_q + 2 * swa_kv, 3 * d)

    hp = x_prompt.reshape(bp * tp, d)
    hs = x_sample.reshape(bs * ts, d)
    inv = ROPE_THETA ** (-jnp.arange(SWA_HD // 2, dtype=F32) * (2.0 / SWA_HD))
    inv = jnp.tile(inv, 2 * LANES // SWA_HD).reshape(1, LANES)
    cos_p, sin_p = _rope_table(jnp.arange(tp, dtype=jnp.int32), inv)
    cos_s, sin_s = _rope_table(PAST_LEN + (jnp.arange(SUBLANES, dtype=jnp.int32) & (ts - 1)), inv)
    zeros_s5 = jnp.zeros((1, bp, groups * s5_n), F32)
    tm_p, tm_s = _tile(bp * tp, 512), _tile(bs * ts, 512)
    s5_steps = _tile(tp, 128)
    gla_rows = _tile(tp, 512)
    swa_nq = _tile(tp // WINDOW, 4)

    w_t = jnp.swapaxes(w_in, 1, 2)
    w_a = jnp.pad(w_t[:, :o_c].astype(BF16), ((0, 0), (0, LANES - gla_rank), (0, 0)))
    w_b = w_t[:, o_c:].astype(BF16)
    wa2 = jnp.concatenate([w_gla_a2, jnp.zeros((depth, LANES - gla_rank, gla_qk), F32)], axis=1).astype(BF16)
    ba = b_gla_a.reshape(depth, 1, gla_qk)
    gn = g_gla_norm.reshape(depth, 1, gla_dv)
    ar, ai, bbr, bbi = _s5_prep(s5_lambda_re, s5_lambda_im, s5_log_dt, s5_b_re, s5_b_im)
    bre, bim, cre, cim = _s5_pack(bbr, bbi, s5_c_re, s5_c_im)
    s5_params = (ar, ai, bre, bim, cre, cim, s5_d.reshape(depth, 1, s5_w), w_glu.astype(BF16),
                 b_glu.reshape(depth, 1, s5_w))
    h0r_s = state_s5_re.reshape(depth, bs, groups * s5_n)
    h0i_s = state_s5_im.reshape(depth, bs, groups * s5_n)
    g_mix_r, g_ffn_r, gf = g_mix.reshape(depth, 1, d), g_ffn.reshape(depth, 1, d), g_final.reshape(1, d)
    wb, wo = w_branch.astype(BF16), w_out.astype(BF16)
    wg, wu, wd = w_ffn_gate.astype(BF16), w_ffn_up.astype(BF16), w_ffn_down.astype(BF16)
    sink_cols = jnp.repeat(swa_sinks, ts, axis=1).reshape(depth, swa_heads * ts, 1)
    to_lanes = lambda c: jnp.transpose(c, (0, 1, 3, 4, 2))
    from_lanes = lambda c: jnp.transpose(c, (0, 1, 4, 2, 3))
    ck, cv = to_lanes(cache_swa_k), to_lanes(cache_swa_v)

    outs_p = [[] for _ in range(5)]
    outs_s = [[] for _ in range(5)]
    for l in range(depth):
        last = l == depth - 1

        pa, pb, pc, pg = _inproj(hp, g_mix_r, w_a, w_b, l, widths, tm_p)
        ya, hr, hi = _s5(pa.reshape(bp, tp, s5_w), zeros_s5, zeros_s5, 0, s5_params, l, s5_steps)
        ya = ya.reshape(bp * tp, s5_w)
        yb, sg = _gla_prompt(pb, wa2, ba, gn, l, bp, gla_heads, gla_dk, gla_dv, gla_rows)
        yc, nk, nv = _swa_prompt(pc, swa_sinks, cos_p, sin_p, l, bp, swa_heads, swa_nq)
        hp = _merge_ffn(hp, ya, yb, yc, pg, wb, wo, g_ffn_r, wg, wu, wd, gf, l, tm_p, last)
        for lst, val in zip(outs_p, (hr.reshape(bp, groups, s5_n), hi.reshape(bp, groups, s5_n), sg,
                                     nk.reshape(bp, WINDOW, SWA_KV_HEADS, SWA_HD),
                                     nv.reshape(bp, WINDOW, SWA_KV_HEADS, SWA_HD))):
            lst.append(val)

        pa, pb, pc, pg = _inproj(hs, g_mix_r, w_a, w_b, l, widths, tm_s)
        ya, hr, hi = _s5(pa, h0r_s, h0i_s, l, s5_params, l, ts)
        yb, sg = _gla_sample(pb, state_gla, wa2, ba, gn, l, ts, _tile(bs, 16))
        yc, nk, nv = _swa_sample(pc, ck, cv, sink_cols, cos_s, sin_s, l, swa_heads, ts, _tile(bs, 8))
        hs = _merge_ffn(hs, ya, yb, yc, pg, wb, wo, g_ffn_r, wg, wu, wd, gf, l, tm_s, last)
        for lst, val in zip(outs_s, (hr.reshape(bs, groups, s5_n), hi.reshape(bs, groups, s5_n), sg, nk, nv)):
            lst.append(val)

    outs_s = [jnp.stack(o) for o in outs_s]
    outs_s[3], outs_s[4] = from_lanes(outs_s[3]), from_lanes(outs_s[4])
    return (hp.reshape(bp, tp, d), hs.reshape(bs, ts, d), *[jnp.stack(o) for o in outs_p], *outs_s)
```

```python
import functools

import jax
import jax.numpy as jnp
from jax import lax
from jax.experimental import pallas as pl
from jax.experimental.pallas import tpu as pltpu

F32 = jnp.float32
BF16 = jnp.bfloat16

EPS = 1e-6
GLA_TAU = 16.0
GLA_CHUNK = 64
SWA_KV_HEADS = 2
SWA_HD = 64
WINDOW = 128
ROPE_THETA = 10000.0
PAST_LEN = 16384

SUBLANES = 8
LANES = 128
VMEM_LIMIT = 56 * 1024 * 1024

NT_DIMS = (((1,), (1,)), ((), ()))
TN_DIMS = (((0,), (0,)), ((), ()))


def _params(*sem):
    return pltpu.CompilerParams(dimension_semantics=sem, vmem_limit_bytes=VMEM_LIMIT)


def _dot(a, b):
    return jnp.dot(a, b, preferred_element_type=F32)


def _rms(x, g):
    return x * lax.rsqrt(jnp.mean(x * x, axis=-1, keepdims=True) + EPS) * g


def _full(shape):
    nd = len(shape)
    return pl.BlockSpec(shape, lambda *_: (0,) * nd)


def _layer(a, l):
    nd = a.ndim - 1
    return pl.BlockSpec((None,) + a.shape[1:], lambda *_: (l,) + (0,) * nd, pipeline_mode=pl.Buffered(1))


def _inproj_kernel(x_ref, g_ref, wa_ref, wb_ref, *refs, n_cast):
    cast_in, o_refs, cast_out = refs[:n_cast], refs[n_cast:n_cast + 4], refs[n_cast + 4:]
    xb = _rms(x_ref[...], g_ref[...]).astype(BF16)
    for w_ref, group in ((wa_ref, o_refs[:2]), (wb_ref, o_refs[2:])):
        off = 0
        for o_ref in group:
            n = o_ref.shape[-1]
            for c0 in range(0, n, 512):
                c1 = min(c0 + 512, n)
                o_ref[:, c0:c1] = lax.dot_general(xb, w_ref[off + c0:off + c1, :], NT_DIMS,
                                                  preferred_element_type=F32)
            off += n
    for src, dst in zip(cast_in, cast_out):
        dst[...] = src[...].astype(BF16)


def _inproj(x, g, wa, wb, l, widths, tm, to_cast=()):
    n, d = x.shape
    steps = n // tm
    assert wa.shape[1] == sum(widths[:2]) and wb.shape[1] == sum(widths[2:])
    flat = [w.reshape(-1, w.shape[-1]) for w in to_cast]
    assert all(f.shape[0] % (steps * 2 * SUBLANES) == 0 for f in flat)
    slice_spec = lambda f: pl.BlockSpec((f.shape[0] // steps, f.shape[1]), lambda i: (i, 0))
    outs = pl.pallas_call(
        functools.partial(_inproj_kernel, n_cast=len(flat)),
        grid=(steps,),
        in_specs=[pl.BlockSpec((tm, d), lambda i: (i, 0)), _layer(g, l), _layer(wa, l), _layer(wb, l)]
        + [slice_spec(f) for f in flat],
        out_specs=[pl.BlockSpec((tm, wd), lambda i: (i, 0)) for wd in widths] + [slice_spec(f) for f in flat],
        out_shape=[jax.ShapeDtypeStruct((n, wd), F32) for wd in widths]
        + [jax.ShapeDtypeStruct(f.shape, BF16) for f in flat],
        compiler_params=_params("parallel"),
        name="inproj",
    )(x, g, wa, wb, *flat)
    return outs[:4], [o.reshape(w.shape) for o, w in zip(outs[4:], to_cast)]


def _s5_prep_kernel(lr_ref, li_ref, ldt_ref, br_ref, bi_ref, ar_ref, ai_ref, bbr_ref, bbi_ref):
    lr, li = lr_ref[...], li_ref[...]
    dt = jnp.exp(ldt_ref[...])
    mag = jnp.exp(lr * dt)
    ar = mag * jnp.cos(li * dt)
    ai = mag * jnp.sin(li * dt)
    den = lr * lr + li * li
    zr, zi = ar - 1.0, ai
    er = (zr * lr + zi * li) / den
    ei = (zi * lr - zr * li) / den
    ar_ref[...] = ar
    ai_ref[...] = ai
    br, bi = br_ref[...], bi_ref[...]
    bbr_ref[...] = er[None] * br - ei[None] * bi
    bbi_ref[...] = er[None] * bi + ei[None] * br


def _s5_prep(lam_re, lam_im, log_dt, b_re, b_im):
    depth, g, c, n = b_re.shape
    dg = depth * g
    brt = jnp.transpose(b_re.reshape(dg, c, n), (1, 0, 2))
    bit = jnp.transpose(b_im.reshape(dg, c, n), (1, 0, 2))
    ar, ai, bbr, bbi = pl.pallas_call(
        _s5_prep_kernel,
        out_shape=[jax.ShapeDtypeStruct((dg, n), F32)] * 2 + [jax.ShapeDtypeStruct((c, dg, n), F32)] * 2,
        name="s5_prep",
    )(lam_re.reshape(dg, n), lam_im.reshape(dg, n), log_dt.reshape(dg, 1), brt, bit)
    unt = lambda a: jnp.transpose(a, (1, 0, 2)).reshape(depth, g, c, n)
    return ar.reshape(depth, 1, g * n), ai.reshape(depth, 1, g * n), unt(bbr), unt(bbi)


def _s5_pack(bbr, bbi, c_re, c_im):
    depth, g, c, n = bbr.shape
    gp = LANES // c
    ks = g // gp
    eye = jnp.eye(gp, dtype=F32)

    def pack_b(b):
        return jnp.einsum("lkgcn,gh->lkgchn", b.reshape(depth, ks, gp, c, n), eye).reshape(
            depth, ks, gp * c, gp * n).astype(BF16)

    def pack_c(cm):
        return jnp.einsum("lkgnc,gh->lkgnhc", cm.reshape(depth, ks, gp, n, c), eye).reshape(
            depth, ks, gp * n, gp * c).astype(BF16)

    return pack_b(bbr), pack_b(bbi), pack_c(c_re), pack_c(c_im)


S5_COL_BLOCK = 1024
S5_UNROLL = 8


def _s5_kernel(u_ref, h0r_ref, h0i_ref, ar_ref, ai_ref, bre_ref, bim_ref, cre_ref, cim_ref, d_ref, wglu_ref,
               bglu_ref, y_ref, hr_out, hi_out, us_s, ys_s, xr_s, xi_s, hr_s, hi_s, *tmp_s, nb, steps):
    c = pl.program_id(0)

    @pl.when(c == 0)
    def _():
        hr_s[...] = h0r_ref[...]
        hi_s[...] = h0i_ref[...]

    ks, kw, nw = bre_ref.shape
    by_sequence = u_ref.ndim == 3
    if by_sequence:
        for b in range(nb):
            for k in range(ks):
                us_s[k, pl.ds(b, steps, stride=nb), :] = u_ref[b, :, k * kw:(k + 1) * kw]
    else:
        for k in range(ks):
            tmp_s[0][k] = u_ref[:, k * kw:(k + 1) * kw]
        for t in range(steps):
            for k in range(ks):
                us_s[k, t * nb:(t + 1) * nb, :] = tmp_s[0][k, pl.ds(t, nb, stride=steps), :]

    def x_proj(k):
        uk = us_s[k].astype(BF16)
        xr_s[:, k * nw:(k + 1) * nw] = _dot(uk, bre_ref[k])
        xi_s[:, k * nw:(k + 1) * nw] = _dot(uk, bim_ref[k])

    def y_proj(k):
        ss = slice(k * nw, (k + 1) * nw)
        yk = _dot(xr_s[:, ss].astype(BF16), cre_ref[k]) - _dot(xi_s[:, ss].astype(BF16), cim_ref[k])
        ys_s[k] = jax.nn.gelu(yk + d_ref[:, k * kw:(k + 1) * kw] * us_s[k])

    def scan(rb, cs, unrolled):
        rows = slice(rb * SUBLANES, (rb + 1) * SUBLANES)
        width = cs.stop - cs.start
        a_r = jnp.broadcast_to(ar_ref[:, cs], (SUBLANES, width))
        a_i = jnp.broadcast_to(ai_ref[:, cs], (SUBLANES, width))

        def step(t, carry):
            hr, hi = carry
            r0 = t * nb + rb * SUBLANES
            tr = slice(r0, r0 + SUBLANES) if unrolled else pl.ds(pl.multiple_of(r0, SUBLANES), SUBLANES)
            nhr = a_r * hr - a_i * hi + xr_s[tr, cs]
            nhi = a_r * hi + a_i * hr + xi_s[tr, cs]
            xr_s[tr, cs] = nhr
            xi_s[tr, cs] = nhi
            return nhr, nhi

        carry = (hr_s[rows, cs], hi_s[rows, cs])
        if unrolled:
            for t in range(steps):
                carry = step(t, carry)
        else:
            carry = lax.fori_loop(0, steps, step, carry, unroll=min(steps, S5_UNROLL))
        hr_s[rows, cs], hi_s[rows, cs] = carry

    if by_sequence and nb == SUBLANES:
        x_proj(0)
        for k in range(ks):
            if k + 1 < ks:
                x_proj(k + 1)
            if k >= 1:
                y_proj(k - 1)
            scan(0, slice(k * nw, (k + 1) * nw), True)
        y_proj(ks - 1)
    else:
        for k in range(ks):
            x_proj(k)
        for rb in range(nb // SUBLANES):
            for cb in range(xr_s.shape[1] // S5_COL_BLOCK):
                scan(rb, slice(cb * S5_COL_BLOCK, (cb + 1) * S5_COL_BLOCK), False)
        for k in range(ks):
            y_proj(k)
    y = jnp.concatenate([ys_s[k] for k in range(ks)], axis=1)
    y = y * jax.nn.sigmoid(_dot(y.astype(BF16), wglu_ref[...]) + bglu_ref[...])
    for k in range(ks):
        ys_s[k] = y[:, k * kw:(k + 1) * kw]
    if by_sequence:
        for b in range(nb):
            for k in range(ks):
                y_ref[b, :, k * kw:(k + 1) * kw] = ys_s[k, pl.ds(b, steps, stride=nb), :]
    else:
        for t in range(steps):
            for k in range(ks):
                tmp_s[0][k, pl.ds(t, nb, stride=steps), :] = ys_s[k, t * nb:(t + 1) * nb, :]
        for k in range(ks):
            y_ref[:, k * kw:(k + 1) * kw] = tmp_s[0][k]

    @pl.when(c == pl.num_programs(0) - 1)
    def _():
        hr_out[...] = hr_s[...]
        hi_out[...] = hi_s[...]


def _s5(u, h0r, h0i, l0, params, l, steps):
    _, nb, ns = h0r.shape
    w = u.shape[-1]
    blk = steps * nb
    ks = params[2].shape[1]
    if u.ndim == 3:
        grid = (u.shape[1] // steps,)
        u_spec = pl.BlockSpec((nb, steps, w), lambda i: (0, i, 0))
        tmp = []
    else:
        assert u.shape[0] == blk
        grid = (1,)
        u_spec = _full(u.shape)
        tmp = [pltpu.VMEM((ks, blk, LANES), F32)]
    kern = functools.partial(_s5_kernel, nb=nb, steps=steps)
    return pl.pallas_call(
        kern,
        grid=grid,
        in_specs=[u_spec, _layer(h0r, l0), _layer(h0i, l0)] + [_layer(a, l) for a in params],
        out_specs=[u_spec, _full((nb, ns)), _full((nb, ns))],
        out_shape=[jax.ShapeDtypeStruct(u.shape, F32), jax.ShapeDtypeStruct((nb, ns), F32),
                   jax.ShapeDtypeStruct((nb, ns), F32)],
        scratch_shapes=[pltpu.VMEM((ks, blk, LANES), F32), pltpu.VMEM((ks, blk, LANES), F32),
                        pltpu.VMEM((blk, ns), F32), pltpu.VMEM((blk, ns), F32), pltpu.VMEM((nb, ns), F32),
                        pltpu.VMEM((nb, ns), F32)] + tmp,
        compiler_params=_params("arbitrary"),
        name="s5_scan",
    )(u, h0r, h0i, *params)


def _log2(n):
    assert n & (n - 1) == 0, n
    return n.bit_length() - 1


def _chunk_masks(rows, chunk):
    sh = _log2(chunk)
    rr = lax.broadcasted_iota(jnp.int32, (rows, rows), 0)
    cc = lax.broadcasted_iota(jnp.int32, (rows, rows), 1)
    same = lax.shift_right_logical(rr, sh) == lax.shift_right_logical(cc, sh)
    return same, same & (cc <= rr)


def _gla_prologue(p_ref, wa_ref, ba_ref, chunk, qk):
    rows = p_ref.shape[0]
    q = p_ref[:, 0:qk]
    k = p_ref[:, qk:2 * qk]
    alow = p_ref[:, p_ref.shape[1] - LANES:].astype(BF16)
    z = _dot(alow, wa_ref[...]) + ba_ref[...]
    la = (jnp.minimum(z, 0.0) - jnp.log(1.0 + jnp.exp(-jnp.abs(z)))) * (1.0 / GLA_TAU)
    same, causal = _chunk_masks(rows, chunk)
    tri = jnp.where(causal, 1.0, 0.0).astype(BF16)
    blk = jnp.where(same, 1.0, 0.0).astype(BF16)
    la_hi = la.astype(BF16)
    la_lo = (la - la_hi.astype(F32)).astype(BF16)
    b = _dot(tri, la_hi) + _dot(tri, la_lo)
    bl = _dot(blk, la_hi) + _dot(blk, la_lo)
    return q, k, b, bl


def _gla_out(o, gn, r):
    on = o * lax.rsqrt(jnp.mean(o * o, axis=-1, keepdims=True) + EPS) * gn
    return on * (r * jax.nn.sigmoid(r))


def _gla_prompt_kernel(p_ref, wa_ref, ba_ref, gn_ref, y_ref, s_ref, st_s, *, heads, dk, dv, chunk):
    c = pl.program_id(1)

    @pl.when(c == 0)
    def _():
        st_s[...] = jnp.zeros_like(st_s)

    rows = p_ref.shape[0]
    qk, vw = heads * dk, heads * dv
    q, k, b, bl = _gla_prologue(p_ref, wa_ref, ba_ref, chunk, qk)
    v_off, r_off = 2 * qk, 2 * qk + vw
    qd = (q * (dk ** -0.5) * jnp.exp(b)).astype(BF16)
    kd = (k * jnp.exp(-b)).astype(BF16)
    kst = (k * jnp.exp(bl - b)).astype(BF16)
    dec = jnp.exp(bl)
    gn = gn_ref[...]
    zero = jnp.zeros((), BF16)
    qk_head = lax.shift_right_logical(lax.broadcasted_iota(jnp.int32, (chunk, qk), 1), _log2(dk))
    v_head = lax.shift_right_logical(lax.broadcasted_iota(jnp.int32, (chunk, vw), 1), _log2(dv))
    rr = lax.broadcasted_iota(jnp.int32, (chunk, heads * chunk), 0)
    cc = lax.broadcasted_iota(jnp.int32, (chunk, heads * chunk), 1) & (chunk - 1)
    causal = cc <= rr
    nchunk = rows // chunk
    by_head = lambda x, head: jnp.concatenate([jnp.where(head == h, x, zero) for h in range(heads)], axis=0)
    att, ds_t, v_diag, qd_rows = [], [], [], []
    for ci in range(nchunk):
        rs = slice(ci * chunk, (ci + 1) * chunk)
        v_c = p_ref[rs, v_off:v_off + vw].astype(BF16)
        v_rows = jnp.concatenate([v_c[:, h * dv:(h + 1) * dv] for h in range(heads)], axis=0)
        v_diag.append(by_head(v_c, v_head))
        qd_rows.append(by_head(qd[rs], qk_head))
        att.append(lax.dot_general(qd[rs], by_head(kd[rs], qk_head), NT_DIMS, preferred_element_type=F32))
        ds_t.append(lax.dot_general(v_rows, by_head(kst[rs], qk_head), TN_DIMS, preferred_element_type=F32))
    o_intra = [_dot(jnp.where(causal, att[ci], 0.0).astype(BF16), v_diag[ci]) for ci in range(nchunk)]
    st = st_s[...]
    o_inter = []
    for ci in range(nchunk):
        o_inter.append(lax.dot_general(qd_rows[ci], st.astype(BF16), NT_DIMS, preferred_element_type=F32))
        st = dec[ci * chunk:ci * chunk + 1, :] * st + ds_t[ci]
    st_s[...] = st
    for ci in range(nchunk):
        rs = slice(ci * chunk, (ci + 1) * chunk)
        for h in range(heads):
            o = o_intra[ci][:, h * dv:(h + 1) * dv] + o_inter[ci][h * chunk:(h + 1) * chunk]
            r = p_ref[rs, r_off + h * dv:r_off + (h + 1) * dv]
            y_ref[rs, h * dv:(h + 1) * dv] = _gla_out(o, gn, r)

    @pl.when(c == pl.num_programs(1) - 1)
    def _():
        for h in range(heads):
            s_ref[0, h] = st_s[:, h * dk:(h + 1) * dk].T


def _gla_prompt(p, wa, ba, gn, l, bsz, heads, dk, dv, rows):
    n, pw = p.shape
    t = n // bsz
    nblk = t // rows
    kern = functools.partial(_gla_prompt_kernel, heads=heads, dk=dk, dv=dv, chunk=GLA_CHUNK)
    return pl.pallas_call(
        kern,
        grid=(bsz, nblk),
        in_specs=[pl.BlockSpec((rows, pw), lambda b, c: (b * nblk + c, 0)), _layer(wa, l), _layer(ba, l),
                  _layer(gn, l)],
        out_specs=[pl.BlockSpec((rows, heads * dv), lambda b, c: (b * nblk + c, 0)),
                   pl.BlockSpec((1, heads, dk, dv), lambda b, c: (b, 0, 0, 0))],
        out_shape=[jax.ShapeDtypeStruct((n, heads * dv), F32), jax.ShapeDtypeStruct((bsz, heads, dk, dv), F32)],
        scratch_shapes=[pltpu.VMEM((dv, heads * dk), F32)],
        compiler_params=_params("parallel", "arbitrary"),
        name="gla_prompt",
    )(p, wa, ba, gn)


def _gla_sample_kernel(p_ref, s0_ref, wa_ref, ba_ref, gn_ref, y_ref, s_ref, *, heads, dk, dv, t):
    rows = p_ref.shape[0]
    qk = heads * dk
    q, k, b, bl = _gla_prologue(p_ref, wa_ref, ba_ref, t, qk)
    _, cmask = _chunk_masks(SUBLANES, t)
    v_off, r_off = 2 * qk, 2 * qk + heads * dv
    qd = (q * (dk ** -0.5) * jnp.exp(b)).astype(BF16)
    kd = (k * jnp.exp(-b)).astype(BF16)
    kst = k * jnp.exp(bl - b)
    dec = jnp.exp(bl)
    gn = gn_ref[...]
    per_tile = SUBLANES // t
    row_id = lax.broadcasted_iota(jnp.int32, (SUBLANES, 1), 0)
    att, o_state = {}, {}
    for ti in range(rows // SUBLANES):
        rs = slice(ti * SUBLANES, (ti + 1) * SUBLANES)
        dcols = []
        for j in range(per_tile):
            drow = dec[ti * SUBLANES + j * t:ti * SUBLANES + j * t + 1, :]
            dcols.append([jnp.broadcast_to(drow[:, g * LANES:(g + 1) * LANES], (LANES, LANES)).T
                          for g in range(qk // LANES)])
        for h in range(heads):
            hs = slice(h * dk, (h + 1) * dk)
            vh = p_ref[rs, v_off + h * dv:v_off + (h + 1) * dv].astype(BF16)
            att[ti, h] = lax.dot_general(qd[rs, hs], kd[rs, hs], NT_DIMS, preferred_element_type=F32)
            o = None
            for j in range(per_tile):
                bidx = ti * per_tile + j
                in_seq = (row_id >= j * t) & (row_id < (j + 1) * t)
                s0 = s0_ref[bidx, h]
                oj = jnp.where(in_seq, _dot(qd[rs, hs], s0.astype(BF16)), 0.0)
                o = oj if o is None else o + oj
                kj = jnp.where(in_seq, kst[rs, hs], 0.0).astype(BF16)
                ds = lax.dot_general(kj, vh, TN_DIMS, preferred_element_type=F32)
                g, lo = (h * dk) // LANES, (h * dk) % LANES
                dcol = dcols[j][g][lo:lo + dk, 0:dv]
                s_ref[bidx, h] = dcol * s0 + ds
            o_state[ti, h] = o
    for ti in range(rows // SUBLANES):
        rs = slice(ti * SUBLANES, (ti + 1) * SUBLANES)
        for h in range(heads):
            vh = p_ref[rs, v_off + h * dv:v_off + (h + 1) * dv].astype(BF16)
            o = _dot(jnp.where(cmask, att[ti, h], 0.0).astype(BF16), vh) + o_state[ti, h]
            r = p_ref[rs, r_off + h * dv:r_off + (h + 1) * dv]
            y_ref[rs, h * dv:(h + 1) * dv] = _gla_out(o, gn, r)


def _gla_sample(p, s0, wa, ba, gn, l, t, nb):
    n, pw = p.shape
    _, bsz, heads, dk, dv = s0.shape
    rows = nb * t
    kern = functools.partial(_gla_sample_kernel, heads=heads, dk=dk, dv=dv, t=t)
    return pl.pallas_call(
        kern,
        grid=(bsz // nb,),
        in_specs=[pl.BlockSpec((rows, pw), lambda i: (i, 0)),
                  pl.BlockSpec((None, nb, heads, dk, dv), lambda i: (l, i, 0, 0, 0)),
                  _layer(wa, l), _layer(ba, l), _layer(gn, l)],
        out_specs=[pl.BlockSpec((rows, heads * dv), lambda i: (i, 0)),
                   pl.BlockSpec((nb, heads, dk, dv), lambda i: (i, 0, 0, 0))],
        out_shape=[jax.ShapeDtypeStruct((n, heads * dv), F32), jax.ShapeDtypeStruct(s0.shape[1:], F32)],
        compiler_params=_params("parallel"),
        name="gla_sample",
    )(p, s0, wa, ba, gn)


def _first_half(shape):
    lane = lax.broadcasted_iota(jnp.int32, shape, 1)
    return (lane & (SWA_HD - 1)) < SWA_HD // 2


def _rope_table_kernel(pos_ref, inv_ref, cos_ref, sin_ref):
    ang = pos_ref[...] * inv_ref[...]
    s = jnp.sin(ang)
    cos_ref[...] = jnp.cos(ang)
    sin_ref[...] = jnp.where(_first_half(ang.shape), -s, s)


def _rope_table(pos, inv):
    rows = pos.shape[0]
    return pl.pallas_call(
        _rope_table_kernel,
        out_shape=[jax.ShapeDtypeStruct((rows, LANES), F32)] * 2,
        name="rope_table",
    )(pos.astype(F32).reshape(rows, 1), inv)


def _rope(x, cos, sin):
    half = SWA_HD // 2
    first_half = _first_half(cos.shape)
    out = []
    for g in range(x.shape[1] // LANES):
        xg = x[:, g * LANES:(g + 1) * LANES]
        swapped = jnp.where(first_half, pltpu.roll(xg, LANES - half, 1), pltpu.roll(xg, half, 1))
        out.append(xg * cos + swapped * sin)
    return out[0] if len(out) == 1 else jnp.concatenate(out, axis=1)


def _swa_prompt_kernel(sink_ref, cur_ref, cos_ref, sin_ref, y_ref, nk_ref, nv_ref, prev_s, s_s, pc_s, *, heads, l):
    i = pl.program_id(1)
    w = WINDOW
    qw = heads * SWA_HD
    kvw = SWA_KV_HEADS * SWA_HD
    gq = heads // SWA_KV_HEADS
    nq = cur_ref.shape[0] // w

    @pl.when(i == 0)
    def _():
        prev_s[...] = jnp.zeros_like(prev_s)

    cos, sin = cos_ref[...], sin_ref[...]
    q = (_rope(cur_ref[:, 0:qw], cos, sin) * (SWA_HD ** -0.5)).astype(BF16)
    k_cur = _rope(cur_ref[:, qw:qw + kvw], cos, sin)
    v_cur = cur_ref[:, qw + kvw:qw + 2 * kvw]
    new = [x.astype(BF16) for x in (k_cur, pltpu.roll(k_cur, SWA_HD, 1), v_cur, pltpu.roll(v_cur, SWA_HD, 1))]
    k2, k2r, v2, v2r = [jnp.concatenate([prev_s[j], new[j]], axis=0) for j in range(4)]
    zero = jnp.zeros((), BF16)
    lo2 = lax.broadcasted_iota(jnp.int32, ((nq + 1) * w, LANES), 1) < SWA_HD
    lo = lax.broadcasted_iota(jnp.int32, (w, LANES), 1) < SWA_HD
    k_both = (jnp.where(lo2, k2, k2r), jnp.where(lo2, k2r, k2))
    v_half = {(0, 0): jnp.where(lo2, v2, zero), (1, 1): jnp.where(lo2, zero, v2),
              (0, 1): jnp.where(lo2, zero, v2r), (1, 0): jnp.where(lo2, v2r, zero)}
    upper = lax.broadcasted_iota(jnp.int32, (w, w), 1) > lax.broadcasted_iota(jnp.int32, (w, w), 0)
    no_prev = jnp.where(i > 0, 0.0, -jnp.inf)
    slot = lambda c, h: slice((c * heads + h) * w, (c * heads + h + 1) * w)
    for c in range(nq):
        for kv in range(SWA_KV_HEADS):
            qm = []
            for h in range(kv * gq, (kv + 1) * gq):
                qg = q[c * w:(c + 1) * w, (h // 2) * LANES:(h // 2 + 1) * LANES]
                qm.append(jnp.where(lo, qg, zero) if h % 2 == 0 else jnp.where(lo, zero, qg))
            s_s[(c * heads + kv * gq) * w:(c * heads + (kv + 1) * gq) * w, :] = lax.dot_general(
                jnp.concatenate(qm, axis=0), k_both[kv][c * w:(c + 2) * w], NT_DIMS,
                preferred_element_type=F32)
    sink_term = {}
    for c in range(nq):
        for h in range(heads):
            s_prev = s_s[slot(c, h), 0:w] + no_prev if c == 0 else s_s[slot(c, h), 0:w]
            sp = jnp.where(upper, s_prev, s_s[slot(c, h), w:2 * w])
            sink = sink_ref[l, h]
            m = jnp.maximum(jnp.max(sp, axis=-1, keepdims=True), sink)
            p = jnp.exp(sp - m)
            pc_s[slot(c, h), 0:w] = jnp.where(upper, p, 0.0).astype(BF16)
            pc_s[slot(c, h), w:2 * w] = jnp.where(upper, 0.0, p).astype(BF16)
            sink_term[c, h] = jnp.exp(sink - m)
    num = {(c, h): _dot(pc_s[slot(c, h), :], v_half[(h // gq, h % 2)][c * w:(c + 2) * w])
           for c in range(nq) for h in range(heads)}
    den_all = _dot(pc_s[...], jnp.ones((2 * w, LANES), BF16))
    for c in range(nq):
        for g in range(heads // 2):
            he, ho = 2 * g, 2 * g + 1
            den = jnp.where(lo, den_all[slot(c, he)] + sink_term[c, he], den_all[slot(c, ho)] + sink_term[c, ho])
            y_ref[c * w:(c + 1) * w, g * LANES:(g + 1) * LANES] = (num[c, he] + num[c, ho]) * (1.0 / den)
    for j in range(4):
        prev_s[j] = new[j][(nq - 1) * w:]

    @pl.when(i == pl.num_programs(1) - 1)
    def _():
        nk_ref[0] = k_cur[(nq - 1) * w:]
        nv_ref[0] = v_cur[(nq - 1) * w:]


def _swa_prompt(p, sinks, cos, sin, l, bsz, heads, nq):
    n, pw = p.shape
    t = n // bsz
    rows = nq * WINDOW
    nblk = t // rows
    qw, kvw = heads * SWA_HD, SWA_KV_HEADS * SWA_HD
    kern = functools.partial(_swa_prompt_kernel, heads=heads, l=l)
    tab = pl.BlockSpec((rows, LANES), lambda b, i: (i, 0))
    return pl.pallas_call(
        kern,
        grid=(bsz, nblk),
        in_specs=[pl.BlockSpec(memory_space=pltpu.SMEM),
                  pl.BlockSpec((rows, pw), lambda b, i: (b * nblk + i, 0)), tab, tab],
        out_specs=[pl.BlockSpec((rows, qw), lambda b, i: (b * nblk + i, 0)),
                   pl.BlockSpec((1, WINDOW, kvw), lambda b, i: (b, 0, 0)),
                   pl.BlockSpec((1, WINDOW, kvw), lambda b, i: (b, 0, 0))],
        out_shape=[jax.ShapeDtypeStruct((n, qw), F32), jax.ShapeDtypeStruct((bsz, WINDOW, kvw), F32),
                   jax.ShapeDtypeStruct((bsz, WINDOW, kvw), F32)],
        scratch_shapes=[pltpu.VMEM((4, WINDOW, LANES), BF16),
                        pltpu.VMEM((nq * heads * WINDOW, 2 * WINDOW), F32),
                        pltpu.VMEM((nq * heads * WINDOW, 2 * WINDOW), BF16)],
        compiler_params=_params("parallel", "arbitrary"),
        name="swa_prompt",
    )(sinks, p, cos, sin)


def _expand_kv(x, lane):
    other = pltpu.roll(x, SWA_HD, 1)
    first = lane < SWA_HD
    kv0 = jnp.where(first, x, other)
    kv1 = jnp.where(first, other, x)
    return kv0, kv1


def _swa_sample_kernel(sinkcol_ref, p_ref, ck_ref, cv_ref, cos_ref, sin_ref, y_ref, nk_ref, nv_ref, *, heads, t, nb):
    w = ck_ref.shape[3]
    qw = heads * SWA_HD
    kvw = SWA_KV_HEADS * SWA_HD
    gq = heads // SWA_KV_HEADS
    per_tile = SUBLANES // t
    nrow = heads * t
    cos, sin = cos_ref[...], sin_ref[...]
    lane_kv = lax.broadcasted_iota(jnp.int32, (1, kvw), 1)
    qm_row = lax.broadcasted_iota(jnp.int32, (nrow, qw), 0)
    qm_lane = lax.broadcasted_iota(jnp.int32, (nrow, qw), 1)
    qmask = lax.shift_right_logical(qm_row, _log2(t)) == lax.shift_right_logical(qm_lane, _log2(SWA_HD))
    row_step = lax.broadcasted_iota(jnp.int32, (nrow, w), 0) & (t - 1)
    cache_ok = lax.broadcasted_iota(jnp.int32, (nrow, w), 1) > row_step
    new_col = lax.broadcasted_iota(jnp.int32, (nrow, SUBLANES), 1)
    new_row_step = lax.broadcasted_iota(jnp.int32, (nrow, SUBLANES), 0) & (t - 1)
    sink_col = sinkcol_ref[...]
    lane_w = lax.broadcasted_iota(jnp.int32, (SWA_HD, w), 1)
    widen = lambda a, b: jnp.concatenate([a] * (gq // 2) + [b] * (gq // 2), axis=1).astype(BF16)
    per_head = lambda c: jnp.concatenate([c[kv] for kv in range(SWA_KV_HEADS) for _ in range(gq)],
                                         axis=0).astype(BF16)
    pad_rows = jnp.zeros((LANES - 2 * SUBLANES, kvw), F32)
    new_k, new_v, scores = [], [], []
    for ti in range(nb // per_tile):
        rs = slice(ti * SUBLANES, (ti + 1) * SUBLANES)
        q8 = _rope(p_ref[rs, 0:qw], cos, sin) * (SWA_HD ** -0.5)
        kn8 = _rope(p_ref[rs, qw:qw + kvw], cos, sin)
        vn8 = p_ref[rs, qw + kvw:qw + 2 * kvw]
        kn_x = widen(*_expand_kv(kn8, lane_kv))
        new_k.append(kn8)
        new_v.append(vn8)
        for j in range(per_tile):
            bidx = ti * per_tile + j
            qj = q8[j * t:(j + 1) * t, :]
            qm = jnp.where(qmask, jnp.concatenate([qj] * heads, axis=0), 0.0).astype(BF16)
            s_c = _dot(qm, per_head(ck_ref[bidx]))
            s_n = lax.dot_general(qm, kn_x, NT_DIMS, preferred_element_type=F32)
            scores.append((s_c, s_n))
    probs = []
    for bidx, (s_c, s_n) in enumerate(scores):
        j = bidx % per_tile
        s_c = jnp.where(cache_ok, s_c, -jnp.inf)
        new_ok = (new_col >= j * t) & (new_col - j * t <= new_row_step) & (new_col < (j + 1) * t)
        s_n = jnp.where(new_ok, s_n, -jnp.inf)
        m = jnp.maximum(jnp.maximum(jnp.max(s_c, axis=1, keepdims=True), jnp.max(s_n, axis=1, keepdims=True)),
                        sink_col)
        p_c = jnp.exp(s_c - m)
        p_n = jnp.exp(s_n - m)
        den = jnp.sum(p_c, axis=1, keepdims=True) + jnp.sum(p_n, axis=1, keepdims=True) + jnp.exp(sink_col - m)
        rden = 1.0 / den
        probs.append(((p_c * rden).astype(BF16), (p_n * rden).astype(BF16)))
    outs = []
    for bidx, (p_c, p_n) in enumerate(probs):
        ti, j = bidx // per_tile, bidx % per_tile
        vn_x = widen(*_expand_kv(new_v[ti], lane_kv))
        outs.append(lax.dot_general(p_c, per_head(cv_ref[bidx]), NT_DIMS, preferred_element_type=F32)
                    + _dot(p_n, vn_x))
    for bidx, o in enumerate(outs):
        ti, j = bidx // per_tile, bidx % per_tile
        o = jnp.where(qmask, o, 0.0)
        acc = o[0:SUBLANES]
        for g in range(1, nrow // SUBLANES):
            acc = acc + o[g * SUBLANES:(g + 1) * SUBLANES]
        for g in range(1, per_tile):
            acc = acc + pltpu.roll(acc, SUBLANES - g * t, 0)
        y_ref[ti * SUBLANES + j * t:ti * SUBLANES + (j + 1) * t, :] = acc[0:t]
    for ti in range(nb // per_tile):
        kv_t = jnp.concatenate([new_k[ti], new_v[ti], pad_rows], axis=0).T
        for j in range(per_tile):
            bidx = ti * per_tile + j
            for c_ref, n_ref, col0 in ((ck_ref, nk_ref, 0), (cv_ref, nv_ref, SUBLANES)):
                for kv in range(SWA_KV_HEADS):
                    tail = pltpu.roll(kv_t[kv * SWA_HD:(kv + 1) * SWA_HD, :], w - t - col0 - j * t, 1)
                    n_ref[bidx, kv] = jnp.where(lane_w < w - t, pltpu.roll(c_ref[bidx, kv], w - t, 1), tail)


def _swa_sample(p, ck, cv, sink_cols, cos, sin, l, heads, t, nb):
    n, pw = p.shape
    _, bsz, kvh, hd, w = ck.shape
    qw = heads * SWA_HD
    kern = functools.partial(_swa_sample_kernel, heads=heads, t=t, nb=nb)
    cache = pl.BlockSpec((None, nb, kvh, hd, w), lambda i: (l, i, 0, 0, 0))
    new_cache = pl.BlockSpec((nb, kvh, hd, w), lambda i: (i, 0, 0, 0))
    return pl.pallas_call(
        kern,
        grid=(bsz // nb,),
        in_specs=[_layer(sink_cols, l), pl.BlockSpec((nb * t, pw), lambda i: (i, 0)), cache, cache,
                  _full(cos.shape), _full(sin.shape)],
        out_specs=[pl.BlockSpec((nb * t, qw), lambda i: (i, 0)), new_cache, new_cache],
        out_shape=[jax.ShapeDtypeStruct((n, qw), F32), jax.ShapeDtypeStruct(ck.shape[1:], F32),
                   jax.ShapeDtypeStruct(cv.shape[1:], F32)],
        compiler_params=_params("parallel"),
        name="swa_sample",
    )(sink_cols, p, ck, cv, cos, sin)


FFN_HIDDEN_CHUNK = 1024


def _merge_ffn_kernel(x_ref, ya_ref, yb_ref, yc_ref, gt_ref, wb_ref, wo_ref, g_ref, wg_ref, wu_ref, wd_ref, gf_ref,
                      o_ref, *, final_norm):
    d = x_ref.shape[1]
    merged = None
    off = 0
    for i, y_ref in enumerate((ya_ref, yb_ref, yc_ref)):
        wdt = y_ref.shape[1]
        br = _dot(y_ref[...].astype(BF16), wb_ref[off:off + wdt, :])
        term = jax.nn.sigmoid(gt_ref[:, i * d:(i + 1) * d]) * br
        merged = term if merged is None else merged + term
        off += wdt
    x = x_ref[...] + _dot(merged.astype(BF16), wo_ref[...])
    hb = _rms(x, g_ref[...]).astype(BF16)
    hidden = wg_ref.shape[1]
    acc = x
    for c0 in range(0, hidden, FFN_HIDDEN_CHUNK):
        c1 = min(c0 + FFN_HIDDEN_CHUNK, hidden)
        gate = _dot(hb, wg_ref[:, c0:c1])
        up = _dot(hb, wu_ref[:, c0:c1])
        act = (gate * jax.nn.sigmoid(gate) * up).astype(BF16)
        acc = acc + _dot(act, wd_ref[c0:c1, :])
    o_ref[...] = _rms(acc, gf_ref[...]) if final_norm else acc


def _merge_ffn(x, ya, yb, yc, gates, wb, wo, g, wg, wu, wd, gf, l, tm, final_norm):
    n, d = x.shape
    row = lambda a: pl.BlockSpec((tm, a.shape[1]), lambda i: (i, 0))
    return pl.pallas_call(
        functools.partial(_merge_ffn_kernel, final_norm=final_norm),
        grid=(n // tm,),
        in_specs=[row(x), row(ya), row(yb), row(yc), row(gates), _layer(wb, l), _layer(wo, l), _layer(g, l),
                  _layer(wg, l), _layer(wu, l), _layer(wd, l), _full(gf.shape)],
        out_specs=pl.BlockSpec((tm, d), lambda i: (i, 0)),
        out_shape=jax.ShapeDtypeStruct((n, d), F32),
        compiler_params=_params("parallel"),
        name="merge_ffn",
    )(x, ya, yb, yc, gates, wb, wo, g, wg, wu, wd, gf)


def _tile(n, want):
    while n % want:
        want //= 2
    return want


def kernel(x_prompt, x_sample, state_s5_re, state_s5_im, state_gla, cache_swa_k, cache_swa_v, g_mix, w_in, s5_lambda_re, s5_lambda_im, s5_log_dt, s5_b_re, s5_b_im, s5_c_re, s5_c_im, s5_d, w_glu, b_glu, w_gla_a2, b_gla_a, g_gla_norm, swa_sinks, w_branch, w_out, g_ffn, w_ffn_gate, w_ffn_up, w_ffn_down, g_final):
    bp, tp, d = x_prompt.shape
    bs, ts, _ = x_sample.shape
    depth = w_in.shape[0]
    s5_w = s5_d.shape[1]
    groups, s5_n = s5_lambda_re.shape[1:]
    gla_qk = w_gla_a2.shape[2]
    gla_rank = w_gla_a2.shape[1]
    gla_heads, gla_dk, gla_dv = state_gla.shape[2:]
    gla_v = gla_heads * gla_dv
    swa_heads = swa_sinks.shape[1]
    swa_q = swa_heads * SWA_HD
    swa_kv = SWA_KV_HEADS * SWA_HD
    assert SWA_KV_HEADS * SWA_HD == LANES and bp % SUBLANES == 0 and bs % SUBLANES == 0 and SUBLANES % ts == 0
    assert tp % WINDOW == 0 and cache_swa_k.shape[2] == WINDOW

    o_a, o_b = 0, s5_w
    o_low = o_b + 2 * gla_qk + 2 * gla_v
    o_c = o_low + gla_rank
    o_g = o_c + swa_q + 2 * swa_kv
    widths = (s5_w, 2 * gla_qk + 2 * gla_v + LANES, swa_q + 2 * swa_kv, 3 * d)

    hp = x_prompt.reshape(bp * tp, d)
    hs = x_sample.reshape(bs * ts, d)
    inv = ROPE_THETA ** (-jnp.arange(SWA_HD // 2, dtype=F32) * (2.0 / SWA_HD))
    inv = jnp.tile(inv, 2 * LANES // SWA_HD).reshape(1, LANES)
    cos_p, sin_p = _rope_table(jnp.arange(tp, dtype=jnp.int32), inv)
    cos_s, sin_s = _rope_table(PAST_LEN + (jnp.arange(SUBLANES, dtype=jnp.int32) & (ts - 1)), inv)
    zeros_s5 = jnp.zeros((1, bp, groups * s5_n), F32)
    tm_p, tm_s = _tile(bp * tp, 512), _tile(bs * ts, 512)
    s5_steps = _tile(tp, 128)
    gla_rows = _tile(tp, 512)
    swa_nq = _tile(tp // WINDOW, 4)

    w_t = jnp.swapaxes(w_in, 1, 2)
    w_a = jnp.pad(w_t[:, :o_c].astype(BF16), ((0, 0), (0, LANES - gla_rank), (0, 0)))
    w_b = w_t[:, o_c:].astype(BF16)
    wa2 = jnp.concatenate([w_gla_a2, jnp.zeros((depth, LANES - gla_rank, gla_qk), F32)], axis=1).astype(BF16)
    ba = b_gla_a.reshape(depth, 1, gla_qk)
    gn = g_gla_norm.reshape(depth, 1, gla_dv)
    ar, ai, bbr, bbi = _s5_prep(s5_lambda_re, s5_lambda_im, s5_log_dt, s5_b_re, s5_b_im)
    bre, bim, cre, cim = _s5_pack(bbr, bbi, s5_c_re, s5_c_im)
    s5_d_r, b_glu_r = s5_d.reshape(depth, 1, s5_w), b_glu.reshape(depth, 1, s5_w)
    h0r_s = state_s5_re.reshape(depth, bs, groups * s5_n)
    h0i_s = state_s5_im.reshape(depth, bs, groups * s5_n)
    g_mix_r, g_ffn_r, gf = g_mix.reshape(depth, 1, d), g_ffn.reshape(depth, 1, d), g_final.reshape(1, d)
    later = (w_glu, w_branch, w_out, w_ffn_gate, w_ffn_up, w_ffn_down)
    cast_rows = (bp * tp // tm_p) * 2 * SUBLANES
    cast_in_call = all((w.shape[0] * w.shape[1]) % cast_rows == 0 for w in later)
    casted = None if cast_in_call else [w.astype(BF16) for w in later]
    sink_cols = jnp.repeat(swa_sinks, ts, axis=1).reshape(depth, swa_heads * ts, 1)
    to_lanes = lambda c: jnp.transpose(c, (0, 1, 3, 4, 2))
    from_lanes = lambda c: jnp.transpose(c, (0, 1, 4, 2, 3))
    ck, cv = to_lanes(cache_swa_k), to_lanes(cache_swa_v)

    outs_p = [[] for _ in range(5)]
    outs_s = [[] for _ in range(5)]
    for l in range(depth):
        last = l == depth - 1

        (pa, pb, pc, pg), new_casts = _inproj(hp, g_mix_r, w_a, w_b, l, widths, tm_p,
                                              later if casted is None else ())
        if casted is None:
            casted = new_casts
        wglu_b, wb, wo, wg, wu, wd = casted
        s5_params = (ar, ai, bre, bim, cre, cim, s5_d_r, wglu_b, b_glu_r)
        ya, hr, hi = _s5(pa.reshape(bp, tp, s5_w), zeros_s5, zeros_s5, 0, s5_params, l, s5_steps)
        ya = ya.reshape(bp * tp, s5_w)
        yb, sg = _gla_prompt(pb, wa2, ba, gn, l, bp, gla_heads, gla_dk, gla_dv, gla_rows)
        yc, nk, nv = _swa_prompt(pc, swa_sinks, cos_p, sin_p, l, bp, swa_heads, swa_nq)
        hp = _merge_ffn(hp, ya, yb, yc, pg, wb, wo, g_ffn_r, wg, wu, wd, gf, l, tm_p, last)
        for lst, val in zip(outs_p, (hr.reshape(bp, groups, s5_n), hi.reshape(bp, groups, s5_n), sg,
                                     nk.reshape(bp, WINDOW, SWA_KV_HEADS, SWA_HD),
                                     nv.reshape(bp, WINDOW, SWA_KV_HEADS, SWA_HD))):
            lst.append(val)

        (pa, pb, pc, pg), _ = _inproj(hs, g_mix_r, w_a, w_b, l, widths, tm_s)
        ya, hr, hi = _s5(pa, h0r_s, h0i_s, l, s5_params, l, ts)
        yb, sg = _gla_sample(pb, state_gla, wa2, ba, gn, l, ts, _tile(bs, 32))
        yc, nk, nv = _swa_sample(pc, ck, cv, sink_cols, cos_s, sin_s, l, swa_heads, ts, _tile(bs, 8))
        hs = _merge_ffn(hs, ya, yb, yc, pg, wb, wo, g_ffn_r, wg, wu, wd, gf, l, tm_s, last)
        for lst, val in zip(outs_s, (hr.reshape(bs, groups, s5_n), hi.reshape(bs, groups, s5_n), sg, nk, nv)):
            lst.append(val)

    outs_s = [jnp.stack(o) for o in outs_s]
    outs_s[3], outs_s[4] = from_lanes(outs_s[3]), from_lanes(outs_s[4])
    return (hp.reshape(bp, tp, d), hs.reshape(bs, ts, d), *[jnp.stack(o) for o in outs_p], *outs_s)
```

```python
import functools

import jax
import jax.numpy as jnp
from jax import lax
from jax.experimental import pallas as pl
from jax.experimental.pallas import tpu as pltpu

F32 = jnp.float32
BF16 = jnp.bfloat16

EPS = 1e-6
GLA_TAU = 16.0
GLA_CHUNK = 64
SWA_KV_HEADS = 2
SWA_HD = 64
WINDOW = 128
ROPE_THETA = 10000.0
PAST_LEN = 16384

SUBLANES = 8
LANES = 128
VMEM_LIMIT = 56 * 1024 * 1024

NT_DIMS = (((1,), (1,)), ((), ()))
TN_DIMS = (((0,), (0,)), ((), ()))


def _params(*sem):
    return pltpu.CompilerParams(dimension_semantics=sem, vmem_limit_bytes=VMEM_LIMIT)


def _dot(a, b):
    return jnp.dot(a, b, preferred_element_type=F32)


def _rms(x, g):
    return x * lax.rsqrt(jnp.mean(x * x, axis=-1, keepdims=True) + EPS) * g


def _full(shape):
    nd = len(shape)
    return pl.BlockSpec(shape, lambda *_: (0,) * nd)


def _layer(a, l):
    nd = a.ndim - 1
    return pl.BlockSpec((None,) + a.shape[1:], lambda *_: (l,) + (0,) * nd, pipeline_mode=pl.Buffered(1))


def _inproj_kernel(x_ref, g_ref, wa_ref, wb_ref, *refs, n_cast):
    cast_in, o_refs, cast_out = refs[:n_cast], refs[n_cast:n_cast + 4], refs[n_cast + 4:]
    xb = _rms(x_ref[...], g_ref[...]).astype(BF16)
    for w_ref, group in ((wa_ref, o_refs[:2]), (wb_ref, o_refs[2:])):
        off = 0
        for o_ref in group:
            n = o_ref.shape[-1]
            for c0 in range(0, n, 512):
                c1 = min(c0 + 512, n)
                o_ref[:, c0:c1] = lax.dot_general(xb, w_ref[off + c0:off + c1, :], NT_DIMS,
                                                  preferred_element_type=F32)
            off += n
    for src, dst in zip(cast_in, cast_out):
        dst[...] = src[...].astype(BF16)


def _inproj(x, g, wa, wb, l, widths, tm, to_cast=()):
    n, d = x.shape
    steps = n // tm
    assert wa.shape[1] == sum(widths[:2]) and wb.shape[1] == sum(widths[2:])
    flat = [w.reshape(-1, w.shape[-1]) for w in to_cast]
    assert all(f.shape[0] % (steps * 2 * SUBLANES) == 0 for f in flat)
    slice_spec = lambda f: pl.BlockSpec((f.shape[0] // steps, f.shape[1]), lambda i: (i, 0))
    outs = pl.pallas_call(
        functools.partial(_inproj_kernel, n_cast=len(flat)),
        grid=(steps,),
        in_specs=[pl.BlockSpec((tm, d), lambda i: (i, 0)), _layer(g, l), _layer(wa, l), _layer(wb, l)]
        + [slice_spec(f) for f in flat],
        out_specs=[pl.BlockSpec((tm, wd), lambda i: (i, 0)) for wd in widths] + [slice_spec(f) for f in flat],
        out_shape=[jax.ShapeDtypeStruct((n, wd), F32) for wd in widths]
        + [jax.ShapeDtypeStruct(f.shape, BF16) for f in flat],
        compiler_params=_params("parallel"),
        name="inproj",
    )(x, g, wa, wb, *flat)
    return outs[:4], [o.reshape(w.shape) for o, w in zip(outs[4:], to_cast)]


def _s5_prep_kernel(lr_ref, li_ref, ldt_ref, br_ref, bi_ref, ar_ref, ai_ref, bbr_ref, bbi_ref):
    lr, li = lr_ref[...], li_ref[...]
    dt = jnp.exp(ldt_ref[...])
    mag = jnp.exp(lr * dt)
    ar = mag * jnp.cos(li * dt)
    ai = mag * jnp.sin(li * dt)
    den = lr * lr + li * li
    zr, zi = ar - 1.0, ai
    er = (zr * lr + zi * li) / den
    ei = (zi * lr - zr * li) / den
    ar_ref[...] = ar
    ai_ref[...] = ai
    br, bi = br_ref[...], bi_ref[...]
    bbr_ref[...] = er[None] * br - ei[None] * bi
    bbi_ref[...] = er[None] * bi + ei[None] * br


def _s5_prep(lam_re, lam_im, log_dt, b_re, b_im):
    depth, g, c, n = b_re.shape
    dg = depth * g
    brt = jnp.transpose(b_re.reshape(dg, c, n), (1, 0, 2))
    bit = jnp.transpose(b_im.reshape(dg, c, n), (1, 0, 2))
    ar, ai, bbr, bbi = pl.pallas_call(
        _s5_prep_kernel,
        out_shape=[jax.ShapeDtypeStruct((dg, n), F32)] * 2 + [jax.ShapeDtypeStruct((c, dg, n), F32)] * 2,
        name="s5_prep",
    )(lam_re.reshape(dg, n), lam_im.reshape(dg, n), log_dt.reshape(dg, 1), brt, bit)
    unt = lambda a: jnp.transpose(a, (1, 0, 2)).reshape(depth, g, c, n)
    return ar.reshape(depth, 1, g * n), ai.reshape(depth, 1, g * n), unt(bbr), unt(bbi)


def _s5_pack(bbr, bbi, c_re, c_im):
    depth, g, c, n = bbr.shape
    gp = LANES // c
    ks = g // gp
    eye = jnp.eye(gp, dtype=F32)

    def pack_b(b):
        return jnp.einsum("lkgcn,gh->lkgchn", b.reshape(depth, ks, gp, c, n), eye).reshape(
            depth, ks, gp * c, gp * n).astype(BF16)

    def pack_c(cm):
        return jnp.einsum("lkgnc,gh->lkgnhc", cm.reshape(depth, ks, gp, n, c), eye).reshape(
            depth, ks, gp * n, gp * c).astype(BF16)

    return pack_b(bbr), pack_b(bbi), pack_c(c_re), pack_c(c_im)


S5_COL_BLOCK = 1024
S5_UNROLL = 8


def _s5_kernel(u_ref, h0r_ref, h0i_ref, ar_ref, ai_ref, bre_ref, bim_ref, cre_ref, cim_ref, d_ref, wglu_ref,
               bglu_ref, y_ref, hr_out, hi_out, us_s, ys_s, xr_s, xi_s, hr_s, hi_s, *tmp_s, nb, steps):
    c = pl.program_id(0)

    @pl.when(c == 0)
    def _():
        hr_s[...] = h0r_ref[...]
        hi_s[...] = h0i_ref[...]

    ks, kw, nw = bre_ref.shape
    by_sequence = u_ref.ndim == 3
    if by_sequence:
        for b in range(nb):
            for k in range(ks):
                us_s[k, pl.ds(b, steps, stride=nb), :] = u_ref[b, :, k * kw:(k + 1) * kw]
    else:
        for k in range(ks):
            tmp_s[0][k] = u_ref[:, k * kw:(k + 1) * kw]
        for t in range(steps):
            for k in range(ks):
                us_s[k, t * nb:(t + 1) * nb, :] = tmp_s[0][k, pl.ds(t, nb, stride=steps), :]

    def x_proj(k):
        uk = us_s[k].astype(BF16)
        xr_s[:, k * nw:(k + 1) * nw] = _dot(uk, bre_ref[k])
        xi_s[:, k * nw:(k + 1) * nw] = _dot(uk, bim_ref[k])

    def y_proj(k):
        ss = slice(k * nw, (k + 1) * nw)
        yk = _dot(xr_s[:, ss].astype(BF16), cre_ref[k]) - _dot(xi_s[:, ss].astype(BF16), cim_ref[k])
        ys_s[k] = jax.nn.gelu(yk + d_ref[:, k * kw:(k + 1) * kw] * us_s[k])

    def scan(rb, cs, unrolled):
        rows = slice(rb * SUBLANES, (rb + 1) * SUBLANES)
        width = cs.stop - cs.start
        a_r = jnp.broadcast_to(ar_ref[:, cs], (SUBLANES, width))
        a_i = jnp.broadcast_to(ai_ref[:, cs], (SUBLANES, width))

        def step(t, carry):
            hr, hi = carry
            r0 = t * nb + rb * SUBLANES
            tr = slice(r0, r0 + SUBLANES) if unrolled else pl.ds(pl.multiple_of(r0, SUBLANES), SUBLANES)
            nhr = a_r * hr - a_i * hi + xr_s[tr, cs]
            nhi = a_r * hi + a_i * hr + xi_s[tr, cs]
            xr_s[tr, cs] = nhr
            xi_s[tr, cs] = nhi
            return nhr, nhi

        carry = (hr_s[rows, cs], hi_s[rows, cs])
        if unrolled:
            for t in range(steps):
                carry = step(t, carry)
        else:
            carry = lax.fori_loop(0, steps, step, carry, unroll=min(steps, S5_UNROLL))
        hr_s[rows, cs], hi_s[rows, cs] = carry

    if by_sequence and nb == SUBLANES:
        x_proj(0)
        for k in range(ks):
            if k + 1 < ks:
                x_proj(k + 1)
            if k >= 1:
                y_proj(k - 1)
            scan(0, slice(k * nw, (k + 1) * nw), True)
        y_proj(ks - 1)
    else:
        for k in range(ks):
            x_proj(k)
        for rb in range(nb // SUBLANES):
            for cb in range(xr_s.shape[1] // S5_COL_BLOCK):
                scan(rb, slice(cb * S5_COL_BLOCK, (cb + 1) * S5_COL_BLOCK), False)
        for k in range(ks):
            y_proj(k)
    y = jnp.concatenate([ys_s[k] for k in range(ks)], axis=1)
    y = y * jax.nn.sigmoid(_dot(y.astype(BF16), wglu_ref[...]) + bglu_ref[...])
    for k in range(ks):
        ys_s[k] = y[:, k * kw:(k + 1) * kw]
    if by_sequence:
        for b in range(nb):
            for k in range(ks):
                y_ref[b, :, k * kw:(k + 1) * kw] = ys_s[k, pl.ds(b, steps, stride=nb), :]
    else:
        for t in range(steps):
            for k in range(ks):
                tmp_s[0][k, pl.ds(t, nb, stride=steps), :] = ys_s[k, t * nb:(t + 1) * nb, :]
        for k in range(ks):
            y_ref[:, k * kw:(k + 1) * kw] = tmp_s[0][k]

    @pl.when(c == pl.num_programs(0) - 1)
    def _():
        hr_out[...] = hr_s[...]
        hi_out[...] = hi_s[...]


def _s5(u, h0r, h0i, l0, params, l, steps):
    _, nb, ns = h0r.shape
    w = u.shape[-1]
    blk = steps * nb
    ks = params[2].shape[1]
    if u.ndim == 3:
        grid = (u.shape[1] // steps,)
        u_spec = pl.BlockSpec((nb, steps, w), lambda i: (0, i, 0))
        tmp = []
    else:
        assert u.shape[0] == blk
        grid = (1,)
        u_spec = _full(u.shape)
        tmp = [pltpu.VMEM((ks, blk, LANES), F32)]
    kern = functools.partial(_s5_kernel, nb=nb, steps=steps)
    return pl.pallas_call(
        kern,
        grid=grid,
        in_specs=[u_spec, _layer(h0r, l0), _layer(h0i, l0)] + [_layer(a, l) for a in params],
        out_specs=[u_spec, _full((nb, ns)), _full((nb, ns))],
        out_shape=[jax.ShapeDtypeStruct(u.shape, F32), jax.ShapeDtypeStruct((nb, ns), F32),
                   jax.ShapeDtypeStruct((nb, ns), F32)],
        scratch_shapes=[pltpu.VMEM((ks, blk, LANES), F32), pltpu.VMEM((ks, blk, LANES), F32),
                        pltpu.VMEM((blk, ns), F32), pltpu.VMEM((blk, ns), F32), pltpu.VMEM((nb, ns), F32),
                        pltpu.VMEM((nb, ns), F32)] + tmp,
        compiler_params=_params("arbitrary"),
        name="s5_scan",
    )(u, h0r, h0i, *params)


def _log2(n):
    assert n & (n - 1) == 0, n
    return n.bit_length() - 1


def _chunk_masks(rows, chunk):
    sh = _log2(chunk)
    rr = lax.broadcasted_iota(jnp.int32, (rows, rows), 0)
    cc = lax.broadcasted_iota(jnp.int32, (rows, rows), 1)
    same = lax.shift_right_logical(rr, sh) == lax.shift_right_logical(cc, sh)
    return same, same & (cc <= rr)


def _gla_prologue(p_ref, wa_ref, ba_ref, chunk, qk):
    rows = p_ref.shape[0]
    q = p_ref[:, 0:qk]
    k = p_ref[:, qk:2 * qk]
    alow = p_ref[:, p_ref.shape[1] - LANES:].astype(BF16)
    z = _dot(alow, wa_ref[...]) + ba_ref[...]
    la = (jnp.minimum(z, 0.0) - jnp.log(1.0 + jnp.exp(-jnp.abs(z)))) * (1.0 / GLA_TAU)
    same, causal = _chunk_masks(rows, chunk)
    tri = jnp.where(causal, 1.0, 0.0).astype(BF16)
    blk = jnp.where(same, 1.0, 0.0).astype(BF16)
    la_hi = la.astype(BF16)
    la_lo = (la - la_hi.astype(F32)).astype(BF16)
    b = _dot(tri, la_hi) + _dot(tri, la_lo)
    bl = _dot(blk, la_hi) + _dot(blk, la_lo)
    return q, k, b, bl


def _gla_out(o, gn, r):
    on = o * lax.rsqrt(jnp.mean(o * o, axis=-1, keepdims=True) + EPS) * gn
    return on * (r * jax.nn.sigmoid(r))


def _gla_prompt_init(st_s):
    @pl.when(pl.program_id(1) == 0)
    def _():
        st_s[...] = jnp.zeros_like(st_s)


def _gla_prompt_final(s_ref, st_s, *, heads, dk):
    @pl.when(pl.program_id(1) == pl.num_programs(1) - 1)
    def _():
        for h in range(heads):
            s_ref[0, h] = st_s[:, h * dk:(h + 1) * dk].T


def _gla_prompt_body(p_ref, wa_ref, ba_ref, gn_ref, y_ref, st_s, *, heads, dk, dv, chunk):
    rows = p_ref.shape[0]
    qk, vw = heads * dk, heads * dv
    q, k, b, bl = _gla_prologue(p_ref, wa_ref, ba_ref, chunk, qk)
    v_off, r_off = 2 * qk, 2 * qk + vw
    qd = (q * (dk ** -0.5) * jnp.exp(b)).astype(BF16)
    kd = (k * jnp.exp(-b)).astype(BF16)
    kst = (k * jnp.exp(bl - b)).astype(BF16)
    dec = jnp.exp(bl)
    gn = gn_ref[...]
    zero = jnp.zeros((), BF16)
    qk_head = lax.shift_right_logical(lax.broadcasted_iota(jnp.int32, (chunk, qk), 1), _log2(dk))
    v_head = lax.shift_right_logical(lax.broadcasted_iota(jnp.int32, (chunk, vw), 1), _log2(dv))
    rr = lax.broadcasted_iota(jnp.int32, (chunk, heads * chunk), 0)
    cc = lax.broadcasted_iota(jnp.int32, (chunk, heads * chunk), 1) & (chunk - 1)
    causal = cc <= rr
    nchunk = rows // chunk
    by_head = lambda x, head: jnp.concatenate([jnp.where(head == h, x, zero) for h in range(heads)], axis=0)
    att, ds_t, v_diag, qd_rows = [], [], [], []
    for ci in range(nchunk):
        rs = slice(ci * chunk, (ci + 1) * chunk)
        v_c = p_ref[rs, v_off:v_off + vw].astype(BF16)
        v_rows = jnp.concatenate([v_c[:, h * dv:(h + 1) * dv] for h in range(heads)], axis=0)
        v_diag.append(by_head(v_c, v_head))
        qd_rows.append(by_head(qd[rs], qk_head))
        att.append(lax.dot_general(qd[rs], by_head(kd[rs], qk_head), NT_DIMS, preferred_element_type=F32))
        ds_t.append(lax.dot_general(v_rows, by_head(kst[rs], qk_head), TN_DIMS, preferred_element_type=F32))
    yield
    o_intra = [_dot(jnp.where(causal, att[ci], 0.0).astype(BF16), v_diag[ci]) for ci in range(nchunk)]
    yield
    st = st_s[...]
    o_inter = []
    for ci in range(nchunk):
        o_inter.append(lax.dot_general(qd_rows[ci], st.astype(BF16), NT_DIMS, preferred_element_type=F32))
        st = dec[ci * chunk:ci * chunk + 1, :] * st + ds_t[ci]
    st_s[...] = st
    yield
    for ci in range(nchunk):
        rs = slice(ci * chunk, (ci + 1) * chunk)
        for h in range(heads):
            o = o_intra[ci][:, h * dv:(h + 1) * dv] + o_inter[ci][h * chunk:(h + 1) * chunk]
            r = p_ref[rs, r_off + h * dv:r_off + (h + 1) * dv]
            y_ref[rs, h * dv:(h + 1) * dv] = _gla_out(o, gn, r)


def _gla_sample_kernel(p_ref, s0_ref, wa_ref, ba_ref, gn_ref, y_ref, s_ref, *, heads, dk, dv, t):
    rows = p_ref.shape[0]
    qk = heads * dk
    q, k, b, bl = _gla_prologue(p_ref, wa_ref, ba_ref, t, qk)
    _, cmask = _chunk_masks(SUBLANES, t)
    v_off, r_off = 2 * qk, 2 * qk + heads * dv
    qd = (q * (dk ** -0.5) * jnp.exp(b)).astype(BF16)
    kd = (k * jnp.exp(-b)).astype(BF16)
    kst = k * jnp.exp(bl - b)
    dec = jnp.exp(bl)
    gn = gn_ref[...]
    per_tile = SUBLANES // t
    row_id = lax.broadcasted_iota(jnp.int32, (SUBLANES, 1), 0)
    att, o_state = {}, {}
    for ti in range(rows // SUBLANES):
        rs = slice(ti * SUBLANES, (ti + 1) * SUBLANES)
        dcols = []
        for j in range(per_tile):
            drow = dec[ti * SUBLANES + j * t:ti * SUBLANES + j * t + 1, :]
            dcols.append([jnp.broadcast_to(drow[:, g * LANES:(g + 1) * LANES], (LANES, LANES)).T
                          for g in range(qk // LANES)])
        for h in range(heads):
            hs = slice(h * dk, (h + 1) * dk)
            vh = p_ref[rs, v_off + h * dv:v_off + (h + 1) * dv].astype(BF16)
            att[ti, h] = lax.dot_general(qd[rs, hs], kd[rs, hs], NT_DIMS, preferred_element_type=F32)
            o = None
            for j in range(per_tile):
                bidx = ti * per_tile + j
                in_seq = (row_id >= j * t) & (row_id < (j + 1) * t)
                s0 = s0_ref[bidx, h]
                oj = jnp.where(in_seq, _dot(qd[rs, hs], s0.astype(BF16)), 0.0)
                o = oj if o is None else o + oj
                kj = jnp.where(in_seq, kst[rs, hs], 0.0).astype(BF16)
                ds = lax.dot_general(kj, vh, TN_DIMS, preferred_element_type=F32)
                g, lo = (h * dk) // LANES, (h * dk) % LANES
                dcol = dcols[j][g][lo:lo + dk, 0:dv]
                s_ref[bidx, h] = dcol * s0 + ds
            o_state[ti, h] = o
    for ti in range(rows // SUBLANES):
        rs = slice(ti * SUBLANES, (ti + 1) * SUBLANES)
        for h in range(heads):
            vh = p_ref[rs, v_off + h * dv:v_off + (h + 1) * dv].astype(BF16)
            o = _dot(jnp.where(cmask, att[ti, h], 0.0).astype(BF16), vh) + o_state[ti, h]
            r = p_ref[rs, r_off + h * dv:r_off + (h + 1) * dv]
            y_ref[rs, h * dv:(h + 1) * dv] = _gla_out(o, gn, r)


def _into_layers(kern, n_in, acc):
    body = lambda *refs: kern(*refs[:n_in], *refs[n_in + len(acc):])
    return body, [pl.BlockSpec(memory_space=pl.ANY)] * len(acc), {n_in + j: 1 + j for j in range(len(acc))}


def _gla_sample(p, s0, wa, ba, gn, l, t, nb, acc):
    n, pw = p.shape
    _, bsz, heads, dk, dv = s0.shape
    rows = nb * t
    kern = functools.partial(_gla_sample_kernel, heads=heads, dk=dk, dv=dv, t=t)
    kern, acc_specs, aliases = _into_layers(kern, 5, acc)
    state = pl.BlockSpec((None, nb, heads, dk, dv), lambda i: (l, i, 0, 0, 0))
    return pl.pallas_call(
        kern,
        grid=(bsz // nb,),
        in_specs=[pl.BlockSpec((rows, pw), lambda i: (i, 0)), state, _layer(wa, l), _layer(ba, l), _layer(gn, l)]
        + acc_specs,
        out_specs=[pl.BlockSpec((rows, heads * dv), lambda i: (i, 0)), state],
        out_shape=[jax.ShapeDtypeStruct((n, heads * dv), F32), jax.ShapeDtypeStruct(s0.shape, F32)],
        input_output_aliases=aliases,
        compiler_params=_params("parallel"),
        name="gla_sample",
    )(p, s0, wa, ba, gn, *acc)


def _first_half(shape):
    lane = lax.broadcasted_iota(jnp.int32, shape, 1)
    return (lane & (SWA_HD - 1)) < SWA_HD // 2


def _rope_table_kernel(pos_ref, inv_ref, cos_ref, sin_ref):
    ang = pos_ref[...] * inv_ref[...]
    s = jnp.sin(ang)
    cos_ref[...] = jnp.cos(ang)
    sin_ref[...] = jnp.where(_first_half(ang.shape), -s, s)


def _rope_table(pos, inv):
    rows = pos.shape[0]
    return pl.pallas_call(
        _rope_table_kernel,
        out_shape=[jax.ShapeDtypeStruct((rows, LANES), F32)] * 2,
        name="rope_table",
    )(pos.astype(F32).reshape(rows, 1), inv)


def _rope(x, cos, sin):
    half = SWA_HD // 2
    first_half = _first_half(cos.shape)
    out = []
    for g in range(x.shape[1] // LANES):
        xg = x[:, g * LANES:(g + 1) * LANES]
        swapped = jnp.where(first_half, pltpu.roll(xg, LANES - half, 1), pltpu.roll(xg, half, 1))
        out.append(xg * cos + swapped * sin)
    return out[0] if len(out) == 1 else jnp.concatenate(out, axis=1)


def _swa_prompt_init(prev_s):
    @pl.when(pl.program_id(1) == 0)
    def _():
        prev_s[...] = jnp.zeros_like(prev_s)


def _swa_prompt_body(sink_ref, cur_ref, cos_ref, sin_ref, y_ref, nk_ref, nv_ref, prev_s, s_s, pc_s, *, heads, l):
    i = pl.program_id(1)
    w = WINDOW
    qw = heads * SWA_HD
    kvw = SWA_KV_HEADS * SWA_HD
    gq = heads // SWA_KV_HEADS
    nq = cur_ref.shape[0] // w
    cos, sin = cos_ref[...], sin_ref[...]
    q = (_rope(cur_ref[:, 0:qw], cos, sin) * (SWA_HD ** -0.5)).astype(BF16)
    k_cur = _rope(cur_ref[:, qw:qw + kvw], cos, sin)
    v_cur = cur_ref[:, qw + kvw:qw + 2 * kvw]
    new = [x.astype(BF16) for x in (k_cur, pltpu.roll(k_cur, SWA_HD, 1), v_cur, pltpu.roll(v_cur, SWA_HD, 1))]
    k2, k2r, v2, v2r = [jnp.concatenate([prev_s[j], new[j]], axis=0) for j in range(4)]
    zero = jnp.zeros((), BF16)
    lo2 = lax.broadcasted_iota(jnp.int32, ((nq + 1) * w, LANES), 1) < SWA_HD
    lo = lax.broadcasted_iota(jnp.int32, (w, LANES), 1) < SWA_HD
    k_both = (jnp.where(lo2, k2, k2r), jnp.where(lo2, k2r, k2))
    v_half = {(0, 0): jnp.where(lo2, v2, zero), (1, 1): jnp.where(lo2, zero, v2),
              (0, 1): jnp.where(lo2, zero, v2r), (1, 0): jnp.where(lo2, v2r, zero)}
    upper = lax.broadcasted_iota(jnp.int32, (w, w), 1) > lax.broadcasted_iota(jnp.int32, (w, w), 0)
    no_prev = jnp.where(i > 0, 0.0, -jnp.inf)
    slot = lambda c, h: slice((c * heads + h) * w, (c * heads + h + 1) * w)
    for c in range(nq):
        for kv in range(SWA_KV_HEADS):
            qm = []
            for h in range(kv * gq, (kv + 1) * gq):
                qg = q[c * w:(c + 1) * w, (h // 2) * LANES:(h // 2 + 1) * LANES]
                qm.append(jnp.where(lo, qg, zero) if h % 2 == 0 else jnp.where(lo, zero, qg))
            s_s[(c * heads + kv * gq) * w:(c * heads + (kv + 1) * gq) * w, :] = lax.dot_general(
                jnp.concatenate(qm, axis=0), k_both[kv][c * w:(c + 2) * w], NT_DIMS,
                preferred_element_type=F32)
    yield
    sink_term = {}
    for c in range(nq):
        for h in range(heads):
            s_prev = s_s[slot(c, h), 0:w] + no_prev if c == 0 else s_s[slot(c, h), 0:w]
            sp = jnp.where(upper, s_prev, s_s[slot(c, h), w:2 * w])
            sink = sink_ref[l, h]
            m = jnp.maximum(jnp.max(sp, axis=-1, keepdims=True), sink)
            p = jnp.exp(sp - m)
            pc_s[slot(c, h), 0:w] = jnp.where(upper, p, 0.0).astype(BF16)
            pc_s[slot(c, h), w:2 * w] = jnp.where(upper, 0.0, p).astype(BF16)
            sink_term[c, h] = jnp.exp(sink - m)
    yield
    num = {(c, h): _dot(pc_s[slot(c, h), :], v_half[(h // gq, h % 2)][c * w:(c + 2) * w])
           for c in range(nq) for h in range(heads)}
    den_all = _dot(pc_s[...], jnp.ones((2 * w, LANES), BF16))
    yield
    for c in range(nq):
        for g in range(heads // 2):
            he, ho = 2 * g, 2 * g + 1
            den = jnp.where(lo, den_all[slot(c, he)] + sink_term[c, he], den_all[slot(c, ho)] + sink_term[c, ho])
            y_ref[c * w:(c + 1) * w, g * LANES:(g + 1) * LANES] = (num[c, he] + num[c, ho]) * (1.0 / den)
    for j in range(4):
        prev_s[j] = new[j][(nq - 1) * w:]
    nk_ref[0] = k_cur[(nq - 1) * w:]
    nv_ref[0] = v_cur[(nq - 1) * w:]


def _gla_swa_prompt_kernel(pb_ref, wa_ref, ba_ref, gn_ref, sink_ref, pc_ref, cos_ref, sin_ref,
                           yb_ref, s_ref, yc_ref, nk_ref, nv_ref, st_s, prev_s, s_s, pc_s, *, gla, swa):
    _gla_prompt_init(st_s)
    _swa_prompt_init(prev_s)
    bodies = [_gla_prompt_body(pb_ref, wa_ref, ba_ref, gn_ref, yb_ref, st_s, chunk=GLA_CHUNK, **gla),
              _swa_prompt_body(sink_ref, pc_ref, cos_ref, sin_ref, yc_ref, nk_ref, nv_ref, prev_s, s_s, pc_s, **swa)]
    while bodies:
        bodies = [b for b in bodies if next(b, StopIteration) is not StopIteration]
    _gla_prompt_final(s_ref, st_s, heads=gla["heads"], dk=gla["dk"])


def _gla_swa_prompt(pb, pc, wa, ba, gn, sinks, cos, sin, l, bsz, gla_heads, dk, dv, swa_heads, rows):
    n, pbw = pb.shape
    pcw = pc.shape[1]
    nblk = n // bsz // rows
    nq = rows // WINDOW
    qw, kvw = swa_heads * SWA_HD, SWA_KV_HEADS * SWA_HD
    kern = functools.partial(_gla_swa_prompt_kernel, gla=dict(heads=gla_heads, dk=dk, dv=dv),
                             swa=dict(heads=swa_heads, l=l))
    row_blk = lambda wdt: pl.BlockSpec((rows, wdt), lambda b, i: (b * nblk + i, 0))
    tab = pl.BlockSpec((rows, LANES), lambda b, i: (i, 0))
    win = pl.BlockSpec((1, WINDOW, kvw), lambda b, i: (b, 0, 0))
    return pl.pallas_call(
        kern,
        grid=(bsz, nblk),
        in_specs=[row_blk(pbw), _layer(wa, l), _layer(ba, l), _layer(gn, l),
                  pl.BlockSpec(memory_space=pltpu.SMEM), row_blk(pcw), tab, tab],
        out_specs=[row_blk(gla_heads * dv), pl.BlockSpec((1, gla_heads, dk, dv), lambda b, i: (b, 0, 0, 0)),
                   row_blk(qw), win, win],
        out_shape=[jax.ShapeDtypeStruct((n, gla_heads * dv), F32),
                   jax.ShapeDtypeStruct((bsz, gla_heads, dk, dv), F32),
                   jax.ShapeDtypeStruct((n, qw), F32), jax.ShapeDtypeStruct((bsz, WINDOW, kvw), F32),
                   jax.ShapeDtypeStruct((bsz, WINDOW, kvw), F32)],
        scratch_shapes=[pltpu.VMEM((dv, gla_heads * dk), F32), pltpu.VMEM((4, WINDOW, LANES), BF16),
                        pltpu.VMEM((nq * swa_heads * WINDOW, 2 * WINDOW), F32),
                        pltpu.VMEM((nq * swa_heads * WINDOW, 2 * WINDOW), BF16)],
        compiler_params=_params("parallel", "arbitrary"),
        name="gla_swa_prompt",
    )(pb, wa, ba, gn, sinks, pc, cos, sin)


def _expand_kv(x, lane):
    other = pltpu.roll(x, SWA_HD, 1)
    first = lane < SWA_HD
    kv0 = jnp.where(first, x, other)
    kv1 = jnp.where(first, other, x)
    return kv0, kv1


def _swa_sample_kernel(sinkcol_ref, p_ref, ck_ref, cv_ref, cos_ref, sin_ref, y_ref, nk_ref, nv_ref, *, heads, t, nb):
    w = ck_ref.shape[3]
    qw = heads * SWA_HD
    kvw = SWA_KV_HEADS * SWA_HD
    gq = heads // SWA_KV_HEADS
    per_tile = SUBLANES // t
    nrow = heads * t
    cos, sin = cos_ref[...], sin_ref[...]
    lane_kv = lax.broadcasted_iota(jnp.int32, (1, kvw), 1)
    qm_row = lax.broadcasted_iota(jnp.int32, (nrow, qw), 0)
    qm_lane = lax.broadcasted_iota(jnp.int32, (nrow, qw), 1)
    qmask = lax.shift_right_logical(qm_row, _log2(t)) == lax.shift_right_logical(qm_lane, _log2(SWA_HD))
    row_step = lax.broadcasted_iota(jnp.int32, (nrow, w), 0) & (t - 1)
    cache_ok = lax.broadcasted_iota(jnp.int32, (nrow, w), 1) > row_step
    new_col = lax.broadcasted_iota(jnp.int32, (nrow, SUBLANES), 1)
    new_row_step = lax.broadcasted_iota(jnp.int32, (nrow, SUBLANES), 0) & (t - 1)
    sink_col = sinkcol_ref[...]
    lane_w = lax.broadcasted_iota(jnp.int32, (SWA_HD, w), 1)
    widen = lambda a, b: jnp.concatenate([a] * (gq // 2) + [b] * (gq // 2), axis=1).astype(BF16)
    per_head = lambda c: jnp.concatenate([c[kv] for kv in range(SWA_KV_HEADS) for _ in range(gq)],
                                         axis=0).astype(BF16)
    pad_rows = jnp.zeros((LANES - 2 * SUBLANES, kvw), F32)
    new_k, new_v, scores = [], [], []
    for ti in range(nb // per_tile):
        rs = slice(ti * SUBLANES, (ti + 1) * SUBLANES)
        q8 = _rope(p_ref[rs, 0:qw], cos, sin) * (SWA_HD ** -0.5)
        kn8 = _rope(p_ref[rs, qw:qw + kvw], cos, sin)
        vn8 = p_ref[rs, qw + kvw:qw + 2 * kvw]
        kn_x = widen(*_expand_kv(kn8, lane_kv))
        new_k.append(kn8)
        new_v.append(vn8)
        for j in range(per_tile):
            bidx = ti * per_tile + j
            qj = q8[j * t:(j + 1) * t, :]
            qm = jnp.where(qmask, jnp.concatenate([qj] * heads, axis=0), 0.0).astype(BF16)
            s_c = _dot(qm, per_head(ck_ref[bidx]))
            s_n = lax.dot_general(qm, kn_x, NT_DIMS, preferred_element_type=F32)
            scores.append((s_c, s_n))
    probs = []
    for bidx, (s_c, s_n) in enumerate(scores):
        j = bidx % per_tile
        s_c = jnp.where(cache_ok, s_c, -jnp.inf)
        new_ok = (new_col >= j * t) & (new_col - j * t <= new_row_step) & (new_col < (j + 1) * t)
        s_n = jnp.where(new_ok, s_n, -jnp.inf)
        m = jnp.maximum(jnp.maximum(jnp.max(s_c, axis=1, keepdims=True), jnp.max(s_n, axis=1, keepdims=True)),
                        sink_col)
        p_c = jnp.exp(s_c - m)
        p_n = jnp.exp(s_n - m)
        den = jnp.sum(p_c, axis=1, keepdims=True) + jnp.sum(p_n, axis=1, keepdims=True) + jnp.exp(sink_col - m)
        rden = 1.0 / den
        probs.append(((p_c * rden).astype(BF16), (p_n * rden).astype(BF16)))
    outs = []
    for bidx, (p_c, p_n) in enumerate(probs):
        ti, j = bidx // per_tile, bidx % per_tile
        vn_x = widen(*_expand_kv(new_v[ti], lane_kv))
        outs.append(lax.dot_general(p_c, per_head(cv_ref[bidx]), NT_DIMS, preferred_element_type=F32)
                    + _dot(p_n, vn_x))
    for bidx, o in enumerate(outs):
        ti, j = bidx // per_tile, bidx % per_tile
        o = jnp.where(qmask, o, 0.0)
        acc = o[0:SUBLANES]
        for g in range(1, nrow // SUBLANES):
            acc = acc + o[g * SUBLANES:(g + 1) * SUBLANES]
        for g in range(1, per_tile):
            acc = acc + pltpu.roll(acc, SUBLANES - g * t, 0)
        y_ref[ti * SUBLANES + j * t:ti * SUBLANES + (j + 1) * t, :] = acc[0:t]
    for ti in range(nb // per_tile):
        kv_t = jnp.concatenate([new_k[ti], new_v[ti], pad_rows], axis=0).T
        for j in range(per_tile):
            bidx = ti * per_tile + j
            for c_ref, n_ref, col0 in ((ck_ref, nk_ref, 0), (cv_ref, nv_ref, SUBLANES)):
                for kv in range(SWA_KV_HEADS):
                    tail = pltpu.roll(kv_t[kv * SWA_HD:(kv + 1) * SWA_HD, :], w - t - col0 - j * t, 1)
                    n_ref[bidx, kv] = jnp.where(lane_w < w - t, pltpu.roll(c_ref[bidx, kv], w - t, 1), tail)


def _swa_sample(p, ck, cv, sink_cols, cos, sin, l, heads, t, nb, acc):
    n, pw = p.shape
    _, bsz, kvh, hd, w = ck.shape
    qw = heads * SWA_HD
    kern = functools.partial(_swa_sample_kernel, heads=heads, t=t, nb=nb)
    kern, acc_specs, aliases = _into_layers(kern, 6, acc)
    cache = pl.BlockSpec((None, nb, kvh, hd, w), lambda i: (l, i, 0, 0, 0))
    return pl.pallas_call(
        kern,
        grid=(bsz // nb,),
        in_specs=[_layer(sink_cols, l), pl.BlockSpec((nb * t, pw), lambda i: (i, 0)), cache, cache,
                  _full(cos.shape), _full(sin.shape)] + acc_specs,
        out_specs=[pl.BlockSpec((nb * t, qw), lambda i: (i, 0)), cache, cache],
        out_shape=[jax.ShapeDtypeStruct((n, qw), F32), jax.ShapeDtypeStruct(ck.shape, F32),
                   jax.ShapeDtypeStruct(cv.shape, F32)],
        input_output_aliases=aliases,
        compiler_params=_params("parallel"),
        name="swa_sample",
    )(sink_cols, p, ck, cv, cos, sin, *acc)


FFN_HIDDEN_CHUNK = 1024


def _merge_ffn_kernel(x_ref, ya_ref, yb_ref, yc_ref, gt_ref, wb_ref, wo_ref, g_ref, wg_ref, wu_ref, wd_ref, gf_ref,
                      o_ref, *, final_norm):
    d = x_ref.shape[1]
    merged = None
    off = 0
    for i, y_ref in enumerate((ya_ref, yb_ref, yc_ref)):
        wdt = y_ref.shape[1]
        br = _dot(y_ref[...].astype(BF16), wb_ref[off:off + wdt, :])
        term = jax.nn.sigmoid(gt_ref[:, i * d:(i + 1) * d]) * br
        merged = term if merged is None else merged + term
        off += wdt
    x = x_ref[...] + _dot(merged.astype(BF16), wo_ref[...])
    hb = _rms(x, g_ref[...]).astype(BF16)
    hidden = wg_ref.shape[1]
    acc = x
    for c0 in range(0, hidden, FFN_HIDDEN_CHUNK):
        c1 = min(c0 + FFN_HIDDEN_CHUNK, hidden)
        gate = _dot(hb, wg_ref[:, c0:c1])
        up = _dot(hb, wu_ref[:, c0:c1])
        act = (gate * jax.nn.sigmoid(gate) * up).astype(BF16)
        acc = acc + _dot(act, wd_ref[c0:c1, :])
    o_ref[...] = _rms(acc, gf_ref[...]) if final_norm else acc


def _merge_ffn(x, ya, yb, yc, gates, wb, wo, g, wg, wu, wd, gf, l, tm, final_norm):
    n, d = x.shape
    row = lambda a: pl.BlockSpec((tm, a.shape[1]), lambda i: (i, 0))
    return pl.pallas_call(
        functools.partial(_merge_ffn_kernel, final_norm=final_norm),
        grid=(n // tm,),
        in_specs=[row(x), row(ya), row(yb), row(yc), row(gates), _layer(wb, l), _layer(wo, l), _layer(g, l),
                  _layer(wg, l), _layer(wu, l), _layer(wd, l), _full(gf.shape)],
        out_specs=pl.BlockSpec((tm, d), lambda i: (i, 0)),
        out_shape=jax.ShapeDtypeStruct((n, d), F32),
        compiler_params=_params("parallel"),
        name="merge_ffn",
    )(x, ya, yb, yc, gates, wb, wo, g, wg, wu, wd, gf)


def _tile(n, want):
    while n % want:
        want //= 2
    return want


def kernel(x_prompt, x_sample, state_s5_re, state_s5_im, state_gla, cache_swa_k, cache_swa_v, g_mix, w_in, s5_lambda_re, s5_lambda_im, s5_log_dt, s5_b_re, s5_b_im, s5_c_re, s5_c_im, s5_d, w_glu, b_glu, w_gla_a2, b_gla_a, g_gla_norm, swa_sinks, w_branch, w_out, g_ffn, w_ffn_gate, w_ffn_up, w_ffn_down, g_final):
    bp, tp, d = x_prompt.shape
    bs, ts, _ = x_sample.shape
    depth = w_in.shape[0]
    s5_w = s5_d.shape[1]
    groups, s5_n = s5_lambda_re.shape[1:]
    gla_qk = w_gla_a2.shape[2]
    gla_rank = w_gla_a2.shape[1]
    gla_heads, gla_dk, gla_dv = state_gla.shape[2:]
    gla_v = gla_heads * gla_dv
    swa_heads = swa_sinks.shape[1]
    swa_q = swa_heads * SWA_HD
    swa_kv = SWA_KV_HEADS * SWA_HD
    assert SWA_KV_HEADS * SWA_HD == LANES and bp % SUBLANES == 0 and bs % SUBLANES == 0 and SUBLANES % ts == 0
    assert tp % WINDOW == 0 and cache_swa_k.shape[2] == WINDOW

    o_a, o_b = 0, s5_w
    o_low = o_b + 2 * gla_qk + 2 * gla_v
    o_c = o_low + gla_rank
    o_g = o_c + swa_q + 2 * swa_kv
    widths = (s5_w, 2 * gla_qk + 2 * gla_v + LANES, swa_q + 2 * swa_kv, 3 * d)

    hp = x_prompt.reshape(bp * tp, d)
    hs = x_sample.reshape(bs * ts, d)
    inv = ROPE_THETA ** (-jnp.arange(SWA_HD // 2, dtype=F32) * (2.0 / SWA_HD))
    inv = jnp.tile(inv, 2 * LANES // SWA_HD).reshape(1, LANES)
    cos_p, sin_p = _rope_table(jnp.arange(tp, dtype=jnp.int32), inv)
    cos_s, sin_s = _rope_table(PAST_LEN + (jnp.arange(SUBLANES, dtype=jnp.int32) & (ts - 1)), inv)
    zeros_s5 = jnp.zeros((1, bp, groups * s5_n), F32)
    tm_p, tm_s = _tile(bp * tp, 512), _tile(bs * ts, 512)
    s5_steps = _tile(tp, 128)
    mix_rows = WINDOW * _tile(tp // WINDOW, 4)
    assert mix_rows % GLA_CHUNK == 0

    w_t = jnp.swapaxes(w_in, 1, 2)
    w_a = jnp.pad(w_t[:, :o_c].astype(BF16), ((0, 0), (0, LANES - gla_rank), (0, 0)))
    w_b = w_t[:, o_c:].astype(BF16)
    wa2 = jnp.concatenate([w_gla_a2, jnp.zeros((depth, LANES - gla_rank, gla_qk), F32)], axis=1).astype(BF16)
    ba = b_gla_a.reshape(depth, 1, gla_qk)
    gn = g_gla_norm.reshape(depth, 1, gla_dv)
    ar, ai, bbr, bbi = _s5_prep(s5_lambda_re, s5_lambda_im, s5_log_dt, s5_b_re, s5_b_im)
    bre, bim, cre, cim = _s5_pack(bbr, bbi, s5_c_re, s5_c_im)
    s5_d_r, b_glu_r = s5_d.reshape(depth, 1, s5_w), b_glu.reshape(depth, 1, s5_w)
    h0r_s = state_s5_re.reshape(depth, bs, groups * s5_n)
    h0i_s = state_s5_im.reshape(depth, bs, groups * s5_n)
    g_mix_r, g_ffn_r, gf = g_mix.reshape(depth, 1, d), g_ffn.reshape(depth, 1, d), g_final.reshape(1, d)
    later = (w_glu, w_branch, w_out, w_ffn_gate, w_ffn_up, w_ffn_down)
    cast_rows = (bp * tp // tm_p) * 2 * SUBLANES
    cast_in_call = all((w.shape[0] * w.shape[1]) % cast_rows == 0 for w in later)
    casted = None if cast_in_call else [w.astype(BF16) for w in later]
    sink_cols = jnp.repeat(swa_sinks, ts, axis=1).reshape(depth, swa_heads * ts, 1)
    to_lanes = lambda c: jnp.transpose(c, (0, 1, 3, 4, 2))
    from_lanes = lambda c: jnp.transpose(c, (0, 1, 4, 2, 3))
    ck, cv = to_lanes(cache_swa_k), to_lanes(cache_swa_v)

    outs_p = [[] for _ in range(5)]
    outs_s = [[], []]
    gla_s, swa_s = (jnp.zeros_like(state_gla),), (jnp.zeros_like(ck), jnp.zeros_like(cv))
    for l in range(depth):
        last = l == depth - 1

        (pa, pb, pc, pg), new_casts = _inproj(hp, g_mix_r, w_a, w_b, l, widths, tm_p,
                                              later if casted is None else ())
        if casted is None:
            casted = new_casts
        wglu_b, wb, wo, wg, wu, wd = casted
        s5_params = (ar, ai, bre, bim, cre, cim, s5_d_r, wglu_b, b_glu_r)
        ya, hr, hi = _s5(pa.reshape(bp, tp, s5_w), zeros_s5, zeros_s5, 0, s5_params, l, s5_steps)
        ya = ya.reshape(bp * tp, s5_w)
        yb, sg, yc, nk, nv = _gla_swa_prompt(pb, pc, wa2, ba, gn, swa_sinks, cos_p, sin_p, l, bp, gla_heads, gla_dk,
                                             gla_dv, swa_heads, mix_rows)
        hp = _merge_ffn(hp, ya, yb, yc, pg, wb, wo, g_ffn_r, wg, wu, wd, gf, l, tm_p, last)
        for lst, val in zip(outs_p, (hr.reshape(bp, groups, s5_n), hi.reshape(bp, groups, s5_n), sg,
                                     nk.reshape(bp, WINDOW, SWA_KV_HEADS, SWA_HD),
                                     nv.reshape(bp, WINDOW, SWA_KV_HEADS, SWA_HD))):
            lst.append(val)

        (pa, pb, pc, pg), _ = _inproj(hs, g_mix_r, w_a, w_b, l, widths, tm_s)
        ya, hr, hi = _s5(pa, h0r_s, h0i_s, l, s5_params, l, ts)
        yb, gla_s = _gla_sample(pb, state_gla, wa2, ba, gn, l, ts, _tile(bs, 32), gla_s)
        gla_s = (gla_s,)
        yc, *swa_s = _swa_sample(pc, ck, cv, sink_cols, cos_s, sin_s, l, swa_heads, ts, _tile(bs, 8), swa_s)
        hs = _merge_ffn(hs, ya, yb, yc, pg, wb, wo, g_ffn_r, wg, wu, wd, gf, l, tm_s, last)
        for lst, val in zip(outs_s, (hr.reshape(bs, groups, s5_n), hi.reshape(bs, groups, s5_n))):
            lst.append(val)

    outs_s = [jnp.stack(o) for o in outs_s] + [gla_s[0], from_lanes(swa_s[0]), from_lanes(swa_s[1])]
    return (hp.reshape(bp, tp, d), hs.reshape(bs, ts, d), *[jnp.stack(o) for o in outs_p], *outs_s)
```

```python
import functools

import jax
import jax.numpy as jnp
from jax import lax
from jax.experimental import pallas as pl
from jax.experimental.pallas import tpu as pltpu

F32 = jnp.float32
BF16 = jnp.bfloat16

EPS = 1e-6
GLA_TAU = 16.0
GLA_CHUNK = 64
SWA_KV_HEADS = 2
SWA_HD = 64
WINDOW = 128
ROPE_THETA = 10000.0
PAST_LEN = 16384

SUBLANES = 8
LANES = 128
VMEM_LIMIT = 56 * 1024 * 1024

NT_DIMS = (((1,), (1,)), ((), ()))
TN_DIMS = (((0,), (0,)), ((), ()))


def _params(*sem):
    return pltpu.CompilerParams(dimension_semantics=sem, vmem_limit_bytes=VMEM_LIMIT)


def _dot(a, b):
    return jnp.dot(a, b, preferred_element_type=F32)


def _rms(x, g):
    return x * lax.rsqrt(jnp.mean(x * x, axis=-1, keepdims=True) + EPS) * g


def _full(shape):
    nd = len(shape)
    return pl.BlockSpec(shape, lambda *_: (0,) * nd)


def _layer(a, l):
    nd = a.ndim - 1
    return pl.BlockSpec((None,) + a.shape[1:], lambda *_: (l,) + (0,) * nd, pipeline_mode=pl.Buffered(1))


def _inproj_kernel(x_ref, g_ref, wa_ref, wb_ref, *refs, n_cast):
    cast_in, o_refs, cast_out = refs[:n_cast], refs[n_cast:n_cast + 4], refs[n_cast + 4:]
    xb = _rms(x_ref[...], g_ref[...]).astype(BF16)
    for w_ref, group in ((wa_ref, o_refs[:2]), (wb_ref, o_refs[2:])):
        off = 0
        for o_ref in group:
            n = o_ref.shape[-1]
            for c0 in range(0, n, 512):
                c1 = min(c0 + 512, n)
                o_ref[:, c0:c1] = lax.dot_general(xb, w_ref[off + c0:off + c1, :], NT_DIMS,
                                                  preferred_element_type=F32)
            off += n
    for src, dst in zip(cast_in, cast_out):
        dst[...] = src[...].astype(BF16)


def _inproj(x, g, wa, wb, l, widths, tm, to_cast=()):
    n, d = x.shape
    steps = n // tm
    assert wa.shape[1] == sum(widths[:2]) and wb.shape[1] == sum(widths[2:])
    flat = [w.reshape(-1, w.shape[-1]) for w in to_cast]
    assert all(f.shape[0] % (steps * 2 * SUBLANES) == 0 for f in flat)
    slice_spec = lambda f: pl.BlockSpec((f.shape[0] // steps, f.shape[1]), lambda i: (i, 0))
    outs = pl.pallas_call(
        functools.partial(_inproj_kernel, n_cast=len(flat)),
        grid=(steps,),
        in_specs=[pl.BlockSpec((tm, d), lambda i: (i, 0)), _layer(g, l), _layer(wa, l), _layer(wb, l)]
        + [slice_spec(f) for f in flat],
        out_specs=[pl.BlockSpec((tm, wd), lambda i: (i, 0)) for wd in widths] + [slice_spec(f) for f in flat],
        out_shape=[jax.ShapeDtypeStruct((n, wd), F32) for wd in widths]
        + [jax.ShapeDtypeStruct(f.shape, BF16) for f in flat],
        compiler_params=_params("parallel"),
        name="inproj",
    )(x, g, wa, wb, *flat)
    return outs[:4], [o.reshape(w.shape) for o, w in zip(outs[4:], to_cast)]


def _s5_prep_kernel(lr_ref, li_ref, ldt_ref, br_ref, bi_ref, ar_ref, ai_ref, bbr_ref, bbi_ref):
    lr, li = lr_ref[...], li_ref[...]
    dt = jnp.exp(ldt_ref[...])
    mag = jnp.exp(lr * dt)
    ar = mag * jnp.cos(li * dt)
    ai = mag * jnp.sin(li * dt)
    den = lr * lr + li * li
    zr, zi = ar - 1.0, ai
    er = (zr * lr + zi * li) / den
    ei = (zi * lr - zr * li) / den
    ar_ref[...] = ar
    ai_ref[...] = ai
    br, bi = br_ref[...], bi_ref[...]
    bbr_ref[...] = er[None] * br - ei[None] * bi
    bbi_ref[...] = er[None] * bi + ei[None] * br


def _s5_prep(lam_re, lam_im, log_dt, b_re, b_im):
    depth, g, c, n = b_re.shape
    dg = depth * g
    brt = jnp.transpose(b_re.reshape(dg, c, n), (1, 0, 2))
    bit = jnp.transpose(b_im.reshape(dg, c, n), (1, 0, 2))
    ar, ai, bbr, bbi = pl.pallas_call(
        _s5_prep_kernel,
        out_shape=[jax.ShapeDtypeStruct((dg, n), F32)] * 2 + [jax.ShapeDtypeStruct((c, dg, n), F32)] * 2,
        name="s5_prep",
    )(lam_re.reshape(dg, n), lam_im.reshape(dg, n), log_dt.reshape(dg, 1), brt, bit)
    unt = lambda a: jnp.transpose(a, (1, 0, 2)).reshape(depth, g, c, n)
    return ar.reshape(depth, 1, g * n), ai.reshape(depth, 1, g * n), unt(bbr), unt(bbi)


def _s5_pack(bbr, bbi, c_re, c_im):
    depth, g, c, n = bbr.shape
    gp = LANES // c
    ks = g // gp
    eye = jnp.eye(gp, dtype=F32)

    def pack_b(b):
        return jnp.einsum("lkgcn,gh->lkgchn", b.reshape(depth, ks, gp, c, n), eye).reshape(
            depth, ks, gp * c, gp * n).astype(BF16)

    def pack_c(cm):
        return jnp.einsum("lkgnc,gh->lkgnhc", cm.reshape(depth, ks, gp, n, c), eye).reshape(
            depth, ks, gp * n, gp * c).astype(BF16)

    return pack_b(bbr), pack_b(bbi), pack_c(c_re), pack_c(c_im)


S5_COL_BLOCK = 1024
S5_UNROLL = 8


def _s5_kernel(u_ref, h0r_ref, h0i_ref, ar_ref, ai_ref, bre_ref, bim_ref, cre_ref, cim_ref, d_ref, wglu_ref,
               bglu_ref, y_ref, hr_out, hi_out, us_s, ys_s, xr_s, xi_s, hr_s, hi_s, *tmp_s, nb, steps):
    c = pl.program_id(0)

    @pl.when(c == 0)
    def _():
        hr_s[...] = h0r_ref[...]
        hi_s[...] = h0i_ref[...]

    ks, kw, nw = bre_ref.shape
    by_sequence = u_ref.ndim == 3
    if by_sequence:
        for b in range(nb):
            for k in range(ks):
                us_s[k, pl.ds(b, steps, stride=nb), :] = u_ref[b, :, k * kw:(k + 1) * kw]
    else:
        for k in range(ks):
            tmp_s[0][k] = u_ref[:, k * kw:(k + 1) * kw]
        for t in range(steps):
            for k in range(ks):
                us_s[k, t * nb:(t + 1) * nb, :] = tmp_s[0][k, pl.ds(t, nb, stride=steps), :]

    def x_proj(k):
        uk = us_s[k].astype(BF16)
        xr_s[:, k * nw:(k + 1) * nw] = _dot(uk, bre_ref[k])
        xi_s[:, k * nw:(k + 1) * nw] = _dot(uk, bim_ref[k])

    def y_proj(k):
        ss = slice(k * nw, (k + 1) * nw)
        yk = _dot(xr_s[:, ss].astype(BF16), cre_ref[k]) - _dot(xi_s[:, ss].astype(BF16), cim_ref[k])
        ys_s[k] = jax.nn.gelu(yk + d_ref[:, k * kw:(k + 1) * kw] * us_s[k])

    def scan(rb, cs, unrolled):
        rows = slice(rb * SUBLANES, (rb + 1) * SUBLANES)
        width = cs.stop - cs.start
        a_r = jnp.broadcast_to(ar_ref[:, cs], (SUBLANES, width))
        a_i = jnp.broadcast_to(ai_ref[:, cs], (SUBLANES, width))

        def step(t, carry):
            hr, hi = carry
            r0 = t * nb + rb * SUBLANES
            tr = slice(r0, r0 + SUBLANES) if unrolled else pl.ds(pl.multiple_of(r0, SUBLANES), SUBLANES)
            nhr = a_r * hr - a_i * hi + xr_s[tr, cs]
            nhi = a_r * hi + a_i * hr + xi_s[tr, cs]
            xr_s[tr, cs] = nhr
            xi_s[tr, cs] = nhi
            return nhr, nhi

        carry = (hr_s[rows, cs], hi_s[rows, cs])
        if unrolled:
            for t in range(steps):
                carry = step(t, carry)
        else:
            carry = lax.fori_loop(0, steps, step, carry, unroll=min(steps, S5_UNROLL))
        hr_s[rows, cs], hi_s[rows, cs] = carry

    if by_sequence and nb == SUBLANES:
        x_proj(0)
        for k in range(ks):
            if k + 1 < ks:
                x_proj(k + 1)
            if k >= 1:
                y_proj(k - 1)
            scan(0, slice(k * nw, (k + 1) * nw), True)
        y_proj(ks - 1)
    else:
        for k in range(ks):
            x_proj(k)
        for rb in range(nb // SUBLANES):
            for cb in range(xr_s.shape[1] // S5_COL_BLOCK):
                scan(rb, slice(cb * S5_COL_BLOCK, (cb + 1) * S5_COL_BLOCK), False)
        for k in range(ks):
            y_proj(k)
    y = jnp.concatenate([ys_s[k] for k in range(ks)], axis=1)
    y = y * jax.nn.sigmoid(_dot(y.astype(BF16), wglu_ref[...]) + bglu_ref[...])
    for k in range(ks):
        ys_s[k] = y[:, k * kw:(k + 1) * kw]
    if by_sequence:
        for b in range(nb):
            for k in range(ks):
                y_ref[b, :, k * kw:(k + 1) * kw] = ys_s[k, pl.ds(b, steps, stride=nb), :]
    else:
        for t in range(steps):
            for k in range(ks):
                tmp_s[0][k, pl.ds(t, nb, stride=steps), :] = ys_s[k, t * nb:(t + 1) * nb, :]
        for k in range(ks):
            y_ref[:, k * kw:(k + 1) * kw] = tmp_s[0][k]

    @pl.when(c == pl.num_programs(0) - 1)
    def _():
        hr_out[...] = hr_s[...]
        hi_out[...] = hi_s[...]


def _s5(u, h0r, h0i, l0, params, l, steps):
    _, nb, ns = h0r.shape
    w = u.shape[-1]
    blk = steps * nb
    ks = params[2].shape[1]
    if u.ndim == 3:
        grid = (u.shape[1] // steps,)
        u_spec = pl.BlockSpec((nb, steps, w), lambda i: (0, i, 0))
        tmp = []
    else:
        assert u.shape[0] == blk
        grid = (1,)
        u_spec = _full(u.shape)
        tmp = [pltpu.VMEM((ks, blk, LANES), F32)]
    kern = functools.partial(_s5_kernel, nb=nb, steps=steps)
    return pl.pallas_call(
        kern,
        grid=grid,
        in_specs=[u_spec, _layer(h0r, l0), _layer(h0i, l0)] + [_layer(a, l) for a in params],
        out_specs=[u_spec, _full((nb, ns)), _full((nb, ns))],
        out_shape=[jax.ShapeDtypeStruct(u.shape, F32), jax.ShapeDtypeStruct((nb, ns), F32),
                   jax.ShapeDtypeStruct((nb, ns), F32)],
        scratch_shapes=[pltpu.VMEM((ks, blk, LANES), F32), pltpu.VMEM((ks, blk, LANES), F32),
                        pltpu.VMEM((blk, ns), F32), pltpu.VMEM((blk, ns), F32), pltpu.VMEM((nb, ns), F32),
                        pltpu.VMEM((nb, ns), F32)] + tmp,
        compiler_params=_params("arbitrary"),
        name="s5_scan",
    )(u, h0r, h0i, *params)


def _log2(n):
    assert n & (n - 1) == 0, n
    return n.bit_length() - 1


def _chunk_masks(rows, chunk):
    sh = _log2(chunk)
    rr = lax.broadcasted_iota(jnp.int32, (rows, rows), 0)
    cc = lax.broadcasted_iota(jnp.int32, (rows, rows), 1)
    same = lax.shift_right_logical(rr, sh) == lax.shift_right_logical(cc, sh)
    return same, same & (cc <= rr)


def _gla_prologue(p_ref, wa_ref, ba_ref, chunk, qk):
    rows = p_ref.shape[0]
    q = p_ref[:, 0:qk]
    k = p_ref[:, qk:2 * qk]
    alow = p_ref[:, p_ref.shape[1] - LANES:].astype(BF16)
    z = _dot(alow, wa_ref[...]) + ba_ref[...]
    la = (jnp.minimum(z, 0.0) - jnp.log(1.0 + jnp.exp(-jnp.abs(z)))) * (1.0 / GLA_TAU)
    same, causal = _chunk_masks(rows, chunk)
    tri = jnp.where(causal, 1.0, 0.0).astype(BF16)
    blk = jnp.where(same, 1.0, 0.0).astype(BF16)
    la_hi = la.astype(BF16)
    la_lo = (la - la_hi.astype(F32)).astype(BF16)
    b = _dot(tri, la_hi) + _dot(tri, la_lo)
    bl = _dot(blk, la_hi) + _dot(blk, la_lo)
    return q, k, b, bl


def _gla_out(o, gn, r):
    on = o * lax.rsqrt(jnp.mean(o * o, axis=-1, keepdims=True) + EPS) * gn
    return on * (r * jax.nn.sigmoid(r))


def _gla_prompt_init(st_s):
    @pl.when(pl.program_id(1) == 0)
    def _():
        st_s[...] = jnp.zeros_like(st_s)


def _gla_prompt_final(s_ref, st_s, *, heads, dk):
    @pl.when(pl.program_id(1) == pl.num_programs(1) - 1)
    def _():
        for h in range(heads):
            s_ref[0, h] = st_s[:, h * dk:(h + 1) * dk].T


def _gla_prompt_body(p_ref, wa_ref, ba_ref, gn_ref, y_ref, st_s, *, heads, dk, dv, chunk):
    rows = p_ref.shape[0]
    qk, vw = heads * dk, heads * dv
    q, k, b, bl = _gla_prologue(p_ref, wa_ref, ba_ref, chunk, qk)
    v_off, r_off = 2 * qk, 2 * qk + vw
    qd = (q * (dk ** -0.5) * jnp.exp(b)).astype(BF16)
    kd = (k * jnp.exp(-b)).astype(BF16)
    kst = (k * jnp.exp(bl - b)).astype(BF16)
    dec = jnp.exp(bl)
    gn = gn_ref[...]
    zero = jnp.zeros((), BF16)
    qk_head = lax.shift_right_logical(lax.broadcasted_iota(jnp.int32, (chunk, qk), 1), _log2(dk))
    v_head = lax.shift_right_logical(lax.broadcasted_iota(jnp.int32, (chunk, vw), 1), _log2(dv))
    rr = lax.broadcasted_iota(jnp.int32, (chunk, heads * chunk), 0)
    cc = lax.broadcasted_iota(jnp.int32, (chunk, heads * chunk), 1) & (chunk - 1)
    causal = cc <= rr
    nchunk = rows // chunk
    by_head = lambda x, head: jnp.concatenate([jnp.where(head == h, x, zero) for h in range(heads)], axis=0)
    att, ds_t, v_diag, qd_rows = [], [], [], []
    for ci in range(nchunk):
        rs = slice(ci * chunk, (ci + 1) * chunk)
        v_c = p_ref[rs, v_off:v_off + vw].astype(BF16)
        v_rows = jnp.concatenate([v_c[:, h * dv:(h + 1) * dv] for h in range(heads)], axis=0)
        v_diag.append(by_head(v_c, v_head))
        qd_rows.append(by_head(qd[rs], qk_head))
        att.append(lax.dot_general(qd[rs], by_head(kd[rs], qk_head), NT_DIMS, preferred_element_type=F32))
        ds_t.append(lax.dot_general(v_rows, by_head(kst[rs], qk_head), TN_DIMS, preferred_element_type=F32))
    yield
    o_intra = [_dot(jnp.where(causal, att[ci], 0.0).astype(BF16), v_diag[ci]) for ci in range(nchunk)]
    yield
    st = st_s[...]
    o_inter = []
    for ci in range(nchunk):
        o_inter.append(lax.dot_general(qd_rows[ci], st.astype(BF16), NT_DIMS, preferred_element_type=F32))
        st = dec[ci * chunk:ci * chunk + 1, :] * st + ds_t[ci]
    st_s[...] = st
    yield
    for ci in range(nchunk):
        rs = slice(ci * chunk, (ci + 1) * chunk)
        for h in range(heads):
            o = o_intra[ci][:, h * dv:(h + 1) * dv] + o_inter[ci][h * chunk:(h + 1) * chunk]
            r = p_ref[rs, r_off + h * dv:r_off + (h + 1) * dv]
            y_ref[rs, h * dv:(h + 1) * dv] = _gla_out(o, gn, r)


def _gla_sample_body(p_ref, s0_ref, wa_ref, ba_ref, gn_ref, y_ref, s_ref, *, heads, dk, dv, t):
    rows = p_ref.shape[0]
    qk = heads * dk
    q, k, b, bl = _gla_prologue(p_ref, wa_ref, ba_ref, t, qk)
    _, cmask = _chunk_masks(SUBLANES, t)
    v_off, r_off = 2 * qk, 2 * qk + heads * dv
    qd = (q * (dk ** -0.5) * jnp.exp(b)).astype(BF16)
    kd = (k * jnp.exp(-b)).astype(BF16)
    kst = k * jnp.exp(bl - b)
    dec = jnp.exp(bl)
    gn = gn_ref[...]
    per_tile = SUBLANES // t
    row_id = lax.broadcasted_iota(jnp.int32, (SUBLANES, 1), 0)
    att, o_state = {}, {}
    for ti in range(rows // SUBLANES):
        rs = slice(ti * SUBLANES, (ti + 1) * SUBLANES)
        dcols = []
        for j in range(per_tile):
            drow = dec[ti * SUBLANES + j * t:ti * SUBLANES + j * t + 1, :]
            dcols.append([jnp.broadcast_to(drow[:, g * LANES:(g + 1) * LANES], (LANES, LANES)).T
                          for g in range(qk // LANES)])
        for h in range(heads):
            hs = slice(h * dk, (h + 1) * dk)
            vh = p_ref[rs, v_off + h * dv:v_off + (h + 1) * dv].astype(BF16)
            att[ti, h] = lax.dot_general(qd[rs, hs], kd[rs, hs], NT_DIMS, preferred_element_type=F32)
            o = None
            for j in range(per_tile):
                bidx = ti * per_tile + j
                in_seq = (row_id >= j * t) & (row_id < (j + 1) * t)
                s0 = s0_ref[bidx, h]
                oj = jnp.where(in_seq, _dot(qd[rs, hs], s0.astype(BF16)), 0.0)
                o = oj if o is None else o + oj
                kj = jnp.where(in_seq, kst[rs, hs], 0.0).astype(BF16)
                ds = lax.dot_general(kj, vh, TN_DIMS, preferred_element_type=F32)
                g, lo = (h * dk) // LANES, (h * dk) % LANES
                dcol = dcols[j][g][lo:lo + dk, 0:dv]
                s_ref[bidx, h] = dcol * s0 + ds
            o_state[ti, h] = o
    yield
    for ti in range(rows // SUBLANES):
        rs = slice(ti * SUBLANES, (ti + 1) * SUBLANES)
        for h in range(heads):
            vh = p_ref[rs, v_off + h * dv:v_off + (h + 1) * dv].astype(BF16)
            o = _dot(jnp.where(cmask, att[ti, h], 0.0).astype(BF16), vh) + o_state[ti, h]
            r = p_ref[rs, r_off + h * dv:r_off + (h + 1) * dv]
            y_ref[rs, h * dv:(h + 1) * dv] = _gla_out(o, gn, r)


def _take_turns(bodies):
    while bodies:
        bodies = [b for b in bodies if next(b, StopIteration) is not StopIteration]


def _first_half(shape):
    lane = lax.broadcasted_iota(jnp.int32, shape, 1)
    return (lane & (SWA_HD - 1)) < SWA_HD // 2


def _rope_table_kernel(pos_ref, inv_ref, cos_ref, sin_ref):
    ang = pos_ref[...] * inv_ref[...]
    s = jnp.sin(ang)
    cos_ref[...] = jnp.cos(ang)
    sin_ref[...] = jnp.where(_first_half(ang.shape), -s, s)


def _rope_table(pos, inv):
    rows = pos.shape[0]
    return pl.pallas_call(
        _rope_table_kernel,
        out_shape=[jax.ShapeDtypeStruct((rows, LANES), F32)] * 2,
        name="rope_table",
    )(pos.astype(F32).reshape(rows, 1), inv)


def _rope(x, cos, sin):
    half = SWA_HD // 2
    first_half = _first_half(cos.shape)
    out = []
    for g in range(x.shape[1] // LANES):
        xg = x[:, g * LANES:(g + 1) * LANES]
        swapped = jnp.where(first_half, pltpu.roll(xg, LANES - half, 1), pltpu.roll(xg, half, 1))
        out.append(xg * cos + swapped * sin)
    return out[0] if len(out) == 1 else jnp.concatenate(out, axis=1)


def _swa_prompt_init(prev_s):
    @pl.when(pl.program_id(1) == 0)
    def _():
        prev_s[...] = jnp.zeros_like(prev_s)


def _swa_prompt_body(sink_ref, cur_ref, cos_ref, sin_ref, y_ref, nk_ref, nv_ref, prev_s, s_s, pc_s, *, heads, l):
    i = pl.program_id(1)
    w = WINDOW
    qw = heads * SWA_HD
    kvw = SWA_KV_HEADS * SWA_HD
    gq = heads // SWA_KV_HEADS
    nq = cur_ref.shape[0] // w
    cos, sin = cos_ref[...], sin_ref[...]
    q = (_rope(cur_ref[:, 0:qw], cos, sin) * (SWA_HD ** -0.5)).astype(BF16)
    k_cur = _rope(cur_ref[:, qw:qw + kvw], cos, sin)
    v_cur = cur_ref[:, qw + kvw:qw + 2 * kvw]
    new = [x.astype(BF16) for x in (k_cur, pltpu.roll(k_cur, SWA_HD, 1), v_cur, pltpu.roll(v_cur, SWA_HD, 1))]
    k2, k2r, v2, v2r = [jnp.concatenate([prev_s[j], new[j]], axis=0) for j in range(4)]
    zero = jnp.zeros((), BF16)
    lo2 = lax.broadcasted_iota(jnp.int32, ((nq + 1) * w, LANES), 1) < SWA_HD
    lo = lax.broadcasted_iota(jnp.int32, (w, LANES), 1) < SWA_HD
    k_both = (jnp.where(lo2, k2, k2r), jnp.where(lo2, k2r, k2))
    v_half = {(0, 0): jnp.where(lo2, v2, zero), (1, 1): jnp.where(lo2, zero, v2),
              (0, 1): jnp.where(lo2, zero, v2r), (1, 0): jnp.where(lo2, v2r, zero)}
    upper = lax.broadcasted_iota(jnp.int32, (w, w), 1) > lax.broadcasted_iota(jnp.int32, (w, w), 0)
    no_prev = jnp.where(i > 0, 0.0, -jnp.inf)
    slot = lambda c, h: slice((c * heads + h) * w, (c * heads + h + 1) * w)
    for c in range(nq):
        for kv in range(SWA_KV_HEADS):
            qm = []
            for h in range(kv * gq, (kv + 1) * gq):
                qg = q[c * w:(c + 1) * w, (h // 2) * LANES:(h // 2 + 1) * LANES]
                qm.append(jnp.where(lo, qg, zero) if h % 2 == 0 else jnp.where(lo, zero, qg))
            s_s[(c * heads + kv * gq) * w:(c * heads + (kv + 1) * gq) * w, :] = lax.dot_general(
                jnp.concatenate(qm, axis=0), k_both[kv][c * w:(c + 2) * w], NT_DIMS,
                preferred_element_type=F32)
    yield
    sink_term = {}
    for c in range(nq):
        for h in range(heads):
            s_prev = s_s[slot(c, h), 0:w] + no_prev if c == 0 else s_s[slot(c, h), 0:w]
            sp = jnp.where(upper, s_prev, s_s[slot(c, h), w:2 * w])
            sink = sink_ref[l, h]
            m = jnp.maximum(jnp.max(sp, axis=-1, keepdims=True), sink)
            p = jnp.exp(sp - m)
            pc_s[slot(c, h), 0:w] = jnp.where(upper, p, 0.0).astype(BF16)
            pc_s[slot(c, h), w:2 * w] = jnp.where(upper, 0.0, p).astype(BF16)
            sink_term[c, h] = jnp.exp(sink - m)
    yield
    num = {(c, h): _dot(pc_s[slot(c, h), :], v_half[(h // gq, h % 2)][c * w:(c + 2) * w])
           for c in range(nq) for h in range(heads)}
    den_all = _dot(pc_s[...], jnp.ones((2 * w, LANES), BF16))
    yield
    for c in range(nq):
        for g in range(heads // 2):
            he, ho = 2 * g, 2 * g + 1
            den = jnp.where(lo, den_all[slot(c, he)] + sink_term[c, he], den_all[slot(c, ho)] + sink_term[c, ho])
            y_ref[c * w:(c + 1) * w, g * LANES:(g + 1) * LANES] = (num[c, he] + num[c, ho]) * (1.0 / den)
    for j in range(4):
        prev_s[j] = new[j][(nq - 1) * w:]
    nk_ref[0] = k_cur[(nq - 1) * w:]
    nv_ref[0] = v_cur[(nq - 1) * w:]


def _gla_swa_prompt_kernel(pb_ref, wa_ref, ba_ref, gn_ref, sink_ref, pc_ref, cos_ref, sin_ref,
                           yb_ref, s_ref, yc_ref, nk_ref, nv_ref, st_s, prev_s, s_s, pc_s, *, gla, swa):
    _gla_prompt_init(st_s)
    _swa_prompt_init(prev_s)
    _take_turns([_gla_prompt_body(pb_ref, wa_ref, ba_ref, gn_ref, yb_ref, st_s, chunk=GLA_CHUNK, **gla),
                 _swa_prompt_body(sink_ref, pc_ref, cos_ref, sin_ref, yc_ref, nk_ref, nv_ref, prev_s, s_s, pc_s,
                                  **swa)])
    _gla_prompt_final(s_ref, st_s, heads=gla["heads"], dk=gla["dk"])


def _gla_swa_prompt(pb, pc, wa, ba, gn, sinks, cos, sin, l, bsz, gla_heads, dk, dv, swa_heads, rows):
    n, pbw = pb.shape
    pcw = pc.shape[1]
    nblk = n // bsz // rows
    nq = rows // WINDOW
    qw, kvw = swa_heads * SWA_HD, SWA_KV_HEADS * SWA_HD
    kern = functools.partial(_gla_swa_prompt_kernel, gla=dict(heads=gla_heads, dk=dk, dv=dv),
                             swa=dict(heads=swa_heads, l=l))
    row_blk = lambda wdt: pl.BlockSpec((rows, wdt), lambda b, i: (b * nblk + i, 0))
    tab = pl.BlockSpec((rows, LANES), lambda b, i: (i, 0))
    win = pl.BlockSpec((1, WINDOW, kvw), lambda b, i: (b, 0, 0))
    return pl.pallas_call(
        kern,
        grid=(bsz, nblk),
        in_specs=[row_blk(pbw), _layer(wa, l), _layer(ba, l), _layer(gn, l),
                  pl.BlockSpec(memory_space=pltpu.SMEM), row_blk(pcw), tab, tab],
        out_specs=[row_blk(gla_heads * dv), pl.BlockSpec((1, gla_heads, dk, dv), lambda b, i: (b, 0, 0, 0)),
                   row_blk(qw), win, win],
        out_shape=[jax.ShapeDtypeStruct((n, gla_heads * dv), F32),
                   jax.ShapeDtypeStruct((bsz, gla_heads, dk, dv), F32),
                   jax.ShapeDtypeStruct((n, qw), F32), jax.ShapeDtypeStruct((bsz, WINDOW, kvw), F32),
                   jax.ShapeDtypeStruct((bsz, WINDOW, kvw), F32)],
        scratch_shapes=[pltpu.VMEM((dv, gla_heads * dk), F32), pltpu.VMEM((4, WINDOW, LANES), BF16),
                        pltpu.VMEM((nq * swa_heads * WINDOW, 2 * WINDOW), F32),
                        pltpu.VMEM((nq * swa_heads * WINDOW, 2 * WINDOW), BF16)],
        compiler_params=_params("parallel", "arbitrary"),
        name="gla_swa_prompt",
    )(pb, wa, ba, gn, sinks, pc, cos, sin)


def _swa_sample_body(sinkcol_ref, p_ref, ck_ref, cv_ref, cos_ref, sin_ref, y_ref, nk_ref, nv_ref, *, heads, t, nb):
    w = ck_ref.shape[3]
    qw = heads * SWA_HD
    kvw = SWA_KV_HEADS * SWA_HD
    gq = heads // SWA_KV_HEADS
    per_tile = SUBLANES // t
    nrow = heads * SUBLANES
    cos, sin = cos_ref[...], sin_ref[...]
    lo = lax.broadcasted_iota(jnp.int32, (SUBLANES, LANES), 1) < SWA_HD
    row_r = lax.broadcasted_iota(jnp.int32, (nrow, 1), 0) & (SUBLANES - 1)
    row_seq = lax.shift_right_logical(row_r, _log2(t))
    row_step = row_r & (t - 1)
    cache_ok = lax.broadcasted_iota(jnp.int32, (nrow, w), 1) > row_step
    new_r = lax.broadcasted_iota(jnp.int32, (nrow, SUBLANES), 1)
    new_ok = (lax.shift_right_logical(new_r, _log2(t)) == row_seq) & ((new_r & (t - 1)) <= row_step)
    sink_col = sinkcol_ref[...]
    lane_w = lax.broadcasted_iota(jnp.int32, (SWA_HD, w), 1)
    flat = lambda c: jnp.concatenate([c[kv] for kv in range(SWA_KV_HEADS)], axis=0).astype(BF16)
    pad_rows = jnp.zeros((LANES - 2 * SUBLANES, kvw), F32)

    def on_kv_half(x, h, back=False):
        own, kv = h % 2, h // gq
        moved = x if own == kv else pltpu.roll(x, SWA_HD, 1)
        keep = own if back else kv
        return jnp.where(lo, moved, 0.0) if keep == 0 else jnp.where(lo, 0.0, moved)

    new_k, new_v, scores = [], [], []
    for ti in range(nb // per_tile):
        rs = slice(ti * SUBLANES, (ti + 1) * SUBLANES)
        q8 = _rope(p_ref[rs, 0:qw], cos, sin) * (SWA_HD ** -0.5)
        kn8 = _rope(p_ref[rs, qw:qw + kvw], cos, sin)
        new_k.append(kn8)
        new_v.append(p_ref[rs, qw + kvw:qw + 2 * kvw])
        qm = jnp.concatenate([on_kv_half(q8[:, (h // 2) * LANES:(h // 2 + 1) * LANES], h) for h in range(heads)],
                             axis=0).astype(BF16)
        s_c = None
        for j in range(per_tile):
            s_j = _dot(qm, flat(ck_ref[ti * per_tile + j]))
            s_c = s_j if s_c is None else jnp.where(row_seq == j, s_j, s_c)
        s_n = lax.dot_general(qm, kn8.astype(BF16), NT_DIMS, preferred_element_type=F32)
        scores.append((s_c, s_n))
    yield
    probs = []
    for s_c, s_n in scores:
        s_c = jnp.where(cache_ok, s_c, -jnp.inf)
        s_n = jnp.where(new_ok, s_n, -jnp.inf)
        m = jnp.maximum(jnp.maximum(jnp.max(s_c, axis=1, keepdims=True), jnp.max(s_n, axis=1, keepdims=True)),
                        sink_col)
        p_c = jnp.exp(s_c - m)
        p_n = jnp.exp(s_n - m)
        den = jnp.sum(p_c, axis=1, keepdims=True) + jnp.sum(p_n, axis=1, keepdims=True) + jnp.exp(sink_col - m)
        rden = 1.0 / den
        probs.append((p_c * rden, (p_n * rden).astype(BF16)))
    yield
    outs = []
    for ti, (p_c, p_n) in enumerate(probs):
        o = _dot(p_n, new_v[ti].astype(BF16))
        for j in range(per_tile):
            p_j = jnp.where(row_seq == j, p_c, 0.0).astype(BF16)
            o = o + lax.dot_general(p_j, flat(cv_ref[ti * per_tile + j]), NT_DIMS, preferred_element_type=F32)
        outs.append(o)
    yield
    for ti, o in enumerate(outs):
        for g in range(heads // 2):
            pair = [on_kv_half(o[h * SUBLANES:(h + 1) * SUBLANES], h, back=True) for h in (2 * g, 2 * g + 1)]
            y_ref[ti * SUBLANES:(ti + 1) * SUBLANES, g * LANES:(g + 1) * LANES] = pair[0] + pair[1]
    yield
    for ti in range(nb // per_tile):
        kv_t = jnp.concatenate([new_k[ti], new_v[ti], pad_rows], axis=0).T
        for j in range(per_tile):
            bidx = ti * per_tile + j
            for c_ref, n_ref, col0 in ((ck_ref, nk_ref, 0), (cv_ref, nv_ref, SUBLANES)):
                for kv in range(SWA_KV_HEADS):
                    tail = pltpu.roll(kv_t[kv * SWA_HD:(kv + 1) * SWA_HD, :], w - t - col0 - j * t, 1)
                    n_ref[bidx, kv] = jnp.where(lane_w < w - t, pltpu.roll(c_ref[bidx, kv], w - t, 1), tail)


def _into_layers(body, n_in, n_acc):
    kern = lambda *refs: _take_turns([body(*refs[:n_in], *refs[n_in + n_acc:])])
    return kern, [pl.BlockSpec(memory_space=pl.ANY)] * n_acc, {n_in + j: 1 + j for j in range(n_acc)}


def _gla_sample(p, s0, wa, ba, gn, l, t, nb, acc):
    n, pw = p.shape
    _, bsz, heads, dk, dv = s0.shape
    rows = nb * t
    body = functools.partial(_gla_sample_body, heads=heads, dk=dk, dv=dv, t=t)
    kern, acc_specs, aliases = _into_layers(body, 5, len(acc))
    state = pl.BlockSpec((None, nb, heads, dk, dv), lambda i: (l, i, 0, 0, 0))
    return pl.pallas_call(
        kern,
        grid=(bsz // nb,),
        in_specs=[pl.BlockSpec((rows, pw), lambda i: (i, 0)), state, _layer(wa, l), _layer(ba, l), _layer(gn, l)]
        + acc_specs,
        out_specs=[pl.BlockSpec((rows, heads * dv), lambda i: (i, 0)), state],
        out_shape=[jax.ShapeDtypeStruct((n, heads * dv), F32), jax.ShapeDtypeStruct(s0.shape, F32)],
        input_output_aliases=aliases,
        compiler_params=_params("parallel"),
        name="gla_sample",
    )(p, s0, wa, ba, gn, *acc)


def _swa_sample(p, ck, cv, sink_cols, cos, sin, l, heads, t, nb, acc):
    n, pw = p.shape
    _, bsz, kvh, hd, w = ck.shape
    qw = heads * SWA_HD
    body = functools.partial(_swa_sample_body, heads=heads, t=t, nb=nb)
    kern, acc_specs, aliases = _into_layers(body, 6, len(acc))
    cache = pl.BlockSpec((None, nb, kvh, hd, w), lambda i: (l, i, 0, 0, 0))
    return pl.pallas_call(
        kern,
        grid=(bsz // nb,),
        in_specs=[_layer(sink_cols, l), pl.BlockSpec((nb * t, pw), lambda i: (i, 0)), cache, cache,
                  _full(cos.shape), _full(sin.shape)] + acc_specs,
        out_specs=[pl.BlockSpec((nb * t, qw), lambda i: (i, 0)), cache, cache],
        out_shape=[jax.ShapeDtypeStruct((n, qw), F32), jax.ShapeDtypeStruct(ck.shape, F32),
                   jax.ShapeDtypeStruct(cv.shape, F32)],
        input_output_aliases=aliases,
        compiler_params=_params("parallel"),
        name="swa_sample",
    )(sink_cols, p, ck, cv, cos, sin, *acc)


FFN_HIDDEN_CHUNK = 1024


def _merge_ffn_kernel(x_ref, ya_ref, yb_ref, yc_ref, gt_ref, wb_ref, wo_ref, g_ref, wg_ref, wu_ref, wd_ref, gf_ref,
                      o_ref, *, final_norm):
    d = x_ref.shape[1]
    merged = None
    off = 0
    for i, y_ref in enumerate((ya_ref, yb_ref, yc_ref)):
        wdt = y_ref.shape[1]
        br = _dot(y_ref[...].astype(BF16), wb_ref[off:off + wdt, :])
        term = jax.nn.sigmoid(gt_ref[:, i * d:(i + 1) * d]) * br
        merged = term if merged is None else merged + term
        off += wdt
    x = x_ref[...] + _dot(merged.astype(BF16), wo_ref[...])
    hb = _rms(x, g_ref[...]).astype(BF16)
    hidden = wg_ref.shape[1]
    acc = x
    for c0 in range(0, hidden, FFN_HIDDEN_CHUNK):
        c1 = min(c0 + FFN_HIDDEN_CHUNK, hidden)
        gate = _dot(hb, wg_ref[:, c0:c1])
        up = _dot(hb, wu_ref[:, c0:c1])
        act = (gate * jax.nn.sigmoid(gate) * up).astype(BF16)
        acc = acc + _dot(act, wd_ref[c0:c1, :])
    o_ref[...] = _rms(acc, gf_ref[...]) if final_norm else acc


def _merge_ffn(x, ya, yb, yc, gates, wb, wo, g, wg, wu, wd, gf, l, tm, final_norm):
    n, d = x.shape
    row = lambda a: pl.BlockSpec((tm, a.shape[1]), lambda i: (i, 0))
    return pl.pallas_call(
        functools.partial(_merge_ffn_kernel, final_norm=final_norm),
        grid=(n // tm,),
        in_specs=[row(x), row(ya), row(yb), row(yc), row(gates), _layer(wb, l), _layer(wo, l), _layer(g, l),
                  _layer(wg, l), _layer(wu, l), _layer(wd, l), _full(gf.shape)],
        out_specs=pl.BlockSpec((tm, d), lambda i: (i, 0)),
        out_shape=jax.ShapeDtypeStruct((n, d), F32),
        compiler_params=_params("parallel"),
        name="merge_ffn",
    )(x, ya, yb, yc, gates, wb, wo, g, wg, wu, wd, gf)


def _tile(n, want):
    while n % want:
        want //= 2
    return want


def kernel(x_prompt, x_sample, state_s5_re, state_s5_im, state_gla, cache_swa_k, cache_swa_v, g_mix, w_in, s5_lambda_re, s5_lambda_im, s5_log_dt, s5_b_re, s5_b_im, s5_c_re, s5_c_im, s5_d, w_glu, b_glu, w_gla_a2, b_gla_a, g_gla_norm, swa_sinks, w_branch, w_out, g_ffn, w_ffn_gate, w_ffn_up, w_ffn_down, g_final):
    bp, tp, d = x_prompt.shape
    bs, ts, _ = x_sample.shape
    depth = w_in.shape[0]
    s5_w = s5_d.shape[1]
    groups, s5_n = s5_lambda_re.shape[1:]
    gla_qk = w_gla_a2.shape[2]
    gla_rank = w_gla_a2.shape[1]
    gla_heads, gla_dk, gla_dv = state_gla.shape[2:]
    gla_v = gla_heads * gla_dv
    swa_heads = swa_sinks.shape[1]
    swa_q = swa_heads * SWA_HD
    swa_kv = SWA_KV_HEADS * SWA_HD
    assert SWA_KV_HEADS * SWA_HD == LANES and bp % SUBLANES == 0 and bs % SUBLANES == 0 and SUBLANES % ts == 0
    assert tp % WINDOW == 0 and cache_swa_k.shape[2] == WINDOW

    o_a, o_b = 0, s5_w
    o_low = o_b + 2 * gla_qk + 2 * gla_v
    o_c = o_low + gla_rank
    o_g = o_c + swa_q + 2 * swa_kv
    widths = (s5_w, 2 * gla_qk + 2 * gla_v + LANES, swa_q + 2 * swa_kv, 3 * d)

    hp = x_prompt.reshape(bp * tp, d)
    hs = x_sample.reshape(bs * ts, d)
    inv = ROPE_THETA ** (-jnp.arange(SWA_HD // 2, dtype=F32) * (2.0 / SWA_HD))
    inv = jnp.tile(inv, 2 * LANES // SWA_HD).reshape(1, LANES)
    cos_p, sin_p = _rope_table(jnp.arange(tp, dtype=jnp.int32), inv)
    cos_s, sin_s = _rope_table(PAST_LEN + (jnp.arange(SUBLANES, dtype=jnp.int32) & (ts - 1)), inv)
    zeros_s5 = jnp.zeros((1, bp, groups * s5_n), F32)
    tm_p, tm_s = _tile(bp * tp, 512), _tile(bs * ts, 512)
    s5_steps = _tile(tp, 128)
    mix_rows = WINDOW * _tile(tp // WINDOW, 4)
    assert mix_rows % GLA_CHUNK == 0

    w_t = jnp.swapaxes(w_in, 1, 2)
    w_a = jnp.pad(w_t[:, :o_c].astype(BF16), ((0, 0), (0, LANES - gla_rank), (0, 0)))
    w_b = w_t[:, o_c:].astype(BF16)
    wa2 = jnp.concatenate([w_gla_a2, jnp.zeros((depth, LANES - gla_rank, gla_qk), F32)], axis=1).astype(BF16)
    ba = b_gla_a.reshape(depth, 1, gla_qk)
    gn = g_gla_norm.reshape(depth, 1, gla_dv)
    ar, ai, bbr, bbi = _s5_prep(s5_lambda_re, s5_lambda_im, s5_log_dt, s5_b_re, s5_b_im)
    bre, bim, cre, cim = _s5_pack(bbr, bbi, s5_c_re, s5_c_im)
    s5_d_r, b_glu_r = s5_d.reshape(depth, 1, s5_w), b_glu.reshape(depth, 1, s5_w)
    h0r_s = state_s5_re.reshape(depth, bs, groups * s5_n)
    h0i_s = state_s5_im.reshape(depth, bs, groups * s5_n)
    g_mix_r, g_ffn_r, gf = g_mix.reshape(depth, 1, d), g_ffn.reshape(depth, 1, d), g_final.reshape(1, d)
    later = (w_glu, w_branch, w_out, w_ffn_gate, w_ffn_up, w_ffn_down)
    cast_rows = (bp * tp // tm_p) * 2 * SUBLANES
    cast_in_call = all((w.shape[0] * w.shape[1]) % cast_rows == 0 for w in later)
    casted = None if cast_in_call else [w.astype(BF16) for w in later]
    sink_cols = jnp.repeat(swa_sinks, SUBLANES, axis=1).reshape(depth, swa_heads * SUBLANES, 1)
    to_lanes = lambda c: jnp.transpose(c, (0, 1, 3, 4, 2))
    from_lanes = lambda c: jnp.transpose(c, (0, 1, 4, 2, 3))
    ck, cv = to_lanes(cache_swa_k), to_lanes(cache_swa_v)

    outs_p = [[] for _ in range(5)]
    outs_s = [[], []]
    gla_s, nk_s, nv_s = jnp.zeros_like(state_gla), jnp.zeros_like(ck), jnp.zeros_like(cv)
    for l in range(depth):
        last = l == depth - 1

        (pa, pb, pc, pg), new_casts = _inproj(hp, g_mix_r, w_a, w_b, l, widths, tm_p,
                                              later if casted is None else ())
        if casted is None:
            casted = new_casts
        wglu_b, wb, wo, wg, wu, wd = casted
        s5_params = (ar, ai, bre, bim, cre, cim, s5_d_r, wglu_b, b_glu_r)
        ya, hr, hi = _s5(pa.reshape(bp, tp, s5_w), zeros_s5, zeros_s5, 0, s5_params, l, s5_steps)
        ya = ya.reshape(bp * tp, s5_w)
        yb, sg, yc, nk, nv = _gla_swa_prompt(pb, pc, wa2, ba, gn, swa_sinks, cos_p, sin_p, l, bp, gla_heads, gla_dk,
                                             gla_dv, swa_heads, mix_rows)
        hp = _merge_ffn(hp, ya, yb, yc, pg, wb, wo, g_ffn_r, wg, wu, wd, gf, l, tm_p, last)
        for lst, val in zip(outs_p, (hr.reshape(bp, groups, s5_n), hi.reshape(bp, groups, s5_n), sg,
                                     nk.reshape(bp, WINDOW, SWA_KV_HEADS, SWA_HD),
                                     nv.reshape(bp, WINDOW, SWA_KV_HEADS, SWA_HD))):
            lst.append(val)

        (pa, pb, pc, pg), _ = _inproj(hs, g_mix_r, w_a, w_b, l, widths, tm_s)
        ya, hr, hi = _s5(pa, h0r_s, h0i_s, l, s5_params, l, ts)
        yb, gla_s = _gla_sample(pb, state_gla, wa2, ba, gn, l, ts, _tile(bs, 32), (gla_s,))
        yc, nk_s, nv_s = _swa_sample(pc, ck, cv, sink_cols, cos_s, sin_s, l, swa_heads, ts, _tile(bs, 16),
                                     (nk_s, nv_s))
        hs = _merge_ffn(hs, ya, yb, yc, pg, wb, wo, g_ffn_r, wg, wu, wd, gf, l, tm_s, last)
        for lst, val in zip(outs_s, (hr.reshape(bs, groups, s5_n), hi.reshape(bs, groups, s5_n))):
            lst.append(val)

    outs_s = [jnp.stack(o) for o in outs_s] + [gla_s, from_lanes(nk_s), from_lanes(nv_s)]
    return (hp.reshape(bp, tp, d), hs.reshape(bs, ts, d), *[jnp.stack(o) for o in outs_p], *outs_s)
```

```python
import functools

import jax
import jax.numpy as jnp
from jax import lax
from jax.experimental import pallas as pl
from jax.experimental.pallas import tpu as pltpu

F32 = jnp.float32
BF16 = jnp.bfloat16

EPS = 1e-6
GLA_TAU = 16.0
GLA_CHUNK = 64
SWA_KV_HEADS = 2
SWA_HD = 64
WINDOW = 128
ROPE_THETA = 10000.0
PAST_LEN = 16384

SUBLANES = 8
LANES = 128
VMEM_LIMIT = 56 * 1024 * 1024

NT_DIMS = (((1,), (1,)), ((), ()))
TN_DIMS = (((0,), (0,)), ((), ()))


def _params(*sem):
    return pltpu.CompilerParams(dimension_semantics=sem, vmem_limit_bytes=VMEM_LIMIT)


def _dot(a, b):
    return jnp.dot(a, b, preferred_element_type=F32)


def _rms(x, g):
    return x * lax.rsqrt(jnp.mean(x * x, axis=-1, keepdims=True) + EPS) * g


def _full(shape):
    nd = len(shape)
    return pl.BlockSpec(shape, lambda *_: (0,) * nd)


def _layer(a, l):
    nd = a.ndim - 1
    return pl.BlockSpec((None,) + a.shape[1:], lambda *_: (l,) + (0,) * nd, pipeline_mode=pl.Buffered(1))


def _inproj_kernel(x_ref, g_ref, wa_ref, wb_ref, *refs, n_cast):
    cast_in, o_refs, cast_out = refs[:n_cast], refs[n_cast:n_cast + 4], refs[n_cast + 4:]
    xb = _rms(x_ref[...], g_ref[...]).astype(BF16)
    for w_ref, group in ((wa_ref, o_refs[:2]), (wb_ref, o_refs[2:])):
        off = 0
        for o_ref in group:
            n = o_ref.shape[-1]
            for c0 in range(0, n, 512):
                c1 = min(c0 + 512, n)
                o_ref[:, c0:c1] = lax.dot_general(xb, w_ref[off + c0:off + c1, :], NT_DIMS,
                                                  preferred_element_type=F32)
            off += n
    for src, dst in zip(cast_in, cast_out):
        dst[...] = src[...].astype(BF16)


def _inproj(x, g, w, row_b, l, widths, tm, to_cast=()):
    n, d = x.shape
    steps = n // tm
    rows_a, rows_b = sum(widths[:2]), sum(widths[2:])
    assert row_b + rows_b == w.shape[1] and row_b % (2 * SUBLANES) == 0 and rows_a <= w.shape[1]
    w_rows = lambda start, size: pl.BlockSpec((None, pl.Element(size), pl.Element(d)), lambda i: (l, start, 0),
                                              pipeline_mode=pl.Buffered(1))
    flat = [a.reshape(-1, a.shape[-1]) for a in to_cast]
    assert all(f.shape[0] % (steps * 2 * SUBLANES) == 0 for f in flat)
    slice_spec = lambda f: pl.BlockSpec((f.shape[0] // steps, f.shape[1]), lambda i: (i, 0))
    outs = pl.pallas_call(
        functools.partial(_inproj_kernel, n_cast=len(flat)),
        grid=(steps,),
        in_specs=[pl.BlockSpec((tm, d), lambda i: (i, 0)), _layer(g, l), w_rows(0, rows_a), w_rows(row_b, rows_b)]
        + [slice_spec(f) for f in flat],
        out_specs=[pl.BlockSpec((tm, wd), lambda i: (i, 0)) for wd in widths] + [slice_spec(f) for f in flat],
        out_shape=[jax.ShapeDtypeStruct((n, wd), F32) for wd in widths]
        + [jax.ShapeDtypeStruct(f.shape, BF16) for f in flat],
        compiler_params=_params("parallel"),
        name="inproj",
    )(x, g, w, w, *flat)
    return outs[:4], [o.reshape(a.shape) for o, a in zip(outs[4:], to_cast)]


def _s5_prep_kernel(lr_ref, li_ref, ldt_ref, br_ref, bi_ref, ar_ref, ai_ref, bbr_ref, bbi_ref):
    lr, li = lr_ref[...], li_ref[...]
    dt = jnp.exp(ldt_ref[...])
    mag = jnp.exp(lr * dt)
    ar = mag * jnp.cos(li * dt)
    ai = mag * jnp.sin(li * dt)
    den = lr * lr + li * li
    zr, zi = ar - 1.0, ai
    er = (zr * lr + zi * li) / den
    ei = (zi * lr - zr * li) / den
    ar_ref[...] = ar
    ai_ref[...] = ai
    br, bi = br_ref[...], bi_ref[...]
    bbr_ref[...] = er[None] * br - ei[None] * bi
    bbi_ref[...] = er[None] * bi + ei[None] * br


def _s5_prep(lam_re, lam_im, log_dt, b_re, b_im):
    depth, g, c, n = b_re.shape
    dg = depth * g
    brt = jnp.transpose(b_re.reshape(dg, c, n), (1, 0, 2))
    bit = jnp.transpose(b_im.reshape(dg, c, n), (1, 0, 2))
    ar, ai, bbr, bbi = pl.pallas_call(
        _s5_prep_kernel,
        out_shape=[jax.ShapeDtypeStruct((dg, n), F32)] * 2 + [jax.ShapeDtypeStruct((c, dg, n), F32)] * 2,
        name="s5_prep",
    )(lam_re.reshape(dg, n), lam_im.reshape(dg, n), log_dt.reshape(dg, 1), brt, bit)
    unt = lambda a: jnp.transpose(a, (1, 0, 2)).reshape(depth, g, c, n)
    return ar.reshape(depth, 1, g * n), ai.reshape(depth, 1, g * n), unt(bbr), unt(bbi)


def _s5_pack(bbr, bbi, c_re, c_im):
    depth, g, c, n = bbr.shape
    gp = LANES // c
    ks = g // gp
    eye = jnp.eye(gp, dtype=F32)

    def pack_b(b):
        return jnp.einsum("lkgcn,gh->lkgchn", b.reshape(depth, ks, gp, c, n), eye).reshape(
            depth, ks, gp * c, gp * n).astype(BF16)

    def pack_c(cm):
        return jnp.einsum("lkgnc,gh->lkgnhc", cm.reshape(depth, ks, gp, n, c), eye).reshape(
            depth, ks, gp * n, gp * c).astype(BF16)

    return pack_b(bbr), pack_b(bbi), pack_c(c_re), pack_c(c_im)


S5_COL_BLOCK = 1024
S5_UNROLL = 8


def _s5_kernel(u_ref, h0r_ref, h0i_ref, ar_ref, ai_ref, bre_ref, bim_ref, cre_ref, cim_ref, d_ref, wglu_ref,
               bglu_ref, y_ref, hr_out, hi_out, us_s, ys_s, xr_s, xi_s, hr_s, hi_s, *tmp_s, nb, steps):
    c = pl.program_id(0)

    @pl.when(c == 0)
    def _():
        hr_s[...] = h0r_ref[...]
        hi_s[...] = h0i_ref[...]

    ks, kw, nw = bre_ref.shape
    by_sequence = u_ref.ndim == 3
    if by_sequence:
        for b in range(nb):
            for k in range(ks):
                us_s[k, pl.ds(b, steps, stride=nb), :] = u_ref[b, :, k * kw:(k + 1) * kw]
    else:
        for k in range(ks):
            tmp_s[0][k] = u_ref[:, k * kw:(k + 1) * kw]
        for t in range(steps):
            for k in range(ks):
                us_s[k, t * nb:(t + 1) * nb, :] = tmp_s[0][k, pl.ds(t, nb, stride=steps), :]

    def x_proj(k):
        uk = us_s[k].astype(BF16)
        xr_s[:, k * nw:(k + 1) * nw] = _dot(uk, bre_ref[k])
        xi_s[:, k * nw:(k + 1) * nw] = _dot(uk, bim_ref[k])

    def y_proj(k):
        ss = slice(k * nw, (k + 1) * nw)
        yk = _dot(xr_s[:, ss].astype(BF16), cre_ref[k]) - _dot(xi_s[:, ss].astype(BF16), cim_ref[k])
        ys_s[k] = jax.nn.gelu(yk + d_ref[:, k * kw:(k + 1) * kw] * us_s[k])

    def scan(rb, cs, unrolled):
        rows = slice(rb * SUBLANES, (rb + 1) * SUBLANES)
        width = cs.stop - cs.start
        a_r = jnp.broadcast_to(ar_ref[:, cs], (SUBLANES, width))
        a_i = jnp.broadcast_to(ai_ref[:, cs], (SUBLANES, width))

        def step(t, carry):
            hr, hi = carry
            r0 = t * nb + rb * SUBLANES
            tr = slice(r0, r0 + SUBLANES) if unrolled else pl.ds(pl.multiple_of(r0, SUBLANES), SUBLANES)
            nhr = a_r * hr - a_i * hi + xr_s[tr, cs]
            nhi = a_r * hi + a_i * hr + xi_s[tr, cs]
            xr_s[tr, cs] = nhr
            xi_s[tr, cs] = nhi
            return nhr, nhi

        carry = (hr_s[rows, cs], hi_s[rows, cs])
        if unrolled:
            for t in range(steps):
                carry = step(t, carry)
        else:
            carry = lax.fori_loop(0, steps, step, carry, unroll=min(steps, S5_UNROLL))
        hr_s[rows, cs], hi_s[rows, cs] = carry

    if by_sequence and nb == SUBLANES:
        x_proj(0)
        for k in range(ks):
            if k + 1 < ks:
                x_proj(k + 1)
            if k >= 1:
                y_proj(k - 1)
            scan(0, slice(k * nw, (k + 1) * nw), True)
        y_proj(ks - 1)
    else:
        for k in range(ks):
            x_proj(k)
        for rb in range(nb // SUBLANES):
            for cb in range(xr_s.shape[1] // S5_COL_BLOCK):
                scan(rb, slice(cb * S5_COL_BLOCK, (cb + 1) * S5_COL_BLOCK), False)
        for k in range(ks):
            y_proj(k)
    y = jnp.concatenate([ys_s[k] for k in range(ks)], axis=1)
    y = y * jax.nn.sigmoid(_dot(y.astype(BF16), wglu_ref[...]) + bglu_ref[...])
    for k in range(ks):
        ys_s[k] = y[:, k * kw:(k + 1) * kw]
    if by_sequence:
        for b in range(nb):
            for k in range(ks):
                y_ref[b, :, k * kw:(k + 1) * kw] = ys_s[k, pl.ds(b, steps, stride=nb), :]
    else:
        for t in range(steps):
            for k in range(ks):
                tmp_s[0][k, pl.ds(t, nb, stride=steps), :] = ys_s[k, t * nb:(t + 1) * nb, :]
        for k in range(ks):
            y_ref[:, k * kw:(k + 1) * kw] = tmp_s[0][k]

    @pl.when(c == pl.num_programs(0) - 1)
    def _():
        hr_out[...] = hr_s[...]
        hi_out[...] = hi_s[...]


def _s5(u, h0r, h0i, l0, params, l, steps):
    _, nb, ns = h0r.shape
    w = u.shape[-1]
    blk = steps * nb
    ks = params[2].shape[1]
    if u.ndim == 3:
        grid = (u.shape[1] // steps,)
        u_spec = pl.BlockSpec((nb, steps, w), lambda i: (0, i, 0))
        tmp = []
    else:
        assert u.shape[0] == blk
        grid = (1,)
        u_spec = _full(u.shape)
        tmp = [pltpu.VMEM((ks, blk, LANES), F32)]
    kern = functools.partial(_s5_kernel, nb=nb, steps=steps)
    return pl.pallas_call(
        kern,
        grid=grid,
        in_specs=[u_spec, _layer(h0r, l0), _layer(h0i, l0)] + [_layer(a, l) for a in params],
        out_specs=[u_spec, _full((nb, ns)), _full((nb, ns))],
        out_shape=[jax.ShapeDtypeStruct(u.shape, F32), jax.ShapeDtypeStruct((nb, ns), F32),
                   jax.ShapeDtypeStruct((nb, ns), F32)],
        scratch_shapes=[pltpu.VMEM((ks, blk, LANES), F32), pltpu.VMEM((ks, blk, LANES), F32),
                        pltpu.VMEM((blk, ns), F32), pltpu.VMEM((blk, ns), F32), pltpu.VMEM((nb, ns), F32),
                        pltpu.VMEM((nb, ns), F32)] + tmp,
        compiler_params=_params("arbitrary"),
        name="s5_scan",
    )(u, h0r, h0i, *params)


def _log2(n):
    assert n & (n - 1) == 0, n
    return n.bit_length() - 1


def _chunk_masks(rows, chunk):
    sh = _log2(chunk)
    rr = lax.broadcasted_iota(jnp.int32, (rows, rows), 0)
    cc = lax.broadcasted_iota(jnp.int32, (rows, rows), 1)
    same = lax.shift_right_logical(rr, sh) == lax.shift_right_logical(cc, sh)
    return same, same & (cc <= rr)


def _gla_prologue(p_ref, wa_ref, ba_ref, chunk, qk):
    rows = p_ref.shape[0]
    q = p_ref[:, 0:qk]
    k = p_ref[:, qk:2 * qk]
    alow = p_ref[:, p_ref.shape[1] - LANES:].astype(BF16)
    z = _dot(alow, wa_ref[...]) + ba_ref[...]
    la = (jnp.minimum(z, 0.0) - jnp.log(1.0 + jnp.exp(-jnp.abs(z)))) * (1.0 / GLA_TAU)
    same, causal = _chunk_masks(rows, chunk)
    tri = jnp.where(causal, 1.0, 0.0).astype(BF16)
    blk = jnp.where(same, 1.0, 0.0).astype(BF16)
    la_hi = la.astype(BF16)
    la_lo = (la - la_hi.astype(F32)).astype(BF16)
    b = _dot(tri, la_hi) + _dot(tri, la_lo)
    bl = _dot(blk, la_hi) + _dot(blk, la_lo)
    return q, k, b, bl


def _gla_out(o, gn, r):
    on = o * lax.rsqrt(jnp.mean(o * o, axis=-1, keepdims=True) + EPS) * gn
    return on * (r * jax.nn.sigmoid(r))


def _gla_prompt_init(st_s):
    @pl.when(pl.program_id(1) == 0)
    def _():
        st_s[...] = jnp.zeros_like(st_s)


def _gla_prompt_final(s_ref, st_s, *, heads, dk):
    @pl.when(pl.program_id(1) == pl.num_programs(1) - 1)
    def _():
        for h in range(heads):
            s_ref[0, h] = st_s[:, h * dk:(h + 1) * dk].T


def _gla_prompt_body(p_ref, wa_ref, ba_ref, gn_ref, y_ref, st_s, *, heads, dk, dv, chunk):
    rows = p_ref.shape[0]
    qk, vw = heads * dk, heads * dv
    q, k, b, bl = _gla_prologue(p_ref, wa_ref, ba_ref, chunk, qk)
    yield
    v_off, r_off = 2 * qk, 2 * qk + vw
    qd = (q * (dk ** -0.5) * jnp.exp(b)).astype(BF16)
    kd = (k * jnp.exp(-b)).astype(BF16)
    kst = (k * jnp.exp(bl - b)).astype(BF16)
    dec = jnp.exp(bl)
    gn = gn_ref[...]
    zero = jnp.zeros((), BF16)
    qk_head = lax.shift_right_logical(lax.broadcasted_iota(jnp.int32, (chunk, qk), 1), _log2(dk))
    v_head = lax.shift_right_logical(lax.broadcasted_iota(jnp.int32, (chunk, vw), 1), _log2(dv))
    rr = lax.broadcasted_iota(jnp.int32, (chunk, heads * chunk), 0)
    cc = lax.broadcasted_iota(jnp.int32, (chunk, heads * chunk), 1) & (chunk - 1)
    causal = cc <= rr
    nchunk = rows // chunk
    by_head = lambda x, head: jnp.concatenate([jnp.where(head == h, x, zero) for h in range(heads)], axis=0)
    att, ds_t, v_diag, qd_rows = [], [], [], []
    for ci in range(nchunk):
        rs = slice(ci * chunk, (ci + 1) * chunk)
        v_c = p_ref[rs, v_off:v_off + vw].astype(BF16)
        v_rows = jnp.concatenate([v_c[:, h * dv:(h + 1) * dv] for h in range(heads)], axis=0)
        v_diag.append(by_head(v_c, v_head))
        qd_rows.append(by_head(qd[rs], qk_head))
        att.append(lax.dot_general(qd[rs], by_head(kd[rs], qk_head), NT_DIMS, preferred_element_type=F32))
        ds_t.append(lax.dot_general(v_rows, by_head(kst[rs], qk_head), TN_DIMS, preferred_element_type=F32))
    yield
    o_intra = [_dot(jnp.where(causal, att[ci], 0.0).astype(BF16), v_diag[ci]) for ci in range(nchunk)]
    yield
    st = st_s[...]
    o_inter = []
    for ci in range(nchunk):
        o_inter.append(lax.dot_general(qd_rows[ci], st.astype(BF16), NT_DIMS, preferred_element_type=F32))
        st = dec[ci * chunk:ci * chunk + 1, :] * st + ds_t[ci]
    st_s[...] = st
    yield
    for ci in range(nchunk):
        rs = slice(ci * chunk, (ci + 1) * chunk)
        for h in range(heads):
            o = o_intra[ci][:, h * dv:(h + 1) * dv] + o_inter[ci][h * chunk:(h + 1) * chunk]
            r = p_ref[rs, r_off + h * dv:r_off + (h + 1) * dv]
            y_ref[rs, h * dv:(h + 1) * dv] = _gla_out(o, gn, r)


def _gla_sample_body(p_ref, s0_ref, wa_ref, ba_ref, gn_ref, y_ref, s_ref, *, heads, dk, dv, t):
    rows = p_ref.shape[0]
    qk = heads * dk
    q, k, b, bl = _gla_prologue(p_ref, wa_ref, ba_ref, t, qk)
    _, cmask = _chunk_masks(SUBLANES, t)
    v_off, r_off = 2 * qk, 2 * qk + heads * dv
    qd = (q * (dk ** -0.5) * jnp.exp(b)).astype(BF16)
    kd = (k * jnp.exp(-b)).astype(BF16)
    kst = k * jnp.exp(bl - b)
    dec = jnp.exp(bl)
    gn = gn_ref[...]
    per_tile = SUBLANES // t
    row_id = lax.broadcasted_iota(jnp.int32, (SUBLANES, 1), 0)
    att, o_state = {}, {}
    for ti in range(rows // SUBLANES):
        rs = slice(ti * SUBLANES, (ti + 1) * SUBLANES)
        dcols = []
        for j in range(per_tile):
            drow = dec[ti * SUBLANES + j * t:ti * SUBLANES + j * t + 1, :]
            dcols.append([jnp.broadcast_to(drow[:, g * LANES:(g + 1) * LANES], (LANES, LANES)).T
                          for g in range(qk // LANES)])
        for h in range(heads):
            hs = slice(h * dk, (h + 1) * dk)
            vh = p_ref[rs, v_off + h * dv:v_off + (h + 1) * dv].astype(BF16)
            att[ti, h] = lax.dot_general(qd[rs, hs], kd[rs, hs], NT_DIMS, preferred_element_type=F32)
            o = None
            for j in range(per_tile):
                bidx = ti * per_tile + j
                in_seq = (row_id >= j * t) & (row_id < (j + 1) * t)
                s0 = s0_ref[bidx, h]
                oj = jnp.where(in_seq, _dot(qd[rs, hs], s0.astype(BF16)), 0.0)
                o = oj if o is None else o + oj
                kj = jnp.where(in_seq, kst[rs, hs], 0.0).astype(BF16)
                ds = lax.dot_general(kj, vh, TN_DIMS, preferred_element_type=F32)
                g, lo = (h * dk) // LANES, (h * dk) % LANES
                dcol = dcols[j][g][lo:lo + dk, 0:dv]
                s_ref[bidx, h] = dcol * s0 + ds
            o_state[ti, h] = o
    yield
    for ti in range(rows // SUBLANES):
        rs = slice(ti * SUBLANES, (ti + 1) * SUBLANES)
        for h in range(heads):
            vh = p_ref[rs, v_off + h * dv:v_off + (h + 1) * dv].astype(BF16)
            o = _dot(jnp.where(cmask, att[ti, h], 0.0).astype(BF16), vh) + o_state[ti, h]
            r = p_ref[rs, r_off + h * dv:r_off + (h + 1) * dv]
            y_ref[rs, h * dv:(h + 1) * dv] = _gla_out(o, gn, r)


def _take_turns(bodies):
    while bodies:
        bodies = [b for b in bodies if next(b, StopIteration) is not StopIteration]


def _first_half(shape):
    lane = lax.broadcasted_iota(jnp.int32, shape, 1)
    return (lane & (SWA_HD - 1)) < SWA_HD // 2


def _rope_table_kernel(pos_ref, inv_ref, cos_ref, sin_ref):
    ang = pos_ref[...] * inv_ref[...]
    s = jnp.sin(ang)
    cos_ref[...] = jnp.cos(ang)
    sin_ref[...] = jnp.where(_first_half(ang.shape), -s, s)


def _rope_table(pos, inv):
    rows = pos.shape[0]
    return pl.pallas_call(
        _rope_table_kernel,
        out_shape=[jax.ShapeDtypeStruct((rows, LANES), F32)] * 2,
        name="rope_table",
    )(pos.astype(F32).reshape(rows, 1), inv)


def _rope(x, cos, sin):
    half = SWA_HD // 2
    first_half = _first_half(cos.shape)
    out = []
    for g in range(x.shape[1] // LANES):
        xg = x[:, g * LANES:(g + 1) * LANES]
        swapped = jnp.where(first_half, pltpu.roll(xg, LANES - half, 1), pltpu.roll(xg, half, 1))
        out.append(xg * cos + swapped * sin)
    return out[0] if len(out) == 1 else jnp.concatenate(out, axis=1)


def _swa_prompt_init(prev_s):
    @pl.when(pl.program_id(1) == 0)
    def _():
        prev_s[...] = jnp.zeros_like(prev_s)


def _swa_prompt_body(sink_ref, cur_ref, cos_ref, sin_ref, y_ref, nk_ref, nv_ref, prev_s, s_s, pc_s, *, heads, l):
    i = pl.program_id(1)
    w = WINDOW
    qw = heads * SWA_HD
    kvw = SWA_KV_HEADS * SWA_HD
    gq = heads // SWA_KV_HEADS
    nq = cur_ref.shape[0] // w
    cos, sin = cos_ref[...], sin_ref[...]
    q = (_rope(cur_ref[:, 0:qw], cos, sin) * (SWA_HD ** -0.5)).astype(BF16)
    k_cur = _rope(cur_ref[:, qw:qw + kvw], cos, sin)
    v_cur = cur_ref[:, qw + kvw:qw + 2 * kvw]
    new = [x.astype(BF16) for x in (k_cur, pltpu.roll(k_cur, SWA_HD, 1), v_cur, pltpu.roll(v_cur, SWA_HD, 1))]
    k2, k2r, v2, v2r = [jnp.concatenate([prev_s[j], new[j]], axis=0) for j in range(4)]
    zero = jnp.zeros((), BF16)
    lo2 = lax.broadcasted_iota(jnp.int32, ((nq + 1) * w, LANES), 1) < SWA_HD
    lo = lax.broadcasted_iota(jnp.int32, (w, LANES), 1) < SWA_HD
    k_both = (jnp.where(lo2, k2, k2r), jnp.where(lo2, k2r, k2))
    v_half = {(0, 0): jnp.where(lo2, v2, zero), (1, 1): jnp.where(lo2, zero, v2),
              (0, 1): jnp.where(lo2, zero, v2r), (1, 0): jnp.where(lo2, v2r, zero)}
    upper = lax.broadcasted_iota(jnp.int32, (w, w), 1) > lax.broadcasted_iota(jnp.int32, (w, w), 0)
    no_prev = jnp.where(i > 0, 0.0, -jnp.inf)
    slot = lambda c, h: slice((c * heads + h) * w, (c * heads + h + 1) * w)
    for c in range(nq):
        for kv in range(SWA_KV_HEADS):
            qm = []
            for h in range(kv * gq, (kv + 1) * gq):
                qg = q[c * w:(c + 1) * w, (h // 2) * LANES:(h // 2 + 1) * LANES]
                qm.append(jnp.where(lo, qg, zero) if h % 2 == 0 else jnp.where(lo, zero, qg))
            s_s[(c * heads + kv * gq) * w:(c * heads + (kv + 1) * gq) * w, :] = lax.dot_general(
                jnp.concatenate(qm, axis=0), k_both[kv][c * w:(c + 2) * w], NT_DIMS,
                preferred_element_type=F32)
    yield
    sink_term = {}
    for c in range(nq):
        for h in range(heads):
            s_prev = s_s[slot(c, h), 0:w] + no_prev if c == 0 else s_s[slot(c, h), 0:w]
            sp = jnp.where(upper, s_prev, s_s[slot(c, h), w:2 * w])
            sink = sink_ref[l, h]
            m = jnp.maximum(jnp.max(sp, axis=-1, keepdims=True), sink)
            p = jnp.exp(sp - m)
            pc_s[slot(c, h), 0:w] = jnp.where(upper, p, 0.0).astype(BF16)
            pc_s[slot(c, h), w:2 * w] = jnp.where(upper, 0.0, p).astype(BF16)
            sink_term[c, h] = jnp.exp(sink - m)
    yield
    num = {(c, h): _dot(pc_s[slot(c, h), :], v_half[(h // gq, h % 2)][c * w:(c + 2) * w])
           for c in range(nq) for h in range(heads)}
    den_all = _dot(pc_s[...], jnp.ones((2 * w, LANES), BF16))
    yield
    for c in range(nq):
        for g in range(heads // 2):
            he, ho = 2 * g, 2 * g + 1
            den = jnp.where(lo, den_all[slot(c, he)] + sink_term[c, he], den_all[slot(c, ho)] + sink_term[c, ho])
            y_ref[c * w:(c + 1) * w, g * LANES:(g + 1) * LANES] = (num[c, he] + num[c, ho]) * (1.0 / den)
    for j in range(4):
        prev_s[j] = new[j][(nq - 1) * w:]
    nk_ref[0] = k_cur[(nq - 1) * w:]
    nv_ref[0] = v_cur[(nq - 1) * w:]


def _gla_swa_prompt_kernel(pb_ref, wa_ref, ba_ref, gn_ref, sink_ref, pc_ref, cos_ref, sin_ref,
                           yb_ref, s_ref, yc_ref, nk_ref, nv_ref, st_s, prev_s, s_s, pc_s, *, gla, swa):
    _gla_prompt_init(st_s)
    _swa_prompt_init(prev_s)
    _take_turns([_gla_prompt_body(pb_ref, wa_ref, ba_ref, gn_ref, yb_ref, st_s, chunk=GLA_CHUNK, **gla),
                 _swa_prompt_body(sink_ref, pc_ref, cos_ref, sin_ref, yc_ref, nk_ref, nv_ref, prev_s, s_s, pc_s,
                                  **swa)])
    _gla_prompt_final(s_ref, st_s, heads=gla["heads"], dk=gla["dk"])


def _gla_swa_prompt(pb, pc, wa, ba, gn, sinks, cos, sin, l, bsz, gla_heads, dk, dv, swa_heads, rows):
    n, pbw = pb.shape
    pcw = pc.shape[1]
    nblk = n // bsz // rows
    nq = rows // WINDOW
    qw, kvw = swa_heads * SWA_HD, SWA_KV_HEADS * SWA_HD
    kern = functools.partial(_gla_swa_prompt_kernel, gla=dict(heads=gla_heads, dk=dk, dv=dv),
                             swa=dict(heads=swa_heads, l=l))
    row_blk = lambda wdt: pl.BlockSpec((rows, wdt), lambda b, i: (b * nblk + i, 0))
    tab = pl.BlockSpec((rows, LANES), lambda b, i: (i, 0))
    win = pl.BlockSpec((1, WINDOW, kvw), lambda b, i: (b, 0, 0))
    return pl.pallas_call(
        kern,
        grid=(bsz, nblk),
        in_specs=[row_blk(pbw), _layer(wa, l), _layer(ba, l), _layer(gn, l),
                  pl.BlockSpec(memory_space=pltpu.SMEM), row_blk(pcw), tab, tab],
        out_specs=[row_blk(gla_heads * dv), pl.BlockSpec((1, gla_heads, dk, dv), lambda b, i: (b, 0, 0, 0)),
                   row_blk(qw), win, win],
        out_shape=[jax.ShapeDtypeStruct((n, gla_heads * dv), F32),
                   jax.ShapeDtypeStruct((bsz, gla_heads, dk, dv), F32),
                   jax.ShapeDtypeStruct((n, qw), F32), jax.ShapeDtypeStruct((bsz, WINDOW, kvw), F32),
                   jax.ShapeDtypeStruct((bsz, WINDOW, kvw), F32)],
        scratch_shapes=[pltpu.VMEM((dv, gla_heads * dk), F32), pltpu.VMEM((4, WINDOW, LANES), BF16),
                        pltpu.VMEM((nq * swa_heads * WINDOW, 2 * WINDOW), F32),
                        pltpu.VMEM((nq * swa_heads * WINDOW, 2 * WINDOW), BF16)],
        compiler_params=_params("parallel", "arbitrary"),
        name="gla_swa_prompt",
    )(pb, wa, ba, gn, sinks, pc, cos, sin)


def _swa_sample_body(sinkcol_ref, p_ref, ck_ref, cv_ref, cos_ref, sin_ref, y_ref, nk_ref, nv_ref, *, heads, t, nb):
    w = ck_ref.shape[3]
    qw = heads * SWA_HD
    kvw = SWA_KV_HEADS * SWA_HD
    gq = heads // SWA_KV_HEADS
    per_tile = SUBLANES // t
    nrow = heads * SUBLANES
    cos, sin = cos_ref[...], sin_ref[...]
    lo = lax.broadcasted_iota(jnp.int32, (SUBLANES, LANES), 1) < SWA_HD
    row_r = lax.broadcasted_iota(jnp.int32, (nrow, 1), 0) & (SUBLANES - 1)
    row_seq = lax.shift_right_logical(row_r, _log2(t))
    row_step = row_r & (t - 1)
    cache_ok = lax.broadcasted_iota(jnp.int32, (nrow, w), 1) > row_step
    new_r = lax.broadcasted_iota(jnp.int32, (nrow, SUBLANES), 1)
    new_ok = (lax.shift_right_logical(new_r, _log2(t)) == row_seq) & ((new_r & (t - 1)) <= row_step)
    sink_col = sinkcol_ref[...]
    lane_w = lax.broadcasted_iota(jnp.int32, (SWA_HD, w), 1)
    flat = lambda c: jnp.concatenate([c[kv] for kv in range(SWA_KV_HEADS)], axis=0).astype(BF16)
    pad_rows = jnp.zeros((LANES - 2 * SUBLANES, kvw), F32)

    def on_kv_half(x, h, back=False):
        own, kv = h % 2, h // gq
        moved = x if own == kv else pltpu.roll(x, SWA_HD, 1)
        keep = own if back else kv
        return jnp.where(lo, moved, 0.0) if keep == 0 else jnp.where(lo, 0.0, moved)

    new_k, new_v, scores = [], [], []
    for ti in range(nb // per_tile):
        rs = slice(ti * SUBLANES, (ti + 1) * SUBLANES)
        q8 = _rope(p_ref[rs, 0:qw], cos, sin) * (SWA_HD ** -0.5)
        kn8 = _rope(p_ref[rs, qw:qw + kvw], cos, sin)
        new_k.append(kn8)
        new_v.append(p_ref[rs, qw + kvw:qw + 2 * kvw])
        qm = jnp.concatenate([on_kv_half(q8[:, (h // 2) * LANES:(h // 2 + 1) * LANES], h) for h in range(heads)],
                             axis=0).astype(BF16)
        s_c = None
        for j in range(per_tile):
            s_j = _dot(qm, flat(ck_ref[ti * per_tile + j]))
            s_c = s_j if s_c is None else jnp.where(row_seq == j, s_j, s_c)
        s_n = lax.dot_general(qm, kn8.astype(BF16), NT_DIMS, preferred_element_type=F32)
        scores.append((s_c, s_n))
    yield
    probs = []
    for s_c, s_n in scores:
        s_c = jnp.where(cache_ok, s_c, -jnp.inf)
        s_n = jnp.where(new_ok, s_n, -jnp.inf)
        m = jnp.maximum(jnp.maximum(jnp.max(s_c, axis=1, keepdims=True), jnp.max(s_n, axis=1, keepdims=True)),
                        sink_col)
        p_c = jnp.exp(s_c - m)
        p_n = jnp.exp(s_n - m)
        den = jnp.sum(p_c, axis=1, keepdims=True) + jnp.sum(p_n, axis=1, keepdims=True) + jnp.exp(sink_col - m)
        rden = 1.0 / den
        probs.append((p_c * rden, (p_n * rden).astype(BF16)))
    yield
    outs = []
    for ti, (p_c, p_n) in enumerate(probs):
        o = _dot(p_n, new_v[ti].astype(BF16))
        for j in range(per_tile):
            p_j = jnp.where(row_seq == j, p_c, 0.0).astype(BF16)
            o = o + lax.dot_general(p_j, flat(cv_ref[ti * per_tile + j]), NT_DIMS, preferred_element_type=F32)
        outs.append(o)
    yield
    for ti, o in enumerate(outs):
        for g in range(heads // 2):
            pair = [on_kv_half(o[h * SUBLANES:(h + 1) * SUBLANES], h, back=True) for h in (2 * g, 2 * g + 1)]
            y_ref[ti * SUBLANES:(ti + 1) * SUBLANES, g * LANES:(g + 1) * LANES] = pair[0] + pair[1]
    yield
    for ti in range(nb // per_tile):
        kv_t = jnp.concatenate([new_k[ti], new_v[ti], pad_rows], axis=0).T
        for j in range(per_tile):
            bidx = ti * per_tile + j
            for c_ref, n_ref, col0 in ((ck_ref, nk_ref, 0), (cv_ref, nv_ref, SUBLANES)):
                for kv in range(SWA_KV_HEADS):
                    tail = pltpu.roll(kv_t[kv * SWA_HD:(kv + 1) * SWA_HD, :], w - t - col0 - j * t, 1)
                    n_ref[bidx, kv] = jnp.where(lane_w < w - t, pltpu.roll(c_ref[bidx, kv], w - t, 1), tail)


def _into_layers(body, n_in, n_acc):
    kern = lambda *refs: _take_turns([body(*refs[:n_in], *refs[n_in + n_acc:])])
    return kern, [pl.BlockSpec(memory_space=pl.ANY)] * n_acc, {n_in + j: 1 + j for j in range(n_acc)}


def _gla_sample(p, s0, wa, ba, gn, l, t, nb, acc):
    n, pw = p.shape
    _, bsz, heads, dk, dv = s0.shape
    rows = nb * t
    body = functools.partial(_gla_sample_body, heads=heads, dk=dk, dv=dv, t=t)
    kern, acc_specs, aliases = _into_layers(body, 5, len(acc))
    state = pl.BlockSpec((None, nb, heads, dk, dv), lambda i: (l, i, 0, 0, 0))
    return pl.pallas_call(
        kern,
        grid=(bsz // nb,),
        in_specs=[pl.BlockSpec((rows, pw), lambda i: (i, 0)), state, _layer(wa, l), _layer(ba, l), _layer(gn, l)]
        + acc_specs,
        out_specs=[pl.BlockSpec((rows, heads * dv), lambda i: (i, 0)), state],
        out_shape=[jax.ShapeDtypeStruct((n, heads * dv), F32), jax.ShapeDtypeStruct(s0.shape, F32)],
        input_output_aliases=aliases,
        compiler_params=_params("parallel"),
        name="gla_sample",
    )(p, s0, wa, ba, gn, *acc)


def _swa_sample(p, ck, cv, sink_cols, cos, sin, l, heads, t, nb, acc):
    n, pw = p.shape
    _, bsz, kvh, hd, w = ck.shape
    qw = heads * SWA_HD
    body = functools.partial(_swa_sample_body, heads=heads, t=t, nb=nb)
    kern, acc_specs, aliases = _into_layers(body, 6, len(acc))
    cache = pl.BlockSpec((None, nb, kvh, hd, w), lambda i: (l, i, 0, 0, 0))
    return pl.pallas_call(
        kern,
        grid=(bsz // nb,),
        in_specs=[_layer(sink_cols, l), pl.BlockSpec((nb * t, pw), lambda i: (i, 0)), cache, cache,
                  _full(cos.shape), _full(sin.shape)] + acc_specs,
        out_specs=[pl.BlockSpec((nb * t, qw), lambda i: (i, 0)), cache, cache],
        out_shape=[jax.ShapeDtypeStruct((n, qw), F32), jax.ShapeDtypeStruct(ck.shape, F32),
                   jax.ShapeDtypeStruct(cv.shape, F32)],
        input_output_aliases=aliases,
        compiler_params=_params("parallel"),
        name="swa_sample",
    )(sink_cols, p, ck, cv, cos, sin, *acc)


FFN_HIDDEN_CHUNK = 1024


def _merge_ffn_kernel(x_ref, ya_ref, yb_ref, yc_ref, gt_ref, wb_ref, wo_ref, g_ref, wg_ref, wu_ref, wd_ref, gf_ref,
                      o_ref, *, final_norm):
    d = x_ref.shape[1]
    merged = None
    off = 0
    for i, y_ref in enumerate((ya_ref, yb_ref, yc_ref)):
        wdt = y_ref.shape[1]
        br = _dot(y_ref[...].astype(BF16), wb_ref[off:off + wdt, :])
        term = jax.nn.sigmoid(gt_ref[:, i * d:(i + 1) * d]) * br
        merged = term if merged is None else merged + term
        off += wdt
    x = x_ref[...] + _dot(merged.astype(BF16), wo_ref[...])
    hb = _rms(x, g_ref[...]).astype(BF16)
    hidden = wg_ref.shape[1]
    acc = x
    for c0 in range(0, hidden, FFN_HIDDEN_CHUNK):
        c1 = min(c0 + FFN_HIDDEN_CHUNK, hidden)
        gate = _dot(hb, wg_ref[:, c0:c1])
        up = _dot(hb, wu_ref[:, c0:c1])
        act = (gate * jax.nn.sigmoid(gate) * up).astype(BF16)
        acc = acc + _dot(act, wd_ref[c0:c1, :])
    o_ref[...] = _rms(acc, gf_ref[...]) if final_norm else acc


def _merge_ffn(x, ya, yb, yc, gates, wb, wo, g, wg, wu, wd, gf, l, tm, final_norm):
    n, d = x.shape
    row = lambda a: pl.BlockSpec((tm, a.shape[1]), lambda i: (i, 0))
    return pl.pallas_call(
        functools.partial(_merge_ffn_kernel, final_norm=final_norm),
        grid=(n // tm,),
        in_specs=[row(x), row(ya), row(yb), row(yc), row(gates), _layer(wb, l), _layer(wo, l), _layer(g, l),
                  _layer(wg, l), _layer(wu, l), _layer(wd, l), _full(gf.shape)],
        out_specs=pl.BlockSpec((tm, d), lambda i: (i, 0)),
        out_shape=jax.ShapeDtypeStruct((n, d), F32),
        compiler_params=_params("parallel"),
        name="merge_ffn",
    )(x, ya, yb, yc, gates, wb, wo, g, wg, wu, wd, gf)


def _tile(n, want):
    while n % want:
        want //= 2
    return want


def kernel(x_prompt, x_sample, state_s5_re, state_s5_im, state_gla, cache_swa_k, cache_swa_v, g_mix, w_in, s5_lambda_re, s5_lambda_im, s5_log_dt, s5_b_re, s5_b_im, s5_c_re, s5_c_im, s5_d, w_glu, b_glu, w_gla_a2, b_gla_a, g_gla_norm, swa_sinks, w_branch, w_out, g_ffn, w_ffn_gate, w_ffn_up, w_ffn_down, g_final):
    bp, tp, d = x_prompt.shape
    bs, ts, _ = x_sample.shape
    depth = w_in.shape[0]
    s5_w = s5_d.shape[1]
    groups, s5_n = s5_lambda_re.shape[1:]
    gla_qk = w_gla_a2.shape[2]
    gla_rank = w_gla_a2.shape[1]
    gla_heads, gla_dk, gla_dv = state_gla.shape[2:]
    gla_v = gla_heads * gla_dv
    swa_heads = swa_sinks.shape[1]
    swa_q = swa_heads * SWA_HD
    swa_kv = SWA_KV_HEADS * SWA_HD
    assert SWA_KV_HEADS * SWA_HD == LANES and bp % SUBLANES == 0 and bs % SUBLANES == 0 and SUBLANES % ts == 0
    assert tp % WINDOW == 0 and cache_swa_k.shape[2] == WINDOW

    o_a, o_b = 0, s5_w
    o_low = o_b + 2 * gla_qk + 2 * gla_v
    o_c = o_low + gla_rank
    o_g = o_c + swa_q + 2 * swa_kv
    widths = (s5_w, 2 * gla_qk + 2 * gla_v + LANES, swa_q + 2 * swa_kv, 3 * d)

    hp = x_prompt.reshape(bp * tp, d)
    hs = x_sample.reshape(bs * ts, d)
    inv = ROPE_THETA ** (-jnp.arange(SWA_HD // 2, dtype=F32) * (2.0 / SWA_HD))
    inv = jnp.tile(inv, 2 * LANES // SWA_HD).reshape(1, LANES)
    cos_p, sin_p = _rope_table(jnp.arange(tp, dtype=jnp.int32), inv)
    cos_s, sin_s = _rope_table(PAST_LEN + (jnp.arange(SUBLANES, dtype=jnp.int32) & (ts - 1)), inv)
    zeros_s5 = jnp.zeros((1, bp, groups * s5_n), F32)
    tm_p, tm_s = _tile(bp * tp, 512), _tile(bs * ts, 512)
    s5_steps = _tile(tp, 128)
    mix_rows = WINDOW * _tile(tp // WINDOW, 4)
    assert mix_rows % GLA_CHUNK == 0

    w_t = jnp.swapaxes(w_in, 1, 2).astype(BF16)
    wa2 = jnp.concatenate([w_gla_a2, jnp.zeros((depth, LANES - gla_rank, gla_qk), F32)], axis=1).astype(BF16)
    ba = b_gla_a.reshape(depth, 1, gla_qk)
    gn = g_gla_norm.reshape(depth, 1, gla_dv)
    ar, ai, bbr, bbi = _s5_prep(s5_lambda_re, s5_lambda_im, s5_log_dt, s5_b_re, s5_b_im)
    bre, bim, cre, cim = _s5_pack(bbr, bbi, s5_c_re, s5_c_im)
    s5_d_r, b_glu_r = s5_d.reshape(depth, 1, s5_w), b_glu.reshape(depth, 1, s5_w)
    h0r_s = state_s5_re.reshape(depth, bs, groups * s5_n)
    h0i_s = state_s5_im.reshape(depth, bs, groups * s5_n)
    g_mix_r, g_ffn_r, gf = g_mix.reshape(depth, 1, d), g_ffn.reshape(depth, 1, d), g_final.reshape(1, d)
    later = (w_glu, w_branch, w_out, w_ffn_gate, w_ffn_up, w_ffn_down)
    cast_rows = (bp * tp // tm_p) * 2 * SUBLANES
    cast_in_call = all((w.shape[0] * w.shape[1]) % cast_rows == 0 for w in later)
    casted = None if cast_in_call else [w.astype(BF16) for w in later]
    sink_cols = jnp.repeat(swa_sinks, SUBLANES, axis=1).reshape(depth, swa_heads * SUBLANES, 1)
    to_lanes = lambda c: jnp.transpose(c, (0, 1, 3, 4, 2))
    from_lanes = lambda c: jnp.transpose(c, (0, 1, 4, 2, 3))
    ck, cv = to_lanes(cache_swa_k), to_lanes(cache_swa_v)

    outs_p = [[] for _ in range(5)]
    outs_s = [[], []]
    gla_s, nk_s, nv_s = jnp.zeros_like(state_gla), jnp.zeros_like(ck), jnp.zeros_like(cv)
    for l in range(depth):
        last = l == depth - 1

        (pa, pb, pc, pg), new_casts = _inproj(hp, g_mix_r, w_t, o_c, l, widths, tm_p,
                                              later if casted is None else ())
        if casted is None:
            casted = new_casts
        wglu_b, wb, wo, wg, wu, wd = casted
        s5_params = (ar, ai, bre, bim, cre, cim, s5_d_r, wglu_b, b_glu_r)
        ya, hr, hi = _s5(pa.reshape(bp, tp, s5_w), zeros_s5, zeros_s5, 0, s5_params, l, s5_steps)
        ya = ya.reshape(bp * tp, s5_w)
        yb, sg, yc, nk, nv = _gla_swa_prompt(pb, pc, wa2, ba, gn, swa_sinks, cos_p, sin_p, l, bp, gla_heads, gla_dk,
                                             gla_dv, swa_heads, mix_rows)
        hp = _merge_ffn(hp, ya, yb, yc, pg, wb, wo, g_ffn_r, wg, wu, wd, gf, l, tm_p, last)
        for lst, val in zip(outs_p, (hr.reshape(bp, groups, s5_n), hi.reshape(bp, groups, s5_n), sg,
                                     nk.reshape(bp, WINDOW, SWA_KV_HEADS, SWA_HD),
                                     nv.reshape(bp, WINDOW, SWA_KV_HEADS, SWA_HD))):
            lst.append(val)

        (pa, pb, pc, pg), _ = _inproj(hs, g_mix_r, w_t, o_c, l, widths, tm_s)
        ya, hr, hi = _s5(pa, h0r_s, h0i_s, l, s5_params, l, ts)
        yb, gla_s = _gla_sample(pb, state_gla, wa2, ba, gn, l, ts, _tile(bs, 32), (gla_s,))
        yc, nk_s, nv_s = _swa_sample(pc, ck, cv, sink_cols, cos_s, sin_s, l, swa_heads, ts, _tile(bs, 16),
                                     (nk_s, nv_s))
        hs = _merge_ffn(hs, ya, yb, yc, pg, wb, wo, g_ffn_r, wg, wu, wd, gf, l, tm_s, last)
        for lst, val in zip(outs_s, (hr.reshape(bs, groups, s5_n), hi.reshape(bs, groups, s5_n))):
            lst.append(val)

    outs_s = [jnp.stack(o) for o in outs_s] + [gla_s, from_lanes(nk_s), from_lanes(nv_s)]
    return (hp.reshape(bp, tp, d), hs.reshape(bs, ts, d), *[jnp.stack(o) for o in outs_p], *outs_s)
```

```python
import functools

import jax
import jax.numpy as jnp
from jax import lax
from jax.experimental import pallas as pl
from jax.experimental.pallas import tpu as pltpu

F32 = jnp.float32
BF16 = jnp.bfloat16

EPS = 1e-6
GLA_TAU = 16.0
GLA_CHUNK = 64
SWA_KV_HEADS = 2
SWA_HD = 64
WINDOW = 128
ROPE_THETA = 10000.0
PAST_LEN = 16384

SUBLANES = 8
LANES = 128
VMEM_LIMIT = 56 * 1024 * 1024

NT_DIMS = (((1,), (1,)), ((), ()))
TN_DIMS = (((0,), (0,)), ((), ()))


def _params(*sem):
    return pltpu.CompilerParams(dimension_semantics=sem, vmem_limit_bytes=VMEM_LIMIT)


def _dot(a, b):
    return jnp.dot(a, b, preferred_element_type=F32)


def _rms(x, g):
    return x * lax.rsqrt(jnp.mean(x * x, axis=-1, keepdims=True) + EPS) * g


def _full(shape):
    nd = len(shape)
    return pl.BlockSpec(shape, lambda *_: (0,) * nd)


def _layer(a, l):
    nd = a.ndim - 1
    return pl.BlockSpec((None,) + a.shape[1:], lambda *_: (l,) + (0,) * nd, pipeline_mode=pl.Buffered(1))


def _inproj_kernel(x_ref, g_ref, wa_ref, wb_ref, *refs, n_cast):
    cast_in, o_refs, cast_out = refs[:n_cast], refs[n_cast:n_cast + 4], refs[n_cast + 4:]
    xb = _rms(x_ref[...], g_ref[...]).astype(BF16)
    for w_ref, group in ((wa_ref, o_refs[:2]), (wb_ref, o_refs[2:])):
        off = 0
        for o_ref in group:
            n = o_ref.shape[-1]
            for c0 in range(0, n, 512):
                c1 = min(c0 + 512, n)
                o_ref[:, c0:c1] = lax.dot_general(xb, w_ref[off + c0:off + c1, :], NT_DIMS,
                                                  preferred_element_type=F32)
            off += n
    for src, dst in zip(cast_in, cast_out):
        dst[...] = src[...].astype(BF16)


def _inproj(x, g, w, row_b, l, widths, tm, to_cast=()):
    n, d = x.shape
    steps = n // tm
    rows_a, rows_b = sum(widths[:2]), sum(widths[2:])
    assert row_b + rows_b == w.shape[1] and row_b % (2 * SUBLANES) == 0 and rows_a <= w.shape[1]
    w_rows = lambda start, size: pl.BlockSpec((None, pl.Element(size), pl.Element(d)), lambda i: (l, start, 0),
                                              pipeline_mode=pl.Buffered(1))
    flat = [a.reshape(-1, a.shape[-1]) for a in to_cast]
    assert all(f.shape[0] % (steps * 2 * SUBLANES) == 0 for f in flat)
    slice_spec = lambda f: pl.BlockSpec((f.shape[0] // steps, f.shape[1]), lambda i: (i, 0))
    outs = pl.pallas_call(
        functools.partial(_inproj_kernel, n_cast=len(flat)),
        grid=(steps,),
        in_specs=[pl.BlockSpec((tm, d), lambda i: (i, 0)), _layer(g, l), w_rows(0, rows_a), w_rows(row_b, rows_b)]
        + [slice_spec(f) for f in flat],
        out_specs=[pl.BlockSpec((tm, wd), lambda i: (i, 0)) for wd in widths] + [slice_spec(f) for f in flat],
        out_shape=[jax.ShapeDtypeStruct((n, wd), F32) for wd in widths]
        + [jax.ShapeDtypeStruct(f.shape, BF16) for f in flat],
        compiler_params=_params("parallel"),
        name="inproj",
    )(x, g, w, w, *flat)
    return outs[:4], [o.reshape(a.shape) for o, a in zip(outs[4:], to_cast)]


def _s5_prep_kernel(lr_ref, li_ref, ldt_ref, br_ref, bi_ref, ar_ref, ai_ref, bbr_ref, bbi_ref):
    lr, li = lr_ref[...], li_ref[...]
    dt = jnp.exp(ldt_ref[...])
    mag = jnp.exp(lr * dt)
    ar = mag * jnp.cos(li * dt)
    ai = mag * jnp.sin(li * dt)
    den = lr * lr + li * li
    zr, zi = ar - 1.0, ai
    er = (zr * lr + zi * li) / den
    ei = (zi * lr - zr * li) / den
    ar_ref[...] = ar
    ai_ref[...] = ai
    br, bi = br_ref[...], bi_ref[...]
    bbr_ref[...] = er[None] * br - ei[None] * bi
    bbi_ref[...] = er[None] * bi + ei[None] * br


def _s5_prep(lam_re, lam_im, log_dt, b_re, b_im):
    depth, g, c, n = b_re.shape
    dg = depth * g
    brt = jnp.transpose(b_re.reshape(dg, c, n), (1, 0, 2))
    bit = jnp.transpose(b_im.reshape(dg, c, n), (1, 0, 2))
    ar, ai, bbr, bbi = pl.pallas_call(
        _s5_prep_kernel,
        out_shape=[jax.ShapeDtypeStruct((dg, n), F32)] * 2 + [jax.ShapeDtypeStruct((c, dg, n), F32)] * 2,
        name="s5_prep",
    )(lam_re.reshape(dg, n), lam_im.reshape(dg, n), log_dt.reshape(dg, 1), brt, bit)
    unt = lambda a: jnp.transpose(a, (1, 0, 2)).reshape(depth, g, c, n)
    return ar.reshape(depth, 1, g * n), ai.reshape(depth, 1, g * n), unt(bbr), unt(bbi)


def _s5_pack(bbr, bbi, c_re, c_im):
    depth, g, c, n = bbr.shape
    gp = LANES // c
    ks = g // gp
    eye = jnp.eye(gp, dtype=F32)

    def pack_b(b):
        return jnp.einsum("lkgcn,gh->lkgchn", b.reshape(depth, ks, gp, c, n), eye).reshape(
            depth, ks, gp * c, gp * n).astype(BF16)

    def pack_c(cm):
        return jnp.einsum("lkgnc,gh->lkgnhc", cm.reshape(depth, ks, gp, n, c), eye).reshape(
            depth, ks, gp * n, gp * c).astype(BF16)

    return pack_b(bbr), pack_b(bbi), pack_c(c_re), pack_c(c_im)


S5_COL_BLOCK = 1024
S5_UNROLL = 8


def _s5_kernel(u_ref, h0r_ref, h0i_ref, ar_ref, ai_ref, bre_ref, bim_ref, cre_ref, cim_ref, d_ref, wglu_ref,
               bglu_ref, y_ref, hr_out, hi_out, us_s, ys_s, xr_s, xi_s, hr_s, hi_s, *tmp_s, nb, steps):
    c = pl.program_id(0)

    @pl.when(c == 0)
    def _():
        hr_s[...] = h0r_ref[...]
        hi_s[...] = h0i_ref[...]

    ks, kw, nw = bre_ref.shape
    by_sequence = u_ref.ndim == 3
    if by_sequence:
        for b in range(nb):
            for k in range(ks):
                us_s[k, pl.ds(b, steps, stride=nb), :] = u_ref[b, :, k * kw:(k + 1) * kw]
    else:
        for k in range(ks):
            tmp_s[0][k] = u_ref[:, k * kw:(k + 1) * kw]
        for t in range(steps):
            for k in range(ks):
                us_s[k, t * nb:(t + 1) * nb, :] = tmp_s[0][k, pl.ds(t, nb, stride=steps), :]

    def x_proj(k):
        uk = us_s[k].astype(BF16)
        xr_s[:, k * nw:(k + 1) * nw] = _dot(uk, bre_ref[k])
        xi_s[:, k * nw:(k + 1) * nw] = _dot(uk, bim_ref[k])

    def y_proj(k):
        ss = slice(k * nw, (k + 1) * nw)
        yk = _dot(xr_s[:, ss].astype(BF16), cre_ref[k]) - _dot(xi_s[:, ss].astype(BF16), cim_ref[k])
        ys_s[k] = jax.nn.gelu(yk + d_ref[:, k * kw:(k + 1) * kw] * us_s[k])

    def scan(rb, cs, unrolled):
        rows = slice(rb * SUBLANES, (rb + 1) * SUBLANES)
        width = cs.stop - cs.start
        a_r = jnp.broadcast_to(ar_ref[:, cs], (SUBLANES, width))
        a_i = jnp.broadcast_to(ai_ref[:, cs], (SUBLANES, width))

        def step(t, carry):
            hr, hi = carry
            r0 = t * nb + rb * SUBLANES
            tr = slice(r0, r0 + SUBLANES) if unrolled else pl.ds(pl.multiple_of(r0, SUBLANES), SUBLANES)
            nhr = a_r * hr - a_i * hi + xr_s[tr, cs]
            nhi = a_r * hi + a_i * hr + xi_s[tr, cs]
            xr_s[tr, cs] = nhr
            xi_s[tr, cs] = nhi
            return nhr, nhi

        carry = (hr_s[rows, cs], hi_s[rows, cs])
        if unrolled:
            for t in range(steps):
                carry = step(t, carry)
        else:
            carry = lax.fori_loop(0, steps, step, carry, unroll=min(steps, S5_UNROLL))
        hr_s[rows, cs], hi_s[rows, cs] = carry

    if by_sequence and nb == SUBLANES:
        x_proj(0)
        for k in range(ks):
            if k + 1 < ks:
                x_proj(k + 1)
            if k >= 1:
                y_proj(k - 1)
            scan(0, slice(k * nw, (k + 1) * nw), True)
        y_proj(ks - 1)
    else:
        for k in range(ks):
            x_proj(k)
        for rb in range(nb // SUBLANES):
            for cb in range(xr_s.shape[1] // S5_COL_BLOCK):
                scan(rb, slice(cb * S5_COL_BLOCK, (cb + 1) * S5_COL_BLOCK), False)
        for k in range(ks):
            y_proj(k)
    y = jnp.concatenate([ys_s[k] for k in range(ks)], axis=1)
    y = y * jax.nn.sigmoid(_dot(y.astype(BF16), wglu_ref[...]) + bglu_ref[...])
    for k in range(ks):
        ys_s[k] = y[:, k * kw:(k + 1) * kw]
    if by_sequence:
        for b in range(nb):
            for k in range(ks):
                y_ref[b, :, k * kw:(k + 1) * kw] = ys_s[k, pl.ds(b, steps, stride=nb), :]
    else:
        for t in range(steps):
            for k in range(ks):
                tmp_s[0][k, pl.ds(t, nb, stride=steps), :] = ys_s[k, t * nb:(t + 1) * nb, :]
        for k in range(ks):
            y_ref[:, k * kw:(k + 1) * kw] = tmp_s[0][k]

    @pl.when(c == pl.num_programs(0) - 1)
    def _():
        hr_out[...] = hr_s[...]
        hi_out[...] = hi_s[...]


def _s5(u, h0r, h0i, l0, params, l, steps):
    _, nb, ns = h0r.shape
    w = u.shape[-1]
    blk = steps * nb
    ks = params[2].shape[1]
    if u.ndim == 3:
        grid = (u.shape[1] // steps,)
        u_spec = pl.BlockSpec((nb, steps, w), lambda i: (0, i, 0))
        tmp = []
    else:
        assert u.shape[0] == blk
        grid = (1,)
        u_spec = _full(u.shape)
        tmp = [pltpu.VMEM((ks, blk, LANES), F32)]
    kern = functools.partial(_s5_kernel, nb=nb, steps=steps)
    return pl.pallas_call(
        kern,
        grid=grid,
        in_specs=[u_spec, _layer(h0r, l0), _layer(h0i, l0)] + [_layer(a, l) for a in params],
        out_specs=[u_spec, _full((nb, ns)), _full((nb, ns))],
        out_shape=[jax.ShapeDtypeStruct(u.shape, F32), jax.ShapeDtypeStruct((nb, ns), F32),
                   jax.ShapeDtypeStruct((nb, ns), F32)],
        scratch_shapes=[pltpu.VMEM((ks, blk, LANES), F32), pltpu.VMEM((ks, blk, LANES), F32),
                        pltpu.VMEM((blk, ns), F32), pltpu.VMEM((blk, ns), F32), pltpu.VMEM((nb, ns), F32),
                        pltpu.VMEM((nb, ns), F32)] + tmp,
        compiler_params=_params("arbitrary"),
        name="s5_scan",
    )(u, h0r, h0i, *params)


def _log2(n):
    assert n & (n - 1) == 0, n
    return n.bit_length() - 1


def _chunk_masks(rows, chunk):
    sh = _log2(chunk)
    rr = lax.broadcasted_iota(jnp.int32, (rows, rows), 0)
    cc = lax.broadcasted_iota(jnp.int32, (rows, rows), 1)
    same = lax.shift_right_logical(rr, sh) == lax.shift_right_logical(cc, sh)
    return same, same & (cc <= rr)


def _gla_prologue(p_ref, wa_ref, ba_ref, chunk, qk):
    rows = p_ref.shape[0]
    q = p_ref[:, 0:qk]
    k = p_ref[:, qk:2 * qk]
    alow = p_ref[:, p_ref.shape[1] - LANES:].astype(BF16)
    z = _dot(alow, wa_ref[...]) + ba_ref[...]
    la = (jnp.minimum(z, 0.0) - jnp.log(1.0 + jnp.exp(-jnp.abs(z)))) * (1.0 / GLA_TAU)
    same, causal = _chunk_masks(rows, chunk)
    tri = jnp.where(causal, 1.0, 0.0).astype(BF16)
    blk = jnp.where(same, 1.0, 0.0).astype(BF16)
    la_hi = la.astype(BF16)
    la_lo = (la - la_hi.astype(F32)).astype(BF16)
    b = _dot(tri, la_hi) + _dot(tri, la_lo)
    bl = _dot(blk, la_hi) + _dot(blk, la_lo)
    return q, k, b, bl


def _gla_out(o, gn, r):
    on = o * lax.rsqrt(jnp.mean(o * o, axis=-1, keepdims=True) + EPS) * gn
    return on * (r * jax.nn.sigmoid(r))


def _gla_prompt_init(st_s):
    @pl.when(pl.program_id(1) == 0)
    def _():
        st_s[...] = jnp.zeros_like(st_s)


def _gla_prompt_final(s_ref, st_s, *, heads, dk):
    @pl.when(pl.program_id(1) == pl.num_programs(1) - 1)
    def _():
        for h in range(heads):
            s_ref[0, h] = st_s[:, h * dk:(h + 1) * dk].T


def _gla_prompt_body(p_ref, wa_ref, ba_ref, gn_ref, y_ref, st_s, *, heads, dk, dv, chunk):
    rows = p_ref.shape[0]
    qk, vw = heads * dk, heads * dv
    q, k, b, bl = _gla_prologue(p_ref, wa_ref, ba_ref, chunk, qk)
    v_off, r_off = 2 * qk, 2 * qk + vw
    qd = (q * (dk ** -0.5) * jnp.exp(b)).astype(BF16)
    kd = (k * jnp.exp(-b)).astype(BF16)
    kst = (k * jnp.exp(bl - b)).astype(BF16)
    dec = jnp.exp(bl)
    gn = gn_ref[...]
    zero = jnp.zeros((), BF16)
    qk_head = lax.shift_right_logical(lax.broadcasted_iota(jnp.int32, (chunk, qk), 1), _log2(dk))
    v_head = lax.shift_right_logical(lax.broadcasted_iota(jnp.int32, (chunk, vw), 1), _log2(dv))
    rr = lax.broadcasted_iota(jnp.int32, (chunk, heads * chunk), 0)
    cc = lax.broadcasted_iota(jnp.int32, (chunk, heads * chunk), 1) & (chunk - 1)
    causal = cc <= rr
    nchunk = rows // chunk
    by_head = lambda x, head: jnp.concatenate([jnp.where(head == h, x, zero) for h in range(heads)], axis=0)
    att, ds_t, v_diag, qd_rows = [], [], [], []
    for ci in range(nchunk):
        rs = slice(ci * chunk, (ci + 1) * chunk)
        v_c = p_ref[rs, v_off:v_off + vw].astype(BF16)
        v_rows = jnp.concatenate([v_c[:, h * dv:(h + 1) * dv] for h in range(heads)], axis=0)
        v_diag.append(by_head(v_c, v_head))
        qd_rows.append(by_head(qd[rs], qk_head))
        att.append(lax.dot_general(qd[rs], by_head(kd[rs], qk_head), NT_DIMS, preferred_element_type=F32))
        ds_t.append(lax.dot_general(v_rows, by_head(kst[rs], qk_head), TN_DIMS, preferred_element_type=F32))
    yield
    o_intra = [_dot(jnp.where(causal, att[ci], 0.0).astype(BF16), v_diag[ci]) for ci in range(nchunk)]
    yield
    st = st_s[...]
    o_inter = []
    for ci in range(nchunk):
        o_inter.append(lax.dot_general(qd_rows[ci], st.astype(BF16), NT_DIMS, preferred_element_type=F32))
        st = dec[ci * chunk:ci * chunk + 1, :] * st + ds_t[ci]
    st_s[...] = st
    yield
    for ci in range(nchunk):
        rs = slice(ci * chunk, (ci + 1) * chunk)
        for h in range(heads):
            o = o_intra[ci][:, h * dv:(h + 1) * dv] + o_inter[ci][h * chunk:(h + 1) * chunk]
            r = p_ref[rs, r_off + h * dv:r_off + (h + 1) * dv]
            y_ref[rs, h * dv:(h + 1) * dv] = _gla_out(o, gn, r)


def _gla_sample_body(p_ref, s0_ref, wa_ref, ba_ref, gn_ref, y_ref, s_ref, *, heads, dk, dv, t):
    rows = p_ref.shape[0]
    qk = heads * dk
    q, k, b, bl = _gla_prologue(p_ref, wa_ref, ba_ref, t, qk)
    _, cmask = _chunk_masks(SUBLANES, t)
    v_off, r_off = 2 * qk, 2 * qk + heads * dv
    qd = (q * (dk ** -0.5) * jnp.exp(b)).astype(BF16)
    kd = (k * jnp.exp(-b)).astype(BF16)
    kst = k * jnp.exp(bl - b)
    dec = jnp.exp(bl)
    gn = gn_ref[...]
    per_tile = SUBLANES // t
    row_id = lax.broadcasted_iota(jnp.int32, (SUBLANES, 1), 0)
    att, o_state = {}, {}
    for ti in range(rows // SUBLANES):
        rs = slice(ti * SUBLANES, (ti + 1) * SUBLANES)
        dcols = []
        for j in range(per_tile):
            drow = dec[ti * SUBLANES + j * t:ti * SUBLANES + j * t + 1, :]
            dcols.append([jnp.broadcast_to(drow[:, g * LANES:(g + 1) * LANES], (LANES, LANES)).T
                          for g in range(qk // LANES)])
        for h in range(heads):
            hs = slice(h * dk, (h + 1) * dk)
            vh = p_ref[rs, v_off + h * dv:v_off + (h + 1) * dv].astype(BF16)
            att[ti, h] = lax.dot_general(qd[rs, hs], kd[rs, hs], NT_DIMS, preferred_element_type=F32)
            o = None
            for j in range(per_tile):
                bidx = ti * per_tile + j
                in_seq = (row_id >= j * t) & (row_id < (j + 1) * t)
                s0 = s0_ref[bidx, h]
                oj = jnp.where(in_seq, _dot(qd[rs, hs], s0.astype(BF16)), 0.0)
                o = oj if o is None else o + oj
                kj = jnp.where(in_seq, kst[rs, hs], 0.0).astype(BF16)
                ds = lax.dot_general(kj, vh, TN_DIMS, preferred_element_type=F32)
                g, lo = (h * dk) // LANES, (h * dk) % LANES
                dcol = dcols[j][g][lo:lo + dk, 0:dv]
                s_ref[bidx, h] = dcol * s0 + ds
            o_state[ti, h] = o
    yield
    for ti in range(rows // SUBLANES):
        rs = slice(ti * SUBLANES, (ti + 1) * SUBLANES)
        for h in range(heads):
            vh = p_ref[rs, v_off + h * dv:v_off + (h + 1) * dv].astype(BF16)
            o = _dot(jnp.where(cmask, att[ti, h], 0.0).astype(BF16), vh) + o_state[ti, h]
            r = p_ref[rs, r_off + h * dv:r_off + (h + 1) * dv]
            y_ref[rs, h * dv:(h + 1) * dv] = _gla_out(o, gn, r)


def _take_turns(bodies):
    while bodies:
        bodies = [b for b in bodies if next(b, StopIteration) is not StopIteration]


def _first_half(shape):
    lane = lax.broadcasted_iota(jnp.int32, shape, 1)
    return (lane & (SWA_HD - 1)) < SWA_HD // 2


def _rope_table_kernel(pos_ref, inv_ref, cos_ref, sin_ref):
    ang = pos_ref[...] * inv_ref[...]
    s = jnp.sin(ang)
    cos_ref[...] = jnp.cos(ang)
    sin_ref[...] = jnp.where(_first_half(ang.shape), -s, s)


def _rope_table(pos, inv):
    rows = pos.shape[0]
    return pl.pallas_call(
        _rope_table_kernel,
        out_shape=[jax.ShapeDtypeStruct((rows, LANES), F32)] * 2,
        name="rope_table",
    )(pos.astype(F32).reshape(rows, 1), inv)


def _rope(x, cos, sin):
    half = SWA_HD // 2
    first_half = _first_half(cos.shape)
    out = []
    for g in range(x.shape[1] // LANES):
        xg = x[:, g * LANES:(g + 1) * LANES]
        swapped = jnp.where(first_half, pltpu.roll(xg, LANES - half, 1), pltpu.roll(xg, half, 1))
        out.append(xg * cos + swapped * sin)
    return out[0] if len(out) == 1 else jnp.concatenate(out, axis=1)


def _swa_prompt_init(prev_s):
    @pl.when(pl.program_id(1) == 0)
    def _():
        prev_s[...] = jnp.zeros_like(prev_s)


def _swa_prompt_body(sink_ref, cur_ref, cos_ref, sin_ref, y_ref, nk_ref, nv_ref, prev_s, s_s, pc_s, *, heads, l):
    i = pl.program_id(1)
    w = WINDOW
    qw = heads * SWA_HD
    kvw = SWA_KV_HEADS * SWA_HD
    gq = heads // SWA_KV_HEADS
    nq = cur_ref.shape[0] // w
    cos, sin = cos_ref[...], sin_ref[...]
    q = (_rope(cur_ref[:, 0:qw], cos, sin) * (SWA_HD ** -0.5)).astype(BF16)
    k_cur = _rope(cur_ref[:, qw:qw + kvw], cos, sin)
    v_cur = cur_ref[:, qw + kvw:qw + 2 * kvw]
    new = [x.astype(BF16) for x in (k_cur, pltpu.roll(k_cur, SWA_HD, 1), v_cur, pltpu.roll(v_cur, SWA_HD, 1))]
    k2, k2r, v2, v2r = [jnp.concatenate([prev_s[j], new[j]], axis=0) for j in range(4)]
    zero = jnp.zeros((), BF16)
    lo2 = lax.broadcasted_iota(jnp.int32, ((nq + 1) * w, LANES), 1) < SWA_HD
    lo = lax.broadcasted_iota(jnp.int32, (w, LANES), 1) < SWA_HD
    k_both = (jnp.where(lo2, k2, k2r), jnp.where(lo2, k2r, k2))
    v_half = {(0, 0): jnp.where(lo2, v2, zero), (1, 1): jnp.where(lo2, zero, v2),
              (0, 1): jnp.where(lo2, zero, v2r), (1, 0): jnp.where(lo2, v2r, zero)}
    upper = lax.broadcasted_iota(jnp.int32, (w, w), 1) > lax.broadcasted_iota(jnp.int32, (w, w), 0)
    no_prev = jnp.where(i > 0, 0.0, -jnp.inf)
    slot = lambda c, h: slice((c * heads + h) * w, (c * heads + h + 1) * w)
    for c in range(nq):
        for kv in range(SWA_KV_HEADS):
            qm = []
            for h in range(kv * gq, (kv + 1) * gq):
                qg = q[c * w:(c + 1) * w, (h // 2) * LANES:(h // 2 + 1) * LANES]
                qm.append(jnp.where(lo, qg, zero) if h % 2 == 0 else jnp.where(lo, zero, qg))
            s_s[(c * heads + kv * gq) * w:(c * heads + (kv + 1) * gq) * w, :] = lax.dot_general(
                jnp.concatenate(qm, axis=0), k_both[kv][c * w:(c + 2) * w], NT_DIMS,
                preferred_element_type=F32)
    yield
    sink_term = {}
    for c in range(nq):
        for h in range(heads):
            s_prev = s_s[slot(c, h), 0:w] + no_prev if c == 0 else s_s[slot(c, h), 0:w]
            sp = jnp.where(upper, s_prev, s_s[slot(c, h), w:2 * w])
            sink = sink_ref[l, h]
            m = jnp.maximum(jnp.max(sp, axis=-1, keepdims=True), sink)
            p = jnp.exp(sp - m)
            pc_s[slot(c, h), 0:w] = jnp.where(upper, p, 0.0).astype(BF16)
            pc_s[slot(c, h), w:2 * w] = jnp.where(upper, 0.0, p).astype(BF16)
            sink_term[c, h] = jnp.exp(sink - m)
    yield
    num = {(c, h): _dot(pc_s[slot(c, h), :], v_half[(h // gq, h % 2)][c * w:(c + 2) * w])
           for c in range(nq) for h in range(heads)}
    den_all = _dot(pc_s[...], jnp.ones((2 * w, LANES), BF16))
    yield
    for c in range(nq):
        for g in range(heads // 2):
            he, ho = 2 * g, 2 * g + 1
            den = jnp.where(lo, den_all[slot(c, he)] + sink_term[c, he], den_all[slot(c, ho)] + sink_term[c, ho])
            y_ref[c * w:(c + 1) * w, g * LANES:(g + 1) * LANES] = (num[c, he] + num[c, ho]) * (1.0 / den)
    for j in range(4):
        prev_s[j] = new[j][(nq - 1) * w:]
    nk_ref[0] = k_cur[(nq - 1) * w:]
    nv_ref[0] = v_cur[(nq - 1) * w:]


def _gla_swa_prompt_kernel(pb_ref, wa_ref, ba_ref, gn_ref, sink_ref, pc_ref, cos_ref, sin_ref,
                           yb_ref, s_ref, yc_ref, nk_ref, nv_ref, st_s, prev_s, s_s, pc_s, *, gla, swa):
    _gla_prompt_init(st_s)
    _swa_prompt_init(prev_s)
    _take_turns([_gla_prompt_body(pb_ref, wa_ref, ba_ref, gn_ref, yb_ref, st_s, chunk=GLA_CHUNK, **gla),
                 _swa_prompt_body(sink_ref, pc_ref, cos_ref, sin_ref, yc_ref, nk_ref, nv_ref, prev_s, s_s, pc_s,
                                  **swa)])
    _gla_prompt_final(s_ref, st_s, heads=gla["heads"], dk=gla["dk"])


def _gla_swa_prompt(pb, pc, wa, ba, gn, sinks, cos, sin, l, bsz, gla_heads, dk, dv, swa_heads, rows):
    n, pbw = pb.shape
    pcw = pc.shape[1]
    nblk = n // bsz // rows
    nq = rows // WINDOW
    qw, kvw = swa_heads * SWA_HD, SWA_KV_HEADS * SWA_HD
    kern = functools.partial(_gla_swa_prompt_kernel, gla=dict(heads=gla_heads, dk=dk, dv=dv),
                             swa=dict(heads=swa_heads, l=l))
    row_blk = lambda wdt: pl.BlockSpec((rows, wdt), lambda b, i: (b * nblk + i, 0))
    tab = pl.BlockSpec((rows, LANES), lambda b, i: (i, 0))
    win = pl.BlockSpec((1, WINDOW, kvw), lambda b, i: (b, 0, 0))
    return pl.pallas_call(
        kern,
        grid=(bsz, nblk),
        in_specs=[row_blk(pbw), _layer(wa, l), _layer(ba, l), _layer(gn, l),
                  pl.BlockSpec(memory_space=pltpu.SMEM), row_blk(pcw), tab, tab],
        out_specs=[row_blk(gla_heads * dv), pl.BlockSpec((1, gla_heads, dk, dv), lambda b, i: (b, 0, 0, 0)),
                   row_blk(qw), win, win],
        out_shape=[jax.ShapeDtypeStruct((n, gla_heads * dv), F32),
                   jax.ShapeDtypeStruct((bsz, gla_heads, dk, dv), F32),
                   jax.ShapeDtypeStruct((n, qw), F32), jax.ShapeDtypeStruct((bsz, WINDOW, kvw), F32),
                   jax.ShapeDtypeStruct((bsz, WINDOW, kvw), F32)],
        scratch_shapes=[pltpu.VMEM((dv, gla_heads * dk), F32), pltpu.VMEM((4, WINDOW, LANES), BF16),
                        pltpu.VMEM((nq * swa_heads * WINDOW, 2 * WINDOW), F32),
                        pltpu.VMEM((nq * swa_heads * WINDOW, 2 * WINDOW), BF16)],
        compiler_params=_params("parallel", "arbitrary"),
        name="gla_swa_prompt",
    )(pb, wa, ba, gn, sinks, pc, cos, sin)


def _swa_sample_body(sinkcol_ref, p_ref, ck_ref, cv_ref, cos_ref, sin_ref, y_ref, nk_ref, nv_ref, *, heads, t, nb):
    w = ck_ref.shape[3]
    qw = heads * SWA_HD
    kvw = SWA_KV_HEADS * SWA_HD
    gq = heads // SWA_KV_HEADS
    per_tile = SUBLANES // t
    nrow = heads * SUBLANES
    cos, sin = cos_ref[...], sin_ref[...]
    lo = lax.broadcasted_iota(jnp.int32, (SUBLANES, LANES), 1) < SWA_HD
    row_r = lax.broadcasted_iota(jnp.int32, (nrow, 1), 0) & (SUBLANES - 1)
    row_seq = lax.shift_right_logical(row_r, _log2(t))
    row_step = row_r & (t - 1)
    cache_ok = lax.broadcasted_iota(jnp.int32, (nrow, w), 1) > row_step
    new_r = lax.broadcasted_iota(jnp.int32, (nrow, SUBLANES), 1)
    new_ok = (lax.shift_right_logical(new_r, _log2(t)) == row_seq) & ((new_r & (t - 1)) <= row_step)
    sink_col = sinkcol_ref[...]
    lane_w = lax.broadcasted_iota(jnp.int32, (SWA_HD, w), 1)
    flat = lambda c: jnp.concatenate([c[kv] for kv in range(SWA_KV_HEADS)], axis=0).astype(BF16)
    pad_rows = jnp.zeros((LANES - 2 * SUBLANES, kvw), F32)

    def on_kv_half(x, h, back=False):
        own, kv = h % 2, h // gq
        moved = x if own == kv else pltpu.roll(x, SWA_HD, 1)
        keep = own if back else kv
        return jnp.where(lo, moved, 0.0) if keep == 0 else jnp.where(lo, 0.0, moved)

    new_k, new_v, scores = [], [], []
    for ti in range(nb // per_tile):
        rs = slice(ti * SUBLANES, (ti + 1) * SUBLANES)
        q8 = _rope(p_ref[rs, 0:qw], cos, sin) * (SWA_HD ** -0.5)
        kn8 = _rope(p_ref[rs, qw:qw + kvw], cos, sin)
        new_k.append(kn8)
        new_v.append(p_ref[rs, qw + kvw:qw + 2 * kvw])
        qm = jnp.concatenate([on_kv_half(q8[:, (h // 2) * LANES:(h // 2 + 1) * LANES], h) for h in range(heads)],
                             axis=0).astype(BF16)
        s_c = None
        for j in range(per_tile):
            s_j = _dot(qm, flat(ck_ref[ti * per_tile + j]))
            s_c = s_j if s_c is None else jnp.where(row_seq == j, s_j, s_c)
        s_n = lax.dot_general(qm, kn8.astype(BF16), NT_DIMS, preferred_element_type=F32)
        scores.append((s_c, s_n))
    yield
    probs = []
    for s_c, s_n in scores:
        s_c = jnp.where(cache_ok, s_c, -jnp.inf)
        s_n = jnp.where(new_ok, s_n, -jnp.inf)
        m = jnp.maximum(jnp.maximum(jnp.max(s_c, axis=1, keepdims=True), jnp.max(s_n, axis=1, keepdims=True)),
                        sink_col)
        p_c = jnp.exp(s_c - m)
        p_n = jnp.exp(s_n - m)
        den = jnp.sum(p_c, axis=1, keepdims=True) + jnp.sum(p_n, axis=1, keepdims=True) + jnp.exp(sink_col - m)
        rden = 1.0 / den
        probs.append((p_c * rden, (p_n * rden).astype(BF16)))
    yield
    outs = []
    for ti, (p_c, p_n) in enumerate(probs):
        o = _dot(p_n, new_v[ti].astype(BF16))
        for j in range(per_tile):
            p_j = jnp.where(row_seq == j, p_c, 0.0).astype(BF16)
            o = o + lax.dot_general(p_j, flat(cv_ref[ti * per_tile + j]), NT_DIMS, preferred_element_type=F32)
        outs.append(o)
    yield
    for ti, o in enumerate(outs):
        for g in range(heads // 2):
            pair = [on_kv_half(o[h * SUBLANES:(h + 1) * SUBLANES], h, back=True) for h in (2 * g, 2 * g + 1)]
            y_ref[ti * SUBLANES:(ti + 1) * SUBLANES, g * LANES:(g + 1) * LANES] = pair[0] + pair[1]
    yield
    for ti in range(nb // per_tile):
        kv_t = jnp.concatenate([new_k[ti], new_v[ti], pad_rows], axis=0).T
        for j in range(per_tile):
            bidx = ti * per_tile + j
            for c_ref, n_ref, col0 in ((ck_ref, nk_ref, 0), (cv_ref, nv_ref, SUBLANES)):
                for kv in range(SWA_KV_HEADS):
                    tail = pltpu.roll(kv_t[kv * SWA_HD:(kv + 1) * SWA_HD, :], w - t - col0 - j * t, 1)
                    n_ref[bidx, kv] = jnp.where(lane_w < w - t, pltpu.roll(c_ref[bidx, kv], w - t, 1), tail)


def _layered_outputs(body, n_in, blocks, prev):
    n_prev = len(prev[0]) if prev else 0
    if not n_prev:
        kern = lambda *refs: _take_turns([body(*refs)])
        return (kern, [], [], [pl.BlockSpec(b, m) for b, m, _ in blocks],
                [jax.ShapeDtypeStruct(s, F32) for _, _, s in blocks])

    def kern(*refs):
        ins, earlier, outs = refs[:n_in], refs[n_in:n_in + len(blocks) * n_prev], refs[n_in + len(blocks) * n_prev:]
        stacked = outs[1:1 + len(blocks)]
        _take_turns([body(*ins, outs[0], *[s.at[n_prev] for s in stacked])])
        for k, s in enumerate(stacked):
            for ll in range(n_prev):
                s[ll] = earlier[k * n_prev + ll][...]

    in_specs = [pl.BlockSpec(b, m) for b, m, _ in blocks for _ in range(n_prev)]
    operands = [a for arrays in prev for a in arrays]
    out_specs = [pl.BlockSpec((n_prev + 1,) + b, lambda i, m=m: (0,) + m(i)) for b, m, _ in blocks]
    out_shapes = [jax.ShapeDtypeStruct((n_prev + 1,) + s, F32) for _, _, s in blocks]
    return kern, in_specs, operands, out_specs, out_shapes


def _gla_sample(p, s0, wa, ba, gn, l, t, nb, prev):
    n, pw = p.shape
    _, bsz, heads, dk, dv = s0.shape
    rows = nb * t
    body = functools.partial(_gla_sample_body, heads=heads, dk=dk, dv=dv, t=t)
    blocks = [((nb, heads, dk, dv), lambda i: (i, 0, 0, 0), s0.shape[1:])]
    kern, more_specs, more, st_specs, st_shapes = _layered_outputs(body, 5, blocks, prev)
    return pl.pallas_call(
        kern,
        grid=(bsz // nb,),
        in_specs=[pl.BlockSpec((rows, pw), lambda i: (i, 0)),
                  pl.BlockSpec((None, nb, heads, dk, dv), lambda i: (l, i, 0, 0, 0)),
                  _layer(wa, l), _layer(ba, l), _layer(gn, l)] + more_specs,
        out_specs=[pl.BlockSpec((rows, heads * dv), lambda i: (i, 0))] + st_specs,
        out_shape=[jax.ShapeDtypeStruct((n, heads * dv), F32)] + st_shapes,
        compiler_params=_params("parallel"),
        name="gla_sample",
    )(p, s0, wa, ba, gn, *more)


def _swa_sample(p, ck, cv, sink_cols, cos, sin, l, heads, t, nb, prev):
    n, pw = p.shape
    _, bsz, kvh, hd, w = ck.shape
    qw = heads * SWA_HD
    body = functools.partial(_swa_sample_body, heads=heads, t=t, nb=nb)
    blocks = [((nb, kvh, hd, w), lambda i: (i, 0, 0, 0), ck.shape[1:])] * 2
    kern, more_specs, more, c_specs, c_shapes = _layered_outputs(body, 6, blocks, prev)
    cache = pl.BlockSpec((None, nb, kvh, hd, w), lambda i: (l, i, 0, 0, 0))
    return pl.pallas_call(
        kern,
        grid=(bsz // nb,),
        in_specs=[_layer(sink_cols, l), pl.BlockSpec((nb * t, pw), lambda i: (i, 0)), cache, cache,
                  _full(cos.shape), _full(sin.shape)] + more_specs,
        out_specs=[pl.BlockSpec((nb * t, qw), lambda i: (i, 0))] + c_specs,
        out_shape=[jax.ShapeDtypeStruct((n, qw), F32)] + c_shapes,
        compiler_params=_params("parallel"),
        name="swa_sample",
    )(sink_cols, p, ck, cv, cos, sin, *more)


FFN_HIDDEN_CHUNK = 1024


def _merge_ffn_kernel(x_ref, ya_ref, yb_ref, yc_ref, gt_ref, wb_ref, wo_ref, g_ref, wg_ref, wu_ref, wd_ref, gf_ref,
                      o_ref, *, final_norm):
    d = x_ref.shape[1]
    merged = None
    off = 0
    for i, y_ref in enumerate((ya_ref, yb_ref, yc_ref)):
        wdt = y_ref.shape[1]
        br = _dot(y_ref[...].astype(BF16), wb_ref[off:off + wdt, :])
        term = jax.nn.sigmoid(gt_ref[:, i * d:(i + 1) * d]) * br
        merged = term if merged is None else merged + term
        off += wdt
    x = x_ref[...] + _dot(merged.astype(BF16), wo_ref[...])
    hb = _rms(x, g_ref[...]).astype(BF16)
    hidden = wg_ref.shape[1]
    acc = x
    for c0 in range(0, hidden, FFN_HIDDEN_CHUNK):
        c1 = min(c0 + FFN_HIDDEN_CHUNK, hidden)
        gate = _dot(hb, wg_ref[:, c0:c1])
        up = _dot(hb, wu_ref[:, c0:c1])
        act = (gate * jax.nn.sigmoid(gate) * up).astype(BF16)
        acc = acc + _dot(act, wd_ref[c0:c1, :])
    o_ref[...] = _rms(acc, gf_ref[...]) if final_norm else acc


def _merge_ffn(x, ya, yb, yc, gates, wb, wo, g, wg, wu, wd, gf, l, tm, final_norm):
    n, d = x.shape
    row = lambda a: pl.BlockSpec((tm, a.shape[1]), lambda i: (i, 0))
    return pl.pallas_call(
        functools.partial(_merge_ffn_kernel, final_norm=final_norm),
        grid=(n // tm,),
        in_specs=[row(x), row(ya), row(yb), row(yc), row(gates), _layer(wb, l), _layer(wo, l), _layer(g, l),
                  _layer(wg, l), _layer(wu, l), _layer(wd, l), _full(gf.shape)],
        out_specs=pl.BlockSpec((tm, d), lambda i: (i, 0)),
        out_shape=jax.ShapeDtypeStruct((n, d), F32),
        compiler_params=_params("parallel"),
        name="merge_ffn",
    )(x, ya, yb, yc, gates, wb, wo, g, wg, wu, wd, gf)


def _tile(n, want):
    while n % want:
        want //= 2
    return want


def kernel(x_prompt, x_sample, state_s5_re, state_s5_im, state_gla, cache_swa_k, cache_swa_v, g_mix, w_in, s5_lambda_re, s5_lambda_im, s5_log_dt, s5_b_re, s5_b_im, s5_c_re, s5_c_im, s5_d, w_glu, b_glu, w_gla_a2, b_gla_a, g_gla_norm, swa_sinks, w_branch, w_out, g_ffn, w_ffn_gate, w_ffn_up, w_ffn_down, g_final):
    bp, tp, d = x_prompt.shape
    bs, ts, _ = x_sample.shape
    depth = w_in.shape[0]
    s5_w = s5_d.shape[1]
    groups, s5_n = s5_lambda_re.shape[1:]
    gla_qk = w_gla_a2.shape[2]
    gla_rank = w_gla_a2.shape[1]
    gla_heads, gla_dk, gla_dv = state_gla.shape[2:]
    gla_v = gla_heads * gla_dv
    swa_heads = swa_sinks.shape[1]
    swa_q = swa_heads * SWA_HD
    swa_kv = SWA_KV_HEADS * SWA_HD
    assert SWA_KV_HEADS * SWA_HD == LANES and bp % SUBLANES == 0 and bs % SUBLANES == 0 and SUBLANES % ts == 0
    assert tp % WINDOW == 0 and cache_swa_k.shape[2] == WINDOW

    o_a, o_b = 0, s5_w
    o_low = o_b + 2 * gla_qk + 2 * gla_v
    o_c = o_low + gla_rank
    o_g = o_c + swa_q + 2 * swa_kv
    widths = (s5_w, 2 * gla_qk + 2 * gla_v + LANES, swa_q + 2 * swa_kv, 3 * d)

    hp = x_prompt.reshape(bp * tp, d)
    hs = x_sample.reshape(bs * ts, d)
    inv = ROPE_THETA ** (-jnp.arange(SWA_HD // 2, dtype=F32) * (2.0 / SWA_HD))
    inv = jnp.tile(inv, 2 * LANES // SWA_HD).reshape(1, LANES)
    cos_p, sin_p = _rope_table(jnp.arange(tp, dtype=jnp.int32), inv)
    cos_s, sin_s = _rope_table(PAST_LEN + (jnp.arange(SUBLANES, dtype=jnp.int32) & (ts - 1)), inv)
    zeros_s5 = jnp.zeros((1, bp, groups * s5_n), F32)
    tm_p, tm_s = _tile(bp * tp, 512), _tile(bs * ts, 512)
    s5_steps = _tile(tp, 128)
    mix_rows = WINDOW * _tile(tp // WINDOW, 4)
    assert mix_rows % GLA_CHUNK == 0

    w_t = jnp.swapaxes(w_in, 1, 2).astype(BF16)
    wa2 = jnp.concatenate([w_gla_a2, jnp.zeros((depth, LANES - gla_rank, gla_qk), F32)], axis=1).astype(BF16)
    ba = b_gla_a.reshape(depth, 1, gla_qk)
    gn = g_gla_norm.reshape(depth, 1, gla_dv)
    ar, ai, bbr, bbi = _s5_prep(s5_lambda_re, s5_lambda_im, s5_log_dt, s5_b_re, s5_b_im)
    bre, bim, cre, cim = _s5_pack(bbr, bbi, s5_c_re, s5_c_im)
    s5_d_r, b_glu_r = s5_d.reshape(depth, 1, s5_w), b_glu.reshape(depth, 1, s5_w)
    h0r_s = state_s5_re.reshape(depth, bs, groups * s5_n)
    h0i_s = state_s5_im.reshape(depth, bs, groups * s5_n)
    g_mix_r, g_ffn_r, gf = g_mix.reshape(depth, 1, d), g_ffn.reshape(depth, 1, d), g_final.reshape(1, d)
    later = (w_glu, w_branch, w_out, w_ffn_gate, w_ffn_up, w_ffn_down)
    cast_rows = (bp * tp // tm_p) * 2 * SUBLANES
    cast_in_call = all((w.shape[0] * w.shape[1]) % cast_rows == 0 for w in later)
    casted = None if cast_in_call else [w.astype(BF16) for w in later]
    sink_cols = jnp.repeat(swa_sinks, SUBLANES, axis=1).reshape(depth, swa_heads * SUBLANES, 1)
    to_lanes = lambda c: jnp.transpose(c, (0, 1, 3, 4, 2))
    from_lanes = lambda c: jnp.transpose(c, (0, 1, 4, 2, 3))
    ck, cv = to_lanes(cache_swa_k), to_lanes(cache_swa_v)

    outs_p = [[] for _ in range(5)]
    outs_s = [[], []]
    gla_s, nk_s, nv_s = [], [], []
    for l in range(depth):
        last = l == depth - 1

        (pa, pb, pc, pg), new_casts = _inproj(hp, g_mix_r, w_t, o_c, l, widths, tm_p,
                                              later if casted is None else ())
        if casted is None:
            casted = new_casts
        wglu_b, wb, wo, wg, wu, wd = casted
        s5_params = (ar, ai, bre, bim, cre, cim, s5_d_r, wglu_b, b_glu_r)
        ya, hr, hi = _s5(pa.reshape(bp, tp, s5_w), zeros_s5, zeros_s5, 0, s5_params, l, s5_steps)
        ya = ya.reshape(bp * tp, s5_w)
        yb, sg, yc, nk, nv = _gla_swa_prompt(pb, pc, wa2, ba, gn, swa_sinks, cos_p, sin_p, l, bp, gla_heads, gla_dk,
                                             gla_dv, swa_heads, mix_rows)
        hp = _merge_ffn(hp, ya, yb, yc, pg, wb, wo, g_ffn_r, wg, wu, wd, gf, l, tm_p, last)
        for lst, val in zip(outs_p, (hr.reshape(bp, groups, s5_n), hi.reshape(bp, groups, s5_n), sg,
                                     nk.reshape(bp, WINDOW, SWA_KV_HEADS, SWA_HD),
                                     nv.reshape(bp, WINDOW, SWA_KV_HEADS, SWA_HD))):
            lst.append(val)

        (pa, pb, pc, pg), _ = _inproj(hs, g_mix_r, w_t, o_c, l, widths, tm_s)
        ya, hr, hi = _s5(pa, h0r_s, h0i_s, l, s5_params, l, ts)
        yb, sg = _gla_sample(pb, state_gla, wa2, ba, gn, l, ts, _tile(bs, 32), (gla_s,) if last else ())
        yc, nk, nv = _swa_sample(pc, ck, cv, sink_cols, cos_s, sin_s, l, swa_heads, ts, _tile(bs, 16),
                                 (nk_s, nv_s) if last else ())
        gla_s, nk_s, nv_s = gla_s + [sg], nk_s + [nk], nv_s + [nv]
        hs = _merge_ffn(hs, ya, yb, yc, pg, wb, wo, g_ffn_r, wg, wu, wd, gf, l, tm_s, last)
        for lst, val in zip(outs_s, (hr.reshape(bs, groups, s5_n), hi.reshape(bs, groups, s5_n))):
            lst.append(val)

    layered = lambda a: a if a.ndim == ck.ndim else a[None]
    outs_s = [jnp.stack(o) for o in outs_s] + [layered(gla_s[-1]), from_lanes(layered(nk_s[-1])),
                                               from_lanes(layered(nv_s[-1]))]
    return (hp.reshape(bp, tp, d), hs.reshape(bs, ts, d), *[jnp.stack(o) for o in outs_p], *outs_s)
```

```python
import functools

import jax
import jax.numpy as jnp
from jax import lax
from jax.experimental import pallas as pl
from jax.experimental.pallas import tpu as pltpu

F32 = jnp.float32
BF16 = jnp.bfloat16

EPS = 1e-6
GLA_TAU = 16.0
GLA_CHUNK = 64
SWA_KV_HEADS = 2
SWA_HD = 64
WINDOW = 128
ROPE_THETA = 10000.0
PAST_LEN = 16384

SUBLANES = 8
LANES = 128
VMEM_LIMIT = 56 * 1024 * 1024

NT_DIMS = (((1,), (1,)), ((), ()))
TN_DIMS = (((0,), (0,)), ((), ()))


def _params(*sem):
    return pltpu.CompilerParams(dimension_semantics=sem, vmem_limit_bytes=VMEM_LIMIT)


def _dot(a, b):
    return jnp.dot(a, b, preferred_element_type=F32)


def _rms(x, g):
    return x * lax.rsqrt(jnp.mean(x * x, axis=-1, keepdims=True) + EPS) * g


def _full(shape):
    nd = len(shape)
    return pl.BlockSpec(shape, lambda *_: (0,) * nd)


def _layer(a, l):
    nd = a.ndim - 1
    return pl.BlockSpec((None,) + a.shape[1:], lambda *_: (l,) + (0,) * nd, pipeline_mode=pl.Buffered(1))


def _inproj_kernel(x_ref, g_ref, wa_ref, wb_ref, *refs, n_cast):
    cast_in, o_refs, cast_out = refs[:n_cast], refs[n_cast:n_cast + 4], refs[n_cast + 4:]
    xb = _rms(x_ref[...], g_ref[...]).astype(BF16)
    for w_ref, group in ((wa_ref, o_refs[:2]), (wb_ref, o_refs[2:])):
        off = 0
        for o_ref in group:
            n = o_ref.shape[-1]
            o_ref[...] = lax.dot_general(xb, w_ref[off:off + n, :], NT_DIMS, preferred_element_type=F32)
            off += n
    for src, dst in zip(cast_in, cast_out):
        dst[...] = src[...].astype(BF16)


def _inproj(x, g, w, row_b, l, widths, tm, to_cast=()):
    n, d = x.shape
    steps = n // tm
    rows_a, rows_b = sum(widths[:2]), sum(widths[2:])
    assert row_b + rows_b == w.shape[1] and row_b % (2 * SUBLANES) == 0 and rows_a <= w.shape[1]
    w_rows = lambda start, size: pl.BlockSpec((None, pl.Element(size), pl.Element(d)), lambda i: (l, start, 0),
                                              pipeline_mode=pl.Buffered(1))
    flat = [a.reshape(-1, a.shape[-1]) for a in to_cast]
    assert all(f.shape[0] % (steps * 2 * SUBLANES) == 0 for f in flat)
    slice_spec = lambda f: pl.BlockSpec((f.shape[0] // steps, f.shape[1]), lambda i: (i, 0))
    outs = pl.pallas_call(
        functools.partial(_inproj_kernel, n_cast=len(flat)),
        grid=(steps,),
        in_specs=[pl.BlockSpec((tm, d), lambda i: (i, 0)), _layer(g, l), w_rows(0, rows_a), w_rows(row_b, rows_b)]
        + [slice_spec(f) for f in flat],
        out_specs=[pl.BlockSpec((tm, wd), lambda i: (i, 0)) for wd in widths] + [slice_spec(f) for f in flat],
        out_shape=[jax.ShapeDtypeStruct((n, wd), F32) for wd in widths]
        + [jax.ShapeDtypeStruct(f.shape, BF16) for f in flat],
        compiler_params=_params("parallel"),
        name="inproj",
    )(x, g, w, w, *flat)
    return outs[:4], [o.reshape(a.shape) for o, a in zip(outs[4:], to_cast)]


def _s5_prep_kernel(lr_ref, li_ref, ldt_ref, br_ref, bi_ref, ar_ref, ai_ref, bbr_ref, bbi_ref):
    lr, li = lr_ref[...], li_ref[...]
    dt = jnp.exp(ldt_ref[...])
    mag = jnp.exp(lr * dt)
    ar = mag * jnp.cos(li * dt)
    ai = mag * jnp.sin(li * dt)
    den = lr * lr + li * li
    zr, zi = ar - 1.0, ai
    er = (zr * lr + zi * li) / den
    ei = (zi * lr - zr * li) / den
    ar_ref[...] = ar
    ai_ref[...] = ai
    br, bi = br_ref[...], bi_ref[...]
    bbr_ref[...] = er[None] * br - ei[None] * bi
    bbi_ref[...] = er[None] * bi + ei[None] * br


def _s5_prep(lam_re, lam_im, log_dt, b_re, b_im):
    depth, g, c, n = b_re.shape
    dg = depth * g
    brt = jnp.transpose(b_re.reshape(dg, c, n), (1, 0, 2))
    bit = jnp.transpose(b_im.reshape(dg, c, n), (1, 0, 2))
    ar, ai, bbr, bbi = pl.pallas_call(
        _s5_prep_kernel,
        out_shape=[jax.ShapeDtypeStruct((dg, n), F32)] * 2 + [jax.ShapeDtypeStruct((c, dg, n), F32)] * 2,
        name="s5_prep",
    )(lam_re.reshape(dg, n), lam_im.reshape(dg, n), log_dt.reshape(dg, 1), brt, bit)
    unt = lambda a: jnp.transpose(a, (1, 0, 2)).reshape(depth, g, c, n)
    return ar.reshape(depth, 1, g * n), ai.reshape(depth, 1, g * n), unt(bbr), unt(bbi)


def _s5_pack(bbr, bbi, c_re, c_im):
    depth, g, c, n = bbr.shape
    gp = LANES // c
    ks = g // gp
    eye = jnp.eye(gp, dtype=F32)

    def pack_b(b):
        return jnp.einsum("lkgcn,gh->lkgchn", b.reshape(depth, ks, gp, c, n), eye).reshape(
            depth, ks, gp * c, gp * n).astype(BF16)

    def pack_c(cm):
        return jnp.einsum("lkgnc,gh->lkgnhc", cm.reshape(depth, ks, gp, n, c), eye).reshape(
            depth, ks, gp * n, gp * c).astype(BF16)

    return pack_b(bbr), pack_b(bbi), pack_c(c_re), pack_c(c_im)


S5_COL_BLOCK = 1024
S5_UNROLL = 8


def _s5_kernel(u_ref, h0r_ref, h0i_ref, ar_ref, ai_ref, bre_ref, bim_ref, cre_ref, cim_ref, d_ref, wglu_ref,
               bglu_ref, y_ref, hr_out, hi_out, us_s, ys_s, xr_s, xi_s, hr_s, hi_s, *tmp_s, nb, steps):
    c = pl.program_id(0)

    @pl.when(c == 0)
    def _():
        hr_s[...] = h0r_ref[...]
        hi_s[...] = h0i_ref[...]

    ks, kw, nw = bre_ref.shape
    by_sequence = u_ref.ndim == 3
    if by_sequence:
        for b in range(nb):
            for k in range(ks):
                us_s[k, pl.ds(b, steps, stride=nb), :] = u_ref[b, :, k * kw:(k + 1) * kw]
    else:
        for k in range(ks):
            tmp_s[0][k] = u_ref[:, k * kw:(k + 1) * kw]
        for t in range(steps):
            for k in range(ks):
                us_s[k, t * nb:(t + 1) * nb, :] = tmp_s[0][k, pl.ds(t, nb, stride=steps), :]

    def x_proj(k):
        uk = us_s[k].astype(BF16)
        xr_s[:, k * nw:(k + 1) * nw] = _dot(uk, bre_ref[k])
        xi_s[:, k * nw:(k + 1) * nw] = _dot(uk, bim_ref[k])

    def y_proj(k):
        ss = slice(k * nw, (k + 1) * nw)
        yk = _dot(xr_s[:, ss].astype(BF16), cre_ref[k]) - _dot(xi_s[:, ss].astype(BF16), cim_ref[k])
        ys_s[k] = jax.nn.gelu(yk + d_ref[:, k * kw:(k + 1) * kw] * us_s[k])

    def scan(rb, cs, unrolled):
        rows = slice(rb * SUBLANES, (rb + 1) * SUBLANES)
        width = cs.stop - cs.start
        a_r = jnp.broadcast_to(ar_ref[:, cs], (SUBLANES, width))
        a_i = jnp.broadcast_to(ai_ref[:, cs], (SUBLANES, width))

        def step(t, carry):
            hr, hi = carry
            r0 = t * nb + rb * SUBLANES
            tr = slice(r0, r0 + SUBLANES) if unrolled else pl.ds(pl.multiple_of(r0, SUBLANES), SUBLANES)
            nhr = a_r * hr - a_i * hi + xr_s[tr, cs]
            nhi = a_r * hi + a_i * hr + xi_s[tr, cs]
            xr_s[tr, cs] = nhr
            xi_s[tr, cs] = nhi
            return nhr, nhi

        carry = (hr_s[rows, cs], hi_s[rows, cs])
        if unrolled:
            for t in range(steps):
                carry = step(t, carry)
        else:
            carry = lax.fori_loop(0, steps, step, carry, unroll=min(steps, S5_UNROLL))
        hr_s[rows, cs], hi_s[rows, cs] = carry

    if by_sequence and nb == SUBLANES:
        x_proj(0)
        for k in range(ks):
            if k + 1 < ks:
                x_proj(k + 1)
            if k >= 1:
                y_proj(k - 1)
            scan(0, slice(k * nw, (k + 1) * nw), True)
        y_proj(ks - 1)
    else:
        for k in range(ks):
            x_proj(k)
        for rb in range(nb // SUBLANES):
            for cb in range(xr_s.shape[1] // S5_COL_BLOCK):
                scan(rb, slice(cb * S5_COL_BLOCK, (cb + 1) * S5_COL_BLOCK), False)
        for k in range(ks):
            y_proj(k)
    y = jnp.concatenate([ys_s[k] for k in range(ks)], axis=1)
    y = y * jax.nn.sigmoid(_dot(y.astype(BF16), wglu_ref[...]) + bglu_ref[...])
    for k in range(ks):
        ys_s[k] = y[:, k * kw:(k + 1) * kw]
    if by_sequence:
        for b in range(nb):
            for k in range(ks):
                y_ref[b, :, k * kw:(k + 1) * kw] = ys_s[k, pl.ds(b, steps, stride=nb), :]
    else:
        for t in range(steps):
            for k in range(ks):
                tmp_s[0][k, pl.ds(t, nb, stride=steps), :] = ys_s[k, t * nb:(t + 1) * nb, :]
        for k in range(ks):
            y_ref[:, k * kw:(k + 1) * kw] = tmp_s[0][k]

    @pl.when(c == pl.num_programs(0) - 1)
    def _():
        hr_out[...] = hr_s[...]
        hi_out[...] = hi_s[...]


def _s5(u, h0r, h0i, l0, params, l, steps):
    _, nb, ns = h0r.shape
    w = u.shape[-1]
    blk = steps * nb
    ks = params[2].shape[1]
    if u.ndim == 3:
        grid = (u.shape[1] // steps,)
        u_spec = pl.BlockSpec((nb, steps, w), lambda i: (0, i, 0))
        tmp = []
    else:
        assert u.shape[0] == blk
        grid = (1,)
        u_spec = _full(u.shape)
        tmp = [pltpu.VMEM((ks, blk, LANES), F32)]
    kern = functools.partial(_s5_kernel, nb=nb, steps=steps)
    return pl.pallas_call(
        kern,
        grid=grid,
        in_specs=[u_spec, _layer(h0r, l0), _layer(h0i, l0)] + [_layer(a, l) for a in params],
        out_specs=[u_spec, _full((nb, ns)), _full((nb, ns))],
        out_shape=[jax.ShapeDtypeStruct(u.shape, F32), jax.ShapeDtypeStruct((nb, ns), F32),
                   jax.ShapeDtypeStruct((nb, ns), F32)],
        scratch_shapes=[pltpu.VMEM((ks, blk, LANES), F32), pltpu.VMEM((ks, blk, LANES), F32),
                        pltpu.VMEM((blk, ns), F32), pltpu.VMEM((blk, ns), F32), pltpu.VMEM((nb, ns), F32),
                        pltpu.VMEM((nb, ns), F32)] + tmp,
        compiler_params=_params("arbitrary"),
        name="s5_scan",
    )(u, h0r, h0i, *params)


def _log2(n):
    assert n & (n - 1) == 0, n
    return n.bit_length() - 1


def _chunk_masks(rows, chunk):
    sh = _log2(chunk)
    rr = lax.broadcasted_iota(jnp.int32, (rows, rows), 0)
    cc = lax.broadcasted_iota(jnp.int32, (rows, rows), 1)
    same = lax.shift_right_logical(rr, sh) == lax.shift_right_logical(cc, sh)
    return same, same & (cc <= rr)


def _gla_prologue(p_ref, wa_ref, ba_ref, chunk, qk):
    rows = p_ref.shape[0]
    q = p_ref[:, 0:qk]
    k = p_ref[:, qk:2 * qk]
    alow = p_ref[:, p_ref.shape[1] - LANES:].astype(BF16)
    z = _dot(alow, wa_ref[...]) + ba_ref[...]
    la = (jnp.minimum(z, 0.0) - jnp.log(1.0 + jnp.exp(-jnp.abs(z)))) * (1.0 / GLA_TAU)
    same, causal = _chunk_masks(rows, chunk)
    tri = jnp.where(causal, 1.0, 0.0).astype(BF16)
    blk = jnp.where(same, 1.0, 0.0).astype(BF16)
    la_hi = la.astype(BF16)
    la_lo = (la - la_hi.astype(F32)).astype(BF16)
    b = _dot(tri, la_hi) + _dot(tri, la_lo)
    bl = _dot(blk, la_hi) + _dot(blk, la_lo)
    return q, k, b, bl


def _gla_out(o, gn, r):
    on = o * lax.rsqrt(jnp.mean(o * o, axis=-1, keepdims=True) + EPS) * gn
    return on * (r * jax.nn.sigmoid(r))


def _gla_prompt_init(st_s):
    @pl.when(pl.program_id(1) == 0)
    def _():
        st_s[...] = jnp.zeros_like(st_s)


def _gla_prompt_final(s_ref, st_s, *, heads, dk):
    @pl.when(pl.program_id(1) == pl.num_programs(1) - 1)
    def _():
        for h in range(heads):
            s_ref[0, h] = st_s[:, h * dk:(h + 1) * dk].T


def _gla_prompt_body(p_ref, wa_ref, ba_ref, gn_ref, y_ref, st_s, *, heads, dk, dv, chunk):
    rows = p_ref.shape[0]
    qk, vw = heads * dk, heads * dv
    q, k, b, bl = _gla_prologue(p_ref, wa_ref, ba_ref, chunk, qk)
    v_off, r_off = 2 * qk, 2 * qk + vw
    qd = (q * (dk ** -0.5) * jnp.exp(b)).astype(BF16)
    kd = (k * jnp.exp(-b)).astype(BF16)
    kst = (k * jnp.exp(bl - b)).astype(BF16)
    dec = jnp.exp(bl)
    gn = gn_ref[...]
    zero = jnp.zeros((), BF16)
    qk_head = lax.shift_right_logical(lax.broadcasted_iota(jnp.int32, (chunk, qk), 1), _log2(dk))
    v_head = lax.shift_right_logical(lax.broadcasted_iota(jnp.int32, (chunk, vw), 1), _log2(dv))
    rr = lax.broadcasted_iota(jnp.int32, (chunk, heads * chunk), 0)
    cc = lax.broadcasted_iota(jnp.int32, (chunk, heads * chunk), 1) & (chunk - 1)
    causal = cc <= rr
    nchunk = rows // chunk
    by_head = lambda x, head: jnp.concatenate([jnp.where(head == h, x, zero) for h in range(heads)], axis=0)
    att, ds_t, v_diag, qd_rows = [], [], [], []
    for ci in range(nchunk):
        rs = slice(ci * chunk, (ci + 1) * chunk)
        v_c = p_ref[rs, v_off:v_off + vw].astype(BF16)
        v_rows = jnp.concatenate([v_c[:, h * dv:(h + 1) * dv] for h in range(heads)], axis=0)
        v_diag.append(by_head(v_c, v_head))
        qd_rows.append(by_head(qd[rs], qk_head))
        att.append(lax.dot_general(qd[rs], by_head(kd[rs], qk_head), NT_DIMS, preferred_element_type=F32))
        ds_t.append(lax.dot_general(v_rows, by_head(kst[rs], qk_head), TN_DIMS, preferred_element_type=F32))
    yield
    o_intra = [_dot(jnp.where(causal, att[ci], 0.0).astype(BF16), v_diag[ci]) for ci in range(nchunk)]
    yield
    st = st_s[...]
    o_inter = []
    for ci in range(nchunk):
        o_inter.append(lax.dot_general(qd_rows[ci], st.astype(BF16), NT_DIMS, preferred_element_type=F32))
        st = dec[ci * chunk:ci * chunk + 1, :] * st + ds_t[ci]
    st_s[...] = st
    yield
    for ci in range(nchunk):
        rs = slice(ci * chunk, (ci + 1) * chunk)
        for h in range(heads):
            o = o_intra[ci][:, h * dv:(h + 1) * dv] + o_inter[ci][h * chunk:(h + 1) * chunk]
            r = p_ref[rs, r_off + h * dv:r_off + (h + 1) * dv]
            y_ref[rs, h * dv:(h + 1) * dv] = _gla_out(o, gn, r)


def _gla_sample_body(p_ref, s0_ref, wa_ref, ba_ref, gn_ref, y_ref, s_ref, *, heads, dk, dv, t):
    rows = p_ref.shape[0]
    qk, vw = heads * dk, heads * dv
    assert dv == LANES and qk % LANES == 0
    q, k, b, bl = _gla_prologue(p_ref, wa_ref, ba_ref, t, qk)
    v_off, r_off = 2 * qk, 2 * qk + vw
    qd = (q * (dk ** -0.5) * jnp.exp(b)).astype(BF16)
    kd = (k * jnp.exp(-b)).astype(BF16)
    kst = (k * jnp.exp(bl - b)).astype(BF16)
    dec = jnp.exp(bl)
    gn = gn_ref[...]
    per_tile = SUBLANES // t
    zero = jnp.zeros((), BF16)
    qk_head = lax.shift_right_logical(lax.broadcasted_iota(jnp.int32, (SUBLANES, qk), 1), _log2(dk))
    v_head = lax.shift_right_logical(lax.broadcasted_iota(jnp.int32, (SUBLANES, vw), 1), _log2(dv))
    by_head = lambda x, head: jnp.concatenate([jnp.where(head == h, x, zero) for h in range(heads)], axis=0)
    stack_seq = lax.shift_right_logical(
        lax.broadcasted_iota(jnp.int32, (heads * SUBLANES, 1), 0) & (SUBLANES - 1), _log2(t))
    rr = lax.broadcasted_iota(jnp.int32, (SUBLANES, heads * SUBLANES), 0)
    cc = lax.broadcasted_iota(jnp.int32, (SUBLANES, heads * SUBLANES), 1) & (SUBLANES - 1)
    sh = _log2(t)
    causal = (lax.shift_right_logical(rr, sh) == lax.shift_right_logical(cc, sh)) & (cc <= rr)
    att, o_state, v_diag = [], [], []
    for ti in range(rows // SUBLANES):
        rs = slice(ti * SUBLANES, (ti + 1) * SUBLANES)
        v_t = p_ref[rs, v_off:v_off + vw].astype(BF16)
        v_rows = jnp.concatenate([v_t[:, h * dv:(h + 1) * dv] for h in range(heads)], axis=0)
        v_diag.append(by_head(v_t, v_head))
        qd_rows, kst_rows = by_head(qd[rs], qk_head), by_head(kst[rs], qk_head)
        att.append(lax.dot_general(qd[rs], by_head(kd[rs], qk_head), NT_DIMS, preferred_element_type=F32))
        o = None
        for j in range(per_tile):
            bidx = ti * per_tile + j
            mine = stack_seq == j
            s0 = jnp.concatenate([s0_ref[bidx, h] for h in range(heads)], axis=0)
            oj = _dot(jnp.where(mine, qd_rows, zero), s0.astype(BF16))
            o = oj if o is None else o + oj
            ds = lax.dot_general(jnp.where(mine, kst_rows, zero), v_rows, TN_DIMS, preferred_element_type=F32)
            drow = dec[ti * SUBLANES + j * t:ti * SUBLANES + j * t + 1, :]
            dcol = jnp.concatenate([jnp.broadcast_to(drow[:, g * LANES:(g + 1) * LANES], (LANES, LANES)).T
                                    for g in range(qk // LANES)], axis=0)
            s_new = dcol * s0 + ds
            for h in range(heads):
                s_ref[bidx, h] = s_new[h * dk:(h + 1) * dk]
        o_state.append(o)
    yield
    for ti in range(rows // SUBLANES):
        rs = slice(ti * SUBLANES, (ti + 1) * SUBLANES)
        o_intra = _dot(jnp.where(causal, att[ti], 0.0).astype(BF16), v_diag[ti])
        for h in range(heads):
            o = o_intra[:, h * dv:(h + 1) * dv] + o_state[ti][h * SUBLANES:(h + 1) * SUBLANES]
            r = p_ref[rs, r_off + h * dv:r_off + (h + 1) * dv]
            y_ref[rs, h * dv:(h + 1) * dv] = _gla_out(o, gn, r)


def _take_turns(bodies):
    while bodies:
        bodies = [b for b in bodies if next(b, StopIteration) is not StopIteration]


def _first_half(shape):
    lane = lax.broadcasted_iota(jnp.int32, shape, 1)
    return (lane & (SWA_HD - 1)) < SWA_HD // 2


def _rope_table_kernel(pos_ref, inv_ref, cos_ref, sin_ref):
    ang = pos_ref[...] * inv_ref[...]
    s = jnp.sin(ang)
    cos_ref[...] = jnp.cos(ang)
    sin_ref[...] = jnp.where(_first_half(ang.shape), -s, s)


def _rope_table(pos, inv):
    rows = pos.shape[0]
    return pl.pallas_call(
        _rope_table_kernel,
        out_shape=[jax.ShapeDtypeStruct((rows, LANES), F32)] * 2,
        name="rope_table",
    )(pos.astype(F32).reshape(rows, 1), inv)


def _rope(x, cos, sin):
    half = SWA_HD // 2
    first_half = _first_half(cos.shape)
    out = []
    for g in range(x.shape[1] // LANES):
        xg = x[:, g * LANES:(g + 1) * LANES]
        swapped = jnp.where(first_half, pltpu.roll(xg, LANES - half, 1), pltpu.roll(xg, half, 1))
        out.append(xg * cos + swapped * sin)
    return out[0] if len(out) == 1 else jnp.concatenate(out, axis=1)


def _swa_prompt_init(prev_s):
    @pl.when(pl.program_id(1) == 0)
    def _():
        prev_s[...] = jnp.zeros_like(prev_s)


def _swa_prompt_body(sink_ref, cur_ref, cos_ref, sin_ref, y_ref, nk_ref, nv_ref, prev_s, s_s, pc_s, *, heads, l):
    i = pl.program_id(1)
    w = WINDOW
    qw = heads * SWA_HD
    kvw = SWA_KV_HEADS * SWA_HD
    gq = heads // SWA_KV_HEADS
    nq = cur_ref.shape[0] // w
    cos, sin = cos_ref[...], sin_ref[...]
    q = (_rope(cur_ref[:, 0:qw], cos, sin) * (SWA_HD ** -0.5)).astype(BF16)
    k_cur = _rope(cur_ref[:, qw:qw + kvw], cos, sin)
    v_cur = cur_ref[:, qw + kvw:qw + 2 * kvw]
    new = [x.astype(BF16) for x in (k_cur, pltpu.roll(k_cur, SWA_HD, 1), v_cur, pltpu.roll(v_cur, SWA_HD, 1))]
    k2, k2r, v2, v2r = [jnp.concatenate([prev_s[j], new[j]], axis=0) for j in range(4)]
    zero = jnp.zeros((), BF16)
    lo2 = lax.broadcasted_iota(jnp.int32, ((nq + 1) * w, LANES), 1) < SWA_HD
    lo = lax.broadcasted_iota(jnp.int32, (w, LANES), 1) < SWA_HD
    k_both = (jnp.where(lo2, k2, k2r), jnp.where(lo2, k2r, k2))
    v_half = {(0, 0): jnp.where(lo2, v2, zero), (1, 1): jnp.where(lo2, zero, v2),
              (0, 1): jnp.where(lo2, zero, v2r), (1, 0): jnp.where(lo2, v2r, zero)}
    upper = lax.broadcasted_iota(jnp.int32, (w, w), 1) > lax.broadcasted_iota(jnp.int32, (w, w), 0)
    no_prev = jnp.where(i > 0, 0.0, -jnp.inf)
    slot = lambda c, h: slice((c * heads + h) * w, (c * heads + h + 1) * w)
    for c in range(nq):
        for kv in range(SWA_KV_HEADS):
            qm = []
            for h in range(kv * gq, (kv + 1) * gq):
                qg = q[c * w:(c + 1) * w, (h // 2) * LANES:(h // 2 + 1) * LANES]
                qm.append(jnp.where(lo, qg, zero) if h % 2 == 0 else jnp.where(lo, zero, qg))
            s_s[(c * heads + kv * gq) * w:(c * heads + (kv + 1) * gq) * w, :] = lax.dot_general(
                jnp.concatenate(qm, axis=0), k_both[kv][c * w:(c + 2) * w], NT_DIMS,
                preferred_element_type=F32)
    yield
    sink_term = {}
    for c in range(nq):
        for h in range(heads):
            s_prev = s_s[slot(c, h), 0:w] + no_prev if c == 0 else s_s[slot(c, h), 0:w]
            sp = jnp.where(upper, s_prev, s_s[slot(c, h), w:2 * w])
            sink = sink_ref[l, h]
            m = jnp.maximum(jnp.max(sp, axis=-1, keepdims=True), sink)
            p = jnp.exp(sp - m)
            pc_s[slot(c, h), 0:w] = jnp.where(upper, p, 0.0).astype(BF16)
            pc_s[slot(c, h), w:2 * w] = jnp.where(upper, 0.0, p).astype(BF16)
            sink_term[c, h] = jnp.exp(sink - m)
    yield
    num = {(c, h): _dot(pc_s[slot(c, h), :], v_half[(h // gq, h % 2)][c * w:(c + 2) * w])
           for c in range(nq) for h in range(heads)}
    den_all = _dot(pc_s[...], jnp.ones((2 * w, LANES), BF16))
    yield
    for c in range(nq):
        for g in range(heads // 2):
            he, ho = 2 * g, 2 * g + 1
            den = jnp.where(lo, den_all[slot(c, he)] + sink_term[c, he], den_all[slot(c, ho)] + sink_term[c, ho])
            y_ref[c * w:(c + 1) * w, g * LANES:(g + 1) * LANES] = (num[c, he] + num[c, ho]) * (1.0 / den)
    for j in range(4):
        prev_s[j] = new[j][(nq - 1) * w:]
    nk_ref[0] = k_cur[(nq - 1) * w:]
    nv_ref[0] = v_cur[(nq - 1) * w:]


def _gla_swa_prompt_kernel(pb_ref, wa_ref, ba_ref, gn_ref, sink_ref, pc_ref, cos_ref, sin_ref,
                           yb_ref, s_ref, yc_ref, nk_ref, nv_ref, st_s, prev_s, s_s, pc_s, *, gla, swa):
    _gla_prompt_init(st_s)
    _swa_prompt_init(prev_s)
    _take_turns([_gla_prompt_body(pb_ref, wa_ref, ba_ref, gn_ref, yb_ref, st_s, chunk=GLA_CHUNK, **gla),
                 _swa_prompt_body(sink_ref, pc_ref, cos_ref, sin_ref, yc_ref, nk_ref, nv_ref, prev_s, s_s, pc_s,
                                  **swa)])
    _gla_prompt_final(s_ref, st_s, heads=gla["heads"], dk=gla["dk"])


def _gla_swa_prompt(pb, pc, wa, ba, gn, sinks, cos, sin, l, bsz, gla_heads, dk, dv, swa_heads, rows):
    n, pbw = pb.shape
    pcw = pc.shape[1]
    nblk = n // bsz // rows
    nq = rows // WINDOW
    qw, kvw = swa_heads * SWA_HD, SWA_KV_HEADS * SWA_HD
    kern = functools.partial(_gla_swa_prompt_kernel, gla=dict(heads=gla_heads, dk=dk, dv=dv),
                             swa=dict(heads=swa_heads, l=l))
    row_blk = lambda wdt: pl.BlockSpec((rows, wdt), lambda b, i: (b * nblk + i, 0))
    tab = pl.BlockSpec((rows, LANES), lambda b, i: (i, 0))
    win = pl.BlockSpec((1, WINDOW, kvw), lambda b, i: (b, 0, 0))
    return pl.pallas_call(
        kern,
        grid=(bsz, nblk),
        in_specs=[row_blk(pbw), _layer(wa, l), _layer(ba, l), _layer(gn, l),
                  pl.BlockSpec(memory_space=pltpu.SMEM), row_blk(pcw), tab, tab],
        out_specs=[row_blk(gla_heads * dv), pl.BlockSpec((1, gla_heads, dk, dv), lambda b, i: (b, 0, 0, 0)),
                   row_blk(qw), win, win],
        out_shape=[jax.ShapeDtypeStruct((n, gla_heads * dv), F32),
                   jax.ShapeDtypeStruct((bsz, gla_heads, dk, dv), F32),
                   jax.ShapeDtypeStruct((n, qw), F32), jax.ShapeDtypeStruct((bsz, WINDOW, kvw), F32),
                   jax.ShapeDtypeStruct((bsz, WINDOW, kvw), F32)],
        scratch_shapes=[pltpu.VMEM((dv, gla_heads * dk), F32), pltpu.VMEM((4, WINDOW, LANES), BF16),
                        pltpu.VMEM((nq * swa_heads * WINDOW, 2 * WINDOW), F32),
                        pltpu.VMEM((nq * swa_heads * WINDOW, 2 * WINDOW), BF16)],
        compiler_params=_params("parallel", "arbitrary"),
        name="gla_swa_prompt",
    )(pb, wa, ba, gn, sinks, pc, cos, sin)


def _swa_sample_body(sinkcol_ref, p_ref, ck_ref, cv_ref, cos_ref, sin_ref, y_ref, nk_ref, nv_ref, *, heads, t, nb):
    w = ck_ref.shape[3]
    qw = heads * SWA_HD
    kvw = SWA_KV_HEADS * SWA_HD
    gq = heads // SWA_KV_HEADS
    per_tile = SUBLANES // t
    nrow = heads * SUBLANES
    cos, sin = cos_ref[...], sin_ref[...]
    lo = lax.broadcasted_iota(jnp.int32, (SUBLANES, LANES), 1) < SWA_HD
    row_r = lax.broadcasted_iota(jnp.int32, (nrow, 1), 0) & (SUBLANES - 1)
    row_seq = lax.shift_right_logical(row_r, _log2(t))
    row_step = row_r & (t - 1)
    cache_ok = lax.broadcasted_iota(jnp.int32, (nrow, w), 1) > row_step
    new_r = lax.broadcasted_iota(jnp.int32, (nrow, SUBLANES), 1)
    new_ok = (lax.shift_right_logical(new_r, _log2(t)) == row_seq) & ((new_r & (t - 1)) <= row_step)
    sink_col = sinkcol_ref[...]
    lane_w = lax.broadcasted_iota(jnp.int32, (SWA_HD, w), 1)
    flat = lambda c: jnp.concatenate([c[kv] for kv in range(SWA_KV_HEADS)], axis=0).astype(BF16)
    pad_rows = jnp.zeros((LANES - 2 * SUBLANES, kvw), F32)

    def on_kv_half(x, h, back=False):
        own, kv = h % 2, h // gq
        moved = x if own == kv else pltpu.roll(x, SWA_HD, 1)
        keep = own if back else kv
        return jnp.where(lo, moved, 0.0) if keep == 0 else jnp.where(lo, 0.0, moved)

    new_k, new_v, scores = [], [], []
    for ti in range(nb // per_tile):
        rs = slice(ti * SUBLANES, (ti + 1) * SUBLANES)
        q8 = _rope(p_ref[rs, 0:qw], cos, sin) * (SWA_HD ** -0.5)
        kn8 = _rope(p_ref[rs, qw:qw + kvw], cos, sin)
        new_k.append(kn8)
        new_v.append(p_ref[rs, qw + kvw:qw + 2 * kvw])
        qm = jnp.concatenate([on_kv_half(q8[:, (h // 2) * LANES:(h // 2 + 1) * LANES], h) for h in range(heads)],
                             axis=0).astype(BF16)
        s_c = None
        for j in range(per_tile):
            s_j = _dot(qm, flat(ck_ref[ti * per_tile + j]))
            s_c = s_j if s_c is None else jnp.where(row_seq == j, s_j, s_c)
        s_n = lax.dot_general(qm, kn8.astype(BF16), NT_DIMS, preferred_element_type=F32)
        scores.append((s_c, s_n))
    yield
    probs = []
    for s_c, s_n in scores:
        s_c = jnp.where(cache_ok, s_c, -jnp.inf)
        s_n = jnp.where(new_ok, s_n, -jnp.inf)
        m = jnp.maximum(jnp.maximum(jnp.max(s_c, axis=1, keepdims=True), jnp.max(s_n, axis=1, keepdims=True)),
                        sink_col)
        p_c = jnp.exp(s_c - m)
        p_n = jnp.exp(s_n - m)
        den = jnp.sum(p_c, axis=1, keepdims=True) + jnp.sum(p_n, axis=1, keepdims=True) + jnp.exp(sink_col - m)
        rden = 1.0 / den
        probs.append((p_c * rden, (p_n * rden).astype(BF16)))
    yield
    outs = []
    for ti, (p_c, p_n) in enumerate(probs):
        o = _dot(p_n, new_v[ti].astype(BF16))
        for j in range(per_tile):
            p_j = jnp.where(row_seq == j, p_c, 0.0).astype(BF16)
            o = o + lax.dot_general(p_j, flat(cv_ref[ti * per_tile + j]), NT_DIMS, preferred_element_type=F32)
        outs.append(o)
    yield
    for ti, o in enumerate(outs):
        for g in range(heads // 2):
            pair = [on_kv_half(o[h * SUBLANES:(h + 1) * SUBLANES], h, back=True) for h in (2 * g, 2 * g + 1)]
            y_ref[ti * SUBLANES:(ti + 1) * SUBLANES, g * LANES:(g + 1) * LANES] = pair[0] + pair[1]
    yield
    for ti in range(nb // per_tile):
        kv_t = jnp.concatenate([new_k[ti], new_v[ti], pad_rows], axis=0).T
        for j in range(per_tile):
            bidx = ti * per_tile + j
            for c_ref, n_ref, col0 in ((ck_ref, nk_ref, 0), (cv_ref, nv_ref, SUBLANES)):
                for kv in range(SWA_KV_HEADS):
                    tail = pltpu.roll(kv_t[kv * SWA_HD:(kv + 1) * SWA_HD, :], w - t - col0 - j * t, 1)
                    n_ref[bidx, kv] = jnp.where(lane_w < w - t, pltpu.roll(c_ref[bidx, kv], w - t, 1), tail)


def _layered_outputs(body, n_in, blocks, prev):
    n_prev = len(prev[0]) if prev else 0
    if not n_prev:
        kern = lambda *refs: _take_turns([body(*refs)])
        return (kern, [], [], [pl.BlockSpec(b, m) for b, m, _ in blocks],
                [jax.ShapeDtypeStruct(s, F32) for _, _, s in blocks])

    def kern(*refs):
        ins, earlier, outs = refs[:n_in], refs[n_in:n_in + len(blocks) * n_prev], refs[n_in + len(blocks) * n_prev:]
        stacked = outs[1:1 + len(blocks)]
        _take_turns([body(*ins, outs[0], *[s.at[n_prev] for s in stacked])])
        for k, s in enumerate(stacked):
            for ll in range(n_prev):
                s[ll] = earlier[k * n_prev + ll][...]

    in_specs = [pl.BlockSpec(b, m) for b, m, _ in blocks for _ in range(n_prev)]
    operands = [a for arrays in prev for a in arrays]
    out_specs = [pl.BlockSpec((n_prev + 1,) + b, lambda i, m=m: (0,) + m(i)) for b, m, _ in blocks]
    out_shapes = [jax.ShapeDtypeStruct((n_prev + 1,) + s, F32) for _, _, s in blocks]
    return kern, in_specs, operands, out_specs, out_shapes


def _gla_sample(p, s0, wa, ba, gn, l, t, nb, prev):
    n, pw = p.shape
    _, bsz, heads, dk, dv = s0.shape
    rows = nb * t
    body = functools.partial(_gla_sample_body, heads=heads, dk=dk, dv=dv, t=t)
    blocks = [((nb, heads, dk, dv), lambda i: (i, 0, 0, 0), s0.shape[1:])]
    kern, more_specs, more, st_specs, st_shapes = _layered_outputs(body, 5, blocks, prev)
    return pl.pallas_call(
        kern,
        grid=(bsz // nb,),
        in_specs=[pl.BlockSpec((rows, pw), lambda i: (i, 0)),
                  pl.BlockSpec((None, nb, heads, dk, dv), lambda i: (l, i, 0, 0, 0)),
                  _layer(wa, l), _layer(ba, l), _layer(gn, l)] + more_specs,
        out_specs=[pl.BlockSpec((rows, heads * dv), lambda i: (i, 0))] + st_specs,
        out_shape=[jax.ShapeDtypeStruct((n, heads * dv), F32)] + st_shapes,
        compiler_params=_params("parallel"),
        name="gla_sample",
    )(p, s0, wa, ba, gn, *more)


def _swa_sample(p, ck, cv, sink_cols, cos, sin, l, heads, t, nb, prev):
    n, pw = p.shape
    _, bsz, kvh, hd, w = ck.shape
    qw = heads * SWA_HD
    body = functools.partial(_swa_sample_body, heads=heads, t=t, nb=nb)
    blocks = [((nb, kvh, hd, w), lambda i: (i, 0, 0, 0), ck.shape[1:])] * 2
    kern, more_specs, more, c_specs, c_shapes = _layered_outputs(body, 6, blocks, prev)
    cache = pl.BlockSpec((None, nb, kvh, hd, w), lambda i: (l, i, 0, 0, 0))
    return pl.pallas_call(
        kern,
        grid=(bsz // nb,),
        in_specs=[_layer(sink_cols, l), pl.BlockSpec((nb * t, pw), lambda i: (i, 0)), cache, cache,
                  _full(cos.shape), _full(sin.shape)] + more_specs,
        out_specs=[pl.BlockSpec((nb * t, qw), lambda i: (i, 0))] + c_specs,
        out_shape=[jax.ShapeDtypeStruct((n, qw), F32)] + c_shapes,
        compiler_params=_params("parallel"),
        name="swa_sample",
    )(sink_cols, p, ck, cv, cos, sin, *more)


FFN_HIDDEN_CHUNK = 1024


def _merge_ffn_kernel(x_ref, ya_ref, yb_ref, yc_ref, gt_ref, wb_ref, wo_ref, g_ref, wg_ref, wu_ref, wd_ref, gf_ref,
                      o_ref, *, final_norm):
    d = x_ref.shape[1]
    merged = None
    off = 0
    for i, y_ref in enumerate((ya_ref, yb_ref, yc_ref)):
        wdt = y_ref.shape[1]
        br = _dot(y_ref[...].astype(BF16), wb_ref[off:off + wdt, :])
        term = jax.nn.sigmoid(gt_ref[:, i * d:(i + 1) * d]) * br
        merged = term if merged is None else merged + term
        off += wdt
    x = x_ref[...] + _dot(merged.astype(BF16), wo_ref[...])
    hb = _rms(x, g_ref[...]).astype(BF16)
    hidden = wg_ref.shape[1]
    acc = x
    for c0 in range(0, hidden, FFN_HIDDEN_CHUNK):
        c1 = min(c0 + FFN_HIDDEN_CHUNK, hidden)
        gate = _dot(hb, wg_ref[:, c0:c1])
        up = _dot(hb, wu_ref[:, c0:c1])
        act = (gate * jax.nn.sigmoid(gate) * up).astype(BF16)
        acc = acc + _dot(act, wd_ref[c0:c1, :])
    o_ref[...] = _rms(acc, gf_ref[...]) if final_norm else acc


def _merge_ffn(x, ya, yb, yc, gates, wb, wo, g, wg, wu, wd, gf, l, tm, final_norm):
    n, d = x.shape
    row = lambda a: pl.BlockSpec((tm, a.shape[1]), lambda i: (i, 0))
    return pl.pallas_call(
        functools.partial(_merge_ffn_kernel, final_norm=final_norm),
        grid=(n // tm,),
        in_specs=[row(x), row(ya), row(yb), row(yc), row(gates), _layer(wb, l), _layer(wo, l), _layer(g, l),
                  _layer(wg, l), _layer(wu, l), _layer(wd, l), _full(gf.shape)],
        out_specs=pl.BlockSpec((tm, d), lambda i: (i, 0)),
        out_shape=jax.ShapeDtypeStruct((n, d), F32),
        compiler_params=_params("parallel"),
        name="merge_ffn",
    )(x, ya, yb, yc, gates, wb, wo, g, wg, wu, wd, gf)


DENSE_ROWS = 512
S5_STEPS = 128
MIX_WINDOWS = 4
GLA_SAMPLE_SEQS = 32
SWA_SAMPLE_SEQS = 16


def _tile(n, want):
    while n % want:
        want //= 2
    return want


def kernel(x_prompt, x_sample, state_s5_re, state_s5_im, state_gla, cache_swa_k, cache_swa_v, g_mix, w_in, s5_lambda_re, s5_lambda_im, s5_log_dt, s5_b_re, s5_b_im, s5_c_re, s5_c_im, s5_d, w_glu, b_glu, w_gla_a2, b_gla_a, g_gla_norm, swa_sinks, w_branch, w_out, g_ffn, w_ffn_gate, w_ffn_up, w_ffn_down, g_final):
    bp, tp, d = x_prompt.shape
    bs, ts, _ = x_sample.shape
    depth = w_in.shape[0]
    s5_w = s5_d.shape[1]
    groups, s5_n = s5_lambda_re.shape[1:]
    gla_qk = w_gla_a2.shape[2]
    gla_rank = w_gla_a2.shape[1]
    gla_heads, gla_dk, gla_dv = state_gla.shape[2:]
    gla_v = gla_heads * gla_dv
    swa_heads = swa_sinks.shape[1]
    swa_q = swa_heads * SWA_HD
    swa_kv = SWA_KV_HEADS * SWA_HD
    assert SWA_KV_HEADS * SWA_HD == LANES and bp % SUBLANES == 0 and bs % SUBLANES == 0 and SUBLANES % ts == 0
    assert tp % WINDOW == 0 and cache_swa_k.shape[2] == WINDOW

    o_a, o_b = 0, s5_w
    o_low = o_b + 2 * gla_qk + 2 * gla_v
    o_c = o_low + gla_rank
    o_g = o_c + swa_q + 2 * swa_kv
    widths = (s5_w, 2 * gla_qk + 2 * gla_v + LANES, swa_q + 2 * swa_kv, 3 * d)

    hp = x_prompt.reshape(bp * tp, d)
    hs = x_sample.reshape(bs * ts, d)
    inv = ROPE_THETA ** (-jnp.arange(SWA_HD // 2, dtype=F32) * (2.0 / SWA_HD))
    inv = jnp.tile(inv, 2 * LANES // SWA_HD).reshape(1, LANES)
    cos_p, sin_p = _rope_table(jnp.arange(tp, dtype=jnp.int32), inv)
    cos_s, sin_s = _rope_table(PAST_LEN + (jnp.arange(SUBLANES, dtype=jnp.int32) & (ts - 1)), inv)
    zeros_s5 = jnp.zeros((1, bp, groups * s5_n), F32)
    tm_p, tm_s = _tile(bp * tp, DENSE_ROWS), _tile(bs * ts, DENSE_ROWS)
    s5_steps = _tile(tp, S5_STEPS)
    mix_rows = WINDOW * _tile(tp // WINDOW, MIX_WINDOWS)
    assert mix_rows % GLA_CHUNK == 0

    w_t = jnp.swapaxes(w_in, 1, 2).astype(BF16)
    wa2 = jnp.concatenate([w_gla_a2, jnp.zeros((depth, LANES - gla_rank, gla_qk), F32)], axis=1).astype(BF16)
    ba = b_gla_a.reshape(depth, 1, gla_qk)
    gn = g_gla_norm.reshape(depth, 1, gla_dv)
    ar, ai, bbr, bbi = _s5_prep(s5_lambda_re, s5_lambda_im, s5_log_dt, s5_b_re, s5_b_im)
    bre, bim, cre, cim = _s5_pack(bbr, bbi, s5_c_re, s5_c_im)
    s5_d_r, b_glu_r = s5_d.reshape(depth, 1, s5_w), b_glu.reshape(depth, 1, s5_w)
    h0r_s = state_s5_re.reshape(depth, bs, groups * s5_n)
    h0i_s = state_s5_im.reshape(depth, bs, groups * s5_n)
    g_mix_r, g_ffn_r, gf = g_mix.reshape(depth, 1, d), g_ffn.reshape(depth, 1, d), g_final.reshape(1, d)
    later = (w_glu, w_branch, w_out, w_ffn_gate, w_ffn_up, w_ffn_down)
    cast_rows = (bp * tp // tm_p) * 2 * SUBLANES
    cast_in_call = all((w.shape[0] * w.shape[1]) % cast_rows == 0 for w in later)
    casted = None if cast_in_call else [w.astype(BF16) for w in later]
    sink_cols = jnp.repeat(swa_sinks, SUBLANES, axis=1).reshape(depth, swa_heads * SUBLANES, 1)
    to_lanes = lambda c: jnp.transpose(c, (0, 1, 3, 4, 2))
    from_lanes = lambda c: jnp.transpose(c, (0, 1, 4, 2, 3))
    ck, cv = to_lanes(cache_swa_k), to_lanes(cache_swa_v)

    outs_p = [[] for _ in range(5)]
    outs_s = [[], []]
    gla_s, nk_s, nv_s = [], [], []
    for l in range(depth):
        last = l == depth - 1

        (pa, pb, pc, pg), new_casts = _inproj(hp, g_mix_r, w_t, o_c, l, widths, tm_p,
                                              later if casted is None else ())
        if casted is None:
            casted = new_casts
        wglu_b, wb, wo, wg, wu, wd = casted
        s5_params = (ar, ai, bre, bim, cre, cim, s5_d_r, wglu_b, b_glu_r)
        ya, hr, hi = _s5(pa.reshape(bp, tp, s5_w), zeros_s5, zeros_s5, 0, s5_params, l, s5_steps)
        ya = ya.reshape(bp * tp, s5_w)
        yb, sg, yc, nk, nv = _gla_swa_prompt(pb, pc, wa2, ba, gn, swa_sinks, cos_p, sin_p, l, bp, gla_heads, gla_dk,
                                             gla_dv, swa_heads, mix_rows)
        hp = _merge_ffn(hp, ya, yb, yc, pg, wb, wo, g_ffn_r, wg, wu, wd, gf, l, tm_p, last)
        for lst, val in zip(outs_p, (hr.reshape(bp, groups, s5_n), hi.reshape(bp, groups, s5_n), sg,
                                     nk.reshape(bp, WINDOW, SWA_KV_HEADS, SWA_HD),
                                     nv.reshape(bp, WINDOW, SWA_KV_HEADS, SWA_HD))):
            lst.append(val)

        (pa, pb, pc, pg), _ = _inproj(hs, g_mix_r, w_t, o_c, l, widths, tm_s)
        ya, hr, hi = _s5(pa, h0r_s, h0i_s, l, s5_params, l, ts)
        yb, sg = _gla_sample(pb, state_gla, wa2, ba, gn, l, ts, _tile(bs, GLA_SAMPLE_SEQS),
                             (gla_s,) if last else ())
        yc, nk, nv = _swa_sample(pc, ck, cv, sink_cols, cos_s, sin_s, l, swa_heads, ts, _tile(bs, SWA_SAMPLE_SEQS),
                                 (nk_s, nv_s) if last else ())
        gla_s, nk_s, nv_s = gla_s + [sg], nk_s + [nk], nv_s + [nv]
        hs = _merge_ffn(hs, ya, yb, yc, pg, wb, wo, g_ffn_r, wg, wu, wd, gf, l, tm_s, last)
        for lst, val in zip(outs_s, (hr.reshape(bs, groups, s5_n), hi.reshape(bs, groups, s5_n))):
            lst.append(val)

    layered = lambda a: a if a.ndim == ck.ndim else a[None]
    outs_s = [jnp.stack(o) for o in outs_s] + [layered(gla_s[-1]), from_lanes(layered(nk_s[-1])),
                                               from_lanes(layered(nv_s[-1]))]
    return (hp.reshape(bp, tp, d), hs.reshape(bs, ts, d), *[jnp.stack(o) for o in outs_p], *outs_s)
```

```python
import functools

import jax
import jax.numpy as jnp
from jax import lax
from jax.experimental import pallas as pl
from jax.experimental.pallas import tpu as pltpu

F32 = jnp.float32
BF16 = jnp.bfloat16

EPS = 1e-6
GLA_TAU = 16.0
GLA_CHUNK = 64
SWA_KV_HEADS = 2
SWA_HD = 64
WINDOW = 128
ROPE_THETA = 10000.0
PAST_LEN = 16384

SUBLANES = 8
LANES = 128
VMEM_LIMIT = 56 * 1024 * 1024

NT_DIMS = (((1,), (1,)), ((), ()))
TN_DIMS = (((0,), (0,)), ((), ()))


def _params(*sem):
    return pltpu.CompilerParams(dimension_semantics=sem, vmem_limit_bytes=VMEM_LIMIT)


def _dot(a, b):
    return jnp.dot(a, b, preferred_element_type=F32)


def _rms(x, g):
    return x * lax.rsqrt(jnp.mean(x * x, axis=-1, keepdims=True) + EPS) * g


def _full(shape):
    nd = len(shape)
    return pl.BlockSpec(shape, lambda *_: (0,) * nd)


def _layer(a, l):
    nd = a.ndim - 1
    return pl.BlockSpec((None,) + a.shape[1:], lambda *_: (l,) + (0,) * nd, pipeline_mode=pl.Buffered(1))


def _inproj_kernel(x_ref, g_ref, wa_ref, wb_ref, *refs, n_cast):
    cast_in, o_refs, cast_out = refs[:n_cast], refs[n_cast:n_cast + 4], refs[n_cast + 4:]
    xb = _rms(x_ref[...], g_ref[...]).astype(BF16)
    for w_ref, group in ((wa_ref, o_refs[:2]), (wb_ref, o_refs[2:])):
        off = 0
        for o_ref in group:
            n = o_ref.shape[-1]
            o_ref[...] = lax.dot_general(xb, w_ref[off:off + n, :], NT_DIMS, preferred_element_type=F32)
            off += n
    for src, dst in zip(cast_in, cast_out):
        dst[...] = src[...].astype(BF16)


def _inproj(x, g, w, row_b, l, widths, tm, to_cast=()):
    n, d = x.shape
    steps = n // tm
    rows_a, rows_b = sum(widths[:2]), sum(widths[2:])
    assert row_b + rows_b == w.shape[1] and row_b % (2 * SUBLANES) == 0 and rows_a <= w.shape[1]
    w_rows = lambda start, size: pl.BlockSpec((None, pl.Element(size), pl.Element(d)), lambda i: (l, start, 0),
                                              pipeline_mode=pl.Buffered(1))
    flat = [a.reshape(-1, a.shape[-1]) for a in to_cast]
    assert all(f.shape[0] % (steps * 2 * SUBLANES) == 0 for f in flat)
    slice_spec = lambda f: pl.BlockSpec((f.shape[0] // steps, f.shape[1]), lambda i: (i, 0))
    outs = pl.pallas_call(
        functools.partial(_inproj_kernel, n_cast=len(flat)),
        grid=(steps,),
        in_specs=[pl.BlockSpec((tm, d), lambda i: (i, 0)), _layer(g, l), w_rows(0, rows_a), w_rows(row_b, rows_b)]
        + [slice_spec(f) for f in flat],
        out_specs=[pl.BlockSpec((tm, wd), lambda i: (i, 0)) for wd in widths] + [slice_spec(f) for f in flat],
        out_shape=[jax.ShapeDtypeStruct((n, wd), F32) for wd in widths]
        + [jax.ShapeDtypeStruct(f.shape, BF16) for f in flat],
        compiler_params=_params("parallel"),
        name="inproj",
    )(x, g, w, w, *flat)
    return outs[:4], [o.reshape(a.shape) for o, a in zip(outs[4:], to_cast)]


def _s5_prep_kernel(lr_ref, li_ref, ldt_ref, br_ref, bi_ref, ar_ref, ai_ref, bbr_ref, bbi_ref):
    lr, li = lr_ref[...], li_ref[...]
    dt = jnp.exp(ldt_ref[...])
    mag = jnp.exp(lr * dt)
    ar = mag * jnp.cos(li * dt)
    ai = mag * jnp.sin(li * dt)
    den = lr * lr + li * li
    zr, zi = ar - 1.0, ai
    er = (zr * lr + zi * li) / den
    ei = (zi * lr - zr * li) / den
    ar_ref[...] = ar
    ai_ref[...] = ai
    br, bi = br_ref[...], bi_ref[...]
    bbr_ref[...] = er[None] * br - ei[None] * bi
    bbi_ref[...] = er[None] * bi + ei[None] * br


def _s5_prep(lam_re, lam_im, log_dt, b_re, b_im):
    depth, g, c, n = b_re.shape
    dg = depth * g
    brt = jnp.transpose(b_re.reshape(dg, c, n), (1, 0, 2))
    bit = jnp.transpose(b_im.reshape(dg, c, n), (1, 0, 2))
    ar, ai, bbr, bbi = pl.pallas_call(
        _s5_prep_kernel,
        out_shape=[jax.ShapeDtypeStruct((dg, n), F32)] * 2 + [jax.ShapeDtypeStruct((c, dg, n), F32)] * 2,
        name="s5_prep",
    )(lam_re.reshape(dg, n), lam_im.reshape(dg, n), log_dt.reshape(dg, 1), brt, bit)
    unt = lambda a: jnp.transpose(a, (1, 0, 2)).reshape(depth, g, c, n)
    return ar.reshape(depth, 1, g * n), ai.reshape(depth, 1, g * n), unt(bbr), unt(bbi)


def _s5_pack(bbr, bbi, c_re, c_im):
    depth, g, c, n = bbr.shape
    gp = LANES // c
    ks = g // gp
    eye = jnp.eye(gp, dtype=F32)

    def pack_b(b):
        return jnp.einsum("lkgcn,gh->lkgchn", b.reshape(depth, ks, gp, c, n), eye).reshape(
            depth, ks, gp * c, gp * n).astype(BF16)

    def pack_c(cm):
        return jnp.einsum("lkgnc,gh->lkgnhc", cm.reshape(depth, ks, gp, n, c), eye).reshape(
            depth, ks, gp * n, gp * c).astype(BF16)

    return pack_b(bbr), pack_b(bbi), pack_c(c_re), pack_c(c_im)


S5_COL_BLOCK = 1024
S5_UNROLL = 8


def _s5_kernel(u_ref, h0r_ref, h0i_ref, ar_ref, ai_ref, bre_ref, bim_ref, cre_ref, cim_ref, d_ref, wglu_ref,
               bglu_ref, y_ref, hr_out, hi_out, us_s, ys_s, xr_s, xi_s, hr_s, hi_s, *tmp_s, nb, steps):
    c = pl.program_id(0)

    @pl.when(c == 0)
    def _():
        hr_s[...] = h0r_ref[...]
        hi_s[...] = h0i_ref[...]

    ks, kw, nw = bre_ref.shape
    by_sequence = u_ref.ndim == 3
    if by_sequence:
        for b in range(nb):
            for k in range(ks):
                us_s[k, pl.ds(b, steps, stride=nb), :] = u_ref[b, :, k * kw:(k + 1) * kw]
    else:
        for k in range(ks):
            tmp_s[0][k] = u_ref[:, k * kw:(k + 1) * kw]
        for t in range(steps):
            for k in range(ks):
                us_s[k, t * nb:(t + 1) * nb, :] = tmp_s[0][k, pl.ds(t, nb, stride=steps), :]

    def x_proj(k):
        uk = us_s[k].astype(BF16)
        xr_s[:, k * nw:(k + 1) * nw] = _dot(uk, bre_ref[k])
        xi_s[:, k * nw:(k + 1) * nw] = _dot(uk, bim_ref[k])

    def y_proj(k):
        ss = slice(k * nw, (k + 1) * nw)
        yk = _dot(xr_s[:, ss].astype(BF16), cre_ref[k]) - _dot(xi_s[:, ss].astype(BF16), cim_ref[k])
        ys_s[k] = jax.nn.gelu(yk + d_ref[:, k * kw:(k + 1) * kw] * us_s[k])

    def scan(rb, cs, unrolled):
        rows = slice(rb * SUBLANES, (rb + 1) * SUBLANES)
        width = cs.stop - cs.start
        a_r = jnp.broadcast_to(ar_ref[:, cs], (SUBLANES, width))
        a_i = jnp.broadcast_to(ai_ref[:, cs], (SUBLANES, width))

        def step(t, carry):
            hr, hi = carry
            r0 = t * nb + rb * SUBLANES
            tr = slice(r0, r0 + SUBLANES) if unrolled else pl.ds(pl.multiple_of(r0, SUBLANES), SUBLANES)
            nhr = a_r * hr - a_i * hi + xr_s[tr, cs]
            nhi = a_r * hi + a_i * hr + xi_s[tr, cs]
            xr_s[tr, cs] = nhr
            xi_s[tr, cs] = nhi
            return nhr, nhi

        carry = (hr_s[rows, cs], hi_s[rows, cs])
        if unrolled:
            for t in range(steps):
                carry = step(t, carry)
        else:
            carry = lax.fori_loop(0, steps, step, carry, unroll=min(steps, S5_UNROLL))
        hr_s[rows, cs], hi_s[rows, cs] = carry

    if by_sequence and nb == SUBLANES:
        x_proj(0)
        for k in range(ks):
            if k + 1 < ks:
                x_proj(k + 1)
            if k >= 1:
                y_proj(k - 1)
            scan(0, slice(k * nw, (k + 1) * nw), True)
        y_proj(ks - 1)
    else:
        for k in range(ks):
            x_proj(k)
        for rb in range(nb // SUBLANES):
            for cb in range(xr_s.shape[1] // S5_COL_BLOCK):
                scan(rb, slice(cb * S5_COL_BLOCK, (cb + 1) * S5_COL_BLOCK), False)
        for k in range(ks):
            y_proj(k)
    y = jnp.concatenate([ys_s[k] for k in range(ks)], axis=1)
    y = y * jax.nn.sigmoid(_dot(y.astype(BF16), wglu_ref[...]) + bglu_ref[...])
    for k in range(ks):
        ys_s[k] = y[:, k * kw:(k + 1) * kw]
    if by_sequence:
        for b in range(nb):
            for k in range(ks):
                y_ref[b, :, k * kw:(k + 1) * kw] = ys_s[k, pl.ds(b, steps, stride=nb), :]
    else:
        for t in range(steps):
            for k in range(ks):
                tmp_s[0][k, pl.ds(t, nb, stride=steps), :] = ys_s[k, t * nb:(t + 1) * nb, :]
        for k in range(ks):
            y_ref[:, k * kw:(k + 1) * kw] = tmp_s[0][k]

    @pl.when(c == pl.num_programs(0) - 1)
    def _():
        hr_out[...] = hr_s[...]
        hi_out[...] = hi_s[...]


def _s5(u, h0r, h0i, l0, params, l, steps):
    _, nb, ns = h0r.shape
    w = u.shape[-1]
    blk = steps * nb
    ks = params[2].shape[1]
    if u.ndim == 3:
        grid = (u.shape[1] // steps,)
        u_spec = pl.BlockSpec((nb, steps, w), lambda i: (0, i, 0))
        tmp = []
    else:
        assert u.shape[0] == blk
        grid = (1,)
        u_spec = _full(u.shape)
        tmp = [pltpu.VMEM((ks, blk, LANES), F32)]
    kern = functools.partial(_s5_kernel, nb=nb, steps=steps)
    return pl.pallas_call(
        kern,
        grid=grid,
        in_specs=[u_spec, _layer(h0r, l0), _layer(h0i, l0)] + [_layer(a, l) for a in params],
        out_specs=[u_spec, _full((nb, ns)), _full((nb, ns))],
        out_shape=[jax.ShapeDtypeStruct(u.shape, F32), jax.ShapeDtypeStruct((nb, ns), F32),
                   jax.ShapeDtypeStruct((nb, ns), F32)],
        scratch_shapes=[pltpu.VMEM((ks, blk, LANES), F32), pltpu.VMEM((ks, blk, LANES), F32),
                        pltpu.VMEM((blk, ns), F32), pltpu.VMEM((blk, ns), F32), pltpu.VMEM((nb, ns), F32),
                        pltpu.VMEM((nb, ns), F32)] + tmp,
        compiler_params=_params("arbitrary"),
        name="s5_scan",
    )(u, h0r, h0i, *params)


def _log2(n):
    assert n & (n - 1) == 0, n
    return n.bit_length() - 1


def _chunk_masks(rows, chunk):
    sh = _log2(chunk)
    rr = lax.broadcasted_iota(jnp.int32, (rows, rows), 0)
    cc = lax.broadcasted_iota(jnp.int32, (rows, rows), 1)
    same = lax.shift_right_logical(rr, sh) == lax.shift_right_logical(cc, sh)
    return same, same & (cc <= rr)


def _gla_prologue(p_ref, wa_ref, ba_ref, chunk, qk):
    rows = p_ref.shape[0]
    q = p_ref[:, 0:qk]
    k = p_ref[:, qk:2 * qk]
    alow = p_ref[:, p_ref.shape[1] - LANES:].astype(BF16)
    z = _dot(alow, wa_ref[...]) + ba_ref[...]
    la = (jnp.minimum(z, 0.0) - jnp.log(1.0 + jnp.exp(-jnp.abs(z)))) * (1.0 / GLA_TAU)
    same, causal = _chunk_masks(rows, chunk)
    tri = jnp.where(causal, 1.0, 0.0).astype(BF16)
    blk = jnp.where(same, 1.0, 0.0).astype(BF16)
    la_hi = la.astype(BF16)
    la_lo = (la - la_hi.astype(F32)).astype(BF16)
    b = _dot(tri, la_hi) + _dot(tri, la_lo)
    bl = _dot(blk, la_hi) + _dot(blk, la_lo)
    return q, k, b, bl


def _gla_out(o, gn, r):
    on = o * lax.rsqrt(jnp.mean(o * o, axis=-1, keepdims=True) + EPS) * gn
    return on * (r * jax.nn.sigmoid(r))


def _gla_prompt_init(st_s):
    @pl.when(pl.program_id(1) == 0)
    def _():
        st_s[...] = jnp.zeros_like(st_s)


def _gla_prompt_final(s_ref, st_s, *, heads, dk):
    @pl.when(pl.program_id(1) == pl.num_programs(1) - 1)
    def _():
        for h in range(heads):
            s_ref[0, h] = st_s[:, h * dk:(h + 1) * dk].T


def _gla_prompt_body(p_ref, wa_ref, ba_ref, gn_ref, y_ref, st_s, *, heads, dk, dv, chunk):
    rows = p_ref.shape[0]
    qk, vw = heads * dk, heads * dv
    q, k, b, bl = _gla_prologue(p_ref, wa_ref, ba_ref, chunk, qk)
    v_off, r_off = 2 * qk, 2 * qk + vw
    qd = (q * (dk ** -0.5) * jnp.exp(b)).astype(BF16)
    kd = (k * jnp.exp(-b)).astype(BF16)
    kst = (k * jnp.exp(bl - b)).astype(BF16)
    dec = jnp.exp(bl)
    gn = gn_ref[...]
    zero = jnp.zeros((), BF16)
    qk_head = lax.shift_right_logical(lax.broadcasted_iota(jnp.int32, (chunk, qk), 1), _log2(dk))
    v_head = lax.shift_right_logical(lax.broadcasted_iota(jnp.int32, (chunk, vw), 1), _log2(dv))
    rr = lax.broadcasted_iota(jnp.int32, (chunk, heads * chunk), 0)
    cc = lax.broadcasted_iota(jnp.int32, (chunk, heads * chunk), 1) & (chunk - 1)
    causal = cc <= rr
    nchunk = rows // chunk
    by_head = lambda x, head: jnp.concatenate([jnp.where(head == h, x, zero) for h in range(heads)], axis=0)
    att, ds_t, v_diag, qd_rows = [], [], [], []
    for ci in range(nchunk):
        rs = slice(ci * chunk, (ci + 1) * chunk)
        v_c = p_ref[rs, v_off:v_off + vw].astype(BF16)
        v_rows = jnp.concatenate([v_c[:, h * dv:(h + 1) * dv] for h in range(heads)], axis=0)
        v_diag.append(by_head(v_c, v_head))
        qd_rows.append(by_head(qd[rs], qk_head))
        att.append(lax.dot_general(qd[rs], by_head(kd[rs], qk_head), NT_DIMS, preferred_element_type=F32))
        ds_t.append(lax.dot_general(v_rows, by_head(kst[rs], qk_head), TN_DIMS, preferred_element_type=F32))
    yield
    o_intra = [_dot(jnp.where(causal, att[ci], 0.0).astype(BF16), v_diag[ci]) for ci in range(nchunk)]
    yield
    st = st_s[...]
    o_inter = []
    for ci in range(nchunk):
        o_inter.append(lax.dot_general(qd_rows[ci], st.astype(BF16), NT_DIMS, preferred_element_type=F32))
        st = dec[ci * chunk:ci * chunk + 1, :] * st + ds_t[ci]
    st_s[...] = st
    yield
    for ci in range(nchunk):
        rs = slice(ci * chunk, (ci + 1) * chunk)
        for h in range(heads):
            o = o_intra[ci][:, h * dv:(h + 1) * dv] + o_inter[ci][h * chunk:(h + 1) * chunk]
            r = p_ref[rs, r_off + h * dv:r_off + (h + 1) * dv]
            y_ref[rs, h * dv:(h + 1) * dv] = _gla_out(o, gn, r)


def _gla_sample_body(p_ref, s0_ref, wa_ref, ba_ref, gn_ref, y_ref, s_ref, *, heads, dk, dv, t):
    rows = p_ref.shape[0]
    qk, vw = heads * dk, heads * dv
    assert dv == LANES and qk % LANES == 0
    q, k, b, bl = _gla_prologue(p_ref, wa_ref, ba_ref, t, qk)
    v_off, r_off = 2 * qk, 2 * qk + vw
    qd = (q * (dk ** -0.5) * jnp.exp(b)).astype(BF16)
    kd = (k * jnp.exp(-b)).astype(BF16)
    kst = (k * jnp.exp(bl - b)).astype(BF16)
    dec = jnp.exp(bl)
    gn = gn_ref[...]
    per_tile = SUBLANES // t
    zero = jnp.zeros((), BF16)
    qk_head = lax.shift_right_logical(lax.broadcasted_iota(jnp.int32, (SUBLANES, qk), 1), _log2(dk))
    v_head = lax.shift_right_logical(lax.broadcasted_iota(jnp.int32, (SUBLANES, vw), 1), _log2(dv))
    by_head = lambda x, head: jnp.concatenate([jnp.where(head == h, x, zero) for h in range(heads)], axis=0)
    stack_seq = lax.shift_right_logical(
        lax.broadcasted_iota(jnp.int32, (heads * SUBLANES, 1), 0) & (SUBLANES - 1), _log2(t))
    rr = lax.broadcasted_iota(jnp.int32, (SUBLANES, heads * SUBLANES), 0)
    cc = lax.broadcasted_iota(jnp.int32, (SUBLANES, heads * SUBLANES), 1) & (SUBLANES - 1)
    sh = _log2(t)
    causal = (lax.shift_right_logical(rr, sh) == lax.shift_right_logical(cc, sh)) & (cc <= rr)
    att, o_state, v_diag = [], [], []
    for ti in range(rows // SUBLANES):
        rs = slice(ti * SUBLANES, (ti + 1) * SUBLANES)
        v_t = p_ref[rs, v_off:v_off + vw].astype(BF16)
        v_rows = jnp.concatenate([v_t[:, h * dv:(h + 1) * dv] for h in range(heads)], axis=0)
        v_diag.append(by_head(v_t, v_head))
        qd_rows, kst_rows = by_head(qd[rs], qk_head), by_head(kst[rs], qk_head)
        att.append(lax.dot_general(qd[rs], by_head(kd[rs], qk_head), NT_DIMS, preferred_element_type=F32))
        o = None
        for j in range(per_tile):
            bidx = ti * per_tile + j
            mine = stack_seq == j
            s0 = jnp.concatenate([s0_ref[bidx, h] for h in range(heads)], axis=0)
            oj = _dot(jnp.where(mine, qd_rows, zero), s0.astype(BF16))
            o = oj if o is None else o + oj
            ds = lax.dot_general(jnp.where(mine, kst_rows, zero), v_rows, TN_DIMS, preferred_element_type=F32)
            drow = dec[ti * SUBLANES + j * t:ti * SUBLANES + j * t + 1, :]
            dcol = jnp.concatenate([jnp.broadcast_to(drow[:, g * LANES:(g + 1) * LANES], (LANES, LANES)).T
                                    for g in range(qk // LANES)], axis=0)
            s_new = dcol * s0 + ds
            for h in range(heads):
                s_ref[bidx, h] = s_new[h * dk:(h + 1) * dk]
        o_state.append(o)
    yield
    for ti in range(rows // SUBLANES):
        rs = slice(ti * SUBLANES, (ti + 1) * SUBLANES)
        o_intra = _dot(jnp.where(causal, att[ti], 0.0).astype(BF16), v_diag[ti])
        for h in range(heads):
            o = o_intra[:, h * dv:(h + 1) * dv] + o_state[ti][h * SUBLANES:(h + 1) * SUBLANES]
            r = p_ref[rs, r_off + h * dv:r_off + (h + 1) * dv]
            y_ref[rs, h * dv:(h + 1) * dv] = _gla_out(o, gn, r)


def _take_turns(bodies):
    while bodies:
        bodies = [b for b in bodies if next(b, StopIteration) is not StopIteration]


def _first_half(shape):
    lane = lax.broadcasted_iota(jnp.int32, shape, 1)
    return (lane & (SWA_HD - 1)) < SWA_HD // 2


def _rope_table_kernel(pos_ref, inv_ref, cos_ref, sin_ref):
    ang = pos_ref[...] * inv_ref[...]
    s = jnp.sin(ang)
    cos_ref[...] = jnp.cos(ang)
    sin_ref[...] = jnp.where(_first_half(ang.shape), -s, s)


def _rope_table(pos, inv):
    rows = pos.shape[0]
    return pl.pallas_call(
        _rope_table_kernel,
        out_shape=[jax.ShapeDtypeStruct((rows, LANES), F32)] * 2,
        name="rope_table",
    )(pos.astype(F32).reshape(rows, 1), inv)


def _rope(x, cos, sin):
    half = SWA_HD // 2
    first_half = _first_half(cos.shape)
    out = []
    for g in range(x.shape[1] // LANES):
        xg = x[:, g * LANES:(g + 1) * LANES]
        swapped = jnp.where(first_half, pltpu.roll(xg, LANES - half, 1), pltpu.roll(xg, half, 1))
        out.append(xg * cos + swapped * sin)
    return out[0] if len(out) == 1 else jnp.concatenate(out, axis=1)


def _swa_prompt_init(prev_s):
    @pl.when(pl.program_id(1) == 0)
    def _():
        prev_s[...] = jnp.zeros_like(prev_s)


def _swa_prompt_body(sink_ref, cur_ref, cos_ref, sin_ref, y_ref, nk_ref, nv_ref, prev_s, s_s, pc_s, *, heads, l):
    i = pl.program_id(1)
    w = WINDOW
    qw = heads * SWA_HD
    kvw = SWA_KV_HEADS * SWA_HD
    gq = heads // SWA_KV_HEADS
    nq = cur_ref.shape[0] // w
    cos, sin = cos_ref[...], sin_ref[...]
    q = (_rope(cur_ref[:, 0:qw], cos, sin) * (SWA_HD ** -0.5)).astype(BF16)
    k_cur = _rope(cur_ref[:, qw:qw + kvw], cos, sin)
    v_cur = cur_ref[:, qw + kvw:qw + 2 * kvw]
    new = [x.astype(BF16) for x in (k_cur, pltpu.roll(k_cur, SWA_HD, 1), v_cur, pltpu.roll(v_cur, SWA_HD, 1))]
    k2, k2r, v2, v2r = [jnp.concatenate([prev_s[j], new[j]], axis=0) for j in range(4)]
    zero = jnp.zeros((), BF16)
    lo2 = lax.broadcasted_iota(jnp.int32, ((nq + 1) * w, LANES), 1) < SWA_HD
    lo = lax.broadcasted_iota(jnp.int32, (w, LANES), 1) < SWA_HD
    k_both = (jnp.where(lo2, k2, k2r), jnp.where(lo2, k2r, k2))
    v_half = {(0, 0): jnp.where(lo2, v2, zero), (1, 1): jnp.where(lo2, zero, v2),
              (0, 1): jnp.where(lo2, zero, v2r), (1, 0): jnp.where(lo2, v2r, zero)}
    upper = lax.broadcasted_iota(jnp.int32, (w, w), 1) > lax.broadcasted_iota(jnp.int32, (w, w), 0)
    no_prev = jnp.where(i > 0, 0.0, -jnp.inf)
    slot = lambda c, h: slice((c * heads + h) * w, (c * heads + h + 1) * w)
    for c in range(nq):
        for kv in range(SWA_KV_HEADS):
            qm = []
            for h in range(kv * gq, (kv + 1) * gq):
                qg = q[c * w:(c + 1) * w, (h // 2) * LANES:(h // 2 + 1) * LANES]
                qm.append(jnp.where(lo, qg, zero) if h % 2 == 0 else jnp.where(lo, zero, qg))
            s_s[(c * heads + kv * gq) * w:(c * heads + (kv + 1) * gq) * w, :] = lax.dot_general(
                jnp.concatenate(qm, axis=0), k_both[kv][c * w:(c + 2) * w], NT_DIMS,
                preferred_element_type=F32)
    yield
    sink_term = {}
    for c in range(nq):
        for h in range(heads):
            s_prev = s_s[slot(c, h), 0:w] + no_prev if c == 0 else s_s[slot(c, h), 0:w]
            sp = jnp.where(upper, s_prev, s_s[slot(c, h), w:2 * w])
            sink = sink_ref[l, h]
            m = jnp.maximum(jnp.max(sp, axis=-1, keepdims=True), sink)
            p = jnp.exp(sp - m)
            pc_s[slot(c, h), 0:w] = jnp.where(upper, p, 0.0).astype(BF16)
            pc_s[slot(c, h), w:2 * w] = jnp.where(upper, 0.0, p).astype(BF16)
            sink_term[c, h] = jnp.exp(sink - m)
    yield
    num = {(c, h): _dot(pc_s[slot(c, h), :], v_half[(h // gq, h % 2)][c * w:(c + 2) * w])
           for c in range(nq) for h in range(heads)}
    den_all = _dot(pc_s[...], jnp.ones((2 * w, LANES), BF16))
    yield
    for c in range(nq):
        for g in range(heads // 2):
            he, ho = 2 * g, 2 * g + 1
            den = jnp.where(lo, den_all[slot(c, he)] + sink_term[c, he], den_all[slot(c, ho)] + sink_term[c, ho])
            y_ref[c * w:(c + 1) * w, g * LANES:(g + 1) * LANES] = (num[c, he] + num[c, ho]) * (1.0 / den)
    for j in range(4):
        prev_s[j] = new[j][(nq - 1) * w:]
    nk_ref[0] = k_cur[(nq - 1) * w:]
    nv_ref[0] = v_cur[(nq - 1) * w:]


def _gla_swa_prompt_kernel(pb_ref, wa_ref, ba_ref, gn_ref, sink_ref, pc_ref, cos_ref, sin_ref,
                           yb_ref, s_ref, yc_ref, nk_ref, nv_ref, st_s, prev_s, s_s, pc_s, *, gla, swa):
    _gla_prompt_init(st_s)
    _swa_prompt_init(prev_s)
    _take_turns([_gla_prompt_body(pb_ref, wa_ref, ba_ref, gn_ref, yb_ref, st_s, chunk=GLA_CHUNK, **gla),
                 _swa_prompt_body(sink_ref, pc_ref, cos_ref, sin_ref, yc_ref, nk_ref, nv_ref, prev_s, s_s, pc_s,
                                  **swa)])
    _gla_prompt_final(s_ref, st_s, heads=gla["heads"], dk=gla["dk"])


def _gla_swa_prompt(pb, pc, wa, ba, gn, sinks, cos, sin, l, bsz, gla_heads, dk, dv, swa_heads, rows):
    n, pbw = pb.shape
    pcw = pc.shape[1]
    nblk = n // bsz // rows
    nq = rows // WINDOW
    qw, kvw = swa_heads * SWA_HD, SWA_KV_HEADS * SWA_HD
    kern = functools.partial(_gla_swa_prompt_kernel, gla=dict(heads=gla_heads, dk=dk, dv=dv),
                             swa=dict(heads=swa_heads, l=l))
    row_blk = lambda wdt: pl.BlockSpec((rows, wdt), lambda b, i: (b * nblk + i, 0))
    tab = pl.BlockSpec((rows, LANES), lambda b, i: (i, 0))
    win = pl.BlockSpec((1, WINDOW, kvw), lambda b, i: (b, 0, 0))
    return pl.pallas_call(
        kern,
        grid=(bsz, nblk),
        in_specs=[row_blk(pbw), _layer(wa, l), _layer(ba, l), _layer(gn, l),
                  pl.BlockSpec(memory_space=pltpu.SMEM), row_blk(pcw), tab, tab],
        out_specs=[row_blk(gla_heads * dv), pl.BlockSpec((1, gla_heads, dk, dv), lambda b, i: (b, 0, 0, 0)),
                   row_blk(qw), win, win],
        out_shape=[jax.ShapeDtypeStruct((n, gla_heads * dv), F32),
                   jax.ShapeDtypeStruct((bsz, gla_heads, dk, dv), F32),
                   jax.ShapeDtypeStruct((n, qw), F32), jax.ShapeDtypeStruct((bsz, WINDOW, kvw), F32),
                   jax.ShapeDtypeStruct((bsz, WINDOW, kvw), F32)],
        scratch_shapes=[pltpu.VMEM((dv, gla_heads * dk), F32), pltpu.VMEM((4, WINDOW, LANES), BF16),
                        pltpu.VMEM((nq * swa_heads * WINDOW, 2 * WINDOW), F32),
                        pltpu.VMEM((nq * swa_heads * WINDOW, 2 * WINDOW), BF16)],
        compiler_params=_params("parallel", "arbitrary"),
        name="gla_swa_prompt",
    )(pb, wa, ba, gn, sinks, pc, cos, sin)


def _swa_sample_body(sinkcol_ref, p_ref, ck_ref, cv_ref, cos_ref, sin_ref, y_ref, nk_ref, nv_ref, *, heads, t, nb):
    w = ck_ref.shape[3]
    qw = heads * SWA_HD
    kvw = SWA_KV_HEADS * SWA_HD
    gq = heads // SWA_KV_HEADS
    per_tile = SUBLANES // t
    nrow = heads * SUBLANES
    cos, sin = cos_ref[...], sin_ref[...]
    lo = lax.broadcasted_iota(jnp.int32, (SUBLANES, LANES), 1) < SWA_HD
    row_r = lax.broadcasted_iota(jnp.int32, (nrow, 1), 0) & (SUBLANES - 1)
    row_seq = lax.shift_right_logical(row_r, _log2(t))
    row_step = row_r & (t - 1)
    cache_ok = lax.broadcasted_iota(jnp.int32, (nrow, w), 1) > row_step
    new_r = lax.broadcasted_iota(jnp.int32, (nrow, SUBLANES), 1)
    new_ok = (lax.shift_right_logical(new_r, _log2(t)) == row_seq) & ((new_r & (t - 1)) <= row_step)
    sink_col = sinkcol_ref[...]
    lane_w = lax.broadcasted_iota(jnp.int32, (SWA_HD, w), 1)
    flat = lambda c: jnp.concatenate([c[kv] for kv in range(SWA_KV_HEADS)], axis=0).astype(BF16)
    pad_rows = jnp.zeros((LANES - 2 * SUBLANES, kvw), F32)

    def on_kv_half(x, h, back=False):
        own, kv = h % 2, h // gq
        moved = x if own == kv else pltpu.roll(x, SWA_HD, 1)
        keep = own if back else kv
        return jnp.where(lo, moved, 0.0) if keep == 0 else jnp.where(lo, 0.0, moved)

    new_k, new_v, scores = [], [], []
    for ti in range(nb // per_tile):
        rs = slice(ti * SUBLANES, (ti + 1) * SUBLANES)
        q8 = _rope(p_ref[rs, 0:qw], cos, sin) * (SWA_HD ** -0.5)
        kn8 = _rope(p_ref[rs, qw:qw + kvw], cos, sin)
        new_k.append(kn8)
        new_v.append(p_ref[rs, qw + kvw:qw + 2 * kvw])
        qm = jnp.concatenate([on_kv_half(q8[:, (h // 2) * LANES:(h // 2 + 1) * LANES], h) for h in range(heads)],
                             axis=0).astype(BF16)
        s_c = None
        for j in range(per_tile):
            s_j = _dot(qm, flat(ck_ref[ti * per_tile + j]))
            s_c = s_j if s_c is None else jnp.where(row_seq == j, s_j, s_c)
        s_n = lax.dot_general(qm, kn8.astype(BF16), NT_DIMS, preferred_element_type=F32)
        scores.append((s_c, s_n))
    yield
    probs = []
    for s_c, s_n in scores:
        s_c = jnp.where(cache_ok, s_c, -jnp.inf)
        s_n = jnp.where(new_ok, s_n, -jnp.inf)
        m = jnp.maximum(jnp.maximum(jnp.max(s_c, axis=1, keepdims=True), jnp.max(s_n, axis=1, keepdims=True)),
                        sink_col)
        p_c = jnp.exp(s_c - m)
        p_n = jnp.exp(s_n - m)
        den = jnp.sum(p_c, axis=1, keepdims=True) + jnp.sum(p_n, axis=1, keepdims=True) + jnp.exp(sink_col - m)
        rden = 1.0 / den
        probs.append((p_c * rden, (p_n * rden).astype(BF16)))
    yield
    outs = []
    for ti, (p_c, p_n) in enumerate(probs):
        o = _dot(p_n, new_v[ti].astype(BF16))
        for j in range(per_tile):
            p_j = jnp.where(row_seq == j, p_c, 0.0).astype(BF16)
            o = o + lax.dot_general(p_j, flat(cv_ref[ti * per_tile + j]), NT_DIMS, preferred_element_type=F32)
        outs.append(o)
    yield
    for ti, o in enumerate(outs):
        for g in range(heads // 2):
            pair = [on_kv_half(o[h * SUBLANES:(h + 1) * SUBLANES], h, back=True) for h in (2 * g, 2 * g + 1)]
            y_ref[ti * SUBLANES:(ti + 1) * SUBLANES, g * LANES:(g + 1) * LANES] = pair[0] + pair[1]
    yield
    for ti in range(nb // per_tile):
        kv_t = jnp.concatenate([new_k[ti], new_v[ti], pad_rows], axis=0).T
        for j in range(per_tile):
            bidx = ti * per_tile + j
            for c_ref, n_ref, col0 in ((ck_ref, nk_ref, 0), (cv_ref, nv_ref, SUBLANES)):
                for kv in range(SWA_KV_HEADS):
                    tail = pltpu.roll(kv_t[kv * SWA_HD:(kv + 1) * SWA_HD, :], w - t - col0 - j * t, 1)
                    n_ref[bidx, kv] = jnp.where(lane_w < w - t, pltpu.roll(c_ref[bidx, kv], w - t, 1), tail)


def _layered_outputs(body, n_in, blocks, prev):
    n_prev = len(prev[0]) if prev else 0
    if not n_prev:
        kern = lambda *refs: _take_turns([body(*refs)])
        return (kern, [], [], [pl.BlockSpec(b, m) for b, m, _ in blocks],
                [jax.ShapeDtypeStruct(s, F32) for _, _, s in blocks])

    def kern(*refs):
        ins, earlier, outs = refs[:n_in], refs[n_in:n_in + len(blocks) * n_prev], refs[n_in + len(blocks) * n_prev:]
        stacked = outs[1:1 + len(blocks)]
        _take_turns([body(*ins, outs[0], *[s.at[n_prev] for s in stacked])])
        for k, s in enumerate(stacked):
            for ll in range(n_prev):
                s[ll] = earlier[k * n_prev + ll][...]

    in_specs = [pl.BlockSpec(b, m) for b, m, _ in blocks for _ in range(n_prev)]
    operands = [a for arrays in prev for a in arrays]
    out_specs = [pl.BlockSpec((n_prev + 1,) + b, lambda i, m=m: (0,) + m(i)) for b, m, _ in blocks]
    out_shapes = [jax.ShapeDtypeStruct((n_prev + 1,) + s, F32) for _, _, s in blocks]
    return kern, in_specs, operands, out_specs, out_shapes


def _gla_sample(p, s0, wa, ba, gn, l, t, nb, prev):
    n, pw = p.shape
    _, bsz, heads, dk, dv = s0.shape
    rows = nb * t
    body = functools.partial(_gla_sample_body, heads=heads, dk=dk, dv=dv, t=t)
    blocks = [((nb, heads, dk, dv), lambda i: (i, 0, 0, 0), s0.shape[1:])]
    kern, more_specs, more, st_specs, st_shapes = _layered_outputs(body, 5, blocks, prev)
    return pl.pallas_call(
        kern,
        grid=(bsz // nb,),
        in_specs=[pl.BlockSpec((rows, pw), lambda i: (i, 0)),
                  pl.BlockSpec((None, nb, heads, dk, dv), lambda i: (l, i, 0, 0, 0)),
                  _layer(wa, l), _layer(ba, l), _layer(gn, l)] + more_specs,
        out_specs=[pl.BlockSpec((rows, heads * dv), lambda i: (i, 0))] + st_specs,
        out_shape=[jax.ShapeDtypeStruct((n, heads * dv), F32)] + st_shapes,
        compiler_params=_params("parallel"),
        name="gla_sample",
    )(p, s0, wa, ba, gn, *more)


def _swa_sample(p, ck, cv, sink_cols, cos, sin, l, heads, t, nb, prev):
    n, pw = p.shape
    _, bsz, kvh, hd, w = ck.shape
    qw = heads * SWA_HD
    body = functools.partial(_swa_sample_body, heads=heads, t=t, nb=nb)
    blocks = [((nb, kvh, hd, w), lambda i: (i, 0, 0, 0), ck.shape[1:])] * 2
    kern, more_specs, more, c_specs, c_shapes = _layered_outputs(body, 6, blocks, prev)
    cache = pl.BlockSpec((None, nb, kvh, hd, w), lambda i: (l, i, 0, 0, 0))
    return pl.pallas_call(
        kern,
        grid=(bsz // nb,),
        in_specs=[_layer(sink_cols, l), pl.BlockSpec((nb * t, pw), lambda i: (i, 0)), cache, cache,
                  _full(cos.shape), _full(sin.shape)] + more_specs,
        out_specs=[pl.BlockSpec((nb * t, qw), lambda i: (i, 0))] + c_specs,
        out_shape=[jax.ShapeDtypeStruct((n, qw), F32)] + c_shapes,
        compiler_params=_params("parallel"),
        name="swa_sample",
    )(sink_cols, p, ck, cv, cos, sin, *more)


FFN_HIDDEN_CHUNK = 1024


MERGE_FFN_SPLIT = 2


def _merge_ffn_rows(rows, x_ref, ya_ref, yb_ref, yc_ref, gt_ref, wb_ref, wo_ref, g_ref, wg_ref, wu_ref, wd_ref,
                    gf_ref, o_ref, final_norm):
    d = x_ref.shape[1]
    merged = None
    off = 0
    for i, y_ref in enumerate((ya_ref, yb_ref, yc_ref)):
        wdt = y_ref.shape[1]
        br = _dot(y_ref[rows, :].astype(BF16), wb_ref[off:off + wdt, :])
        term = jax.nn.sigmoid(gt_ref[rows, i * d:(i + 1) * d]) * br
        merged = term if merged is None else merged + term
        off += wdt
    yield
    x = x_ref[rows, :] + _dot(merged.astype(BF16), wo_ref[...])
    yield
    hb = _rms(x, g_ref[...]).astype(BF16)
    hidden = wg_ref.shape[1]
    acc = x
    for c0 in range(0, hidden, FFN_HIDDEN_CHUNK):
        c1 = min(c0 + FFN_HIDDEN_CHUNK, hidden)
        gate = _dot(hb, wg_ref[:, c0:c1])
        up = _dot(hb, wu_ref[:, c0:c1])
        act = (gate * jax.nn.sigmoid(gate) * up).astype(BF16)
        acc = acc + _dot(act, wd_ref[c0:c1, :])
        yield
    o_ref[rows, :] = _rms(acc, gf_ref[...]) if final_norm else acc


def _merge_ffn_kernel(*refs, final_norm):
    tm = refs[0].shape[0]
    split = MERGE_FFN_SPLIT if tm % (MERGE_FFN_SPLIT * SUBLANES) == 0 else 1
    _take_turns([_merge_ffn_rows(slice(j * tm // split, (j + 1) * tm // split), *refs, final_norm)
                 for j in range(split)])


def _merge_ffn(x, ya, yb, yc, gates, wb, wo, g, wg, wu, wd, gf, l, tm, final_norm):
    n, d = x.shape
    row = lambda a: pl.BlockSpec((tm, a.shape[1]), lambda i: (i, 0))
    return pl.pallas_call(
        functools.partial(_merge_ffn_kernel, final_norm=final_norm),
        grid=(n // tm,),
        in_specs=[row(x), row(ya), row(yb), row(yc), row(gates), _layer(wb, l), _layer(wo, l), _layer(g, l),
                  _layer(wg, l), _layer(wu, l), _layer(wd, l), _full(gf.shape)],
        out_specs=pl.BlockSpec((tm, d), lambda i: (i, 0)),
        out_shape=jax.ShapeDtypeStruct((n, d), F32),
        compiler_params=_params("parallel"),
        name="merge_ffn",
    )(x, ya, yb, yc, gates, wb, wo, g, wg, wu, wd, gf)


DENSE_ROWS = 512
S5_STEPS = 128
MIX_WINDOWS = 4
GLA_SAMPLE_SEQS = 32
SWA_SAMPLE_SEQS = 16


def _tile(n, want):
    while n % want:
        want //= 2
    return want


def kernel(x_prompt, x_sample, state_s5_re, state_s5_im, state_gla, cache_swa_k, cache_swa_v, g_mix, w_in, s5_lambda_re, s5_lambda_im, s5_log_dt, s5_b_re, s5_b_im, s5_c_re, s5_c_im, s5_d, w_glu, b_glu, w_gla_a2, b_gla_a, g_gla_norm, swa_sinks, w_branch, w_out, g_ffn, w_ffn_gate, w_ffn_up, w_ffn_down, g_final):
    bp, tp, d = x_prompt.shape
    bs, ts, _ = x_sample.shape
    depth = w_in.shape[0]
    s5_w = s5_d.shape[1]
    groups, s5_n = s5_lambda_re.shape[1:]
    gla_qk = w_gla_a2.shape[2]
    gla_rank = w_gla_a2.shape[1]
    gla_heads, gla_dk, gla_dv = state_gla.shape[2:]
    gla_v = gla_heads * gla_dv
    swa_heads = swa_sinks.shape[1]
    swa_q = swa_heads * SWA_HD
    swa_kv = SWA_KV_HEADS * SWA_HD
    assert SWA_KV_HEADS * SWA_HD == LANES and bp % SUBLANES == 0 and bs % SUBLANES == 0 and SUBLANES % ts == 0
    assert tp % WINDOW == 0 and cache_swa_k.shape[2] == WINDOW

    o_a, o_b = 0, s5_w
    o_low = o_b + 2 * gla_qk + 2 * gla_v
    o_c = o_low + gla_rank
    o_g = o_c + swa_q + 2 * swa_kv
    widths = (s5_w, 2 * gla_qk + 2 * gla_v + LANES, swa_q + 2 * swa_kv, 3 * d)

    hp = x_prompt.reshape(bp * tp, d)
    hs = x_sample.reshape(bs * ts, d)
    inv = ROPE_THETA ** (-jnp.arange(SWA_HD // 2, dtype=F32) * (2.0 / SWA_HD))
    inv = jnp.tile(inv, 2 * LANES // SWA_HD).reshape(1, LANES)
    cos_p, sin_p = _rope_table(jnp.arange(tp, dtype=jnp.int32), inv)
    cos_s, sin_s = _rope_table(PAST_LEN + (jnp.arange(SUBLANES, dtype=jnp.int32) & (ts - 1)), inv)
    zeros_s5 = jnp.zeros((1, bp, groups * s5_n), F32)
    tm_p, tm_s = _tile(bp * tp, DENSE_ROWS), _tile(bs * ts, DENSE_ROWS)
    s5_steps = _tile(tp, S5_STEPS)
    mix_rows = WINDOW * _tile(tp // WINDOW, MIX_WINDOWS)
    assert mix_rows % GLA_CHUNK == 0

    w_t = jnp.swapaxes(w_in, 1, 2).astype(BF16)
    wa2 = jnp.concatenate([w_gla_a2, jnp.zeros((depth, LANES - gla_rank, gla_qk), F32)], axis=1).astype(BF16)
    ba = b_gla_a.reshape(depth, 1, gla_qk)
    gn = g_gla_norm.reshape(depth, 1, gla_dv)
    ar, ai, bbr, bbi = _s5_prep(s5_lambda_re, s5_lambda_im, s5_log_dt, s5_b_re, s5_b_im)
    bre, bim, cre, cim = _s5_pack(bbr, bbi, s5_c_re, s5_c_im)
    s5_d_r, b_glu_r = s5_d.reshape(depth, 1, s5_w), b_glu.reshape(depth, 1, s5_w)
    h0r_s = state_s5_re.reshape(depth, bs, groups * s5_n)
    h0i_s = state_s5_im.reshape(depth, bs, groups * s5_n)
    g_mix_r, g_ffn_r, gf = g_mix.reshape(depth, 1, d), g_ffn.reshape(depth, 1, d), g_final.reshape(1, d)
    later = (w_glu, w_branch, w_out, w_ffn_gate, w_ffn_up, w_ffn_down)
    cast_rows = (bp * tp // tm_p) * 2 * SUBLANES
    cast_in_call = all((w.shape[0] * w.shape[1]) % cast_rows == 0 for w in later)
    casted = None if cast_in_call else [w.astype(BF16) for w in later]
    sink_cols = jnp.repeat(swa_sinks, SUBLANES, axis=1).reshape(depth, swa_heads * SUBLANES, 1)
    to_lanes = lambda c: jnp.transpose(c, (0, 1, 3, 4, 2))
    from_lanes = lambda c: jnp.transpose(c, (0, 1, 4, 2, 3))
    ck, cv = to_lanes(cache_swa_k), to_lanes(cache_swa_v)

    outs_p = [[] for _ in range(5)]
    outs_s = [[], []]
    gla_s, nk_s, nv_s = [], [], []
    for l in range(depth):
        last = l == depth - 1

        (pa, pb, pc, pg), new_casts = _inproj(hp, g_mix_r, w_t, o_c, l, widths, tm_p,
                                              later if casted is None else ())
        if casted is None:
            casted = new_casts
        wglu_b, wb, wo, wg, wu, wd = casted
        s5_params = (ar, ai, bre, bim, cre, cim, s5_d_r, wglu_b, b_glu_r)
        ya, hr, hi = _s5(pa.reshape(bp, tp, s5_w), zeros_s5, zeros_s5, 0, s5_params, l, s5_steps)
        ya = ya.reshape(bp * tp, s5_w)
        yb, sg, yc, nk, nv = _gla_swa_prompt(pb, pc, wa2, ba, gn, swa_sinks, cos_p, sin_p, l, bp, gla_heads, gla_dk,
                                             gla_dv, swa_heads, mix_rows)
        hp = _merge_ffn(hp, ya, yb, yc, pg, wb, wo, g_ffn_r, wg, wu, wd, gf, l, tm_p, last)
        for lst, val in zip(outs_p, (hr.reshape(bp, groups, s5_n), hi.reshape(bp, groups, s5_n), sg,
                                     nk.reshape(bp, WINDOW, SWA_KV_HEADS, SWA_HD),
                                     nv.reshape(bp, WINDOW, SWA_KV_HEADS, SWA_HD))):
            lst.append(val)

        (pa, pb, pc, pg), _ = _inproj(hs, g_mix_r, w_t, o_c, l, widths, tm_s)
        ya, hr, hi = _s5(pa, h0r_s, h0i_s, l, s5_params, l, ts)
        yb, sg = _gla_sample(pb, state_gla, wa2, ba, gn, l, ts, _tile(bs, GLA_SAMPLE_SEQS),
                             (gla_s,) if last else ())
        yc, nk, nv = _swa_sample(pc, ck, cv, sink_cols, cos_s, sin_s, l, swa_heads, ts, _tile(bs, SWA_SAMPLE_SEQS),
                                 (nk_s, nv_s) if last else ())
        gla_s, nk_s, nv_s = gla_s + [sg], nk_s + [nk], nv_s + [nv]
        hs = _merge_ffn(hs, ya, yb, yc, pg, wb, wo, g_ffn_r, wg, wu, wd, gf, l, tm_s, last)
        for lst, val in zip(outs_s, (hr.reshape(bs, groups, s5_n), hi.reshape(bs, groups, s5_n))):
            lst.append(val)

    layered = lambda a: a if a.ndim == ck.ndim else a[None]
    outs_s = [jnp.stack(o) for o in outs_s] + [layered(gla_s[-1]), from_lanes(layered(nk_s[-1])),
                                               from_lanes(layered(nv_s[-1]))]
    return (hp.reshape(bp, tp, d), hs.reshape(bs, ts, d), *[jnp.stack(o) for o in outs_p], *outs_s)
```

```python
import functools

import jax
import jax.numpy as jnp
from jax import lax
from jax.experimental import pallas as pl
from jax.experimental.pallas import tpu as pltpu

F32 = jnp.float32
BF16 = jnp.bfloat16

EPS = 1e-6
GLA_TAU = 16.0
GLA_CHUNK = 64
SWA_KV_HEADS = 2
SWA_HD = 64
WINDOW = 128
ROPE_THETA = 10000.0
PAST_LEN = 16384

SUBLANES = 8
LANES = 128
VMEM_LIMIT = 56 * 1024 * 1024

NT_DIMS = (((1,), (1,)), ((), ()))
TN_DIMS = (((0,), (0,)), ((), ()))


def _params(*sem):
    return pltpu.CompilerParams(dimension_semantics=sem, vmem_limit_bytes=VMEM_LIMIT)


def _dot(a, b):
    return jnp.dot(a, b, preferred_element_type=F32)


def _rms(x, g):
    return x * lax.rsqrt(jnp.mean(x * x, axis=-1, keepdims=True) + EPS) * g


def _full(shape):
    nd = len(shape)
    return pl.BlockSpec(shape, lambda *_: (0,) * nd)


def _layer(a, l):
    nd = a.ndim - 1
    return pl.BlockSpec((None,) + a.shape[1:], lambda *_: (l,) + (0,) * nd, pipeline_mode=pl.Buffered(1))


def _inproj_kernel(x_ref, g_ref, wa_ref, wb_ref, *refs, n_cast):
    cast_in, o_refs, cast_out = refs[:n_cast], refs[n_cast:n_cast + 4], refs[n_cast + 4:]
    xb = _rms(x_ref[...], g_ref[...]).astype(BF16)
    for w_ref, group in ((wa_ref, o_refs[:2]), (wb_ref, o_refs[2:])):
        off = 0
        for o_ref in group:
            n = o_ref.shape[-1]
            o_ref[...] = lax.dot_general(xb, w_ref[off:off + n, :], NT_DIMS, preferred_element_type=F32)
            off += n
    for src, dst in zip(cast_in, cast_out):
        dst[...] = src[...].astype(BF16)


def _inproj(x, g, w, row_b, l, widths, tm, to_cast=()):
    n, d = x.shape
    steps = n // tm
    rows_a, rows_b = sum(widths[:2]), sum(widths[2:])
    assert row_b + rows_b == w.shape[1] and row_b % (2 * SUBLANES) == 0 and rows_a <= w.shape[1]
    w_rows = lambda start, size: pl.BlockSpec((None, pl.Element(size), pl.Element(d)), lambda i: (l, start, 0),
                                              pipeline_mode=pl.Buffered(1))
    flat = [a.reshape(-1, a.shape[-1]) for a in to_cast]
    assert all(f.shape[0] % (steps * 2 * SUBLANES) == 0 for f in flat)
    slice_spec = lambda f: pl.BlockSpec((f.shape[0] // steps, f.shape[1]), lambda i: (i, 0))
    outs = pl.pallas_call(
        functools.partial(_inproj_kernel, n_cast=len(flat)),
        grid=(steps,),
        in_specs=[pl.BlockSpec((tm, d), lambda i: (i, 0)), _layer(g, l), w_rows(0, rows_a), w_rows(row_b, rows_b)]
        + [slice_spec(f) for f in flat],
        out_specs=[pl.BlockSpec((tm, wd), lambda i: (i, 0)) for wd in widths] + [slice_spec(f) for f in flat],
        out_shape=[jax.ShapeDtypeStruct((n, wd), F32) for wd in widths]
        + [jax.ShapeDtypeStruct(f.shape, BF16) for f in flat],
        compiler_params=_params("parallel"),
        name="inproj",
    )(x, g, w, w, *flat)
    return outs[:4], [o.reshape(a.shape) for o, a in zip(outs[4:], to_cast)]


def _s5_prep_kernel(lr_ref, li_ref, ldt_ref, br_ref, bi_ref, ar_ref, ai_ref, bbr_ref, bbi_ref):
    lr, li = lr_ref[...], li_ref[...]
    dt = jnp.exp(ldt_ref[...])
    mag = jnp.exp(lr * dt)
    ar = mag * jnp.cos(li * dt)
    ai = mag * jnp.sin(li * dt)
    den = lr * lr + li * li
    zr, zi = ar - 1.0, ai
    er = (zr * lr + zi * li) / den
    ei = (zi * lr - zr * li) / den
    ar_ref[...] = ar
    ai_ref[...] = ai
    br, bi = br_ref[...], bi_ref[...]
    bbr_ref[...] = er[None] * br - ei[None] * bi
    bbi_ref[...] = er[None] * bi + ei[None] * br


def _s5_prep(lam_re, lam_im, log_dt, b_re, b_im):
    depth, g, c, n = b_re.shape
    dg = depth * g
    brt = jnp.transpose(b_re.reshape(dg, c, n), (1, 0, 2))
    bit = jnp.transpose(b_im.reshape(dg, c, n), (1, 0, 2))
    ar, ai, bbr, bbi = pl.pallas_call(
        _s5_prep_kernel,
        out_shape=[jax.ShapeDtypeStruct((dg, n), F32)] * 2 + [jax.ShapeDtypeStruct((c, dg, n), F32)] * 2,
        name="s5_prep",
    )(lam_re.reshape(dg, n), lam_im.reshape(dg, n), log_dt.reshape(dg, 1), brt, bit)
    unt = lambda a: jnp.transpose(a, (1, 0, 2)).reshape(depth, g, c, n)
    return ar.reshape(depth, 1, g * n), ai.reshape(depth, 1, g * n), unt(bbr), unt(bbi)


def _s5_pack(bbr, bbi, c_re, c_im):
    depth, g, c, n = bbr.shape
    gp = LANES // c
    ks = g // gp
    eye = jnp.eye(gp, dtype=F32)

    def pack_b(b):
        return jnp.einsum("lkgcn,gh->lkgchn", b.reshape(depth, ks, gp, c, n), eye).reshape(
            depth, ks, gp * c, gp * n).astype(BF16)

    def pack_c(cm):
        return jnp.einsum("lkgnc,gh->lkgnhc", cm.reshape(depth, ks, gp, n, c), eye).reshape(
            depth, ks, gp * n, gp * c).astype(BF16)

    return pack_b(bbr), pack_b(bbi), pack_c(c_re), pack_c(c_im)


S5_COL_BLOCK = 1024
S5_UNROLL = 8
S5_GLU_SPLIT = 4


def _s5_kernel(u_ref, h0r_ref, h0i_ref, ar_ref, ai_ref, bre_ref, bim_ref, cre_ref, cim_ref, d_ref, wglu_ref,
               bglu_ref, y_ref, hr_out, hi_out, us_s, ys_s, xr_s, xi_s, hr_s, hi_s, *tmp_s, nb, steps):
    c = pl.program_id(0)

    @pl.when(c == 0)
    def _():
        hr_s[...] = h0r_ref[...]
        hi_s[...] = h0i_ref[...]

    ks, kw, nw = bre_ref.shape
    by_sequence = u_ref.ndim == 3
    if by_sequence:
        for b in range(nb):
            for k in range(ks):
                us_s[k, pl.ds(b, steps, stride=nb), :] = u_ref[b, :, k * kw:(k + 1) * kw]
    else:
        for k in range(ks):
            tmp_s[0][k] = u_ref[:, k * kw:(k + 1) * kw]
        for t in range(steps):
            for k in range(ks):
                us_s[k, t * nb:(t + 1) * nb, :] = tmp_s[0][k, pl.ds(t, nb, stride=steps), :]

    def x_proj(k):
        uk = us_s[k].astype(BF16)
        xr_s[:, k * nw:(k + 1) * nw] = _dot(uk, bre_ref[k])
        xi_s[:, k * nw:(k + 1) * nw] = _dot(uk, bim_ref[k])

    def y_proj(k):
        ss = slice(k * nw, (k + 1) * nw)
        yk = _dot(xr_s[:, ss].astype(BF16), cre_ref[k]) - _dot(xi_s[:, ss].astype(BF16), cim_ref[k])
        ys_s[k] = jax.nn.gelu(yk + d_ref[:, k * kw:(k + 1) * kw] * us_s[k])

    def scan(rb, cs, unrolled):
        rows = slice(rb * SUBLANES, (rb + 1) * SUBLANES)
        width = cs.stop - cs.start
        a_r = jnp.broadcast_to(ar_ref[:, cs], (SUBLANES, width))
        a_i = jnp.broadcast_to(ai_ref[:, cs], (SUBLANES, width))

        def step(t, carry):
            hr, hi = carry
            r0 = t * nb + rb * SUBLANES
            tr = slice(r0, r0 + SUBLANES) if unrolled else pl.ds(pl.multiple_of(r0, SUBLANES), SUBLANES)
            nhr = a_r * hr - a_i * hi + xr_s[tr, cs]
            nhi = a_r * hi + a_i * hr + xi_s[tr, cs]
            xr_s[tr, cs] = nhr
            xi_s[tr, cs] = nhi
            return nhr, nhi

        carry = (hr_s[rows, cs], hi_s[rows, cs])
        if unrolled:
            for t in range(steps):
                carry = step(t, carry)
        else:
            carry = lax.fori_loop(0, steps, step, carry, unroll=min(steps, S5_UNROLL))
        hr_s[rows, cs], hi_s[rows, cs] = carry

    if by_sequence and nb == SUBLANES:
        x_proj(0)
        for k in range(ks):
            if k + 1 < ks:
                x_proj(k + 1)
            if k >= 1:
                y_proj(k - 1)
            scan(0, slice(k * nw, (k + 1) * nw), True)
        y_proj(ks - 1)
    else:
        for k in range(ks):
            x_proj(k)
        for rb in range(nb // SUBLANES):
            for cb in range(xr_s.shape[1] // S5_COL_BLOCK):
                scan(rb, slice(cb * S5_COL_BLOCK, (cb + 1) * S5_COL_BLOCK), False)
        for k in range(ks):
            y_proj(k)
    nrows = nb * steps
    split = S5_GLU_SPLIT if nrows % (S5_GLU_SPLIT * SUBLANES) == 0 else 1
    for j in range(split):
        rows = slice(j * nrows // split, (j + 1) * nrows // split)
        y = jnp.concatenate([ys_s[k, rows, :] for k in range(ks)], axis=1)
        y = y * jax.nn.sigmoid(_dot(y.astype(BF16), wglu_ref[...]) + bglu_ref[...])
        for k in range(ks):
            ys_s[k, rows, :] = y[:, k * kw:(k + 1) * kw]
    if by_sequence:
        for b in range(nb):
            for k in range(ks):
                y_ref[b, :, k * kw:(k + 1) * kw] = ys_s[k, pl.ds(b, steps, stride=nb), :]
    else:
        for t in range(steps):
            for k in range(ks):
                tmp_s[0][k, pl.ds(t, nb, stride=steps), :] = ys_s[k, t * nb:(t + 1) * nb, :]
        for k in range(ks):
            y_ref[:, k * kw:(k + 1) * kw] = tmp_s[0][k]

    @pl.when(c == pl.num_programs(0) - 1)
    def _():
        hr_out[...] = hr_s[...]
        hi_out[...] = hi_s[...]


def _s5(u, h0r, h0i, l0, params, l, steps):
    _, nb, ns = h0r.shape
    w = u.shape[-1]
    blk = steps * nb
    ks = params[2].shape[1]
    if u.ndim == 3:
        grid = (u.shape[1] // steps,)
        u_spec = pl.BlockSpec((nb, steps, w), lambda i: (0, i, 0))
        tmp = []
    else:
        assert u.shape[0] == blk
        grid = (1,)
        u_spec = _full(u.shape)
        tmp = [pltpu.VMEM((ks, blk, LANES), F32)]
    kern = functools.partial(_s5_kernel, nb=nb, steps=steps)
    return pl.pallas_call(
        kern,
        grid=grid,
        in_specs=[u_spec, _layer(h0r, l0), _layer(h0i, l0)] + [_layer(a, l) for a in params],
        out_specs=[u_spec, _full((nb, ns)), _full((nb, ns))],
        out_shape=[jax.ShapeDtypeStruct(u.shape, F32), jax.ShapeDtypeStruct((nb, ns), F32),
                   jax.ShapeDtypeStruct((nb, ns), F32)],
        scratch_shapes=[pltpu.VMEM((ks, blk, LANES), F32), pltpu.VMEM((ks, blk, LANES), F32),
                        pltpu.VMEM((blk, ns), F32), pltpu.VMEM((blk, ns), F32), pltpu.VMEM((nb, ns), F32),
                        pltpu.VMEM((nb, ns), F32)] + tmp,
        compiler_params=_params("arbitrary"),
        name="s5_scan",
    )(u, h0r, h0i, *params)


def _log2(n):
    assert n & (n - 1) == 0, n
    return n.bit_length() - 1


def _chunk_masks(rows, chunk):
    sh = _log2(chunk)
    rr = lax.broadcasted_iota(jnp.int32, (rows, rows), 0)
    cc = lax.broadcasted_iota(jnp.int32, (rows, rows), 1)
    same = lax.shift_right_logical(rr, sh) == lax.shift_right_logical(cc, sh)
    return same, same & (cc <= rr)


def _gla_prologue(p_ref, wa_ref, ba_ref, chunk, qk):
    rows = p_ref.shape[0]
    q = p_ref[:, 0:qk]
    k = p_ref[:, qk:2 * qk]
    alow = p_ref[:, p_ref.shape[1] - LANES:].astype(BF16)
    z = _dot(alow, wa_ref[...]) + ba_ref[...]
    la = (jnp.minimum(z, 0.0) - jnp.log(1.0 + jnp.exp(-jnp.abs(z)))) * (1.0 / GLA_TAU)
    same, causal = _chunk_masks(rows, chunk)
    tri = jnp.where(causal, 1.0, 0.0).astype(BF16)
    blk = jnp.where(same, 1.0, 0.0).astype(BF16)
    la_hi = la.astype(BF16)
    la_lo = (la - la_hi.astype(F32)).astype(BF16)
    b = _dot(tri, la_hi) + _dot(tri, la_lo)
    bl = _dot(blk, la_hi) + _dot(blk, la_lo)
    return q, k, b, bl


def _gla_out(o, gn, r):
    on = o * lax.rsqrt(jnp.mean(o * o, axis=-1, keepdims=True) + EPS) * gn
    return on * (r * jax.nn.sigmoid(r))


def _gla_prompt_init(st_s):
    @pl.when(pl.program_id(1) == 0)
    def _():
        st_s[...] = jnp.zeros_like(st_s)


def _gla_prompt_final(s_ref, st_s, *, heads, dk):
    @pl.when(pl.program_id(1) == pl.num_programs(1) - 1)
    def _():
        for h in range(heads):
            s_ref[0, h] = st_s[:, h * dk:(h + 1) * dk].T


def _gla_prompt_body(p_ref, wa_ref, ba_ref, gn_ref, y_ref, st_s, *, heads, dk, dv, chunk):
    rows = p_ref.shape[0]
    qk, vw = heads * dk, heads * dv
    q, k, b, bl = _gla_prologue(p_ref, wa_ref, ba_ref, chunk, qk)
    v_off, r_off = 2 * qk, 2 * qk + vw
    qd = (q * (dk ** -0.5) * jnp.exp(b)).astype(BF16)
    kd = (k * jnp.exp(-b)).astype(BF16)
    kst = (k * jnp.exp(bl - b)).astype(BF16)
    dec = jnp.exp(bl)
    gn = gn_ref[...]
    zero = jnp.zeros((), BF16)
    qk_head = lax.shift_right_logical(lax.broadcasted_iota(jnp.int32, (chunk, qk), 1), _log2(dk))
    v_head = lax.shift_right_logical(lax.broadcasted_iota(jnp.int32, (chunk, vw), 1), _log2(dv))
    rr = lax.broadcasted_iota(jnp.int32, (chunk, heads * chunk), 0)
    cc = lax.broadcasted_iota(jnp.int32, (chunk, heads * chunk), 1) & (chunk - 1)
    causal = cc <= rr
    nchunk = rows // chunk
    by_head = lambda x, head: jnp.concatenate([jnp.where(head == h, x, zero) for h in range(heads)], axis=0)
    att, ds_t, v_diag, qd_rows = [], [], [], []
    for ci in range(nchunk):
        rs = slice(ci * chunk, (ci + 1) * chunk)
        v_c = p_ref[rs, v_off:v_off + vw].astype(BF16)
        v_rows = jnp.concatenate([v_c[:, h * dv:(h + 1) * dv] for h in range(heads)], axis=0)
        v_diag.append(by_head(v_c, v_head))
        qd_rows.append(by_head(qd[rs], qk_head))
        att.append(lax.dot_general(qd[rs], by_head(kd[rs], qk_head), NT_DIMS, preferred_element_type=F32))
        ds_t.append(lax.dot_general(v_rows, by_head(kst[rs], qk_head), TN_DIMS, preferred_element_type=F32))
    yield
    o_intra = [_dot(jnp.where(causal, att[ci], 0.0).astype(BF16), v_diag[ci]) for ci in range(nchunk)]
    yield
    st = st_s[...]
    o_inter = []
    for ci in range(nchunk):
        o_inter.append(lax.dot_general(qd_rows[ci], st.astype(BF16), NT_DIMS, preferred_element_type=F32))
        st = dec[ci * chunk:ci * chunk + 1, :] * st + ds_t[ci]
    st_s[...] = st
    yield
    for ci in range(nchunk):
        rs = slice(ci * chunk, (ci + 1) * chunk)
        for h in range(heads):
            o = o_intra[ci][:, h * dv:(h + 1) * dv] + o_inter[ci][h * chunk:(h + 1) * chunk]
            r = p_ref[rs, r_off + h * dv:r_off + (h + 1) * dv]
            y_ref[rs, h * dv:(h + 1) * dv] = _gla_out(o, gn, r)


def _gla_sample_body(p_ref, s0_ref, wa_ref, ba_ref, gn_ref, y_ref, s_ref, *, heads, dk, dv, t):
    rows = p_ref.shape[0]
    qk, vw = heads * dk, heads * dv
    assert dv == LANES and qk % LANES == 0
    q, k, b, bl = _gla_prologue(p_ref, wa_ref, ba_ref, t, qk)
    v_off, r_off = 2 * qk, 2 * qk + vw
    qd = (q * (dk ** -0.5) * jnp.exp(b)).astype(BF16)
    kd = (k * jnp.exp(-b)).astype(BF16)
    kst = (k * jnp.exp(bl - b)).astype(BF16)
    dec = jnp.exp(bl)
    gn = gn_ref[...]
    per_tile = SUBLANES // t
    zero = jnp.zeros((), BF16)
    qk_head = lax.shift_right_logical(lax.broadcasted_iota(jnp.int32, (SUBLANES, qk), 1), _log2(dk))
    v_head = lax.shift_right_logical(lax.broadcasted_iota(jnp.int32, (SUBLANES, vw), 1), _log2(dv))
    by_head = lambda x, head: jnp.concatenate([jnp.where(head == h, x, zero) for h in range(heads)], axis=0)
    stack_seq = lax.shift_right_logical(
        lax.broadcasted_iota(jnp.int32, (heads * SUBLANES, 1), 0) & (SUBLANES - 1), _log2(t))
    rr = lax.broadcasted_iota(jnp.int32, (SUBLANES, heads * SUBLANES), 0)
    cc = lax.broadcasted_iota(jnp.int32, (SUBLANES, heads * SUBLANES), 1) & (SUBLANES - 1)
    sh = _log2(t)
    causal = (lax.shift_right_logical(rr, sh) == lax.shift_right_logical(cc, sh)) & (cc <= rr)
    att, o_state, v_diag = [], [], []
    for ti in range(rows // SUBLANES):
        rs = slice(ti * SUBLANES, (ti + 1) * SUBLANES)
        v_t = p_ref[rs, v_off:v_off + vw].astype(BF16)
        v_rows = jnp.concatenate([v_t[:, h * dv:(h + 1) * dv] for h in range(heads)], axis=0)
        v_diag.append(by_head(v_t, v_head))
        qd_rows, kst_rows = by_head(qd[rs], qk_head), by_head(kst[rs], qk_head)
        att.append(lax.dot_general(qd[rs], by_head(kd[rs], qk_head), NT_DIMS, preferred_element_type=F32))
        o = None
        for j in range(per_tile):
            bidx = ti * per_tile + j
            mine = stack_seq == j
            s0 = jnp.concatenate([s0_ref[bidx, h] for h in range(heads)], axis=0)
            oj = _dot(jnp.where(mine, qd_rows, zero), s0.astype(BF16))
            o = oj if o is None else o + oj
            ds = lax.dot_general(jnp.where(mine, kst_rows, zero), v_rows, TN_DIMS, preferred_element_type=F32)
            drow = dec[ti * SUBLANES + j * t:ti * SUBLANES + j * t + 1, :]
            dcol = jnp.concatenate([jnp.broadcast_to(drow[:, g * LANES:(g + 1) * LANES], (LANES, LANES)).T
                                    for g in range(qk // LANES)], axis=0)
            s_new = dcol * s0 + ds
            for h in range(heads):
                s_ref[bidx, h] = s_new[h * dk:(h + 1) * dk]
        o_state.append(o)
    yield
    for ti in range(rows // SUBLANES):
        rs = slice(ti * SUBLANES, (ti + 1) * SUBLANES)
        o_intra = _dot(jnp.where(causal, att[ti], 0.0).astype(BF16), v_diag[ti])
        for h in range(heads):
            o = o_intra[:, h * dv:(h + 1) * dv] + o_state[ti][h * SUBLANES:(h + 1) * SUBLANES]
            r = p_ref[rs, r_off + h * dv:r_off + (h + 1) * dv]
            y_ref[rs, h * dv:(h + 1) * dv] = _gla_out(o, gn, r)


def _take_turns(bodies):
    while bodies:
        bodies = [b for b in bodies if next(b, StopIteration) is not StopIteration]


def _first_half(shape):
    lane = lax.broadcasted_iota(jnp.int32, shape, 1)
    return (lane & (SWA_HD - 1)) < SWA_HD // 2


def _rope_table_kernel(pos_ref, inv_ref, cos_ref, sin_ref):
    ang = pos_ref[...] * inv_ref[...]
    s = jnp.sin(ang)
    cos_ref[...] = jnp.cos(ang)
    sin_ref[...] = jnp.where(_first_half(ang.shape), -s, s)


def _rope_table(pos, inv):
    rows = pos.shape[0]
    return pl.pallas_call(
        _rope_table_kernel,
        out_shape=[jax.ShapeDtypeStruct((rows, LANES), F32)] * 2,
        name="rope_table",
    )(pos.astype(F32).reshape(rows, 1), inv)


def _rope(x, cos, sin):
    half = SWA_HD // 2
    first_half = _first_half(cos.shape)
    out = []
    for g in range(x.shape[1] // LANES):
        xg = x[:, g * LANES:(g + 1) * LANES]
        swapped = jnp.where(first_half, pltpu.roll(xg, LANES - half, 1), pltpu.roll(xg, half, 1))
        out.append(xg * cos + swapped * sin)
    return out[0] if len(out) == 1 else jnp.concatenate(out, axis=1)


def _swa_prompt_init(prev_s):
    @pl.when(pl.program_id(1) == 0)
    def _():
        prev_s[...] = jnp.zeros_like(prev_s)


def _swa_prompt_body(sink_ref, cur_ref, cos_ref, sin_ref, y_ref, nk_ref, nv_ref, prev_s, s_s, pc_s, *, heads, l):
    i = pl.program_id(1)
    w = WINDOW
    qw = heads * SWA_HD
    kvw = SWA_KV_HEADS * SWA_HD
    gq = heads // SWA_KV_HEADS
    nq = cur_ref.shape[0] // w
    cos, sin = cos_ref[...], sin_ref[...]
    q = (_rope(cur_ref[:, 0:qw], cos, sin) * (SWA_HD ** -0.5)).astype(BF16)
    k_cur = _rope(cur_ref[:, qw:qw + kvw], cos, sin)
    v_cur = cur_ref[:, qw + kvw:qw + 2 * kvw]
    new = [x.astype(BF16) for x in (k_cur, pltpu.roll(k_cur, SWA_HD, 1), v_cur, pltpu.roll(v_cur, SWA_HD, 1))]
    k2, k2r, v2, v2r = [jnp.concatenate([prev_s[j], new[j]], axis=0) for j in range(4)]
    zero = jnp.zeros((), BF16)
    lo2 = lax.broadcasted_iota(jnp.int32, ((nq + 1) * w, LANES), 1) < SWA_HD
    lo = lax.broadcasted_iota(jnp.int32, (w, LANES), 1) < SWA_HD
    k_both = (jnp.where(lo2, k2, k2r), jnp.where(lo2, k2r, k2))
    v_half = {(0, 0): jnp.where(lo2, v2, zero), (1, 1): jnp.where(lo2, zero, v2),
              (0, 1): jnp.where(lo2, zero, v2r), (1, 0): jnp.where(lo2, v2r, zero)}
    upper = lax.broadcasted_iota(jnp.int32, (w, w), 1) > lax.broadcasted_iota(jnp.int32, (w, w), 0)
    no_prev = jnp.where(i > 0, 0.0, -jnp.inf)
    slot = lambda c, h: slice((c * heads + h) * w, (c * heads + h + 1) * w)
    for c in range(nq):
        for kv in range(SWA_KV_HEADS):
            qm = []
            for h in range(kv * gq, (kv + 1) * gq):
                qg = q[c * w:(c + 1) * w, (h // 2) * LANES:(h // 2 + 1) * LANES]
                qm.append(jnp.where(lo, qg, zero) if h % 2 == 0 else jnp.where(lo, zero, qg))
            s_s[(c * heads + kv * gq) * w:(c * heads + (kv + 1) * gq) * w, :] = lax.dot_general(
                jnp.concatenate(qm, axis=0), k_both[kv][c * w:(c + 2) * w], NT_DIMS,
                preferred_element_type=F32)
    yield
    sink_term = {}
    for c in range(nq):
        for h in range(heads):
            s_prev = s_s[slot(c, h), 0:w] + no_prev if c == 0 else s_s[slot(c, h), 0:w]
            sp = jnp.where(upper, s_prev, s_s[slot(c, h), w:2 * w])
            sink = sink_ref[l, h]
            m = jnp.maximum(jnp.max(sp, axis=-1, keepdims=True), sink)
            p = jnp.exp(sp - m)
            pc_s[slot(c, h), 0:w] = jnp.where(upper, p, 0.0).astype(BF16)
            pc_s[slot(c, h), w:2 * w] = jnp.where(upper, 0.0, p).astype(BF16)
            sink_term[c, h] = jnp.exp(sink - m)
    yield
    num = {(c, h): _dot(pc_s[slot(c, h), :], v_half[(h // gq, h % 2)][c * w:(c + 2) * w])
           for c in range(nq) for h in range(heads)}
    den_all = _dot(pc_s[...], jnp.ones((2 * w, LANES), BF16))
    yield
    for c in range(nq):
        for g in range(heads // 2):
            he, ho = 2 * g, 2 * g + 1
            den = jnp.where(lo, den_all[slot(c, he)] + sink_term[c, he], den_all[slot(c, ho)] + sink_term[c, ho])
            y_ref[c * w:(c + 1) * w, g * LANES:(g + 1) * LANES] = (num[c, he] + num[c, ho]) * (1.0 / den)
    for j in range(4):
        prev_s[j] = new[j][(nq - 1) * w:]
    nk_ref[0] = k_cur[(nq - 1) * w:]
    nv_ref[0] = v_cur[(nq - 1) * w:]


def _gla_swa_prompt_kernel(pb_ref, wa_ref, ba_ref, gn_ref, sink_ref, pc_ref, cos_ref, sin_ref,
                           yb_ref, s_ref, yc_ref, nk_ref, nv_ref, st_s, prev_s, s_s, pc_s, *, gla, swa):
    _gla_prompt_init(st_s)
    _swa_prompt_init(prev_s)
    _take_turns([_gla_prompt_body(pb_ref, wa_ref, ba_ref, gn_ref, yb_ref, st_s, chunk=GLA_CHUNK, **gla),
                 _swa_prompt_body(sink_ref, pc_ref, cos_ref, sin_ref, yc_ref, nk_ref, nv_ref, prev_s, s_s, pc_s,
                                  **swa)])
    _gla_prompt_final(s_ref, st_s, heads=gla["heads"], dk=gla["dk"])


def _gla_swa_prompt(pb, pc, wa, ba, gn, sinks, cos, sin, l, bsz, gla_heads, dk, dv, swa_heads, rows):
    n, pbw = pb.shape
    pcw = pc.shape[1]
    nblk = n // bsz // rows
    nq = rows // WINDOW
    qw, kvw = swa_heads * SWA_HD, SWA_KV_HEADS * SWA_HD
    kern = functools.partial(_gla_swa_prompt_kernel, gla=dict(heads=gla_heads, dk=dk, dv=dv),
                             swa=dict(heads=swa_heads, l=l))
    row_blk = lambda wdt: pl.BlockSpec((rows, wdt), lambda b, i: (b * nblk + i, 0))
    tab = pl.BlockSpec((rows, LANES), lambda b, i: (i, 0))
    win = pl.BlockSpec((1, WINDOW, kvw), lambda b, i: (b, 0, 0))
    return pl.pallas_call(
        kern,
        grid=(bsz, nblk),
        in_specs=[row_blk(pbw), _layer(wa, l), _layer(ba, l), _layer(gn, l),
                  pl.BlockSpec(memory_space=pltpu.SMEM), row_blk(pcw), tab, tab],
        out_specs=[row_blk(gla_heads * dv), pl.BlockSpec((1, gla_heads, dk, dv), lambda b, i: (b, 0, 0, 0)),
                   row_blk(qw), win, win],
        out_shape=[jax.ShapeDtypeStruct((n, gla_heads * dv), F32),
                   jax.ShapeDtypeStruct((bsz, gla_heads, dk, dv), F32),
                   jax.ShapeDtypeStruct((n, qw), F32), jax.ShapeDtypeStruct((bsz, WINDOW, kvw), F32),
                   jax.ShapeDtypeStruct((bsz, WINDOW, kvw), F32)],
        scratch_shapes=[pltpu.VMEM((dv, gla_heads * dk), F32), pltpu.VMEM((4, WINDOW, LANES), BF16),
                        pltpu.VMEM((nq * swa_heads * WINDOW, 2 * WINDOW), F32),
                        pltpu.VMEM((nq * swa_heads * WINDOW, 2 * WINDOW), BF16)],
        compiler_params=_params("parallel", "arbitrary"),
        name="gla_swa_prompt",
    )(pb, wa, ba, gn, sinks, pc, cos, sin)


def _swa_sample_body(sinkcol_ref, p_ref, ck_ref, cv_ref, cos_ref, sin_ref, y_ref, nk_ref, nv_ref, *, heads, t, nb):
    w = ck_ref.shape[3]
    qw = heads * SWA_HD
    kvw = SWA_KV_HEADS * SWA_HD
    gq = heads // SWA_KV_HEADS
    per_tile = SUBLANES // t
    nrow = heads * SUBLANES
    cos, sin = cos_ref[...], sin_ref[...]
    lo = lax.broadcasted_iota(jnp.int32, (SUBLANES, LANES), 1) < SWA_HD
    row_r = lax.broadcasted_iota(jnp.int32, (nrow, 1), 0) & (SUBLANES - 1)
    row_seq = lax.shift_right_logical(row_r, _log2(t))
    row_step = row_r & (t - 1)
    cache_ok = lax.broadcasted_iota(jnp.int32, (nrow, w), 1) > row_step
    new_r = lax.broadcasted_iota(jnp.int32, (nrow, SUBLANES), 1)
    new_ok = (lax.shift_right_logical(new_r, _log2(t)) == row_seq) & ((new_r & (t - 1)) <= row_step)
    sink_col = sinkcol_ref[...]
    lane_w = lax.broadcasted_iota(jnp.int32, (SWA_HD, w), 1)
    flat = lambda c: jnp.concatenate([c[kv] for kv in range(SWA_KV_HEADS)], axis=0).astype(BF16)
    pad_rows = jnp.zeros((LANES - 2 * SUBLANES, kvw), F32)

    def on_kv_half(x, h, back=False):
        own, kv = h % 2, h // gq
        moved = x if own == kv else pltpu.roll(x, SWA_HD, 1)
        keep = own if back else kv
        return jnp.where(lo, moved, 0.0) if keep == 0 else jnp.where(lo, 0.0, moved)

    new_k, new_v, scores = [], [], []
    for ti in range(nb // per_tile):
        rs = slice(ti * SUBLANES, (ti + 1) * SUBLANES)
        q8 = _rope(p_ref[rs, 0:qw], cos, sin) * (SWA_HD ** -0.5)
        kn8 = _rope(p_ref[rs, qw:qw + kvw], cos, sin)
        new_k.append(kn8)
        new_v.append(p_ref[rs, qw + kvw:qw + 2 * kvw])
        qm = jnp.concatenate([on_kv_half(q8[:, (h // 2) * LANES:(h // 2 + 1) * LANES], h) for h in range(heads)],
                             axis=0).astype(BF16)
        s_c = None
        for j in range(per_tile):
            s_j = _dot(qm, flat(ck_ref[ti * per_tile + j]))
            s_c = s_j if s_c is None else jnp.where(row_seq == j, s_j, s_c)
        s_n = lax.dot_general(qm, kn8.astype(BF16), NT_DIMS, preferred_element_type=F32)
        scores.append((s_c, s_n))
    yield
    probs = []
    for s_c, s_n in scores:
        s_c = jnp.where(cache_ok, s_c, -jnp.inf)
        s_n = jnp.where(new_ok, s_n, -jnp.inf)
        m = jnp.maximum(jnp.maximum(jnp.max(s_c, axis=1, keepdims=True), jnp.max(s_n, axis=1, keepdims=True)),
                        sink_col)
        p_c = jnp.exp(s_c - m)
        p_n = jnp.exp(s_n - m)
        den = jnp.sum(p_c, axis=1, keepdims=True) + jnp.sum(p_n, axis=1, keepdims=True) + jnp.exp(sink_col - m)
        rden = 1.0 / den
        probs.append((p_c * rden, (p_n * rden).astype(BF16)))
    yield
    outs = []
    for ti, (p_c, p_n) in enumerate(probs):
        o = _dot(p_n, new_v[ti].astype(BF16))
        for j in range(per_tile):
            p_j = jnp.where(row_seq == j, p_c, 0.0).astype(BF16)
            o = o + lax.dot_general(p_j, flat(cv_ref[ti * per_tile + j]), NT_DIMS, preferred_element_type=F32)
        outs.append(o)
    yield
    for ti, o in enumerate(outs):
        for g in range(heads // 2):
            pair = [on_kv_half(o[h * SUBLANES:(h + 1) * SUBLANES], h, back=True) for h in (2 * g, 2 * g + 1)]
            y_ref[ti * SUBLANES:(ti + 1) * SUBLANES, g * LANES:(g + 1) * LANES] = pair[0] + pair[1]
    yield
    for ti in range(nb // per_tile):
        kv_t = jnp.concatenate([new_k[ti], new_v[ti], pad_rows], axis=0).T
        for j in range(per_tile):
            bidx = ti * per_tile + j
            for c_ref, n_ref, col0 in ((ck_ref, nk_ref, 0), (cv_ref, nv_ref, SUBLANES)):
                for kv in range(SWA_KV_HEADS):
                    tail = pltpu.roll(kv_t[kv * SWA_HD:(kv + 1) * SWA_HD, :], w - t - col0 - j * t, 1)
                    n_ref[bidx, kv] = jnp.where(lane_w < w - t, pltpu.roll(c_ref[bidx, kv], w - t, 1), tail)


def _layered_outputs(body, n_in, blocks, prev):
    n_prev = len(prev[0]) if prev else 0
    if not n_prev:
        kern = lambda *refs: _take_turns([body(*refs)])
        return (kern, [], [], [pl.BlockSpec(b, m) for b, m, _ in blocks],
                [jax.ShapeDtypeStruct(s, F32) for _, _, s in blocks])

    def kern(*refs):
        ins, earlier, outs = refs[:n_in], refs[n_in:n_in + len(blocks) * n_prev], refs[n_in + len(blocks) * n_prev:]
        stacked = outs[1:1 + len(blocks)]
        _take_turns([body(*ins, outs[0], *[s.at[n_prev] for s in stacked])])
        for k, s in enumerate(stacked):
            for ll in range(n_prev):
                s[ll] = earlier[k * n_prev + ll][...]

    in_specs = [pl.BlockSpec(b, m) for b, m, _ in blocks for _ in range(n_prev)]
    operands = [a for arrays in prev for a in arrays]
    out_specs = [pl.BlockSpec((n_prev + 1,) + b, lambda i, m=m: (0,) + m(i)) for b, m, _ in blocks]
    out_shapes = [jax.ShapeDtypeStruct((n_prev + 1,) + s, F32) for _, _, s in blocks]
    return kern, in_specs, operands, out_specs, out_shapes


def _gla_sample(p, s0, wa, ba, gn, l, t, nb, prev):
    n, pw = p.shape
    _, bsz, heads, dk, dv = s0.shape
    rows = nb * t
    body = functools.partial(_gla_sample_body, heads=heads, dk=dk, dv=dv, t=t)
    blocks = [((nb, heads, dk, dv), lambda i: (i, 0, 0, 0), s0.shape[1:])]
    kern, more_specs, more, st_specs, st_shapes = _layered_outputs(body, 5, blocks, prev)
    return pl.pallas_call(
        kern,
        grid=(bsz // nb,),
        in_specs=[pl.BlockSpec((rows, pw), lambda i: (i, 0)),
                  pl.BlockSpec((None, nb, heads, dk, dv), lambda i: (l, i, 0, 0, 0)),
                  _layer(wa, l), _layer(ba, l), _layer(gn, l)] + more_specs,
        out_specs=[pl.BlockSpec((rows, heads * dv), lambda i: (i, 0))] + st_specs,
        out_shape=[jax.ShapeDtypeStruct((n, heads * dv), F32)] + st_shapes,
        compiler_params=_params("parallel"),
        name="gla_sample",
    )(p, s0, wa, ba, gn, *more)


def _swa_sample(p, ck, cv, sink_cols, cos, sin, l, heads, t, nb, prev):
    n, pw = p.shape
    _, bsz, kvh, hd, w = ck.shape
    qw = heads * SWA_HD
    body = functools.partial(_swa_sample_body, heads=heads, t=t, nb=nb)
    blocks = [((nb, kvh, hd, w), lambda i: (i, 0, 0, 0), ck.shape[1:])] * 2
    kern, more_specs, more, c_specs, c_shapes = _layered_outputs(body, 6, blocks, prev)
    cache = pl.BlockSpec((None, nb, kvh, hd, w), lambda i: (l, i, 0, 0, 0))
    return pl.pallas_call(
        kern,
        grid=(bsz // nb,),
        in_specs=[_layer(sink_cols, l), pl.BlockSpec((nb * t, pw), lambda i: (i, 0)), cache, cache,
                  _full(cos.shape), _full(sin.shape)] + more_specs,
        out_specs=[pl.BlockSpec((nb * t, qw), lambda i: (i, 0))] + c_specs,
        out_shape=[jax.ShapeDtypeStruct((n, qw), F32)] + c_shapes,
        compiler_params=_params("parallel"),
        name="swa_sample",
    )(sink_cols, p, ck, cv, cos, sin, *more)


FFN_HIDDEN_CHUNK = 1024


MERGE_FFN_SPLIT = 2


def _merge_ffn_rows(rows, x_ref, ya_ref, yb_ref, yc_ref, gt_ref, wb_ref, wo_ref, g_ref, wg_ref, wu_ref, wd_ref,
                    gf_ref, o_ref, final_norm):
    d = x_ref.shape[1]
    merged = None
    off = 0
    for i, y_ref in enumerate((ya_ref, yb_ref, yc_ref)):
        wdt = y_ref.shape[1]
        br = _dot(y_ref[rows, :].astype(BF16), wb_ref[off:off + wdt, :])
        term = jax.nn.sigmoid(gt_ref[rows, i * d:(i + 1) * d]) * br
        merged = term if merged is None else merged + term
        off += wdt
    yield
    x = x_ref[rows, :] + _dot(merged.astype(BF16), wo_ref[...])
    yield
    hb = _rms(x, g_ref[...]).astype(BF16)
    hidden = wg_ref.shape[1]
    acc = x
    for c0 in range(0, hidden, FFN_HIDDEN_CHUNK):
        c1 = min(c0 + FFN_HIDDEN_CHUNK, hidden)
        gate = _dot(hb, wg_ref[:, c0:c1])
        up = _dot(hb, wu_ref[:, c0:c1])
        act = (gate * jax.nn.sigmoid(gate) * up).astype(BF16)
        acc = acc + _dot(act, wd_ref[c0:c1, :])
        yield
    o_ref[rows, :] = _rms(acc, gf_ref[...]) if final_norm else acc


def _merge_ffn_kernel(*refs, final_norm):
    tm = refs[0].shape[0]
    split = MERGE_FFN_SPLIT if tm % (MERGE_FFN_SPLIT * SUBLANES) == 0 else 1
    _take_turns([_merge_ffn_rows(slice(j * tm // split, (j + 1) * tm // split), *refs, final_norm)
                 for j in range(split)])


def _merge_ffn(x, ya, yb, yc, gates, wb, wo, g, wg, wu, wd, gf, l, tm, final_norm):
    n, d = x.shape
    row = lambda a: pl.BlockSpec((tm, a.shape[1]), lambda i: (i, 0))
    return pl.pallas_call(
        functools.partial(_merge_ffn_kernel, final_norm=final_norm),
        grid=(n // tm,),
        in_specs=[row(x), row(ya), row(yb), row(yc), row(gates), _layer(wb, l), _layer(wo, l), _layer(g, l),
                  _layer(wg, l), _layer(wu, l), _layer(wd, l), _full(gf.shape)],
        out_specs=pl.BlockSpec((tm, d), lambda i: (i, 0)),
        out_shape=jax.ShapeDtypeStruct((n, d), F32),
        compiler_params=_params("parallel"),
        name="merge_ffn",
    )(x, ya, yb, yc, gates, wb, wo, g, wg, wu, wd, gf)


DENSE_ROWS = 512
S5_STEPS = 128
MIX_WINDOWS = 4
GLA_SAMPLE_SEQS = 32
SWA_SAMPLE_SEQS = 16


def _tile(n, want):
    while n % want:
        want //= 2
    return want


def kernel(x_prompt, x_sample, state_s5_re, state_s5_im, state_gla, cache_swa_k, cache_swa_v, g_mix, w_in, s5_lambda_re, s5_lambda_im, s5_log_dt, s5_b_re, s5_b_im, s5_c_re, s5_c_im, s5_d, w_glu, b_glu, w_gla_a2, b_gla_a, g_gla_norm, swa_sinks, w_branch, w_out, g_ffn, w_ffn_gate, w_ffn_up, w_ffn_down, g_final):
    bp, tp, d = x_prompt.shape
    bs, ts, _ = x_sample.shape
    depth = w_in.shape[0]
    s5_w = s5_d.shape[1]
    groups, s5_n = s5_lambda_re.shape[1:]
    gla_qk = w_gla_a2.shape[2]
    gla_rank = w_gla_a2.shape[1]
    gla_heads, gla_dk, gla_dv = state_gla.shape[2:]
    gla_v = gla_heads * gla_dv
    swa_heads = swa_sinks.shape[1]
    swa_q = swa_heads * SWA_HD
    swa_kv = SWA_KV_HEADS * SWA_HD
    assert SWA_KV_HEADS * SWA_HD == LANES and bp % SUBLANES == 0 and bs % SUBLANES == 0 and SUBLANES % ts == 0
    assert tp % WINDOW == 0 and cache_swa_k.shape[2] == WINDOW

    o_a, o_b = 0, s5_w
    o_low = o_b + 2 * gla_qk + 2 * gla_v
    o_c = o_low + gla_rank
    o_g = o_c + swa_q + 2 * swa_kv
    widths = (s5_w, 2 * gla_qk + 2 * gla_v + LANES, swa_q + 2 * swa_kv, 3 * d)

    hp = x_prompt.reshape(bp * tp, d)
    hs = x_sample.reshape(bs * ts, d)
    inv = ROPE_THETA ** (-jnp.arange(SWA_HD // 2, dtype=F32) * (2.0 / SWA_HD))
    inv = jnp.tile(inv, 2 * LANES // SWA_HD).reshape(1, LANES)
    cos_p, sin_p = _rope_table(jnp.arange(tp, dtype=jnp.int32), inv)
    cos_s, sin_s = _rope_table(PAST_LEN + (jnp.arange(SUBLANES, dtype=jnp.int32) & (ts - 1)), inv)
    zeros_s5 = jnp.zeros((1, bp, groups * s5_n), F32)
    tm_p, tm_s = _tile(bp * tp, DENSE_ROWS), _tile(bs * ts, DENSE_ROWS)
    s5_steps = _tile(tp, S5_STEPS)
    mix_rows = WINDOW * _tile(tp // WINDOW, MIX_WINDOWS)
    assert mix_rows % GLA_CHUNK == 0

    w_t = jnp.swapaxes(w_in, 1, 2).astype(BF16)
    wa2 = jnp.concatenate([w_gla_a2, jnp.zeros((depth, LANES - gla_rank, gla_qk), F32)], axis=1).astype(BF16)
    ba = b_gla_a.reshape(depth, 1, gla_qk)
    gn = g_gla_norm.reshape(depth, 1, gla_dv)
    ar, ai, bbr, bbi = _s5_prep(s5_lambda_re, s5_lambda_im, s5_log_dt, s5_b_re, s5_b_im)
    bre, bim, cre, cim = _s5_pack(bbr, bbi, s5_c_re, s5_c_im)
    s5_d_r, b_glu_r = s5_d.reshape(depth, 1, s5_w), b_glu.reshape(depth, 1, s5_w)
    h0r_s = state_s5_re.reshape(depth, bs, groups * s5_n)
    h0i_s = state_s5_im.reshape(depth, bs, groups * s5_n)
    g_mix_r, g_ffn_r, gf = g_mix.reshape(depth, 1, d), g_ffn.reshape(depth, 1, d), g_final.reshape(1, d)
    later = (w_glu, w_branch, w_out, w_ffn_gate, w_ffn_up, w_ffn_down)
    cast_rows = (bp * tp // tm_p) * 2 * SUBLANES
    cast_in_call = all((w.shape[0] * w.shape[1]) % cast_rows == 0 for w in later)
    casted = None if cast_in_call else [w.astype(BF16) for w in later]
    sink_cols = jnp.repeat(swa_sinks, SUBLANES, axis=1).reshape(depth, swa_heads * SUBLANES, 1)
    to_lanes = lambda c: jnp.transpose(c, (0, 1, 3, 4, 2))
    from_lanes = lambda c: jnp.transpose(c, (0, 1, 4, 2, 3))
    ck, cv = to_lanes(cache_swa_k), to_lanes(cache_swa_v)

    outs_p = [[] for _ in range(5)]
    outs_s = [[], []]
    gla_s, nk_s, nv_s = [], [], []
    for l in range(depth):
        last = l == depth - 1

        (pa, pb, pc, pg), new_casts = _inproj(hp, g_mix_r, w_t, o_c, l, widths, tm_p,
                                              later if casted is None else ())
        if casted is None:
            casted = new_casts
        wglu_b, wb, wo, wg, wu, wd = casted
        s5_params = (ar, ai, bre, bim, cre, cim, s5_d_r, wglu_b, b_glu_r)
        ya, hr, hi = _s5(pa.reshape(bp, tp, s5_w), zeros_s5, zeros_s5, 0, s5_params, l, s5_steps)
        ya = ya.reshape(bp * tp, s5_w)
        yb, sg, yc, nk, nv = _gla_swa_prompt(pb, pc, wa2, ba, gn, swa_sinks, cos_p, sin_p, l, bp, gla_heads, gla_dk,
                                             gla_dv, swa_heads, mix_rows)
        hp = _merge_ffn(hp, ya, yb, yc, pg, wb, wo, g_ffn_r, wg, wu, wd, gf, l, tm_p, last)
        for lst, val in zip(outs_p, (hr.reshape(bp, groups, s5_n), hi.reshape(bp, groups, s5_n), sg,
                                     nk.reshape(bp, WINDOW, SWA_KV_HEADS, SWA_HD),
                                     nv.reshape(bp, WINDOW, SWA_KV_HEADS, SWA_HD))):
            lst.append(val)

        (pa, pb, pc, pg), _ = _inproj(hs, g_mix_r, w_t, o_c, l, widths, tm_s)
        ya, hr, hi = _s5(pa, h0r_s, h0i_s, l, s5_params, l, ts)
        yb, sg = _gla_sample(pb, state_gla, wa2, ba, gn, l, ts, _tile(bs, GLA_SAMPLE_SEQS),
                             (gla_s,) if last else ())
        yc, nk, nv = _swa_sample(pc, ck, cv, sink_cols, cos_s, sin_s, l, swa_heads, ts, _tile(bs, SWA_SAMPLE_SEQS),
                                 (nk_s, nv_s) if last else ())
        gla_s, nk_s, nv_s = gla_s + [sg], nk_s + [nk], nv_s + [nv]
        hs = _merge_ffn(hs, ya, yb, yc, pg, wb, wo, g_ffn_r, wg, wu, wd, gf, l, tm_s, last)
        for lst, val in zip(outs_s, (hr.reshape(bs, groups, s5_n), hi.reshape(bs, groups, s5_n))):
            lst.append(val)

    layered = lambda a: a if a.ndim == ck.ndim else a[None]
    outs_s = [jnp.stack(o) for o in outs_s] + [layered(gla_s[-1]), from_lanes(layered(nk_s[-1])),
                                               from_lanes(layered(nv_s[-1]))]
    return (hp.reshape(bp, tp, d), hs.reshape(bs, ts, d), *[jnp.stack(o) for o in outs_p], *outs_s)
```
